```python
import jax, jax.numpy as jnp
from jax import lax
import numpy as np

D_MODEL = 1024
BATCH = 8
SEQ = 2048
DEPTH = 2

GRID_W = 64
CTX_LEN = 256
N_MIXERS = 2
EPS = 1e-6
NEG_INF = -1e30
A_CHUNK = 128
A_HALF = 2 * D_MODEL
A_GROUPS = 8
A_GC = A_HALF // A_GROUPS
B_INNER = 2 * D_MODEL
B_HEADS = 8
B_DH = B_INNER // B_HEADS
B_DQK = B_DH // 2
B_DV = B_DH
B_CONV = 5
B_CHUNK = 128
N_GROUPS = 4
EXP_PER_GROUP = 8
N_EXPERTS = N_GROUPS * EXP_PER_GROUP
TOP_K = 2
D_EXPERT = D_MODEL // 2

kernel_name = 'hybrid_gmlp_mlstm_hmoe_dit'


def rmsnorm(x, g):
    xf = x.astype(jnp.float32)
    y = xf * lax.rsqrt(jnp.mean(xf * xf, axis=-1, keepdims=True) + EPS)
    return (y * g.astype(jnp.float32)).astype(x.dtype)


def layernorm(x, g):
    xf = x.astype(jnp.float32)
    mu = jnp.mean(xf, axis=-1, keepdims=True)
    xc = xf - mu
    y = xc * lax.rsqrt(jnp.mean(xc * xc, axis=-1, keepdims=True) + EPS)
    return (y * g.astype(jnp.float32)).astype(x.dtype)


def modulate(h, shift, scale):
    return h * (1 + scale) + shift


def chunk_mlp(h, n_chunks, w_in, b_in, g_v, w_s, b_s, w_out):
    bsz, t, _ = h.shape
    z = jax.nn.gelu(h @ w_in + b_in)
    u, v = jnp.split(z, 2, axis=-1)
    v = layernorm(v, g_v).reshape(bsz, n_chunks, A_CHUNK, A_GROUPS, A_GC)
    s = jnp.einsum('gts,bnsgc->bntgc', w_s, v) + b_s.T[:, :, None]
    y = u * s.reshape(bsz, t, A_HALF).astype(u.dtype)
    return y @ w_out


def conv_centred(x, w, b):
    k = w.shape[0]
    y = lax.conv_general_dilated(x, w[:, None, :].astype(x.dtype), window_strides=(1,),
                                 padding=[(k // 2, k // 2)], dimension_numbers=('NWC', 'WIO', 'NWC'),
                                 feature_group_count=x.shape[-1])
    return y + b


def mlstm_project(h, w_in, conv_w, conv_b, w_q, w_k, w_v, w_gate, b_gate):
    bsz, t, _ = h.shape
    f32 = jnp.float32
    xm, o_pre = jnp.split(h @ w_in, 2, axis=-1)
    xc = jax.nn.silu(conv_centred(xm, conv_w, conv_b))
    xc_h = xc.reshape(bsz, t, B_HEADS, B_DH)
    xm_h = xm.reshape(bsz, t, B_HEADS, B_DH)
    q = jnp.einsum('bthc,hcd->bhtd', xc_h, w_q).astype(f32) * (B_DQK ** -0.5)
    k = jnp.einsum('bthc,hcd->bhtd', xc_h, w_k).astype(f32)
    v = jnp.einsum('bthc,hce->bhte', xm_h, w_v).astype(f32)
    gates = (xm @ w_gate + b_gate).astype(f32).reshape(bsz, t, 4, B_HEADS)
    gates = jnp.transpose(gates, (2, 0, 3, 1))
    fwd = (q, k, v, gates[0], jax.nn.log_sigmoid(gates[1]))
    bwd = (q[:, :, ::-1], k[:, :, ::-1], v[:, :, ::-1],
           gates[2][..., ::-1], jax.nn.log_sigmoid(gates[3])[..., ::-1])
    return fwd, bwd, xc, o_pre


def mlstm_init_state(bsz):
    f32 = jnp.float32
    return (jnp.zeros((bsz, B_HEADS, B_DQK, B_DV), f32),
            jnp.zeros((bsz, B_HEADS, B_DQK), f32),
            jnp.zeros((bsz, B_HEADS), f32))


def mlstm_scan(q, k, v, log_i, log_f, state, return_h):
    bsz, nh, t, _ = q.shape
    nc = t // B_CHUNK

    def to_chunks(a):
        return jnp.moveaxis(a.reshape(bsz, nh, nc, B_CHUNK, *a.shape[3:]), 2, 0)

    xs = tuple(to_chunks(a) for a in (q, k, v, log_i, log_f))
    tri = jnp.tril(jnp.ones((B_CHUNK, B_CHUNK), dtype=bool))

    def step(carry, inp):
        c_mat, n_vec, m = carry
        qc, kc, vc, li, lf = inp
        b = jnp.cumsum(lf, axis=-1)
        b_last = b[..., -1]
        g = b_last[..., None] - b + li
        m_new = jnp.maximum(b_last + m, jnp.max(g, axis=-1))
        decay = jnp.exp(b_last + m - m_new)
        wk = jnp.exp(g - m_new[..., None])
        c_new = decay[..., None, None] * c_mat + jnp.einsum('bhs,bhsd,bhse->bhde', wk, kc, vc)
        n_new = decay[..., None] * n_vec + jnp.einsum('bhs,bhsd->bhd', wk, kc)
        if not return_h:
            return (c_new, n_new, m_new), None
        dmat = jnp.where(tri, b[..., :, None] - b[..., None, :] + li[..., None, :], NEG_INF)
        inter = b + m[..., None]
        m_t = jnp.maximum(inter, jnp.max(dmat, axis=-1))
        scores = jnp.einsum('bhtd,bhsd->bhts', qc, kc) * jnp.exp(dmat - m_t[..., None])
        a = jnp.exp(inter - m_t)
        num = a[..., None] * jnp.einsum('bhtd,bhde->bhte', qc, c_mat) + jnp.einsum('bhts,bhse->bhte', scores, vc)
        den = a * jnp.einsum('bhtd,bhd->bht', qc, n_vec) + jnp.sum(scores, axis=-1)
        h = num / jnp.maximum(jnp.abs(den), jnp.exp(-m_t))[..., None]
        return (c_new, n_new, m_new), h

    state, hs = lax.scan(step, state, xs)
    if not return_h:
        return state, None
    return state, jnp.moveaxis(hs, 0, 2).reshape(bsz, nh, t, B_DV)


def mlstm_combine(h_f, h_b, xc, o_pre, head_g, skip, w_out):
    f32 = jnp.float32
    bsz, _, t, _ = h_f.shape
    hs = jnp.transpose(h_f + h_b, (0, 2, 1, 3))
    hs = hs * lax.rsqrt(jnp.mean(hs * hs, axis=-1, keepdims=True) + EPS)
    hs = hs.reshape(bsz, t, B_INNER) * head_g.astype(f32)
    y = jax.nn.sigmoid(o_pre.astype(f32)) * (hs + skip.astype(f32) * xc.astype(f32))
    return y.astype(xc.dtype) @ w_out


def hier_moe(h, w_grp, b_grp, w_exp, b_exp, w1, w3, w2):
    f32 = jnp.float32
    grp_p = jax.nn.softmax((h @ w_grp + b_grp).astype(f32), axis=-1)
    p_g, g_idx = lax.top_k(grp_p, 1)
    exp_logits = (h @ w_exp + b_exp).astype(f32).reshape(-1, N_GROUPS, EXP_PER_GROUP)
    exp_logits = jnp.take_along_axis(exp_logits, g_idx[:, :, None], axis=1)[:, 0]
    top_p, top_i = lax.top_k(jax.nn.softmax(exp_logits, axis=-1), TOP_K)
    weights = p_g * top_p / jnp.sum(top_p, axis=-1, keepdims=True)
    expert_id = g_idx * EXP_PER_GROUP + top_i
    combine = jnp.einsum('nk,nke->ne', weights, jax.nn.one_hot(expert_id, N_EXPERTS, dtype=f32))
    out = jnp.zeros(h.shape, f32)
    for g in range(N_GROUPS):
        sl = slice(g * EXP_PER_GROUP, (g + 1) * EXP_PER_GROUP)
        a = jnp.einsum('nd,edh->neh', h, w1[sl])
        b = jnp.einsum('nd,edh->neh', h, w3[sl])
        act = jax.nn.silu(a) * b * combine[:, sl, None].astype(a.dtype)
        out = out + jnp.einsum('neh,ehd->nd', act, w2[sl]).astype(f32)
    return out.astype(h.dtype)


def setup_inputs(seed: int = 0) -> dict:
    key = jax.random.key(seed)
    ks = iter(jax.random.split(key, 48))
    f32 = jnp.float32
    n_a = (DEPTH + 1) // 2
    n_b = DEPTH // 2

    def nrm(shape, scale):
        return jax.random.normal(next(ks), shape, f32) * scale

    def gain(shape):
        return 1.0 + nrm(shape, 0.05)

    forget_bias = jnp.linspace(3.0, 6.0, B_HEADS, dtype=f32)
    b_gate = jnp.stack([nrm((n_b, B_HEADS), 0.01), forget_bias + nrm((n_b, B_HEADS), 0.01),
                        nrm((n_b, B_HEADS), 0.01), forget_bias + nrm((n_b, B_HEADS), 0.01)], axis=1)
    return {
        'x': nrm((BATCH, SEQ, D_MODEL), 1.0),
        'c': nrm((BATCH, D_MODEL), 1.0),
        'ctx': nrm((BATCH, CTX_LEN, D_MODEL), 1.0),
        'c_ctx': nrm((D_MODEL,), 1.0),
        'norm_g': gain((DEPTH, 2, D_MODEL)),
        'w_ada': nrm((DEPTH, D_MODEL, 6 * D_MODEL), 0.5 * D_MODEL ** -0.5),
        'b_ada': nrm((DEPTH, 6 * D_MODEL), 0.02),
        'a_w_in': nrm((n_a, D_MODEL, 2 * A_HALF), D_MODEL ** -0.5),
        'a_b_in': nrm((n_a, 2 * A_HALF), 0.02),
        'a_g_v': gain((n_a, A_HALF)),
        'a_w_s': nrm((n_a, A_GROUPS, A_CHUNK, A_CHUNK), A_CHUNK ** -0.5),
        'a_b_s': gain((n_a, A_GROUPS, A_CHUNK)),
        'a_w_out': nrm((n_a, A_HALF, D_MODEL), A_HALF ** -0.5),
        'b_w_in': nrm((n_b, D_MODEL, 2 * B_INNER), D_MODEL ** -0.5),
        'b_conv_w': nrm((n_b, B_CONV, B_INNER), B_CONV ** -0.5),
        'b_conv_b': nrm((n_b, B_INNER), 0.02),
        'b_w_q': nrm((n_b, B_HEADS, B_DH, B_DQK), B_DH ** -0.5),
        'b_w_k': nrm((n_b, B_HEADS, B_DH, B_DQK), B_DH ** -0.5),
        'b_w_v': nrm((n_b, B_HEADS, B_DH, B_DV), B_DH ** -0.5),
        'b_w_gate': nrm((n_b, B_INNER, 4 * B_HEADS), 0.1 * B_INNER ** -0.5),
        'b_b_gate': b_gate.reshape(n_b, 4 * B_HEADS),
        'b_head_g': gain((n_b, B_INNER)),
        'b_skip': gain((n_b, B_INNER)),
        'b_w_out': nrm((n_b, B_INNER, D_MODEL), B_INNER ** -0.5),
        'moe_w_grp': nrm((DEPTH, D_MODEL, N_GROUPS), D_MODEL ** -0.5),
        'moe_b_grp': nrm((DEPTH, N_GROUPS), 0.01),
        'moe_w_exp': nrm((DEPTH, D_MODEL, N_EXPERTS), D_MODEL ** -0.5),
        'moe_b_exp': nrm((DEPTH, N_EXPERTS), 0.01),
        'moe_w1': nrm((DEPTH, N_EXPERTS, D_MODEL, D_EXPERT), D_MODEL ** -0.5),
        'moe_w3': nrm((DEPTH, N_EXPERTS, D_MODEL, D_EXPERT), D_MODEL ** -0.5),
        'moe_w2': nrm((DEPTH, N_EXPERTS, D_EXPERT, D_MODEL), D_EXPERT ** -0.5),
        'final_g': gain((D_MODEL,)),
    }


def reference(x, c, ctx, c_ctx, norm_g, w_ada, b_ada, a_w_in, a_b_in, a_g_v, a_w_s, a_b_s, a_w_out,
              b_w_in, b_conv_w, b_conv_b, b_w_q, b_w_k, b_w_v, b_w_gate, b_b_gate, b_head_g, b_skip, b_w_out,
              moe_w_grp, moe_b_grp, moe_w_exp, moe_b_exp, moe_w1, moe_w3, moe_w2, final_g):
    bsz = x.shape[0]
    rows = x.shape[1] // GRID_W
    lat_chunks = rows // (A_CHUNK // GRID_W)
    ctx_chunks = ctx.shape[1] // A_CHUNK
    for i in range(DEPTH):
        last = i == DEPTH - 1
        j = i // N_MIXERS
        mod_x = (jax.nn.silu(c) @ w_ada[i] + b_ada[i]).reshape(bsz, 6, 1, D_MODEL)
        mod_c = (jax.nn.silu(c_ctx) @ w_ada[i] + b_ada[i]).reshape(6, D_MODEL)
        sh1, sc1, gt1, sh2, sc2, gt2 = (mod_x[:, n] for n in range(6))
        csh1, csc1, cgt1, csh2, csc2, cgt2 = (mod_c[n] for n in range(6))
        hx = modulate(rmsnorm(x, norm_g[i, 0]), sh1, sc1)
        if i % N_MIXERS == 0:
            pa = (a_w_in[j], a_b_in[j], a_g_v[j], a_w_s[j], a_b_s[j], a_w_out[j])
            x = x + gt1 * chunk_mlp(hx, lat_chunks, *pa)
            if not last:
                hc = modulate(rmsnorm(ctx, norm_g[i, 0]), csh1, csc1)
                ctx = ctx + cgt1 * chunk_mlp(hc, ctx_chunks, *pa)
        else:
            pb = (b_w_in[j], b_conv_w[j], b_conv_b[j], b_w_q[j], b_w_k[j], b_w_v[j], b_w_gate[j], b_b_gate[j])
            pc = (b_head_g[j], b_skip[j], b_w_out[j])
            hc = modulate(rmsnorm(ctx, norm_g[i, 0]), csh1, csc1)
            cf, cb, cxc, co = mlstm_project(hc, *pb)
            st0 = mlstm_init_state(bsz)
            st_f, hcf = mlstm_scan(*cf, st0, not last)
            st_b, hcb = mlstm_scan(*cb, st0, not last)
            xf, xb, xxc, xo = mlstm_project(hx, *pb)
            _, hxf = mlstm_scan(*xf, st_f, True)
            _, hxb = mlstm_scan(*xb, st_b, True)
            x = x + gt1 * mlstm_combine(hxf, hxb[:, :, ::-1], xxc, xo, *pc).astype(x.dtype)
            if not last:
                ctx = ctx + cgt1 * mlstm_combine(hcf, hcb[:, :, ::-1], cxc, co, *pc).astype(ctx.dtype)
        pm = (moe_w_grp[i], moe_b_grp[i], moe_w_exp[i], moe_b_exp[i], moe_w1[i], moe_w3[i], moe_w2[i])
        hx = modulate(rmsnorm(x, norm_g[i, 1]), sh2, sc2).reshape(-1, D_MODEL)
        if last:
            x = x + gt2 * hier_moe(hx, *pm).reshape(x.shape)
        else:
            hc = modulate(rmsnorm(ctx, norm_g[i, 1]), csh2, csc2).reshape(-1, D_MODEL)
            n_lat = hx.shape[0]
            y = hier_moe(jnp.concatenate([hx, hc], axis=0), *pm)
            x = x + gt2 * y[:n_lat].reshape(x.shape)
            ctx = ctx + cgt2 * y[n_lat:].reshape(ctx.shape)
    return rmsnorm(x, final_g)
```

```python
import functools

import jax
import jax.numpy as jnp
from jax import lax
from jax.experimental import pallas as pl
from jax.experimental.pallas import tpu as pltpu

F32 = jnp.float32
BF16 = jnp.bfloat16

D = 1024
BATCH = 8
SEQ = 2048
CTX = 256
EPS = 1e-6
NEG_INF = -1e30
N_LAT = BATCH * SEQ
N_CTX = BATCH * CTX
N_ALL = N_LAT + N_CTX

TM = 256
LAT_TILES = N_LAT // TM
ALL_TILES = N_ALL // TM
TILES_PER_SEQ = SEQ // TM
MOD_ROWS = 16

CHUNK = 128
A_HALF = 2048
A_GROUPS = 8
A_GC = A_HALF // A_GROUPS
B_INNER = 2048
HEADS = 8
DH = B_INNER // HEADS
DQK = DH // 2
DV = DH
CONV_K = 5
HALO = 16
N_EXP = 32
N_GRP = 4
EXP_PER_GRP = 8
D_EXP = 512
LANES = 128

VMEM_LIMIT = 56 * 1024 * 1024


def _cparams(sem):
    return pltpu.CompilerParams(dimension_semantics=sem, vmem_limit_bytes=VMEM_LIMIT)


def _mod_row(i):
    return jnp.where(i < LAT_TILES, i // TILES_PER_SEQ, BATCH)


def _norm_mod(x, g, shift, scale):
    y = x * lax.rsqrt(jnp.mean(x * x, axis=-1, keepdims=True) + EPS) * g
    return y * (1.0 + scale) + shift


def _gelu_tanh(x):
    return 0.5 * x * (1.0 + jnp.tanh(0.7978845608028654 * (x + 0.044715 * (x * x * x))))


ADA_BN = 1536


def _ada_kernel(c_ref, w_ref, b_ref, o_ref):
    c = c_ref[...]
    a = c * jax.nn.sigmoid(c)
    o_ref[0] = jnp.dot(a, w_ref[0], preferred_element_type=F32,
                       precision=lax.Precision.HIGHEST) + b_ref[0]


def _ada(cc, w_ada, b_ada):
    depth = w_ada.shape[0]
    return pl.pallas_call(
        _ada_kernel,
        grid=(depth, 6 * D // ADA_BN),
        in_specs=[
            pl.BlockSpec((MOD_ROWS, D), lambda l, j: (0, 0)),
            pl.BlockSpec((1, D, ADA_BN), lambda l, j: (l, 0, j)),
            pl.BlockSpec((1, 1, ADA_BN), lambda l, j: (l, 0, j)),
        ],
        out_specs=pl.BlockSpec((1, MOD_ROWS, ADA_BN), lambda l, j: (l, 0, j)),
        out_shape=jax.ShapeDtypeStruct((depth, MOD_ROWS, 6 * D), F32),
        compiler_params=_cparams(("parallel", "parallel")),
        name="ada",
    )(cc, w_ada, b_ada.reshape(depth, 1, 6 * D))


GM_CH = 512


def _gmlp_kernel(x_ref, mod_ref, g_ref, win_ref, bin_ref, gv_ref, ws_ref, bs_ref, wout_ref,
                 o_ref, z_scr, y_scr):
    x = x_ref[...]
    h = _norm_mod(x, g_ref[...], mod_ref[0, 0:1, :], mod_ref[0, 1:2, :])
    hb = h.astype(BF16)
    s1 = jnp.zeros((TM, 1), F32)
    s2 = jnp.zeros((TM, 1), F32)
    for j in range(2 * A_HALF // GM_CH):
        cs = slice(j * GM_CH, (j + 1) * GM_CH)
        zc = jnp.dot(hb, win_ref[:, cs], preferred_element_type=F32) + bin_ref[:, cs]
        zc = _gelu_tanh(zc)
        z_scr[:, cs] = zc
        if j * GM_CH >= A_HALF:
            s1 = s1 + jnp.sum(zc, axis=-1, keepdims=True)
            s2 = s2 + jnp.sum(zc * zc, axis=-1, keepdims=True)
    mu = s1 * (1.0 / A_HALF)
    rstd = lax.rsqrt(s2 * (1.0 / A_HALF) - mu * mu + EPS)
    for c in range(TM // CHUNK):
        rs = slice(c * CHUNK, (c + 1) * CHUNK)
        for g in range(A_GROUPS):
            cs = slice(g * A_GC, (g + 1) * A_GC)
            vs = slice(A_HALF + g * A_GC, A_HALF + (g + 1) * A_GC)
            v = (z_scr[rs, vs] - mu[rs]) * rstd[rs] * gv_ref[:, cs]
            s = jnp.dot(ws_ref[g], v.astype(BF16), preferred_element_type=F32) + bs_ref[:, cs]
            y_scr[rs, cs] = (z_scr[rs, cs] * s).astype(BF16)
    out = jnp.dot(y_scr[...], wout_ref[...], preferred_element_type=F32)
    o_ref[...] = x + mod_ref[0, 2:3, :] * out


def _gmlp(xa, mod, g, w_in, b_in, g_v, w_s, b_s_full, w_out):
    const = lambda i: (0, 0)
    return pl.pallas_call(
        _gmlp_kernel,
        grid=(ALL_TILES,),
        in_specs=[
            pl.BlockSpec((TM, D), lambda i: (i, 0)),
            pl.BlockSpec((1, 6, D), lambda i: (_mod_row(i), 0, 0)),
            pl.BlockSpec((1, D), const),
            pl.BlockSpec((D, 2 * A_HALF), const),
            pl.BlockSpec((1, 2 * A_HALF), const),
            pl.BlockSpec((1, A_HALF), const),
            pl.BlockSpec((A_GROUPS, CHUNK, CHUNK), lambda i: (0, 0, 0)),
            pl.BlockSpec((CHUNK, A_HALF), const),
            pl.BlockSpec((A_HALF, D), const),
        ],
        out_specs=pl.BlockSpec((TM, D), lambda i: (i, 0)),
        out_shape=jax.ShapeDtypeStruct((N_ALL, D), F32),
        scratch_shapes=[pltpu.VMEM((TM, 2 * A_HALF), F32), pltpu.VMEM((TM, A_HALF), BF16)],
        compiler_params=_cparams(("parallel",)),
        name="gmlp",
    )(xa, mod, g, w_in, b_in, g_v, w_s, b_s_full, w_out)


def _router_kernel(x_ref, mod_ref, g_ref, wr_ref, br_ref, h_ref, comb_ref):
    x = x_ref[...]
    h = _norm_mod(x, g_ref[...], mod_ref[0, 3:4, :], mod_ref[0, 4:5, :])
    h_ref[...] = h.astype(BF16)
    lt = lax.dot_general(wr_ref[...], h, (((1,), (1,)), ((), ())),
                         preferred_element_type=F32, precision=lax.Precision.HIGHEST) + br_ref[...]
    e_t = lt[0:N_EXP]
    row8 = lax.broadcasted_iota(jnp.int32, (EXP_PER_GRP, TM), 0).astype(F32)
    g_t = jnp.where(row8 < N_GRP, lt[N_EXP:N_EXP + EXP_PER_GRP], -jnp.inf)
    gmax = jnp.max(g_t, axis=0, keepdims=True)
    p_g = 1.0 / jnp.sum(jnp.exp(g_t - gmax), axis=0, keepdims=True)
    g_idx = jnp.min(jnp.where(g_t == gmax, row8, float(EXP_PER_GRP)), axis=0, keepdims=True)
    sel = jnp.zeros((EXP_PER_GRP, TM), F32)
    for g in range(N_GRP):
        sel = sel + jnp.where(g_idx == g, e_t[g * EXP_PER_GRP:(g + 1) * EXP_PER_GRP], 0.0)
    m1 = jnp.max(sel, axis=0, keepdims=True)
    i1 = jnp.min(jnp.where(sel == m1, row8, float(EXP_PER_GRP)), axis=0, keepdims=True)
    sel2 = jnp.where(row8 == i1, -jnp.inf, sel)
    m2 = jnp.max(sel2, axis=0, keepdims=True)
    i2 = jnp.min(jnp.where(sel2 == m2, row8, float(EXP_PER_GRP)), axis=0, keepdims=True)
    e2 = jnp.exp(m2 - m1)
    w1 = p_g / (1.0 + e2)
    w2 = p_g * e2 / (1.0 + e2)
    row = lax.broadcasted_iota(jnp.int32, (LANES, TM), 0).astype(F32)
    id1 = g_idx * EXP_PER_GRP + i1
    id2 = g_idx * EXP_PER_GRP + i2
    comb_t = jnp.where(row == id1, w1, 0.0) + jnp.where(row == id2, w2, 0.0)
    comb_ref[...] = comb_t.T


def _router(xa, mod, g, wr_t, br_t, n_tiles):
    const = lambda i: (0, 0)
    n = n_tiles * TM
    return pl.pallas_call(
        _router_kernel,
        grid=(n_tiles,),
        in_specs=[
            pl.BlockSpec((TM, D), lambda i: (i, 0)),
            pl.BlockSpec((1, 6, D), lambda i: (_mod_row(i), 0, 0)),
            pl.BlockSpec((1, D), const),
            pl.BlockSpec((LANES, D), const),
            pl.BlockSpec((LANES, 1), const),
        ],
        out_specs=[pl.BlockSpec((TM, D), lambda i: (i, 0)),
                   pl.BlockSpec((TM, LANES), lambda i: (i, 0))],
        out_shape=[jax.ShapeDtypeStruct((n, D), BF16), jax.ShapeDtypeStruct((n, LANES), F32)],
        compiler_params=_cparams(("parallel",)),
        name="router",
    )(xa, mod, g, wr_t, br_t)


MOE_TM = 1024
MOE_SUB = MOE_TM // TM


def _moe_dense_kernel(x_ref, mod_ref, h_ref, comb_ref, w1_ref, w3_ref, w2_ref, fg_ref, o_ref, acc,
                      *, final_norm):
    e = pl.program_id(1)

    @pl.when(e == 0)
    def _():
        acc[...] = jnp.zeros_like(acc)

    h = h_ref[...]
    a = jnp.dot(h, w1_ref[0], preferred_element_type=F32)
    b = jnp.dot(h, w3_ref[0], preferred_element_type=F32)
    lane = lax.broadcasted_iota(jnp.int32, (MOE_TM, LANES), 1)
    cw = jnp.sum(jnp.where(lane == e, comb_ref[...], 0.0), axis=-1, keepdims=True)
    act = a * jax.nn.sigmoid(a) * b * cw
    acc[...] += jnp.dot(act.astype(BF16), w2_ref[0], preferred_element_type=F32)

    @pl.when(e == N_EXP - 1)
    def _():
        for s in range(MOE_SUB):
            rs = slice(s * TM, (s + 1) * TM)
            y = x_ref[rs, :] + mod_ref[s, 5:6, :] * acc[rs, :]
            if final_norm:
                y = y * lax.rsqrt(jnp.mean(y * y, axis=-1, keepdims=True) + EPS) * fg_ref[...]
            o_ref[rs, :] = y


def _moe_dense(xa, modx, hb, comb, w1, w3, w2, final_g, n_rows, final_norm):
    n_tiles = n_rows // MOE_TM
    return pl.pallas_call(
        functools.partial(_moe_dense_kernel, final_norm=final_norm),
        grid=(n_tiles, N_EXP),
        in_specs=[
            pl.BlockSpec((MOE_TM, D), lambda i, e: (i, 0)),
            pl.BlockSpec((MOE_SUB, 6, D), lambda i, e: (i, 0, 0)),
            pl.BlockSpec((MOE_TM, D), lambda i, e: (i, 0)),
            pl.BlockSpec((MOE_TM, LANES), lambda i, e: (i, 0)),
            pl.BlockSpec((1, D, D_EXP), lambda i, e: (e, 0, 0)),
            pl.BlockSpec((1, D, D_EXP), lambda i, e: (e, 0, 0)),
            pl.BlockSpec((1, D_EXP, D), lambda i, e: (e, 0, 0)),
            pl.BlockSpec((1, D), lambda i, e: (0, 0)),
        ],
        out_specs=pl.BlockSpec((MOE_TM, D), lambda i, e: (i, 0)),
        out_shape=jax.ShapeDtypeStruct((n_rows, D), F32),
        scratch_shapes=[pltpu.VMEM((MOE_TM, D), F32)],
        compiler_params=_cparams(("parallel", "arbitrary")),
        name="moe_dense",
    )(xa, modx, hb, comb, w1, w3, w2, final_g)


def _inproj_kernel(x_ref, mod_ref, g_ref, win_ref, xm_ref, op_ref):
    h = _norm_mod(x_ref[...], g_ref[...], mod_ref[0, 0:1, :], mod_ref[0, 1:2, :])
    hb = h.astype(BF16)
    xm_ref[...] = jnp.dot(hb, win_ref[:, :B_INNER], preferred_element_type=F32).astype(BF16)
    op_ref[...] = jnp.dot(hb, win_ref[:, B_INNER:], preferred_element_type=F32).astype(BF16)


def _inproj(xa, mod, g, w_in):
    const = lambda i: (0, 0)
    return pl.pallas_call(
        _inproj_kernel,
        grid=(ALL_TILES,),
        in_specs=[
            pl.BlockSpec((TM, D), lambda i: (i, 0)),
            pl.BlockSpec((1, 6, D), lambda i: (_mod_row(i), 0, 0)),
            pl.BlockSpec((1, D), const),
            pl.BlockSpec((D, 2 * B_INNER), const),
        ],
        out_specs=[pl.BlockSpec((TM, B_INNER), lambda i: (i, 0)),
                   pl.BlockSpec((TM, B_INNER), lambda i: (i, 0))],
        out_shape=[jax.ShapeDtypeStruct((N_ALL, B_INNER), BF16),
                   jax.ShapeDtypeStruct((N_ALL, B_INNER), BF16)],
        compiler_params=_cparams(("parallel",)),
        name="mlstm_inproj",
    )(xa, mod, g, w_in)


def _log_sigmoid(x):
    return jnp.minimum(x, 0.0) - jnp.log(1.0 + jnp.exp(-jnp.abs(x)))


def _qkv_kernel(xm_ref, prev_ref, next_ref, cw_ref, cb_ref, wqk_ref, wv_ref, wg_ref, wgt_ref,
                bg_ref, bgt_ref, xc_ref, q_ref, k_ref, v_ref, gt_ref, gtt_ref):
    i = pl.program_id(0)
    lat = i < LAT_TILES
    first = jnp.where(lat, i % TILES_PER_SEQ == 0, True)
    last = jnp.where(lat, i % TILES_PER_SEQ == TILES_PER_SEQ - 1, True)
    xmb = xm_ref[...]
    xm = xmb.astype(F32)
    prev = jnp.where(first, 0.0, prev_ref[...].astype(F32))
    nxt = jnp.where(last, 0.0, next_ref[...].astype(F32))
    ext = jnp.concatenate([prev[HALO - 8:], xm, nxt[:8]], axis=0)
    acc = jnp.zeros((TM, B_INNER), F32) + cb_ref[...]
    for t in range(CONV_K):
        off = 8 + t - CONV_K // 2
        acc = acc + ext[off:off + TM] * cw_ref[t:t + 1, :]
    xc = acc * jax.nn.sigmoid(acc)
    xcb = xc.astype(BF16)
    xc_ref[...] = xcb
    for h in range(HEADS):
        hs = slice(h * DH, (h + 1) * DH)
        qk = jnp.dot(xcb[:, hs], wqk_ref[h], preferred_element_type=F32)
        q_ref[:, h * DQK:(h + 1) * DQK] = (qk[:, :DQK] * (DQK ** -0.5)).astype(BF16)
        k_ref[:, h * DQK:(h + 1) * DQK] = qk[:, DQK:].astype(BF16)
        v_ref[:, hs] = jnp.dot(xmb[:, hs], wv_ref[h], preferred_element_type=F32).astype(BF16)
    gts = jnp.dot(xmb, wg_ref[...], preferred_element_type=F32) + bg_ref[...]
    lane = lax.broadcasted_iota(jnp.int32, (TM, LANES), 1)
    gt_ref[...] = jnp.where(((lane >> 3) & 1) == 1, _log_sigmoid(gts), gts)
    gtt = lax.dot_general(wgt_ref[...], xmb, (((1,), (1,)), ((), ())),
                          preferred_element_type=F32) + bgt_ref[...]
    row = lax.broadcasted_iota(jnp.int32, (4 * HEADS, TM), 0)
    gtt_ref[...] = jnp.where(((row >> 3) & 1) == 1, _log_sigmoid(gtt), gtt)


def _qkv(xm, conv_w, conv_b, wqk, wv, wg, wgt, bg, bgt):
    const = lambda i: (0, 0)
    const3 = lambda i: (0, 0, 0)
    hb = TM // HALO
    n_hb = N_ALL // HALO
    return pl.pallas_call(
        _qkv_kernel,
        grid=(ALL_TILES,),
        in_specs=[
            pl.BlockSpec((TM, B_INNER), lambda i: (i, 0)),
            pl.BlockSpec((HALO, B_INNER), lambda i: (jnp.maximum(i * hb - 1, 0), 0)),
            pl.BlockSpec((HALO, B_INNER), lambda i: (jnp.minimum((i + 1) * hb, n_hb - 1), 0)),
            pl.BlockSpec((CONV_K, B_INNER), const),
            pl.BlockSpec((1, B_INNER), const),
            pl.BlockSpec((HEADS, DH, 2 * DQK), const3),
            pl.BlockSpec((HEADS, DH, DV), const3),
            pl.BlockSpec((B_INNER, LANES), const),
            pl.BlockSpec((4 * HEADS, B_INNER), const),
            pl.BlockSpec((1, LANES), const),
            pl.BlockSpec((4 * HEADS, 1), const),
        ],
        out_specs=[
            pl.BlockSpec((TM, B_INNER), lambda i: (i, 0)),
            pl.BlockSpec((TM, HEADS * DQK), lambda i: (i, 0)),
            pl.BlockSpec((TM, HEADS * DQK), lambda i: (i, 0)),
            pl.BlockSpec((TM, B_INNER), lambda i: (i, 0)),
            pl.BlockSpec((TM, LANES), lambda i: (i, 0)),
            pl.BlockSpec((4 * HEADS, TM), lambda i: (0, i)),
        ],
        out_shape=[
            jax.ShapeDtypeStruct((N_ALL, B_INNER), BF16),
            jax.ShapeDtypeStruct((N_ALL, HEADS * DQK), BF16),
            jax.ShapeDtypeStruct((N_ALL, HEADS * DQK), BF16),
            jax.ShapeDtypeStruct((N_ALL, B_INNER), BF16),
            jax.ShapeDtypeStruct((N_ALL, LANES), F32),
            jax.ShapeDtypeStruct((4 * HEADS, N_ALL), F32),
        ],
        compiler_params=_cparams(("parallel",)),
        name="mlstm_qkv",
    )(xm, xm, xm, conv_w, conv_b, wqk, wv, wg, wgt, bg, bgt)


CTX_CHUNKS = CTX // CHUNK
LAT_CHUNKS = SEQ // CHUNK
SCAN_STEPS = CTX_CHUNKS + LAT_CHUNKS


def _scan_dir(d, need_h, q_ref, k_ref, v_ref, g_ref, gt_ref, o_ref, c_scr, n_scr, m_scr):
    r = lax.broadcasted_iota(jnp.int32, (CHUNK, CHUNK), 0)
    c = lax.broadcasted_iota(jnp.int32, (CHUNK, CHUNK), 1)
    if d == 0:
        keep = c <= r
    else:
        keep = c >= r
    tri_col = keep.astype(F32)
    tri_row = (r <= c).astype(F32) if d == 0 else (r >= c).astype(F32)
    gts = g_ref[...]
    gtt = gt_ref[...]
    hi = lax.Precision.HIGHEST
    bcol_all = jnp.dot(tri_col, gts, preferred_element_type=F32, precision=hi)
    brow_all = jnp.dot(gtt, tri_row, preferred_element_type=F32, precision=hi)
    tot_all = jnp.sum(gtt, axis=-1, keepdims=True)
    base = 2 * HEADS * d
    for h in range(HEADS):
        li_c = gts[:, base + h:base + h + 1]
        li_r = gtt[base + h:base + h + 1, :]
        b_c = bcol_all[:, base + HEADS + h:base + HEADS + h + 1]
        b_r = brow_all[base + HEADS + h:base + HEADS + h + 1, :]
        b_last = tot_all[base + HEADS + h:base + HEADS + h + 1, :]
        m_old = m_scr[d, h]
        c_old = c_scr[d, h]
        n_old = n_scr[d, h]
        qh = q_ref[:, h * DQK:(h + 1) * DQK]
        kh = k_ref[:, h * DQK:(h + 1) * DQK]
        vh = v_ref[:, h * DV:(h + 1) * DV]
        g_r = b_last - b_r + li_r
        g_c = b_last - b_c + li_c
        m_new = jnp.maximum(b_last + m_old, jnp.max(g_r, axis=-1, keepdims=True))
        decay = jnp.exp(b_last + m_old - m_new)
        wk = jnp.exp(g_c - m_new)
        kw = kh.astype(F32) * wk
        c_scr[d, h] = decay * c_old + lax.dot_general(
            kw.astype(BF16), vh, (((0,), (0,)), ((), ())), preferred_element_type=F32)
        n_scr[d, h] = decay * n_old + jnp.sum(kw, axis=0, keepdims=True)
        m_scr[d, h] = m_new

        @pl.when(need_h)
        def _():
            dmat = jnp.where(keep, b_c - b_r + li_r, NEG_INF)
            inter = b_c + m_old
            m_t = jnp.maximum(inter, jnp.max(dmat, axis=-1, keepdims=True))
            p = jnp.exp(dmat - m_t)
            s = lax.dot_general(qh, kh, (((1,), (1,)), ((), ())), preferred_element_type=F32) * p
            a = jnp.exp(inter - m_t)
            num = a * jnp.dot(qh, c_old.astype(BF16), preferred_element_type=F32) + jnp.dot(
                s.astype(BF16), vh, preferred_element_type=F32)
            den = a * jnp.sum(qh.astype(F32) * n_old, axis=-1, keepdims=True) + jnp.sum(
                s, axis=-1, keepdims=True)
            hh = num / jnp.maximum(jnp.abs(den), jnp.exp(-m_t))
            o_ref[:, h * DV:(h + 1) * DV] = hh.astype(BF16)


def _scan_kernel(qf, kf, vf, gf, gtf, qb, kb, vb, gb, gtb, of, ob, c_scr, n_scr, m_scr):
    j = pl.program_id(1)

    @pl.when(j == 0)
    def _():
        c_scr[...] = jnp.zeros_like(c_scr)
        n_scr[...] = jnp.zeros_like(n_scr)
        m_scr[...] = jnp.zeros_like(m_scr)

    need_h = j >= CTX_CHUNKS
    _scan_dir(0, need_h, qf, kf, vf, gf, gtf, of, c_scr, n_scr, m_scr)
    _scan_dir(1, need_h, qb, kb, vb, gb, gtb, ob, c_scr, n_scr, m_scr)


def _scan(q, k, v, gts, gtt):
    lat_blk = N_LAT // CHUNK

    def fwd_in(b, j):
        return jnp.where(j < CTX_CHUNKS, lat_blk + b * CTX_CHUNKS + j, b * LAT_CHUNKS + j - CTX_CHUNKS)

    def bwd_in(b, j):
        return jnp.where(j < CTX_CHUNKS, lat_blk + b * CTX_CHUNKS + (CTX_CHUNKS - 1 - j),
                         b * LAT_CHUNKS + (SCAN_STEPS - 1 - j))

    def fwd_out(b, j):
        return b * LAT_CHUNKS + jnp.maximum(j - CTX_CHUNKS, 0)

    def bwd_out(b, j):
        return b * LAT_CHUNKS + jnp.minimum(SCAN_STEPS - 1 - j, LAT_CHUNKS - 1)

    def specs(fn):
        return [
            pl.BlockSpec((CHUNK, HEADS * DQK), lambda b, j: (fn(b, j), 0)),
            pl.BlockSpec((CHUNK, HEADS * DQK), lambda b, j: (fn(b, j), 0)),
            pl.BlockSpec((CHUNK, B_INNER), lambda b, j: (fn(b, j), 0)),
            pl.BlockSpec((CHUNK, LANES), lambda b, j: (fn(b, j), 0)),
            pl.BlockSpec((4 * HEADS, CHUNK), lambda b, j: (0, fn(b, j))),
        ]

    return pl.pallas_call(
        _scan_kernel,
        grid=(BATCH, SCAN_STEPS),
        in_specs=specs(fwd_in) + specs(bwd_in),
        out_specs=[pl.BlockSpec((CHUNK, B_INNER), lambda b, j: (fwd_out(b, j), 0)),
                   pl.BlockSpec((CHUNK, B_INNER), lambda b, j: (bwd_out(b, j), 0))],
        out_shape=[jax.ShapeDtypeStruct((N_LAT, B_INNER), BF16),
                   jax.ShapeDtypeStruct((N_LAT, B_INNER), BF16)],
        scratch_shapes=[pltpu.VMEM((2, HEADS, DQK, DV), F32),
                        pltpu.VMEM((2, HEADS, 1, DQK), F32),
                        pltpu.VMEM((2, HEADS, 1, 1), F32)],
        compiler_params=_cparams(("parallel", "arbitrary")),
        name="mlstm_scan",
    )(q, k, v, gts, gtt, q, k, v, gts, gtt)


def _combine_kernel(x_ref, mod_ref, hf_ref, hb_ref, xc_ref, op_ref, hg_ref, sk_ref, wout_ref,
                    o_ref, y_scr):
    for h in range(HEADS):
        hs = slice(h * DV, (h + 1) * DV)
        s = hf_ref[:, hs].astype(F32) + hb_ref[:, hs].astype(F32)
        s = s * lax.rsqrt(jnp.mean(s * s, axis=-1, keepdims=True) + EPS)
        y = jax.nn.sigmoid(op_ref[:, hs].astype(F32)) * (
            s * hg_ref[:, hs] + sk_ref[:, hs] * xc_ref[:, hs].astype(F32))
        y_scr[:, hs] = y.astype(BF16)
    out = jnp.dot(y_scr[...], wout_ref[...], preferred_element_type=F32)
    o_ref[...] = x_ref[...] + mod_ref[0, 2:3, :] * out


def _combine(xa, mod, hf, hb, xc, op, head_g, skip, w_out):
    const = lambda i: (0, 0)
    row = lambda i: (i, 0)
    return pl.pallas_call(
        _combine_kernel,
        grid=(LAT_TILES,),
        in_specs=[
            pl.BlockSpec((TM, D), row),
            pl.BlockSpec((1, 6, D), lambda i: (_mod_row(i), 0, 0)),
            pl.BlockSpec((TM, B_INNER), row),
            pl.BlockSpec((TM, B_INNER), row),
            pl.BlockSpec((TM, B_INNER), row),
            pl.BlockSpec((TM, B_INNER), row),
            pl.BlockSpec((1, B_INNER), const),
            pl.BlockSpec((1, B_INNER), const),
            pl.BlockSpec((B_INNER, D), const),
        ],
        out_specs=pl.BlockSpec((TM, D), row),
        out_shape=jax.ShapeDtypeStruct((N_LAT, D), F32),
        scratch_shapes=[pltpu.VMEM((TM, B_INNER), BF16)],
        compiler_params=_cparams(("parallel",)),
        name="mlstm_combine",
    )(xa, mod, hf, hb, xc, op, head_g, skip, w_out)


def _router_weights(w_grp, b_grp, w_exp, b_exp):
    wr = jnp.zeros((LANES, D), F32).at[:N_EXP].set(w_exp.T).at[N_EXP:N_EXP + N_GRP].set(w_grp.T)
    br = jnp.zeros((LANES, 1), F32).at[:N_EXP, 0].set(b_exp).at[N_EXP:N_EXP + N_GRP, 0].set(b_grp)
    return wr, br


def _mod_per_tile(mod, n_tiles):
    idx = jnp.where(jnp.arange(n_tiles) < LAT_TILES, jnp.arange(n_tiles) // TILES_PER_SEQ, BATCH)
    return mod[idx]


def _moe(xa, mod, g2, w_grp, b_grp, w_exp, b_exp, w1, w3, w2, final_g, n_rows, final_norm):
    wr, br = _router_weights(w_grp, b_grp, w_exp, b_exp)
    hb, comb = _router(xa, mod, g2, wr, br, n_rows // TM)
    modx = _mod_per_tile(mod, n_rows // TM)
    return _moe_dense(xa, modx, hb, comb, w1.astype(BF16), w3.astype(BF16), w2.astype(BF16),
                      final_g.reshape(1, D), n_rows, final_norm)


def kernel(x, c, ctx, c_ctx, norm_g, w_ada, b_ada, a_w_in, a_b_in, a_g_v, a_w_s, a_b_s, a_w_out,
           b_w_in, b_conv_w, b_conv_b, b_w_q, b_w_k, b_w_v, b_w_gate, b_b_gate, b_head_g, b_skip,
           b_w_out, moe_w_grp, moe_b_grp, moe_w_exp, moe_b_exp, moe_w1, moe_w3, moe_w2, final_g):
    xa = jnp.concatenate([x.reshape(N_LAT, D), ctx.reshape(N_CTX, D)], axis=0)
    cc = jnp.zeros((MOD_ROWS, D), F32).at[:BATCH].set(c).at[BATCH].set(c_ctx)
    mods = _ada(cc, w_ada, b_ada).reshape(2, MOD_ROWS, 6, D)

    mod = mods[0]
    b_s_full = jnp.repeat(a_b_s[0].T, A_GC, axis=1)
    xa = _gmlp(xa, mod, norm_g[0, 0].reshape(1, D), a_w_in[0].astype(BF16),
               a_b_in[0].reshape(1, -1), a_g_v[0].reshape(1, -1), a_w_s[0].astype(BF16),
               b_s_full, a_w_out[0].astype(BF16))
    xa = _moe(xa, mod, norm_g[0, 1].reshape(1, D), moe_w_grp[0], moe_b_grp[0], moe_w_exp[0],
              moe_b_exp[0], moe_w1[0], moe_w3[0], moe_w2[0], final_g, N_ALL, False)

    mod = mods[1]
    xm, op = _inproj(xa, mod, norm_g[1, 0].reshape(1, D), b_w_in[0].astype(BF16))
    wqk = jnp.concatenate([b_w_q[0], b_w_k[0]], axis=-1).astype(BF16)
    wg = jnp.zeros((B_INNER, LANES), F32).at[:, :4 * HEADS].set(b_w_gate[0]).astype(BF16)
    bg = jnp.zeros((1, LANES), F32).at[0, :4 * HEADS].set(b_b_gate[0])
    xc, q, k, v, gts, gtt = _qkv(xm, b_conv_w[0], b_conv_b[0].reshape(1, -1), wqk,
                                 b_w_v[0].astype(BF16), wg, b_w_gate[0].T.astype(BF16), bg,
                                 b_b_gate[0].reshape(-1, 1))
    hf, hb = _scan(q, k, v, gts, gtt)
    xl = _combine(xa, mod, hf, hb, xc, op, b_head_g[0].reshape(1, -1), b_skip[0].reshape(1, -1),
                  b_w_out[0].astype(BF16))
    out = _moe(xl, mod, norm_g[1, 1].reshape(1, D), moe_w_grp[1], moe_b_grp[1], moe_w_exp[1],
               moe_b_exp[1], moe_w1[1], moe_w3[1], moe_w2[1], final_g, N_LAT, True)
    return out.reshape(BATCH, SEQ, D)
```

```python
import functools

import jax
import jax.numpy as jnp
from jax import lax
from jax.experimental import pallas as pl
from jax.experimental.pallas import tpu as pltpu

F32 = jnp.float32
BF16 = jnp.bfloat16

D = 1024
BATCH = 8
SEQ = 2048
CTX = 256
EPS = 1e-6
NEG_INF = -1e30
N_LAT = BATCH * SEQ
N_CTX = BATCH * CTX
N_ALL = N_LAT + N_CTX

TM = 256
LAT_TILES = N_LAT // TM
ALL_TILES = N_ALL // TM
TILES_PER_SEQ = SEQ // TM
MOD_ROWS = 16

CHUNK = 128
A_HALF = 2048
A_GROUPS = 8
A_GC = A_HALF // A_GROUPS
B_INNER = 2048
HEADS = 8
DH = B_INNER // HEADS
DQK = DH // 2
DV = DH
CONV_K = 5
HALO = 16
N_EXP = 32
N_GRP = 4
EXP_PER_GRP = 8
D_EXP = 512
LANES = 128

VMEM_LIMIT = 56 * 1024 * 1024


def _cparams(sem):
    return pltpu.CompilerParams(dimension_semantics=sem, vmem_limit_bytes=VMEM_LIMIT)


def _mod_row(i):
    return jnp.where(i < LAT_TILES, i // TILES_PER_SEQ, BATCH)


def _norm_mod(x, g, shift, scale):
    y = x * lax.rsqrt(jnp.mean(x * x, axis=-1, keepdims=True) + EPS) * g
    return y * (1.0 + scale) + shift


def _gelu_tanh(x):
    return 0.5 * x * (1.0 + jnp.tanh(0.7978845608028654 * (x + 0.044715 * (x * x * x))))


ADA_BN = 1536


def _ada_kernel(c_ref, w_ref, b_ref, o_ref):
    c = c_ref[...]
    a = c * jax.nn.sigmoid(c)
    o_ref[0] = jnp.dot(a, w_ref[0], preferred_element_type=F32,
                       precision=lax.Precision.HIGHEST) + b_ref[0]


def _ada(cc, w_ada, b_ada):
    depth = w_ada.shape[0]
    return pl.pallas_call(
        _ada_kernel,
        grid=(depth, 6 * D // ADA_BN),
        in_specs=[
            pl.BlockSpec((MOD_ROWS, D), lambda l, j: (0, 0)),
            pl.BlockSpec((1, D, ADA_BN), lambda l, j: (l, 0, j)),
            pl.BlockSpec((1, 1, ADA_BN), lambda l, j: (l, 0, j)),
        ],
        out_specs=pl.BlockSpec((1, MOD_ROWS, ADA_BN), lambda l, j: (l, 0, j)),
        out_shape=jax.ShapeDtypeStruct((depth, MOD_ROWS, 6 * D), F32),
        compiler_params=_cparams(("parallel", "parallel")),
        name="ada",
    )(cc, w_ada, b_ada.reshape(depth, 1, 6 * D))


GM_CH = 512


def _gmlp_kernel(x_ref, mod_ref, g_ref, win_ref, bin_ref, gv_ref, ws_ref, bs_ref, wout_ref,
                 o_ref, z_scr, y_scr):
    x = x_ref[...]
    h = _norm_mod(x, g_ref[...], mod_ref[0, 0:1, :], mod_ref[0, 1:2, :])
    hb = h.astype(BF16)
    s1 = jnp.zeros((TM, 1), F32)
    s2 = jnp.zeros((TM, 1), F32)
    for j in range(2 * A_HALF // GM_CH):
        cs = slice(j * GM_CH, (j + 1) * GM_CH)
        zc = jnp.dot(hb, win_ref[:, cs], preferred_element_type=F32) + bin_ref[:, cs]
        zc = _gelu_tanh(zc)
        z_scr[:, cs] = zc
        if j * GM_CH >= A_HALF:
            s1 = s1 + jnp.sum(zc, axis=-1, keepdims=True)
            s2 = s2 + jnp.sum(zc * zc, axis=-1, keepdims=True)
    mu = s1 * (1.0 / A_HALF)
    rstd = lax.rsqrt(s2 * (1.0 / A_HALF) - mu * mu + EPS)
    for c in range(TM // CHUNK):
        rs = slice(c * CHUNK, (c + 1) * CHUNK)
        for g in range(A_GROUPS):
            cs = slice(g * A_GC, (g + 1) * A_GC)
            vs = slice(A_HALF + g * A_GC, A_HALF + (g + 1) * A_GC)
            v = (z_scr[rs, vs] - mu[rs]) * rstd[rs] * gv_ref[:, cs]
            s = jnp.dot(ws_ref[g], v.astype(BF16), preferred_element_type=F32) + bs_ref[:, cs]
            y_scr[rs, cs] = (z_scr[rs, cs] * s).astype(BF16)
    out = jnp.dot(y_scr[...], wout_ref[...], preferred_element_type=F32)
    o_ref[...] = x + mod_ref[0, 2:3, :] * out


def _gmlp(xa, mod, g, w_in, b_in, g_v, w_s, b_s_full, w_out):
    const = lambda i: (0, 0)
    return pl.pallas_call(
        _gmlp_kernel,
        grid=(ALL_TILES,),
        in_specs=[
            pl.BlockSpec((TM, D), lambda i: (i, 0)),
            pl.BlockSpec((1, 6, D), lambda i: (_mod_row(i), 0, 0)),
            pl.BlockSpec((1, D), const),
            pl.BlockSpec((D, 2 * A_HALF), const),
            pl.BlockSpec((1, 2 * A_HALF), const),
            pl.BlockSpec((1, A_HALF), const),
            pl.BlockSpec((A_GROUPS, CHUNK, CHUNK), lambda i: (0, 0, 0)),
            pl.BlockSpec((CHUNK, A_HALF), const),
            pl.BlockSpec((A_HALF, D), const),
        ],
        out_specs=pl.BlockSpec((TM, D), lambda i: (i, 0)),
        out_shape=jax.ShapeDtypeStruct((N_ALL, D), F32),
        scratch_shapes=[pltpu.VMEM((TM, 2 * A_HALF), F32), pltpu.VMEM((TM, A_HALF), BF16)],
        compiler_params=_cparams(("parallel",)),
        name="gmlp",
    )(xa, mod, g, w_in, b_in, g_v, w_s, b_s_full, w_out)


def _router_kernel(x_ref, mod_ref, g_ref, wr_ref, br_ref, meta_ref, tmeta_ref, cnt_ref, carry):
    @pl.when(pl.program_id(0) == 0)
    def _():
        carry[...] = jnp.zeros_like(carry)

    x = x_ref[...]
    h = _norm_mod(x, g_ref[...], mod_ref[0, 3:4, :], mod_ref[0, 4:5, :])
    lt = lax.dot_general(wr_ref[...], h, (((1,), (1,)), ((), ())),
                         preferred_element_type=F32, precision=lax.Precision.HIGHEST) + br_ref[...]
    e_t = lt[0:N_EXP]
    row8 = lax.broadcasted_iota(jnp.int32, (EXP_PER_GRP, TM), 0).astype(F32)
    g_t = jnp.where(row8 < N_GRP, lt[N_EXP:N_EXP + EXP_PER_GRP], -jnp.inf)
    gmax = jnp.max(g_t, axis=0, keepdims=True)
    p_g = 1.0 / jnp.sum(jnp.exp(g_t - gmax), axis=0, keepdims=True)
    g_idx = jnp.min(jnp.where(g_t == gmax, row8, float(EXP_PER_GRP)), axis=0, keepdims=True)
    sel = jnp.zeros((EXP_PER_GRP, TM), F32)
    for g in range(N_GRP):
        sel = sel + jnp.where(g_idx == g, e_t[g * EXP_PER_GRP:(g + 1) * EXP_PER_GRP], 0.0)
    m1 = jnp.max(sel, axis=0, keepdims=True)
    i1 = jnp.min(jnp.where(sel == m1, row8, float(EXP_PER_GRP)), axis=0, keepdims=True)
    sel2 = jnp.where(row8 == i1, -jnp.inf, sel)
    m2 = jnp.max(sel2, axis=0, keepdims=True)
    i2 = jnp.min(jnp.where(sel2 == m2, row8, float(EXP_PER_GRP)), axis=0, keepdims=True)
    e2 = jnp.exp(m2 - m1)
    w1 = p_g / (1.0 + e2)
    w2 = p_g * e2 / (1.0 + e2)
    row = lax.broadcasted_iota(jnp.int32, (LANES, TM), 0).astype(F32)
    id1 = g_idx * EXP_PER_GRP + i1
    id2 = g_idx * EXP_PER_GRP + i2
    oh1 = row == id1
    oh2 = row == id2
    oh = jnp.where(oh1, 1.0, 0.0) + jnp.where(oh2, 1.0, 0.0)
    before = (lax.broadcasted_iota(jnp.int32, (TM, TM), 0)
              < lax.broadcasted_iota(jnp.int32, (TM, TM), 1)).astype(BF16)
    tot = jnp.dot(oh.astype(BF16), before, preferred_element_type=F32) + carry[...]
    r1 = jnp.sum(jnp.where(oh1, tot, 0.0), axis=0, keepdims=True)
    r2 = jnp.sum(jnp.where(oh2, tot, 0.0), axis=0, keepdims=True)
    carry[...] += jnp.sum(oh, axis=-1, keepdims=True)
    cnt_ref[...] = carry[...]
    meta_ref[0:1, :] = id1
    meta_ref[1:2, :] = id2
    meta_ref[2:3, :] = r1
    meta_ref[3:4, :] = r2
    meta_ref[4:8, :] = jnp.zeros((4, TM), F32)
    tmeta_ref[...] = (jnp.where(row == 0.0, w1, 0.0) + jnp.where(row == 1.0, w2, 0.0)).T


def _router(xa, mod, g, wr_t, br_t, n_tiles):
    const = lambda i: (0, 0)
    n = n_tiles * TM
    return pl.pallas_call(
        _router_kernel,
        grid=(n_tiles,),
        in_specs=[
            pl.BlockSpec((TM, D), lambda i: (i, 0)),
            pl.BlockSpec((1, 6, D), lambda i: (_mod_row(i), 0, 0)),
            pl.BlockSpec((1, D), const),
            pl.BlockSpec((LANES, D), const),
            pl.BlockSpec((LANES, 1), const),
        ],
        out_specs=[pl.BlockSpec((8, TM), lambda i: (0, i)),
                   pl.BlockSpec((TM, LANES), lambda i: (i, 0)),
                   pl.BlockSpec((LANES, 1), const)],
        out_shape=[jax.ShapeDtypeStruct((8, n), F32), jax.ShapeDtypeStruct((n, LANES), F32),
                   jax.ShapeDtypeStruct((LANES, 1), F32)],
        scratch_shapes=[pltpu.VMEM((LANES, 1), F32)],
        compiler_params=_cparams(("arbitrary",)),
        name="router",
    )(xa, mod, g, wr_t, br_t)


ISSUE_UNROLL = 8


def _row_copy_wait(buf_slot, sem_slot):
    pltpu.make_async_copy(buf_slot, buf_slot, sem_slot).wait()


def _scatter_kernel(p1_ref, p2_ref, x_ref, mod_ref, g_ref, xs_ref, buf, sem):
    i = pl.program_id(0)
    n_steps = pl.num_programs(0)
    slot = lax.rem(i, 2)

    def wait_slot(s):
        _row_copy_wait(buf.at[s], sem.at[s])
        _row_copy_wait(buf.at[s], sem.at[s])

    @pl.when(i >= 2)
    def _():
        wait_slot(slot)

    buf[slot] = _norm_mod(x_ref[...], g_ref[...], mod_ref[0, 3:4, :], mod_ref[0, 4:5, :])
    base = i * TM

    def body(r, carry):
        src = buf.at[slot, pl.ds(r, 1), :]
        pltpu.make_async_copy(src, xs_ref.at[pl.ds(p1_ref[base + r], 1), :], sem.at[slot]).start()
        pltpu.make_async_copy(src, xs_ref.at[pl.ds(p2_ref[base + r], 1), :], sem.at[slot]).start()
        return carry

    lax.fori_loop(0, TM, body, 0, unroll=ISSUE_UNROLL)

    @pl.when(i == n_steps - 1)
    def _():
        @pl.when(i >= 1)
        def _():
            wait_slot(1 - slot)

        wait_slot(slot)


def _scatter(p1, p2, xa, mod, g, n_tiles):
    n = n_tiles * TM
    return pl.pallas_call(
        _scatter_kernel,
        grid_spec=pltpu.PrefetchScalarGridSpec(
            num_scalar_prefetch=2,
            grid=(n_tiles,),
            in_specs=[
                pl.BlockSpec((TM, D), lambda i, p1, p2: (i, 0)),
                pl.BlockSpec((1, 6, D), lambda i, p1, p2: (_mod_row(i), 0, 0)),
                pl.BlockSpec((1, D), lambda i, p1, p2: (0, 0)),
            ],
            out_specs=pl.BlockSpec(memory_space=pl.ANY),
            scratch_shapes=[pltpu.VMEM((2, TM, D), F32), pltpu.SemaphoreType.DMA((2,))],
        ),
        out_shape=jax.ShapeDtypeStruct((2 * n, D), F32),
        compiler_params=_cparams(("arbitrary",)),
        name="moe_scatter",
    )(p1, p2, xa, mod, g)


def _grouped_kernel(tile_ref, exp_ref, lo_ref, hi_ref, first_ref, newexp_ref, valid_ref,
                    xs_ref, w1_ref, w3_ref, w2_ref, ys_ref, wb1, wb3, wb2):
    q = pl.program_id(0)

    @pl.when(valid_ref[q] == 1)
    def _():
        @pl.when(newexp_ref[q] == 1)
        def _():
            wb1[...] = w1_ref[0].astype(BF16)
            wb3[...] = w3_ref[0].astype(BF16)
            wb2[...] = w2_ref[0].astype(BF16)

        x = xs_ref[...].astype(BF16)
        a = jnp.dot(x, wb1[...], preferred_element_type=F32)
        b = jnp.dot(x, wb3[...], preferred_element_type=F32)
        rows = lax.broadcasted_iota(jnp.int32, (TM, 1), 0)
        mine = (rows >= lo_ref[q]) & (rows < hi_ref[q])
        act = jnp.where(mine, a * jax.nn.sigmoid(a) * b, 0.0)
        y = jnp.dot(act.astype(BF16), wb2[...], preferred_element_type=F32)

        @pl.when(first_ref[q] == 1)
        def _():
            ys_ref[...] = y

        @pl.when(first_ref[q] == 0)
        def _():
            ys_ref[...] += y


def _grouped(plan, xs, w1, w3, w2):
    n_pairs = plan[0].shape[0]
    tile_map = lambda q, tile, exp, *_: (tile[q], 0)
    exp_map = lambda q, tile, exp, *_: (exp[q], 0, 0)
    return pl.pallas_call(
        _grouped_kernel,
        grid_spec=pltpu.PrefetchScalarGridSpec(
            num_scalar_prefetch=7,
            grid=(n_pairs,),
            in_specs=[
                pl.BlockSpec((TM, D), tile_map),
                pl.BlockSpec((1, D, D_EXP), exp_map),
                pl.BlockSpec((1, D, D_EXP), exp_map),
                pl.BlockSpec((1, D_EXP, D), exp_map),
            ],
            out_specs=pl.BlockSpec((TM, D), tile_map),
            scratch_shapes=[pltpu.VMEM((D, D_EXP), BF16), pltpu.VMEM((D, D_EXP), BF16),
                            pltpu.VMEM((D_EXP, D), BF16)],
        ),
        out_shape=jax.ShapeDtypeStruct(xs.shape, F32),
        compiler_params=_cparams(("arbitrary",)),
        name="moe_grouped",
    )(*plan, xs, w1, w3, w2)


def _gather_kernel(p1_ref, p2_ref, x_ref, mod_ref, tm_ref, fg_ref, ys_ref, o_ref, buf, sem,
                   *, final_norm):
    i = pl.program_id(0)
    n_steps = pl.num_programs(0)
    slot = lax.rem(i, 2)

    def issue(tile, s):
        base = tile * TM

        def body(r, carry):
            pltpu.make_async_copy(ys_ref.at[pl.ds(p1_ref[base + r], 1), :],
                                  buf.at[s, 0, pl.ds(r, 1), :], sem.at[s]).start()
            pltpu.make_async_copy(ys_ref.at[pl.ds(p2_ref[base + r], 1), :],
                                  buf.at[s, 1, pl.ds(r, 1), :], sem.at[s]).start()
            return carry

        lax.fori_loop(0, TM, body, 0, unroll=ISSUE_UNROLL)

    @pl.when(i == 0)
    def _():
        issue(0, 0)

    @pl.when(i + 1 < n_steps)
    def _():
        issue(i + 1, 1 - slot)

    _row_copy_wait(buf.at[slot, 0], sem.at[slot])
    _row_copy_wait(buf.at[slot, 1], sem.at[slot])
    y = tm_ref[:, 0:1] * buf[slot, 0] + tm_ref[:, 1:2] * buf[slot, 1]
    out = x_ref[...] + mod_ref[0, 5:6, :] * y
    if final_norm:
        out = out * lax.rsqrt(jnp.mean(out * out, axis=-1, keepdims=True) + EPS) * fg_ref[...]
    o_ref[...] = out


def _gather(p1, p2, xa, mod, tmeta, final_g, ys, n_tiles, final_norm):
    return pl.pallas_call(
        functools.partial(_gather_kernel, final_norm=final_norm),
        grid_spec=pltpu.PrefetchScalarGridSpec(
            num_scalar_prefetch=2,
            grid=(n_tiles,),
            in_specs=[
                pl.BlockSpec((TM, D), lambda i, p1, p2: (i, 0)),
                pl.BlockSpec((1, 6, D), lambda i, p1, p2: (_mod_row(i), 0, 0)),
                pl.BlockSpec((TM, LANES), lambda i, p1, p2: (i, 0)),
                pl.BlockSpec((1, D), lambda i, p1, p2: (0, 0)),
                pl.BlockSpec(memory_space=pl.ANY),
            ],
            out_specs=pl.BlockSpec((TM, D), lambda i, p1, p2: (i, 0)),
            scratch_shapes=[pltpu.VMEM((2, 2, TM, D), F32), pltpu.SemaphoreType.DMA((2,))],
        ),
        out_shape=jax.ShapeDtypeStruct((n_tiles * TM, D), F32),
        compiler_params=_cparams(("arbitrary",)),
        name="moe_gather",
    )(p1, p2, xa, mod, tmeta, final_g, ys)


def _inproj_kernel(x_ref, mod_ref, g_ref, win_ref, xm_ref, op_ref):
    h = _norm_mod(x_ref[...], g_ref[...], mod_ref[0, 0:1, :], mod_ref[0, 1:2, :])
    hb = h.astype(BF16)
    xm_ref[...] = jnp.dot(hb, win_ref[:, :B_INNER], preferred_element_type=F32).astype(BF16)
    op_ref[...] = jnp.dot(hb, win_ref[:, B_INNER:], preferred_element_type=F32).astype(BF16)


def _inproj(xa, mod, g, w_in):
    const = lambda i: (0, 0)
    return pl.pallas_call(
        _inproj_kernel,
        grid=(ALL_TILES,),
        in_specs=[
            pl.BlockSpec((TM, D), lambda i: (i, 0)),
            pl.BlockSpec((1, 6, D), lambda i: (_mod_row(i), 0, 0)),
            pl.BlockSpec((1, D), const),
            pl.BlockSpec((D, 2 * B_INNER), const),
        ],
        out_specs=[pl.BlockSpec((TM, B_INNER), lambda i: (i, 0)),
                   pl.BlockSpec((TM, B_INNER), lambda i: (i, 0))],
        out_shape=[jax.ShapeDtypeStruct((N_ALL, B_INNER), BF16),
                   jax.ShapeDtypeStruct((N_ALL, B_INNER), BF16)],
        compiler_params=_cparams(("parallel",)),
        name="mlstm_inproj",
    )(xa, mod, g, w_in)


def _log_sigmoid(x):
    return jnp.minimum(x, 0.0) - jnp.log(1.0 + jnp.exp(-jnp.abs(x)))


def _qkv_kernel(xm_ref, prev_ref, next_ref, cw_ref, cb_ref, wqk_ref, wv_ref, wg_ref, wgt_ref,
                bg_ref, bgt_ref, xc_ref, q_ref, k_ref, v_ref, gt_ref, gtt_ref):
    i = pl.program_id(0)
    lat = i < LAT_TILES
    first = jnp.where(lat, i % TILES_PER_SEQ == 0, True)
    last = jnp.where(lat, i % TILES_PER_SEQ == TILES_PER_SEQ - 1, True)
    xmb = xm_ref[...]
    xm = xmb.astype(F32)
    prev = jnp.where(first, 0.0, prev_ref[...].astype(F32))
    nxt = jnp.where(last, 0.0, next_ref[...].astype(F32))
    ext = jnp.concatenate([prev[HALO - 8:], xm, nxt[:8]], axis=0)
    acc = jnp.zeros((TM, B_INNER), F32) + cb_ref[...]
    for t in range(CONV_K):
        off = 8 + t - CONV_K // 2
        acc = acc + ext[off:off + TM] * cw_ref[t:t + 1, :]
    xc = acc * jax.nn.sigmoid(acc)
    xcb = xc.astype(BF16)
    xc_ref[...] = xcb
    for h in range(HEADS):
        hs = slice(h * DH, (h + 1) * DH)
        qk = jnp.dot(xcb[:, hs], wqk_ref[h], preferred_element_type=F32)
        q_ref[:, h * DQK:(h + 1) * DQK] = (qk[:, :DQK] * (DQK ** -0.5)).astype(BF16)
        k_ref[:, h * DQK:(h + 1) * DQK] = qk[:, DQK:].astype(BF16)
        v_ref[:, hs] = jnp.dot(xmb[:, hs], wv_ref[h], preferred_element_type=F32).astype(BF16)
    gts = jnp.dot(xmb, wg_ref[...], preferred_element_type=F32) + bg_ref[...]
    lane = lax.broadcasted_iota(jnp.int32, (TM, LANES), 1)
    gt_ref[...] = jnp.where(((lane >> 3) & 1) == 1, _log_sigmoid(gts), gts)
    gtt = lax.dot_general(wgt_ref[...], xmb, (((1,), (1,)), ((), ())),
                          preferred_element_type=F32) + bgt_ref[...]
    row = lax.broadcasted_iota(jnp.int32, (4 * HEADS, TM), 0)
    gtt_ref[...] = jnp.where(((row >> 3) & 1) == 1, _log_sigmoid(gtt), gtt)


def _qkv(xm, conv_w, conv_b, wqk, wv, wg, wgt, bg, bgt):
    const = lambda i: (0, 0)
    const3 = lambda i: (0, 0, 0)
    hb = TM // HALO
    n_hb = N_ALL // HALO
    return pl.pallas_call(
        _qkv_kernel,
        grid=(ALL_TILES,),
        in_specs=[
            pl.BlockSpec((TM, B_INNER), lambda i: (i, 0)),
            pl.BlockSpec((HALO, B_INNER), lambda i: (jnp.maximum(i * hb - 1, 0), 0)),
            pl.BlockSpec((HALO, B_INNER), lambda i: (jnp.minimum((i + 1) * hb, n_hb - 1), 0)),
            pl.BlockSpec((CONV_K, B_INNER), const),
            pl.BlockSpec((1, B_INNER), const),
            pl.BlockSpec((HEADS, DH, 2 * DQK), const3),
            pl.BlockSpec((HEADS, DH, DV), const3),
            pl.BlockSpec((B_INNER, LANES), const),
            pl.BlockSpec((4 * HEADS, B_INNER), const),
            pl.BlockSpec((1, LANES), const),
            pl.BlockSpec((4 * HEADS, 1), const),
        ],
        out_specs=[
            pl.BlockSpec((TM, B_INNER), lambda i: (i, 0)),
            pl.BlockSpec((TM, HEADS * DQK), lambda i: (i, 0)),
            pl.BlockSpec((TM, HEADS * DQK), lambda i: (i, 0)),
            pl.BlockSpec((TM, B_INNER), lambda i: (i, 0)),
            pl.BlockSpec((TM, LANES), lambda i: (i, 0)),
            pl.BlockSpec((4 * HEADS, TM), lambda i: (0, i)),
        ],
        out_shape=[
            jax.ShapeDtypeStruct((N_ALL, B_INNER), BF16),
            jax.ShapeDtypeStruct((N_ALL, HEADS * DQK), BF16),
            jax.ShapeDtypeStruct((N_ALL, HEADS * DQK), BF16),
            jax.ShapeDtypeStruct((N_ALL, B_INNER), BF16),
            jax.ShapeDtypeStruct((N_ALL, LANES), F32),
            jax.ShapeDtypeStruct((4 * HEADS, N_ALL), F32),
        ],
        compiler_params=_cparams(("parallel",)),
        name="mlstm_qkv",
    )(xm, xm, xm, conv_w, conv_b, wqk, wv, wg, wgt, bg, bgt)


CTX_CHUNKS = CTX // CHUNK
LAT_CHUNKS = SEQ // CHUNK
SCAN_STEPS = CTX_CHUNKS + LAT_CHUNKS


def _scan_dir(d, need_h, q_ref, k_ref, v_ref, g_ref, gt_ref, o_ref, c_scr, n_scr, m_scr):
    r = lax.broadcasted_iota(jnp.int32, (CHUNK, CHUNK), 0)
    c = lax.broadcasted_iota(jnp.int32, (CHUNK, CHUNK), 1)
    if d == 0:
        keep = c <= r
    else:
        keep = c >= r
    tri_col = keep.astype(F32)
    tri_row = (r <= c).astype(F32) if d == 0 else (r >= c).astype(F32)
    gts = g_ref[...]
    gtt = gt_ref[...]
    hi = lax.Precision.HIGHEST
    bcol_all = jnp.dot(tri_col, gts, preferred_element_type=F32, precision=hi)
    brow_all = jnp.dot(gtt, tri_row, preferred_element_type=F32, precision=hi)
    tot_all = jnp.sum(gtt, axis=-1, keepdims=True)
    base = 2 * HEADS * d
    for h in range(HEADS):
        li_c = gts[:, base + h:base + h + 1]
        li_r = gtt[base + h:base + h + 1, :]
        b_c = bcol_all[:, base + HEADS + h:base + HEADS + h + 1]
        b_r = brow_all[base + HEADS + h:base + HEADS + h + 1, :]
        b_last = tot_all[base + HEADS + h:base + HEADS + h + 1, :]
        m_old = m_scr[d, h]
        c_old = c_scr[d, h]
        n_old = n_scr[d, h]
        qh = q_ref[:, h * DQK:(h + 1) * DQK]
        kh = k_ref[:, h * DQK:(h + 1) * DQK]
        vh = v_ref[:, h * DV:(h + 1) * DV]
        g_r = b_last - b_r + li_r
        g_c = b_last - b_c + li_c
        m_new = jnp.maximum(b_last + m_old, jnp.max(g_r, axis=-1, keepdims=True))
        decay = jnp.exp(b_last + m_old - m_new)
        wk = jnp.exp(g_c - m_new)
        kw = kh.astype(F32) * wk
        c_scr[d, h] = decay * c_old + lax.dot_general(
            kw.astype(BF16), vh, (((0,), (0,)), ((), ())), preferred_element_type=F32)
        n_scr[d, h] = decay * n_old + jnp.sum(kw, axis=0, keepdims=True)
        m_scr[d, h] = m_new

        @pl.when(need_h)
        def _():
            dmat = jnp.where(keep, b_c - b_r + li_r, NEG_INF)
            inter = b_c + m_old
            m_t = jnp.maximum(inter, jnp.max(dmat, axis=-1, keepdims=True))
            p = jnp.exp(dmat - m_t)
            s = lax.dot_general(qh, kh, (((1,), (1,)), ((), ())), preferred_element_type=F32) * p
            a = jnp.exp(inter - m_t)
            num = a * jnp.dot(qh, c_old.astype(BF16), preferred_element_type=F32) + jnp.dot(
                s.astype(BF16), vh, preferred_element_type=F32)
            den = a * jnp.sum(qh.astype(F32) * n_old, axis=-1, keepdims=True) + jnp.sum(
                s, axis=-1, keepdims=True)
            hh = num / jnp.maximum(jnp.abs(den), jnp.exp(-m_t))
            o_ref[:, h * DV:(h + 1) * DV] = hh.astype(BF16)


def _scan_kernel(qf, kf, vf, gf, gtf, qb, kb, vb, gb, gtb, of, ob, c_scr, n_scr, m_scr):
    j = pl.program_id(1)

    @pl.when(j == 0)
    def _():
        c_scr[...] = jnp.zeros_like(c_scr)
        n_scr[...] = jnp.zeros_like(n_scr)
        m_scr[...] = jnp.zeros_like(m_scr)

    need_h = j >= CTX_CHUNKS
    _scan_dir(0, need_h, qf, kf, vf, gf, gtf, of, c_scr, n_scr, m_scr)
    _scan_dir(1, need_h, qb, kb, vb, gb, gtb, ob, c_scr, n_scr, m_scr)


def _scan(q, k, v, gts, gtt):
    lat_blk = N_LAT // CHUNK

    def fwd_in(b, j):
        return jnp.where(j < CTX_CHUNKS, lat_blk + b * CTX_CHUNKS + j, b * LAT_CHUNKS + j - CTX_CHUNKS)

    def bwd_in(b, j):
        return jnp.where(j < CTX_CHUNKS, lat_blk + b * CTX_CHUNKS + (CTX_CHUNKS - 1 - j),
                         b * LAT_CHUNKS + (SCAN_STEPS - 1 - j))

    def fwd_out(b, j):
        return b * LAT_CHUNKS + jnp.maximum(j - CTX_CHUNKS, 0)

    def bwd_out(b, j):
        return b * LAT_CHUNKS + jnp.minimum(SCAN_STEPS - 1 - j, LAT_CHUNKS - 1)

    def specs(fn):
        return [
            pl.BlockSpec((CHUNK, HEADS * DQK), lambda b, j: (fn(b, j), 0)),
            pl.BlockSpec((CHUNK, HEADS * DQK), lambda b, j: (fn(b, j), 0)),
            pl.BlockSpec((CHUNK, B_INNER), lambda b, j: (fn(b, j), 0)),
            pl.BlockSpec((CHUNK, LANES), lambda b, j: (fn(b, j), 0)),
            pl.BlockSpec((4 * HEADS, CHUNK), lambda b, j: (0, fn(b, j))),
        ]

    return pl.pallas_call(
        _scan_kernel,
        grid=(BATCH, SCAN_STEPS),
        in_specs=specs(fwd_in) + specs(bwd_in),
        out_specs=[pl.BlockSpec((CHUNK, B_INNER), lambda b, j: (fwd_out(b, j), 0)),
                   pl.BlockSpec((CHUNK, B_INNER), lambda b, j: (bwd_out(b, j), 0))],
        out_shape=[jax.ShapeDtypeStruct((N_LAT, B_INNER), BF16),
                   jax.ShapeDtypeStruct((N_LAT, B_INNER), BF16)],
        scratch_shapes=[pltpu.VMEM((2, HEADS, DQK, DV), F32),
                        pltpu.VMEM((2, HEADS, 1, DQK), F32),
                        pltpu.VMEM((2, HEADS, 1, 1), F32)],
        compiler_params=_cparams(("parallel", "arbitrary")),
        name="mlstm_scan",
    )(q, k, v, gts, gtt, q, k, v, gts, gtt)


def _combine_kernel(x_ref, mod_ref, hf_ref, hb_ref, xc_ref, op_ref, hg_ref, sk_ref, wout_ref,
                    o_ref, y_scr):
    for h in range(HEADS):
        hs = slice(h * DV, (h + 1) * DV)
        s = hf_ref[:, hs].astype(F32) + hb_ref[:, hs].astype(F32)
        s = s * lax.rsqrt(jnp.mean(s * s, axis=-1, keepdims=True) + EPS)
        y = jax.nn.sigmoid(op_ref[:, hs].astype(F32)) * (
            s * hg_ref[:, hs] + sk_ref[:, hs] * xc_ref[:, hs].astype(F32))
        y_scr[:, hs] = y.astype(BF16)
    out = jnp.dot(y_scr[...], wout_ref[...], preferred_element_type=F32)
    o_ref[...] = x_ref[...] + mod_ref[0, 2:3, :] * out


def _combine(xa, mod, hf, hb, xc, op, head_g, skip, w_out):
    const = lambda i: (0, 0)
    row = lambda i: (i, 0)
    return pl.pallas_call(
        _combine_kernel,
        grid=(LAT_TILES,),
        in_specs=[
            pl.BlockSpec((TM, D), row),
            pl.BlockSpec((1, 6, D), lambda i: (_mod_row(i), 0, 0)),
            pl.BlockSpec((TM, B_INNER), row),
            pl.BlockSpec((TM, B_INNER), row),
            pl.BlockSpec((TM, B_INNER), row),
            pl.BlockSpec((TM, B_INNER), row),
            pl.BlockSpec((1, B_INNER), const),
            pl.BlockSpec((1, B_INNER), const),
            pl.BlockSpec((B_INNER, D), const),
        ],
        out_specs=pl.BlockSpec((TM, D), row),
        out_shape=jax.ShapeDtypeStruct((N_LAT, D), F32),
        scratch_shapes=[pltpu.VMEM((TM, B_INNER), BF16)],
        compiler_params=_cparams(("parallel",)),
        name="mlstm_combine",
    )(xa, mod, hf, hb, xc, op, head_g, skip, w_out)


def _router_weights(w_grp, b_grp, w_exp, b_exp):
    wr = jnp.zeros((LANES, D), F32).at[:N_EXP].set(w_exp.T).at[N_EXP:N_EXP + N_GRP].set(w_grp.T)
    br = jnp.zeros((LANES, 1), F32).at[:N_EXP, 0].set(b_exp).at[N_EXP:N_EXP + N_GRP, 0].set(b_grp)
    return wr, br


def _route_plan(meta, cnt, n_rows):
    i32 = jnp.int32
    n_sorted_tiles = 2 * n_rows // TM
    n_pairs = n_sorted_tiles + N_EXP - 1
    counts = cnt[:N_EXP, 0].astype(i32)
    ends = jnp.cumsum(counts)
    starts = ends - counts
    id1, id2, r1, r2 = (meta[j].astype(i32) for j in range(4))
    p1 = starts[id1] + r1
    p2 = starts[id2] + r2
    first_tile = starts // TM
    pairs_e = jnp.where(counts > 0, (ends - 1) // TM - first_tile + 1, 0)
    pend = jnp.cumsum(pairs_e)
    pstart = pend - pairs_e
    total = pend[-1]
    q = jnp.arange(n_pairs, dtype=i32)
    qc = jnp.minimum(q, total - 1)
    e_q = jnp.sum((qc[:, None] >= pend[None, :]).astype(i32), axis=1)
    tile_q = first_tile[e_q] + qc - pstart[e_q]
    lo = jnp.clip(starts[e_q] - tile_q * TM, 0, TM)
    hi = jnp.clip(ends[e_q] - tile_q * TM, 0, TM)
    valid = (q < total).astype(i32)
    first = (tile_q != jnp.concatenate([jnp.full((1,), -1, i32), tile_q[:-1]])).astype(i32)
    newexp = (e_q != jnp.concatenate([jnp.full((1,), -1, i32), e_q[:-1]])).astype(i32)
    return p1, p2, (tile_q, e_q, lo, hi, first, newexp, valid)


def _moe(xa, mod, g2, w_grp, b_grp, w_exp, b_exp, w1, w3, w2, final_g, n_rows, final_norm):
    wr, br = _router_weights(w_grp, b_grp, w_exp, b_exp)
    n_tiles = n_rows // TM
    meta, tmeta, cnt = _router(xa, mod, g2, wr, br, n_tiles)
    p1, p2, plan = _route_plan(meta, cnt, n_rows)
    xs = _scatter(p1, p2, xa, mod, g2, n_tiles)
    ys = _grouped(plan, xs, w1, w3, w2)
    return _gather(p1, p2, xa, mod, tmeta, final_g.reshape(1, D), ys, n_tiles, final_norm)


def kernel(x, c, ctx, c_ctx, norm_g, w_ada, b_ada, a_w_in, a_b_in, a_g_v, a_w_s, a_b_s, a_w_out,
           b_w_in, b_conv_w, b_conv_b, b_w_q, b_w_k, b_w_v, b_w_gate, b_b_gate, b_head_g, b_skip,
           b_w_out, moe_w_grp, moe_b_grp, moe_w_exp, moe_b_exp, moe_w1, moe_w3, moe_w2, final_g):
    xa = jnp.concatenate([x.reshape(N_LAT, D), ctx.reshape(N_CTX, D)], axis=0)
    cc = jnp.zeros((MOD_ROWS, D), F32).at[:BATCH].set(c).at[BATCH].set(c_ctx)
    mods = _ada(cc, w_ada, b_ada).reshape(2, MOD_ROWS, 6, D)

    mod = mods[0]
    b_s_full = jnp.repeat(a_b_s[0].T, A_GC, axis=1)
    xa = _gmlp(xa, mod, norm_g[0, 0].reshape(1, D), a_w_in[0].astype(BF16),
               a_b_in[0].reshape(1, -1), a_g_v[0].reshape(1, -1), a_w_s[0].astype(BF16),
               b_s_full, a_w_out[0].astype(BF16))
    xa = _moe(xa, mod, norm_g[0, 1].reshape(1, D), moe_w_grp[0], moe_b_grp[0], moe_w_exp[0],
              moe_b_exp[0], moe_w1[0], moe_w3[0], moe_w2[0], final_g, N_ALL, False)

    mod = mods[1]
    xm, op = _inproj(xa, mod, norm_g[1, 0].reshape(1, D), b_w_in[0].astype(BF16))
    wqk = jnp.concatenate([b_w_q[0], b_w_k[0]], axis=-1).astype(BF16)
    wg = jnp.zeros((B_INNER, LANES), F32).at[:, :4 * HEADS].set(b_w_gate[0]).astype(BF16)
    bg = jnp.zeros((1, LANES), F32).at[0, :4 * HEADS].set(b_b_gate[0])
    xc, q, k, v, gts, gtt = _qkv(xm, b_conv_w[0], b_conv_b[0].reshape(1, -1), wqk,
                                 b_w_v[0].astype(BF16), wg, b_w_gate[0].T.astype(BF16), bg,
                                 b_b_gate[0].reshape(-1, 1))
    hf, hb = _scan(q, k, v, gts, gtt)
    xl = _combine(xa, mod, hf, hb, xc, op, b_head_g[0].reshape(1, -1), b_skip[0].reshape(1, -1),
                  b_w_out[0].astype(BF16))
    out = _moe(xl, mod, norm_g[1, 1].reshape(1, D), moe_w_grp[1], moe_b_grp[1], moe_w_exp[1],
               moe_b_exp[1], moe_w1[1], moe_w3[1], moe_w2[1], final_g, N_LAT, True)
    return out.reshape(BATCH, SEQ, D)
```

```python
import functools

import jax
import jax.numpy as jnp
from jax import lax
from jax.experimental import pallas as pl
from jax.experimental.pallas import tpu as pltpu

F32 = jnp.float32
BF16 = jnp.bfloat16

D = 1024
BATCH = 8
SEQ = 2048
CTX = 256
EPS = 1e-6
NEG_INF = -1e30
N_LAT = BATCH * SEQ
N_CTX = BATCH * CTX
N_ALL = N_LAT + N_CTX

TM = 256
LAT_TILES = N_LAT // TM
ALL_TILES = N_ALL // TM
TILES_PER_SEQ = SEQ // TM
MOD_ROWS = 16

CHUNK = 128
A_HALF = 2048
A_GROUPS = 8
A_GC = A_HALF // A_GROUPS
B_INNER = 2048
HEADS = 8
DH = B_INNER // HEADS
DQK = DH // 2
DV = DH
CONV_K = 5
HALO = 16
N_EXP = 32
N_GRP = 4
EXP_PER_GRP = 8
D_EXP = 512
LANES = 128

VMEM_LIMIT = 56 * 1024 * 1024


def _cparams(sem):
    return pltpu.CompilerParams(dimension_semantics=sem, vmem_limit_bytes=VMEM_LIMIT)


def _mod_row(i):
    return jnp.where(i < LAT_TILES, i // TILES_PER_SEQ, BATCH)


def _norm_mod(x, g, shift, scale):
    y = x * lax.rsqrt(jnp.mean(x * x, axis=-1, keepdims=True) + EPS) * g
    return y * (1.0 + scale) + shift


def _gelu_tanh(x):
    return 0.5 * x * (1.0 + jnp.tanh(0.7978845608028654 * (x + 0.044715 * (x * x * x))))


ADA_BN = 1536


def _ada_kernel(c_ref, w_ref, b_ref, o_ref):
    c = c_ref[...]
    a = c * jax.nn.sigmoid(c)
    o_ref[0] = jnp.dot(a, w_ref[0], preferred_element_type=F32,
                       precision=lax.Precision.HIGHEST) + b_ref[0]


def _ada(cc, w_ada, b_ada):
    depth = w_ada.shape[0]
    return pl.pallas_call(
        _ada_kernel,
        grid=(depth, 6 * D // ADA_BN),
        in_specs=[
            pl.BlockSpec((MOD_ROWS, D), lambda l, j: (0, 0)),
            pl.BlockSpec((1, D, ADA_BN), lambda l, j: (l, 0, j)),
            pl.BlockSpec((1, 1, ADA_BN), lambda l, j: (l, 0, j)),
        ],
        out_specs=pl.BlockSpec((1, MOD_ROWS, ADA_BN), lambda l, j: (l, 0, j)),
        out_shape=jax.ShapeDtypeStruct((depth, MOD_ROWS, 6 * D), F32),
        compiler_params=_cparams(("parallel", "parallel")),
        name="ada",
    )(cc, w_ada, b_ada.reshape(depth, 1, 6 * D))


GM_CH = 512


def _gmlp_kernel(xl_ref, xc_ref, mod_ref, g_ref, win_ref, bin_ref, gv_ref, ws_ref, bs_ref, wout_ref,
                 o_ref, z_scr, y_scr):
    x = jnp.where(pl.program_id(0) < LAT_TILES, xl_ref[...], xc_ref[...])
    h = _norm_mod(x, g_ref[...], mod_ref[0, 0:1, :], mod_ref[0, 1:2, :])
    hb = h.astype(BF16)
    s1 = jnp.zeros((TM, 1), F32)
    s2 = jnp.zeros((TM, 1), F32)
    for j in range(2 * A_HALF // GM_CH):
        cs = slice(j * GM_CH, (j + 1) * GM_CH)
        zc = jnp.dot(hb, win_ref[:, cs], preferred_element_type=F32) + bin_ref[:, cs]
        zc = _gelu_tanh(zc)
        z_scr[:, cs] = zc
        if j * GM_CH >= A_HALF:
            s1 = s1 + jnp.sum(zc, axis=-1, keepdims=True)
            s2 = s2 + jnp.sum(zc * zc, axis=-1, keepdims=True)
    mu = s1 * (1.0 / A_HALF)
    rstd = lax.rsqrt(s2 * (1.0 / A_HALF) - mu * mu + EPS)
    for c in range(TM // CHUNK):
        rs = slice(c * CHUNK, (c + 1) * CHUNK)
        for g in range(A_GROUPS):
            cs = slice(g * A_GC, (g + 1) * A_GC)
            vs = slice(A_HALF + g * A_GC, A_HALF + (g + 1) * A_GC)
            v = (z_scr[rs, vs] - mu[rs]) * rstd[rs] * gv_ref[:, cs]
            s = jnp.dot(ws_ref[g], v.astype(BF16), preferred_element_type=F32) + bs_ref[:, cs]
            y_scr[rs, cs] = (z_scr[rs, cs] * s).astype(BF16)
    out = jnp.dot(y_scr[...], wout_ref[...], preferred_element_type=F32)
    o_ref[...] = x + mod_ref[0, 2:3, :] * out


def _gmlp(xl, xc, mod, g, w_in, b_in, g_v, w_s, b_s_full, w_out):
    const = lambda i: (0, 0)
    return pl.pallas_call(
        _gmlp_kernel,
        grid=(ALL_TILES,),
        in_specs=[
            pl.BlockSpec((TM, D), lambda i: (jnp.minimum(i, LAT_TILES - 1), 0)),
            pl.BlockSpec((TM, D), lambda i: (jnp.maximum(i - LAT_TILES, 0), 0)),
            pl.BlockSpec((1, 6, D), lambda i: (_mod_row(i), 0, 0)),
            pl.BlockSpec((1, D), const),
            pl.BlockSpec((D, 2 * A_HALF), const),
            pl.BlockSpec((1, 2 * A_HALF), const),
            pl.BlockSpec((1, A_HALF), const),
            pl.BlockSpec((A_GROUPS, CHUNK, CHUNK), lambda i: (0, 0, 0)),
            pl.BlockSpec((CHUNK, A_HALF), const),
            pl.BlockSpec((A_HALF, D), const),
        ],
        out_specs=pl.BlockSpec((TM, D), lambda i: (i, 0)),
        out_shape=jax.ShapeDtypeStruct((N_ALL, D), F32),
        scratch_shapes=[pltpu.VMEM((TM, 2 * A_HALF), F32), pltpu.VMEM((TM, A_HALF), BF16)],
        compiler_params=_cparams(("parallel",)),
        name="gmlp",
    )(xl, xc, mod, g, w_in, b_in, g_v, w_s, b_s_full, w_out)


def _router_kernel(x_ref, mod_ref, g_ref, wr_ref, br_ref, meta_ref, tmeta_ref, cnt_ref, carry):
    @pl.when(pl.program_id(0) == 0)
    def _():
        carry[...] = jnp.zeros_like(carry)

    x = x_ref[...]
    h = _norm_mod(x, g_ref[...], mod_ref[0, 3:4, :], mod_ref[0, 4:5, :])
    lt = lax.dot_general(wr_ref[...], h, (((1,), (1,)), ((), ())),
                         preferred_element_type=F32, precision=lax.Precision.HIGHEST) + br_ref[...]
    e_t = lt[0:N_EXP]
    row8 = lax.broadcasted_iota(jnp.int32, (EXP_PER_GRP, TM), 0).astype(F32)
    g_t = jnp.where(row8 < N_GRP, lt[N_EXP:N_EXP + EXP_PER_GRP], -jnp.inf)
    gmax = jnp.max(g_t, axis=0, keepdims=True)
    p_g = 1.0 / jnp.sum(jnp.exp(g_t - gmax), axis=0, keepdims=True)
    g_idx = jnp.min(jnp.where(g_t == gmax, row8, float(EXP_PER_GRP)), axis=0, keepdims=True)
    sel = jnp.zeros((EXP_PER_GRP, TM), F32)
    for g in range(N_GRP):
        sel = sel + jnp.where(g_idx == g, e_t[g * EXP_PER_GRP:(g + 1) * EXP_PER_GRP], 0.0)
    m1 = jnp.max(sel, axis=0, keepdims=True)
    i1 = jnp.min(jnp.where(sel == m1, row8, float(EXP_PER_GRP)), axis=0, keepdims=True)
    sel2 = jnp.where(row8 == i1, -jnp.inf, sel)
    m2 = jnp.max(sel2, axis=0, keepdims=True)
    i2 = jnp.min(jnp.where(sel2 == m2, row8, float(EXP_PER_GRP)), axis=0, keepdims=True)
    e2 = jnp.exp(m2 - m1)
    w1 = p_g / (1.0 + e2)
    w2 = p_g * e2 / (1.0 + e2)
    row = lax.broadcasted_iota(jnp.int32, (LANES, TM), 0).astype(F32)
    id1 = g_idx * EXP_PER_GRP + i1
    id2 = g_idx * EXP_PER_GRP + i2
    oh1 = row == id1
    oh2 = row == id2
    oh = jnp.where(oh1, 1.0, 0.0) + jnp.where(oh2, 1.0, 0.0)
    before = (lax.broadcasted_iota(jnp.int32, (TM, TM), 0)
              < lax.broadcasted_iota(jnp.int32, (TM, TM), 1)).astype(BF16)
    tot = jnp.dot(oh.astype(BF16), before, preferred_element_type=F32) + carry[...]
    r1 = jnp.sum(jnp.where(oh1, tot, 0.0), axis=0, keepdims=True)
    r2 = jnp.sum(jnp.where(oh2, tot, 0.0), axis=0, keepdims=True)
    carry[...] += jnp.sum(oh, axis=-1, keepdims=True)
    cnt_ref[...] = carry[...]
    meta_ref[0:1, :] = id1
    meta_ref[1:2, :] = id2
    meta_ref[2:3, :] = r1
    meta_ref[3:4, :] = r2
    meta_ref[4:8, :] = jnp.zeros((4, TM), F32)
    tmeta_ref[...] = (jnp.where(row == 0.0, w1, 0.0) + jnp.where(row == 1.0, w2, 0.0)).T


def _router(xa, mod, g, wr_t, br_t, n_tiles):
    const = lambda i: (0, 0)
    n = n_tiles * TM
    return pl.pallas_call(
        _router_kernel,
        grid=(n_tiles,),
        in_specs=[
            pl.BlockSpec((TM, D), lambda i: (i, 0)),
            pl.BlockSpec((1, 6, D), lambda i: (_mod_row(i), 0, 0)),
            pl.BlockSpec((1, D), const),
            pl.BlockSpec((LANES, D), const),
            pl.BlockSpec((LANES, 1), const),
        ],
        out_specs=[pl.BlockSpec((8, TM), lambda i: (0, i)),
                   pl.BlockSpec((TM, LANES), lambda i: (i, 0)),
                   pl.BlockSpec((LANES, 1), const)],
        out_shape=[jax.ShapeDtypeStruct((8, n), F32), jax.ShapeDtypeStruct((n, LANES), F32),
                   jax.ShapeDtypeStruct((LANES, 1), F32)],
        scratch_shapes=[pltpu.VMEM((LANES, 1), F32)],
        compiler_params=_cparams(("arbitrary",)),
        name="router",
    )(xa, mod, g, wr_t, br_t)


ISSUE_UNROLL = 8


def _row_copy_wait(buf_slot, sem_slot):
    pltpu.make_async_copy(buf_slot, buf_slot, sem_slot).wait()


def _scatter_kernel(p1_ref, p2_ref, x_ref, mod_ref, g_ref, xs_ref, buf, sem):
    i = pl.program_id(0)
    n_steps = pl.num_programs(0)
    slot = lax.rem(i, 2)

    def wait_slot(s):
        _row_copy_wait(buf.at[s], sem.at[s])
        _row_copy_wait(buf.at[s], sem.at[s])

    @pl.when(i >= 2)
    def _():
        wait_slot(slot)

    buf[slot] = _norm_mod(x_ref[...], g_ref[...], mod_ref[0, 3:4, :], mod_ref[0, 4:5, :])
    base = i * TM

    def body(r, carry):
        src = buf.at[slot, pl.ds(r, 1), :]
        pltpu.make_async_copy(src, xs_ref.at[pl.ds(p1_ref[base + r], 1), :], sem.at[slot]).start()
        pltpu.make_async_copy(src, xs_ref.at[pl.ds(p2_ref[base + r], 1), :], sem.at[slot]).start()
        return carry

    lax.fori_loop(0, TM, body, 0, unroll=ISSUE_UNROLL)

    @pl.when(i == n_steps - 1)
    def _():
        @pl.when(i >= 1)
        def _():
            wait_slot(1 - slot)

        wait_slot(slot)


def _scatter(p1, p2, xa, mod, g, n_tiles):
    n = n_tiles * TM
    return pl.pallas_call(
        _scatter_kernel,
        grid_spec=pltpu.PrefetchScalarGridSpec(
            num_scalar_prefetch=2,
            grid=(n_tiles,),
            in_specs=[
                pl.BlockSpec((TM, D), lambda i, p1, p2: (i, 0)),
                pl.BlockSpec((1, 6, D), lambda i, p1, p2: (_mod_row(i), 0, 0)),
                pl.BlockSpec((1, D), lambda i, p1, p2: (0, 0)),
            ],
            out_specs=pl.BlockSpec(memory_space=pl.ANY),
            scratch_shapes=[pltpu.VMEM((2, TM, D), F32), pltpu.SemaphoreType.DMA((2,))],
        ),
        out_shape=jax.ShapeDtypeStruct((2 * n, D), F32),
        compiler_params=_cparams(("arbitrary",)),
        name="moe_scatter",
    )(p1, p2, xa, mod, g)


def _grouped_kernel(tile_ref, exp_ref, lo_ref, hi_ref, first_ref, newexp_ref, valid_ref,
                    xs_ref, w1_ref, w3_ref, w2_ref, ys_ref, wb1, wb3, wb2):
    q = pl.program_id(0)

    @pl.when(valid_ref[q] == 1)
    def _():
        @pl.when(newexp_ref[q] == 1)
        def _():
            wb1[...] = w1_ref[0, 0].astype(BF16)
            wb3[...] = w3_ref[0, 0].astype(BF16)
            wb2[...] = w2_ref[0, 0].astype(BF16)

        x = xs_ref[...].astype(BF16)
        a = jnp.dot(x, wb1[...], preferred_element_type=F32)
        b = jnp.dot(x, wb3[...], preferred_element_type=F32)
        rows = lax.broadcasted_iota(jnp.int32, (TM, 1), 0)
        mine = (rows >= lo_ref[q]) & (rows < hi_ref[q])
        act = jnp.where(mine, a * jax.nn.sigmoid(a) * b, 0.0)
        y = jnp.dot(act.astype(BF16), wb2[...], preferred_element_type=F32)

        @pl.when(first_ref[q] == 1)
        def _():
            ys_ref[...] = y

        @pl.when(first_ref[q] == 0)
        def _():
            ys_ref[...] += y


def _grouped(plan, xs, w1, w3, w2, layer):
    n_pairs = plan[0].shape[0]
    tile_map = lambda q, tile, exp, *_: (tile[q], 0)
    exp_map = lambda q, tile, exp, *_: (layer, exp[q], 0, 0)
    return pl.pallas_call(
        _grouped_kernel,
        grid_spec=pltpu.PrefetchScalarGridSpec(
            num_scalar_prefetch=7,
            grid=(n_pairs,),
            in_specs=[
                pl.BlockSpec((TM, D), tile_map),
                pl.BlockSpec((1, 1, D, D_EXP), exp_map),
                pl.BlockSpec((1, 1, D, D_EXP), exp_map),
                pl.BlockSpec((1, 1, D_EXP, D), exp_map),
            ],
            out_specs=pl.BlockSpec((TM, D), tile_map),
            scratch_shapes=[pltpu.VMEM((D, D_EXP), BF16), pltpu.VMEM((D, D_EXP), BF16),
                            pltpu.VMEM((D_EXP, D), BF16)],
        ),
        out_shape=jax.ShapeDtypeStruct(xs.shape, F32),
        compiler_params=_cparams(("arbitrary",)),
        name="moe_grouped",
    )(*plan, xs, w1, w3, w2)


def _gather_kernel(p1_ref, p2_ref, x_ref, mod_ref, tm_ref, fg_ref, ys_ref, o_ref, buf, sem,
                   *, final_norm):
    i = pl.program_id(0)
    n_steps = pl.num_programs(0)
    slot = lax.rem(i, 2)

    def issue(tile, s):
        base = tile * TM

        def body(r, carry):
            pltpu.make_async_copy(ys_ref.at[pl.ds(p1_ref[base + r], 1), :],
                                  buf.at[s, 0, pl.ds(r, 1), :], sem.at[s]).start()
            pltpu.make_async_copy(ys_ref.at[pl.ds(p2_ref[base + r], 1), :],
                                  buf.at[s, 1, pl.ds(r, 1), :], sem.at[s]).start()
            return carry

        lax.fori_loop(0, TM, body, 0, unroll=ISSUE_UNROLL)

    @pl.when(i == 0)
    def _():
        issue(0, 0)

    @pl.when(i + 1 < n_steps)
    def _():
        issue(i + 1, 1 - slot)

    _row_copy_wait(buf.at[slot, 0], sem.at[slot])
    _row_copy_wait(buf.at[slot, 1], sem.at[slot])
    y = tm_ref[:, 0:1] * buf[slot, 0] + tm_ref[:, 1:2] * buf[slot, 1]
    out = x_ref[...] + mod_ref[0, 5:6, :] * y
    if final_norm:
        out = out * lax.rsqrt(jnp.mean(out * out, axis=-1, keepdims=True) + EPS) * fg_ref[...]
    o_ref[...] = out


def _gather(p1, p2, xa, mod, tmeta, final_g, ys, n_tiles, final_norm):
    return pl.pallas_call(
        functools.partial(_gather_kernel, final_norm=final_norm),
        grid_spec=pltpu.PrefetchScalarGridSpec(
            num_scalar_prefetch=2,
            grid=(n_tiles,),
            in_specs=[
                pl.BlockSpec((TM, D), lambda i, p1, p2: (i, 0)),
                pl.BlockSpec((1, 6, D), lambda i, p1, p2: (_mod_row(i), 0, 0)),
                pl.BlockSpec((TM, LANES), lambda i, p1, p2: (i, 0)),
                pl.BlockSpec((1, D), lambda i, p1, p2: (0, 0)),
                pl.BlockSpec(memory_space=pl.ANY),
            ],
            out_specs=pl.BlockSpec((TM, D), lambda i, p1, p2: (i, 0)),
            scratch_shapes=[pltpu.VMEM((2, 2, TM, D), F32), pltpu.SemaphoreType.DMA((2,))],
        ),
        out_shape=jax.ShapeDtypeStruct((n_tiles * TM, D), F32),
        compiler_params=_cparams(("arbitrary",)),
        name="moe_gather",
    )(p1, p2, xa, mod, tmeta, final_g, ys)


def _inproj_kernel(x_ref, mod_ref, g_ref, win_ref, xm_ref, op_ref):
    h = _norm_mod(x_ref[...], g_ref[...], mod_ref[0, 0:1, :], mod_ref[0, 1:2, :])
    hb = h.astype(BF16)
    xm_ref[...] = jnp.dot(hb, win_ref[:, :B_INNER], preferred_element_type=F32).astype(BF16)
    op_ref[...] = jnp.dot(hb, win_ref[:, B_INNER:], preferred_element_type=F32).astype(BF16)


def _inproj(xa, mod, g, w_in):
    const = lambda i: (0, 0)
    return pl.pallas_call(
        _inproj_kernel,
        grid=(ALL_TILES,),
        in_specs=[
            pl.BlockSpec((TM, D), lambda i: (i, 0)),
            pl.BlockSpec((1, 6, D), lambda i: (_mod_row(i), 0, 0)),
            pl.BlockSpec((1, D), const),
            pl.BlockSpec((D, 2 * B_INNER), const),
        ],
        out_specs=[pl.BlockSpec((TM, B_INNER), lambda i: (i, 0)),
                   pl.BlockSpec((TM, B_INNER), lambda i: (i, 0))],
        out_shape=[jax.ShapeDtypeStruct((N_ALL, B_INNER), BF16),
                   jax.ShapeDtypeStruct((N_ALL, B_INNER), BF16)],
        compiler_params=_cparams(("parallel",)),
        name="mlstm_inproj",
    )(xa, mod, g, w_in)


def _log_sigmoid(x):
    return jnp.minimum(x, 0.0) - jnp.log(1.0 + jnp.exp(-jnp.abs(x)))


def _qkv_kernel(xm_ref, prev_ref, next_ref, cw_ref, cb_ref, wq_ref, wkt_ref, wv_ref, wg_ref, wgt_ref,
                bg_ref, bgt_ref, xc_ref, q_ref, kt_ref, v_ref, gt_ref, gtt_ref):
    i = pl.program_id(0)
    lat = i < LAT_TILES
    first = jnp.where(lat, i % TILES_PER_SEQ == 0, True)
    last = jnp.where(lat, i % TILES_PER_SEQ == TILES_PER_SEQ - 1, True)
    xmb = xm_ref[...]
    xm = xmb.astype(F32)
    prev = jnp.where(first, 0.0, prev_ref[...].astype(F32))
    nxt = jnp.where(last, 0.0, next_ref[...].astype(F32))
    ext = jnp.concatenate([prev[HALO - 8:], xm, nxt[:8]], axis=0)
    acc = jnp.zeros((TM, B_INNER), F32) + cb_ref[...]
    for t in range(CONV_K):
        off = 8 + t - CONV_K // 2
        acc = acc + ext[off:off + TM] * cw_ref[t:t + 1, :]
    xc = acc * jax.nn.sigmoid(acc)
    xcb = xc.astype(BF16)
    xc_ref[...] = xcb
    for h in range(HEADS):
        hs = slice(h * DH, (h + 1) * DH)
        qh = jnp.dot(xcb[:, hs], wq_ref[h], preferred_element_type=F32)
        q_ref[:, h * DQK:(h + 1) * DQK] = (qh * (DQK ** -0.5)).astype(BF16)
        kt_ref[h * DQK:(h + 1) * DQK, :] = lax.dot_general(
            wkt_ref[h], xcb[:, hs], (((1,), (1,)), ((), ())), preferred_element_type=F32).astype(BF16)
        v_ref[:, hs] = jnp.dot(xmb[:, hs], wv_ref[h], preferred_element_type=F32).astype(BF16)
    gts = jnp.dot(xmb, wg_ref[...], preferred_element_type=F32) + bg_ref[...]
    lane = lax.broadcasted_iota(jnp.int32, (TM, LANES), 1)
    gt_ref[...] = jnp.where(((lane >> 3) & 1) == 1, _log_sigmoid(gts), gts)
    gtt = lax.dot_general(wgt_ref[...], xmb, (((1,), (1,)), ((), ())),
                          preferred_element_type=F32) + bgt_ref[...]
    row = lax.broadcasted_iota(jnp.int32, (4 * HEADS, TM), 0)
    gtt_ref[...] = jnp.where(((row >> 3) & 1) == 1, _log_sigmoid(gtt), gtt)


def _qkv(xm, conv_w, conv_b, wq, wkt, wv, wg, wgt, bg, bgt):
    const = lambda i: (0, 0)
    const3 = lambda i: (0, 0, 0)
    hb = TM // HALO
    n_hb = N_ALL // HALO
    return pl.pallas_call(
        _qkv_kernel,
        grid=(ALL_TILES,),
        in_specs=[
            pl.BlockSpec((TM, B_INNER), lambda i: (i, 0)),
            pl.BlockSpec((HALO, B_INNER), lambda i: (jnp.maximum(i * hb - 1, 0), 0)),
            pl.BlockSpec((HALO, B_INNER), lambda i: (jnp.minimum((i + 1) * hb, n_hb - 1), 0)),
            pl.BlockSpec((CONV_K, B_INNER), const),
            pl.BlockSpec((1, B_INNER), const),
            pl.BlockSpec((HEADS, DH, DQK), const3),
            pl.BlockSpec((HEADS, DQK, DH), const3),
            pl.BlockSpec((HEADS, DH, DV), const3),
            pl.BlockSpec((B_INNER, LANES), const),
            pl.BlockSpec((4 * HEADS, B_INNER), const),
            pl.BlockSpec((1, LANES), const),
            pl.BlockSpec((4 * HEADS, 1), const),
        ],
        out_specs=[
            pl.BlockSpec((TM, B_INNER), lambda i: (i, 0)),
            pl.BlockSpec((TM, HEADS * DQK), lambda i: (i, 0)),
            pl.BlockSpec((HEADS * DQK, TM), lambda i: (0, i)),
            pl.BlockSpec((TM, B_INNER), lambda i: (i, 0)),
            pl.BlockSpec((TM, LANES), lambda i: (i, 0)),
            pl.BlockSpec((4 * HEADS, TM), lambda i: (0, i)),
        ],
        out_shape=[
            jax.ShapeDtypeStruct((N_ALL, B_INNER), BF16),
            jax.ShapeDtypeStruct((N_ALL, HEADS * DQK), BF16),
            jax.ShapeDtypeStruct((HEADS * DQK, N_ALL), BF16),
            jax.ShapeDtypeStruct((N_ALL, B_INNER), BF16),
            jax.ShapeDtypeStruct((N_ALL, LANES), F32),
            jax.ShapeDtypeStruct((4 * HEADS, N_ALL), F32),
        ],
        compiler_params=_cparams(("parallel",)),
        name="mlstm_qkv",
    )(xm, xm, xm, conv_w, conv_b, wq, wkt, wv, wg, wgt, bg, bgt)


CTX_CHUNKS = CTX // CHUNK
LAT_CHUNKS = SEQ // CHUNK
SCAN_STEPS = CTX_CHUNKS + LAT_CHUNKS


def _scan_dir(d, q_ref, kt_ref, v_ref, g_ref, gt_ref, o_ref, c_scr, m_scr):
    r = lax.broadcasted_iota(jnp.int32, (CHUNK, CHUNK), 0)
    c = lax.broadcasted_iota(jnp.int32, (CHUNK, CHUNK), 1)
    if d == 0:
        keep = c <= r
    else:
        keep = c >= r
    tri_col = keep.astype(F32)
    tri_row = (r <= c).astype(F32) if d == 0 else (r >= c).astype(F32)
    gts = g_ref[...]
    gtt = gt_ref[...]
    hi = lax.Precision.HIGHEST
    bcol_all = jnp.dot(tri_col, gts, preferred_element_type=F32, precision=hi)
    brow_all = jnp.dot(gtt, tri_row, preferred_element_type=F32, precision=hi)
    tot_all = jnp.dot(gtt, jnp.ones((CHUNK, LANES), F32), preferred_element_type=F32, precision=hi)
    ones_blk = jnp.ones((CHUNK, LANES), BF16)
    base = 2 * HEADS * d
    for h in range(HEADS):
        li_r = gtt[base + h:base + h + 1, :]
        b_r = brow_all[base + HEADS + h:base + HEADS + h + 1, :]
        b_last = tot_all[base + HEADS + h:base + HEADS + h + 1, :]
        b_full = jnp.broadcast_to(bcol_all[:, base + HEADS + h:base + HEADS + h + 1], (CHUNK, LANES))
        m_old = m_scr[d, h]
        c_old = c_scr[d, h]
        qh = q_ref[:, h * DQK:(h + 1) * DQK]
        kth = kt_ref[h * DQK:(h + 1) * DQK, :]
        vaug = jnp.concatenate([v_ref[:, h * DV:(h + 1) * DV], ones_blk], axis=1)
        g_r = b_last - b_r + li_r
        m_new = jnp.maximum(b_last + m_old, jnp.max(g_r, axis=-1, keepdims=True))
        decay = jnp.exp(b_last + m_old - m_new)
        wk = jnp.exp(g_r - m_new)
        kwt = (kth.astype(F32) * wk).astype(BF16)
        c_scr[d, h] = jnp.concatenate([decay] * 3, axis=1) * c_old + jnp.dot(
            kwt, vaug, preferred_element_type=F32)
        m_scr[d, h] = m_new

        dmat = jnp.where(keep, b_full - (b_r - li_r), NEG_INF)
        inter = b_full + m_old
        m_t = jnp.maximum(inter, jnp.max(dmat, axis=-1, keepdims=True))
        p = jnp.exp(dmat - m_t)
        s = jnp.dot(qh, kth, preferred_element_type=F32) * p
        a = jnp.exp(inter - m_t)
        num = jnp.concatenate([a] * 3, axis=1) * jnp.dot(
            qh, c_old.astype(BF16), preferred_element_type=F32) + jnp.dot(
                s.astype(BF16), vaug, preferred_element_type=F32)
        inv = 1.0 / jnp.maximum(jnp.abs(num[:, DV:]), jnp.exp(-m_t))
        o_ref[:, h * DV:(h + 1) * DV] = (num[:, :DV] * jnp.concatenate([inv] * 2, axis=1)).astype(BF16)


def _scan_kernel(qf, kf, vf, gf, gtf, qb, kb, vb, gb, gtb, of, ob, c_scr, m_scr):
    j = pl.program_id(1)

    @pl.when(j == 0)
    def _():
        c_scr[...] = jnp.zeros_like(c_scr)
        m_scr[...] = jnp.zeros_like(m_scr)

    _scan_dir(0, qf, kf, vf, gf, gtf, of, c_scr, m_scr)
    _scan_dir(1, qb, kb, vb, gb, gtb, ob, c_scr, m_scr)


def _scan(q, kt, v, gts, gtt):
    lat_blk = N_LAT // CHUNK

    def fwd_in(b, j):
        return jnp.where(j < CTX_CHUNKS, lat_blk + b * CTX_CHUNKS + j, b * LAT_CHUNKS + j - CTX_CHUNKS)

    def bwd_in(b, j):
        return jnp.where(j < CTX_CHUNKS, lat_blk + b * CTX_CHUNKS + (CTX_CHUNKS - 1 - j),
                         b * LAT_CHUNKS + (SCAN_STEPS - 1 - j))

    def fwd_out(b, j):
        return b * LAT_CHUNKS + jnp.maximum(j - CTX_CHUNKS, 0)

    def bwd_out(b, j):
        return b * LAT_CHUNKS + jnp.minimum(SCAN_STEPS - 1 - j, LAT_CHUNKS - 1)

    def specs(fn):
        return [
            pl.BlockSpec((CHUNK, HEADS * DQK), lambda b, j: (fn(b, j), 0)),
            pl.BlockSpec((HEADS * DQK, CHUNK), lambda b, j: (0, fn(b, j))),
            pl.BlockSpec((CHUNK, B_INNER), lambda b, j: (fn(b, j), 0)),
            pl.BlockSpec((CHUNK, LANES), lambda b, j: (fn(b, j), 0)),
            pl.BlockSpec((4 * HEADS, CHUNK), lambda b, j: (0, fn(b, j))),
        ]

    return pl.pallas_call(
        _scan_kernel,
        grid=(BATCH, SCAN_STEPS),
        in_specs=specs(fwd_in) + specs(bwd_in),
        out_specs=[pl.BlockSpec((CHUNK, B_INNER), lambda b, j: (fwd_out(b, j), 0)),
                   pl.BlockSpec((CHUNK, B_INNER), lambda b, j: (bwd_out(b, j), 0))],
        out_shape=[jax.ShapeDtypeStruct((N_LAT, B_INNER), BF16),
                   jax.ShapeDtypeStruct((N_LAT, B_INNER), BF16)],
        scratch_shapes=[pltpu.VMEM((2, HEADS, DQK, DV + LANES), F32),
                        pltpu.VMEM((2, HEADS, 1, LANES), F32)],
        compiler_params=_cparams(("parallel", "arbitrary")),
        name="mlstm_scan",
    )(q, kt, v, gts, gtt, q, kt, v, gts, gtt)


def _combine_kernel(x_ref, mod_ref, hf_ref, hb_ref, xc_ref, op_ref, hg_ref, sk_ref, wout_ref,
                    o_ref, y_scr):
    for h in range(HEADS):
        hs = slice(h * DV, (h + 1) * DV)
        s = hf_ref[:, hs].astype(F32) + hb_ref[:, hs].astype(F32)
        s = s * lax.rsqrt(jnp.mean(s * s, axis=-1, keepdims=True) + EPS)
        y = jax.nn.sigmoid(op_ref[:, hs].astype(F32)) * (
            s * hg_ref[:, hs] + sk_ref[:, hs] * xc_ref[:, hs].astype(F32))
        y_scr[:, hs] = y.astype(BF16)
    out = jnp.dot(y_scr[...], wout_ref[...], preferred_element_type=F32)
    o_ref[...] = x_ref[...] + mod_ref[0, 2:3, :] * out


def _combine(xa, mod, hf, hb, xc, op, head_g, skip, w_out):
    const = lambda i: (0, 0)
    row = lambda i: (i, 0)
    return pl.pallas_call(
        _combine_kernel,
        grid=(LAT_TILES,),
        in_specs=[
            pl.BlockSpec((TM, D), row),
            pl.BlockSpec((1, 6, D), lambda i: (_mod_row(i), 0, 0)),
            pl.BlockSpec((TM, B_INNER), row),
            pl.BlockSpec((TM, B_INNER), row),
            pl.BlockSpec((TM, B_INNER), row),
            pl.BlockSpec((TM, B_INNER), row),
            pl.BlockSpec((1, B_INNER), const),
            pl.BlockSpec((1, B_INNER), const),
            pl.BlockSpec((B_INNER, D), const),
        ],
        out_specs=pl.BlockSpec((TM, D), row),
        out_shape=jax.ShapeDtypeStruct((N_LAT, D), F32),
        scratch_shapes=[pltpu.VMEM((TM, B_INNER), BF16)],
        compiler_params=_cparams(("parallel",)),
        name="mlstm_combine",
    )(xa, mod, hf, hb, xc, op, head_g, skip, w_out)


def _router_weights(w_grp, b_grp, w_exp, b_exp):
    wr = jnp.zeros((LANES, D), F32).at[:N_EXP].set(w_exp.T).at[N_EXP:N_EXP + N_GRP].set(w_grp.T)
    br = jnp.zeros((LANES, 1), F32).at[:N_EXP, 0].set(b_exp).at[N_EXP:N_EXP + N_GRP, 0].set(b_grp)
    return wr, br


def _route_plan(meta, cnt, n_rows):
    i32 = jnp.int32
    n_sorted_tiles = 2 * n_rows // TM
    n_pairs = n_sorted_tiles + N_EXP - 1
    counts = cnt[:N_EXP, 0].astype(i32)
    ends = jnp.cumsum(counts)
    starts = ends - counts
    id1, id2, r1, r2 = (meta[j].astype(i32) for j in range(4))
    p1 = starts[id1] + r1
    p2 = starts[id2] + r2
    first_tile = starts // TM
    pairs_e = jnp.where(counts > 0, (ends - 1) // TM - first_tile + 1, 0)
    pend = jnp.cumsum(pairs_e)
    pstart = pend - pairs_e
    total = pend[-1]
    q = jnp.arange(n_pairs, dtype=i32)
    qc = jnp.maximum(jnp.minimum(q, total - 1), 0)
    e_q = jnp.sum((qc[:, None] >= pend[None, :]).astype(i32), axis=1)
    tile_q = first_tile[e_q] + qc - pstart[e_q]
    lo = jnp.clip(starts[e_q] - tile_q * TM, 0, TM)
    hi = jnp.clip(ends[e_q] - tile_q * TM, 0, TM)
    valid = (q < total).astype(i32)
    first = (tile_q != jnp.concatenate([jnp.full((1,), -1, i32), tile_q[:-1]])).astype(i32)
    newexp = (e_q != jnp.concatenate([jnp.full((1,), -1, i32), e_q[:-1]])).astype(i32)
    return p1, p2, (tile_q, e_q, lo, hi, first, newexp, valid)


def _moe(xa, mod, g2, w_grp, b_grp, w_exp, b_exp, w1, w3, w2, layer, final_g, n_rows, final_norm):
    wr, br = _router_weights(w_grp, b_grp, w_exp, b_exp)
    n_tiles = n_rows // TM
    meta, tmeta, cnt = _router(xa, mod, g2, wr, br, n_tiles)
    p1, p2, plan = _route_plan(meta, cnt, n_rows)
    xs = _scatter(p1, p2, xa, mod, g2, n_tiles)
    ys = _grouped(plan, xs, w1, w3, w2, layer)
    return _gather(p1, p2, xa, mod, tmeta, final_g.reshape(1, D), ys, n_tiles, final_norm)


def kernel(x, c, ctx, c_ctx, norm_g, w_ada, b_ada, a_w_in, a_b_in, a_g_v, a_w_s, a_b_s, a_w_out,
           b_w_in, b_conv_w, b_conv_b, b_w_q, b_w_k, b_w_v, b_w_gate, b_b_gate, b_head_g, b_skip,
           b_w_out, moe_w_grp, moe_b_grp, moe_w_exp, moe_b_exp, moe_w1, moe_w3, moe_w2, final_g):
    cc = jnp.zeros((MOD_ROWS, D), F32).at[:BATCH].set(c).at[BATCH].set(c_ctx)
    mods = _ada(cc, w_ada, b_ada).reshape(2, MOD_ROWS, 6, D)

    mod = mods[0]
    b_s_full = jnp.repeat(a_b_s[0].T, A_GC, axis=1)
    xa = _gmlp(x.reshape(N_LAT, D), ctx.reshape(N_CTX, D), mod, norm_g[0, 0].reshape(1, D), a_w_in[0].astype(BF16),
               a_b_in[0].reshape(1, -1), a_g_v[0].reshape(1, -1), a_w_s[0].astype(BF16),
               b_s_full, a_w_out[0].astype(BF16))
    xa = _moe(xa, mod, norm_g[0, 1].reshape(1, D), moe_w_grp[0], moe_b_grp[0], moe_w_exp[0],
              moe_b_exp[0], moe_w1, moe_w3, moe_w2, 0, final_g, N_ALL, False)

    mod = mods[1]
    xm, op = _inproj(xa, mod, norm_g[1, 0].reshape(1, D), b_w_in[0].astype(BF16))
    wg = jnp.zeros((B_INNER, LANES), F32).at[:, :4 * HEADS].set(b_w_gate[0]).astype(BF16)
    bg = jnp.zeros((1, LANES), F32).at[0, :4 * HEADS].set(b_b_gate[0])
    xc, q, kt, v, gts, gtt = _qkv(xm, b_conv_w[0], b_conv_b[0].reshape(1, -1),
                                  b_w_q[0].astype(BF16),
                                  jnp.transpose(b_w_k[0], (0, 2, 1)).astype(BF16),
                                  b_w_v[0].astype(BF16), wg, b_w_gate[0].T.astype(BF16), bg,
                                  b_b_gate[0].reshape(-1, 1))
    hf, hb = _scan(q, kt, v, gts, gtt)
    xl = _combine(xa, mod, hf, hb, xc, op, b_head_g[0].reshape(1, -1), b_skip[0].reshape(1, -1),
                  b_w_out[0].astype(BF16))
    out = _moe(xl, mod, norm_g[1, 1].reshape(1, D), moe_w_grp[1], moe_b_grp[1], moe_w_exp[1],
               moe_b_exp[1], moe_w1, moe_w3, moe_w2, 1, final_g, N_LAT, True)
    return out.reshape(BATCH, SEQ, D)
```

```python
import functools

import jax
import jax.numpy as jnp
from jax import lax
from jax.experimental import pallas as pl
from jax.experimental.pallas import tpu as pltpu

F32 = jnp.float32
BF16 = jnp.bfloat16

D = 1024
BATCH = 8
SEQ = 2048
CTX = 256
EPS = 1e-6
NEG_INF = -1e30
N_LAT = BATCH * SEQ
N_CTX = BATCH * CTX
N_ALL = N_LAT + N_CTX

TM = 256
LAT_TILES = N_LAT // TM
ALL_TILES = N_ALL // TM
TILES_PER_SEQ = SEQ // TM
MOD_ROWS = 16

CHUNK = 128
A_HALF = 2048
A_GROUPS = 8
A_GC = A_HALF // A_GROUPS
B_INNER = 2048
HEADS = 8
DH = B_INNER // HEADS
DQK = DH // 2
DV = DH
CONV_K = 5
HALO = 16
N_EXP = 32
N_GRP = 4
EXP_PER_GRP = 8
D_EXP = 512
LANES = 128

VMEM_LIMIT = 56 * 1024 * 1024


def _cparams(sem):
    return pltpu.CompilerParams(dimension_semantics=sem, vmem_limit_bytes=VMEM_LIMIT)


def _mod_row(i):
    return jnp.where(i < LAT_TILES, i // TILES_PER_SEQ, BATCH)


def _norm_mod(x, g, shift, scale):
    y = x * lax.rsqrt(jnp.mean(x * x, axis=-1, keepdims=True) + EPS) * g
    return y * (1.0 + scale) + shift


def _bf16_terms(x, n):
    terms = []
    r = x
    for _ in range(n):
        t = r.astype(BF16)
        terms.append(t)
        r = r - t.astype(F32)
    return terms


def _dot_nt(a, b):
    return lax.dot_general(a, b, (((1,), (1,)), ((), ())), preferred_element_type=F32)


def _dot_nn(a, b):
    return jnp.dot(a, b, preferred_element_type=F32)


def _dot3(a, b, dot=_dot_nn):
    a1, a2 = _bf16_terms(a, 2)
    b1, b2 = _bf16_terms(b, 2)
    return dot(a1, b1) + (dot(a1, b2) + dot(a2, b1))


def _dot_exact_lhs(a_bf16, b, dot=_dot_nn):
    return sum(dot(a_bf16, t) for t in _bf16_terms(b, 3))


def _dot_exact_rhs(a, b_bf16, dot=_dot_nn):
    return sum(dot(t, b_bf16) for t in _bf16_terms(a, 3))


def _gelu_tanh(x):
    return 0.5 * x * (1.0 + jnp.tanh(0.7978845608028654 * (x + 0.044715 * (x * x * x))))


ADA_BN = 1536


def _ada_kernel(c_ref, w_ref, b_ref, o_ref):
    c = c_ref[...]
    a = c * jax.nn.sigmoid(c)
    o_ref[0] = _dot3(a, w_ref[0]) + b_ref[0]


def _ada(cc, w_ada, b_ada):
    depth = w_ada.shape[0]
    return pl.pallas_call(
        _ada_kernel,
        grid=(depth, 6 * D // ADA_BN),
        in_specs=[
            pl.BlockSpec((MOD_ROWS, D), lambda l, j: (0, 0)),
            pl.BlockSpec((1, D, ADA_BN), lambda l, j: (l, 0, j)),
            pl.BlockSpec((1, 1, ADA_BN), lambda l, j: (l, 0, j)),
        ],
        out_specs=pl.BlockSpec((1, MOD_ROWS, ADA_BN), lambda l, j: (l, 0, j)),
        out_shape=jax.ShapeDtypeStruct((depth, MOD_ROWS, 6 * D), F32),
        compiler_params=_cparams(("parallel", "parallel")),
        name="ada",
    )(cc, w_ada, b_ada.reshape(depth, 1, 6 * D))


GM_CH = 512


def _gmlp_kernel(xl_ref, xc_ref, mod_ref, g_ref, win_ref, bin_ref, gv_ref, ws_ref, bs_ref, wout_ref,
                 o_ref, z_scr, y_scr):
    x = jnp.where(pl.program_id(0) < LAT_TILES, xl_ref[...], xc_ref[...])
    h = _norm_mod(x, g_ref[...], mod_ref[0, 0:1, :], mod_ref[0, 1:2, :])
    hb = h.astype(BF16)
    s1 = jnp.zeros((TM, 1), F32)
    s2 = jnp.zeros((TM, 1), F32)
    for j in range(2 * A_HALF // GM_CH):
        cs = slice(j * GM_CH, (j + 1) * GM_CH)
        zc = jnp.dot(hb, win_ref[:, cs], preferred_element_type=F32) + bin_ref[:, cs]
        zc = _gelu_tanh(zc)
        z_scr[:, cs] = zc
        if j * GM_CH >= A_HALF:
            s1 = s1 + jnp.sum(zc, axis=-1, keepdims=True)
            s2 = s2 + jnp.sum(zc * zc, axis=-1, keepdims=True)
    mu = s1 * (1.0 / A_HALF)
    rstd = lax.rsqrt(s2 * (1.0 / A_HALF) - mu * mu + EPS)
    for c in range(TM // CHUNK):
        rs = slice(c * CHUNK, (c + 1) * CHUNK)
        for g in range(A_GROUPS):
            cs = slice(g * A_GC, (g + 1) * A_GC)
            vs = slice(A_HALF + g * A_GC, A_HALF + (g + 1) * A_GC)
            v = (z_scr[rs, vs] - mu[rs]) * rstd[rs] * gv_ref[:, cs]
            s = jnp.dot(ws_ref[g], v.astype(BF16), preferred_element_type=F32) + bs_ref[:, cs]
            y_scr[rs, cs] = (z_scr[rs, cs] * s).astype(BF16)
    out = jnp.dot(y_scr[...], wout_ref[...], preferred_element_type=F32)
    o_ref[...] = x + mod_ref[0, 2:3, :] * out


def _gmlp(xl, xc, mod, g, w_in, b_in, g_v, w_s, b_s_full, w_out):
    const = lambda i: (0, 0)
    return pl.pallas_call(
        _gmlp_kernel,
        grid=(ALL_TILES,),
        in_specs=[
            pl.BlockSpec((TM, D), lambda i: (jnp.minimum(i, LAT_TILES - 1), 0)),
            pl.BlockSpec((TM, D), lambda i: (jnp.maximum(i - LAT_TILES, 0), 0)),
            pl.BlockSpec((1, 6, D), lambda i: (_mod_row(i), 0, 0)),
            pl.BlockSpec((1, D), const),
            pl.BlockSpec((D, 2 * A_HALF), const),
            pl.BlockSpec((1, 2 * A_HALF), const),
            pl.BlockSpec((1, A_HALF), const),
            pl.BlockSpec((A_GROUPS, CHUNK, CHUNK), lambda i: (0, 0, 0)),
            pl.BlockSpec((CHUNK, A_HALF), const),
            pl.BlockSpec((A_HALF, D), const),
        ],
        out_specs=pl.BlockSpec((TM, D), lambda i: (i, 0)),
        out_shape=jax.ShapeDtypeStruct((N_ALL, D), F32),
        scratch_shapes=[pltpu.VMEM((TM, 2 * A_HALF), F32), pltpu.VMEM((TM, A_HALF), BF16)],
        compiler_params=_cparams(("parallel",)),
        name="gmlp",
    )(xl, xc, mod, g, w_in, b_in, g_v, w_s, b_s_full, w_out)


def _router_kernel(x_ref, mod_ref, g_ref, wr_ref, br_ref, meta_ref, tmeta_ref, cnt_ref, cntr_ref,
                   carry, carry_r):
    @pl.when(pl.program_id(0) == 0)
    def _():
        carry[...] = jnp.zeros_like(carry)
        carry_r[...] = jnp.zeros_like(carry_r)

    x = x_ref[...]
    h = _norm_mod(x, g_ref[...], mod_ref[0, 3:4, :], mod_ref[0, 4:5, :])
    lt = _dot3(wr_ref[...], h, _dot_nt) + br_ref[...]
    e_t = lt[0:N_EXP]
    row8 = lax.broadcasted_iota(jnp.int32, (EXP_PER_GRP, TM), 0).astype(F32)
    g_t = jnp.where(row8 < N_GRP, lt[N_EXP:N_EXP + EXP_PER_GRP], -jnp.inf)
    gmax = jnp.max(g_t, axis=0, keepdims=True)
    p_g = 1.0 / jnp.sum(jnp.exp(g_t - gmax), axis=0, keepdims=True)
    g_idx = jnp.min(jnp.where(g_t == gmax, row8, float(EXP_PER_GRP)), axis=0, keepdims=True)
    sel = jnp.zeros((EXP_PER_GRP, TM), F32)
    for g in range(N_GRP):
        sel = sel + jnp.where(g_idx == g, e_t[g * EXP_PER_GRP:(g + 1) * EXP_PER_GRP], 0.0)
    m1 = jnp.max(sel, axis=0, keepdims=True)
    i1 = jnp.min(jnp.where(sel == m1, row8, float(EXP_PER_GRP)), axis=0, keepdims=True)
    sel2 = jnp.where(row8 == i1, -jnp.inf, sel)
    m2 = jnp.max(sel2, axis=0, keepdims=True)
    i2 = jnp.min(jnp.where(sel2 == m2, row8, float(EXP_PER_GRP)), axis=0, keepdims=True)
    e2 = jnp.exp(m2 - m1)
    w1 = p_g / (1.0 + e2)
    w2 = p_g * e2 / (1.0 + e2)
    row = lax.broadcasted_iota(jnp.int32, (LANES, TM), 0).astype(F32)
    id1 = g_idx * EXP_PER_GRP + i1
    id2 = g_idx * EXP_PER_GRP + i2
    oh1 = row == id1
    oh2 = row == id2
    oh = jnp.where(oh1, 1.0, 0.0) + jnp.where(oh2, 1.0, 0.0)
    before = (lax.broadcasted_iota(jnp.int32, (TM, TM), 0)
              < lax.broadcasted_iota(jnp.int32, (TM, TM), 1)).astype(BF16)
    ohb = oh.astype(BF16)
    tot = jnp.dot(ohb, before, preferred_element_type=F32) + carry[...]
    r1 = jnp.sum(jnp.where(oh1, tot, 0.0), axis=0, keepdims=True)
    r2 = jnp.sum(jnp.where(oh2, tot, 0.0), axis=0, keepdims=True)
    carry[...] += jnp.sum(oh, axis=-1, keepdims=True)
    carry_r[...] += _dot_nt(jnp.ones((8, TM), BF16), ohb)
    cnt_ref[...] = carry[...]
    cntr_ref[...] = carry_r[...]
    meta_ref[0:1, :] = id1
    meta_ref[1:2, :] = id2
    meta_ref[2:3, :] = r1
    meta_ref[3:4, :] = r2
    meta_ref[4:8, :] = jnp.zeros((4, TM), F32)
    tmeta_ref[...] = (jnp.where(row == 0.0, w1, 0.0) + jnp.where(row == 1.0, w2, 0.0)).T


def _router(xa, mod, g, wr_t, br_t, n_tiles):
    const = lambda i: (0, 0)
    n = n_tiles * TM
    return pl.pallas_call(
        _router_kernel,
        grid=(n_tiles,),
        in_specs=[
            pl.BlockSpec((TM, D), lambda i: (i, 0)),
            pl.BlockSpec((1, 6, D), lambda i: (_mod_row(i), 0, 0)),
            pl.BlockSpec((1, D), const),
            pl.BlockSpec((LANES, D), const),
            pl.BlockSpec((LANES, 1), const),
        ],
        out_specs=[pl.BlockSpec((8, TM), lambda i: (0, i)),
                   pl.BlockSpec((TM, LANES), lambda i: (i, 0)),
                   pl.BlockSpec((LANES, 1), const),
                   pl.BlockSpec((8, LANES), const)],
        out_shape=[jax.ShapeDtypeStruct((8, n), F32), jax.ShapeDtypeStruct((n, LANES), F32),
                   jax.ShapeDtypeStruct((LANES, 1), F32), jax.ShapeDtypeStruct((8, LANES), F32)],
        scratch_shapes=[pltpu.VMEM((LANES, 1), F32), pltpu.VMEM((8, LANES), F32)],
        compiler_params=_cparams(("arbitrary",)),
        name="router",
    )(xa, mod, g, wr_t, br_t)


ISSUE_UNROLL = 8


def _row_copy_wait(buf_slot, sem_slot):
    pltpu.make_async_copy(buf_slot, buf_slot, sem_slot).wait()


def _scatter_kernel(p1_ref, p2_ref, x_ref, mod_ref, g_ref, xs_ref, buf, sem):
    i = pl.program_id(0)
    n_steps = pl.num_programs(0)
    slot = lax.rem(i, 2)

    def wait_slot(s):
        _row_copy_wait(buf.at[s], sem.at[s])
        _row_copy_wait(buf.at[s], sem.at[s])

    @pl.when(i >= 2)
    def _():
        wait_slot(slot)

    buf[slot] = _norm_mod(x_ref[...], g_ref[...], mod_ref[0, 3:4, :], mod_ref[0, 4:5, :])
    base = i * TM

    def body(r, carry):
        src = buf.at[slot, pl.ds(r, 1), :]
        pltpu.make_async_copy(src, xs_ref.at[pl.ds(p1_ref[base + r], 1), :], sem.at[slot]).start()
        pltpu.make_async_copy(src, xs_ref.at[pl.ds(p2_ref[base + r], 1), :], sem.at[slot]).start()
        return carry

    lax.fori_loop(0, TM, body, 0, unroll=ISSUE_UNROLL)

    @pl.when(i == n_steps - 1)
    def _():
        @pl.when(i >= 1)
        def _():
            wait_slot(1 - slot)

        wait_slot(slot)


def _scatter(p1, p2, xa, mod, g, n_tiles):
    n = n_tiles * TM
    return pl.pallas_call(
        _scatter_kernel,
        grid_spec=pltpu.PrefetchScalarGridSpec(
            num_scalar_prefetch=2,
            grid=(n_tiles,),
            in_specs=[
                pl.BlockSpec((TM, D), lambda i, p1, p2: (i, 0)),
                pl.BlockSpec((1, 6, D), lambda i, p1, p2: (_mod_row(i), 0, 0)),
                pl.BlockSpec((1, D), lambda i, p1, p2: (0, 0)),
            ],
            out_specs=pl.BlockSpec(memory_space=pl.ANY),
            scratch_shapes=[pltpu.VMEM((2, TM, D), F32), pltpu.SemaphoreType.DMA((2,))],
        ),
        out_shape=jax.ShapeDtypeStruct((2 * n, D), F32),
        compiler_params=_cparams(("arbitrary",)),
        name="moe_scatter",
    )(p1, p2, xa, mod, g)


def _grouped_kernel(tile_ref, exp_ref, lo_ref, hi_ref, first_ref, newexp_ref, valid_ref,
                    xs_ref, w1_ref, w3_ref, w2_ref, ys_ref, wb1, wb3, wb2):
    q = pl.program_id(0)

    @pl.when(valid_ref[q] == 1)
    def _():
        @pl.when(newexp_ref[q] == 1)
        def _():
            wb1[...] = w1_ref[0, 0].astype(BF16)
            wb3[...] = w3_ref[0, 0].astype(BF16)
            wb2[...] = w2_ref[0, 0].astype(BF16)

        x = xs_ref[...].astype(BF16)
        a = jnp.dot(x, wb1[...], preferred_element_type=F32)
        b = jnp.dot(x, wb3[...], preferred_element_type=F32)
        rows = lax.broadcasted_iota(jnp.int32, (TM, 1), 0)
        mine = (rows >= lo_ref[q]) & (rows < hi_ref[q])
        act = jnp.where(mine, a * jax.nn.sigmoid(a) * b, 0.0)
        y = jnp.dot(act.astype(BF16), wb2[...], preferred_element_type=F32)

        @pl.when(first_ref[q] == 1)
        def _():
            ys_ref[...] = y

        @pl.when(first_ref[q] == 0)
        def _():
            ys_ref[...] += y


def _grouped(plan, xs, w1, w3, w2, layer):
    n_pairs = plan[0].shape[0]
    tile_map = lambda q, tile, exp, *_: (tile[q], 0)
    exp_map = lambda q, tile, exp, *_: (layer, exp[q], 0, 0)
    return pl.pallas_call(
        _grouped_kernel,
        grid_spec=pltpu.PrefetchScalarGridSpec(
            num_scalar_prefetch=7,
            grid=(n_pairs,),
            in_specs=[
                pl.BlockSpec((TM, D), tile_map),
                pl.BlockSpec((1, 1, D, D_EXP), exp_map),
                pl.BlockSpec((1, 1, D, D_EXP), exp_map),
                pl.BlockSpec((1, 1, D_EXP, D), exp_map),
            ],
            out_specs=pl.BlockSpec((TM, D), tile_map),
            scratch_shapes=[pltpu.VMEM((D, D_EXP), BF16), pltpu.VMEM((D, D_EXP), BF16),
                            pltpu.VMEM((D_EXP, D), BF16)],
        ),
        out_shape=jax.ShapeDtypeStruct(xs.shape, F32),
        compiler_params=_cparams(("arbitrary",)),
        name="moe_grouped",
    )(*plan, xs, w1, w3, w2)


def _gather_kernel(p1_ref, p2_ref, x_ref, mod_ref, tm_ref, fg_ref, ys_ref, o_ref, buf, sem,
                   *, final_norm):
    i = pl.program_id(0)
    n_steps = pl.num_programs(0)
    slot = lax.rem(i, 2)

    def issue(tile, s):
        base = tile * TM

        def body(r, carry):
            pltpu.make_async_copy(ys_ref.at[pl.ds(p1_ref[base + r], 1), :],
                                  buf.at[s, 0, pl.ds(r, 1), :], sem.at[s]).start()
            pltpu.make_async_copy(ys_ref.at[pl.ds(p2_ref[base + r], 1), :],
                                  buf.at[s, 1, pl.ds(r, 1), :], sem.at[s]).start()
            return carry

        lax.fori_loop(0, TM, body, 0, unroll=ISSUE_UNROLL)

    @pl.when(i == 0)
    def _():
        issue(0, 0)

    @pl.when(i + 1 < n_steps)
    def _():
        issue(i + 1, 1 - slot)

    _row_copy_wait(buf.at[slot, 0], sem.at[slot])
    _row_copy_wait(buf.at[slot, 1], sem.at[slot])
    y = tm_ref[:, 0:1] * buf[slot, 0] + tm_ref[:, 1:2] * buf[slot, 1]
    out = x_ref[...] + mod_ref[0, 5:6, :] * y
    if final_norm:
        out = out * lax.rsqrt(jnp.mean(out * out, axis=-1, keepdims=True) + EPS) * fg_ref[...]
    o_ref[...] = out


def _gather(p1, p2, xa, mod, tmeta, final_g, ys, n_tiles, final_norm):
    return pl.pallas_call(
        functools.partial(_gather_kernel, final_norm=final_norm),
        grid_spec=pltpu.PrefetchScalarGridSpec(
            num_scalar_prefetch=2,
            grid=(n_tiles,),
            in_specs=[
                pl.BlockSpec((TM, D), lambda i, p1, p2: (i, 0)),
                pl.BlockSpec((1, 6, D), lambda i, p1, p2: (_mod_row(i), 0, 0)),
                pl.BlockSpec((TM, LANES), lambda i, p1, p2: (i, 0)),
                pl.BlockSpec((1, D), lambda i, p1, p2: (0, 0)),
                pl.BlockSpec(memory_space=pl.ANY),
            ],
            out_specs=pl.BlockSpec((TM, D), lambda i, p1, p2: (i, 0)),
            scratch_shapes=[pltpu.VMEM((2, 2, TM, D), F32), pltpu.SemaphoreType.DMA((2,))],
        ),
        out_shape=jax.ShapeDtypeStruct((n_tiles * TM, D), F32),
        compiler_params=_cparams(("arbitrary",)),
        name="moe_gather",
    )(p1, p2, xa, mod, tmeta, final_g, ys)


def _inproj_kernel(x_ref, mod_ref, g_ref, win_ref, xm_ref, op_ref):
    h = _norm_mod(x_ref[...], g_ref[...], mod_ref[0, 0:1, :], mod_ref[0, 1:2, :])
    hb = h.astype(BF16)
    xm_ref[...] = jnp.dot(hb, win_ref[:, :B_INNER], preferred_element_type=F32).astype(BF16)
    op_ref[...] = jnp.dot(hb, win_ref[:, B_INNER:], preferred_element_type=F32).astype(BF16)


def _inproj(xa, mod, g, w_in):
    const = lambda i: (0, 0)
    return pl.pallas_call(
        _inproj_kernel,
        grid=(ALL_TILES,),
        in_specs=[
            pl.BlockSpec((TM, D), lambda i: (i, 0)),
            pl.BlockSpec((1, 6, D), lambda i: (_mod_row(i), 0, 0)),
            pl.BlockSpec((1, D), const),
            pl.BlockSpec((D, 2 * B_INNER), const),
        ],
        out_specs=[pl.BlockSpec((TM, B_INNER), lambda i: (i, 0)),
                   pl.BlockSpec((TM, B_INNER), lambda i: (i, 0))],
        out_shape=[jax.ShapeDtypeStruct((N_ALL, B_INNER), BF16),
                   jax.ShapeDtypeStruct((N_ALL, B_INNER), BF16)],
        compiler_params=_cparams(("parallel",)),
        name="mlstm_inproj",
    )(xa, mod, g, w_in)


def _log_sigmoid(x):
    return jnp.minimum(x, 0.0) - jnp.log(1.0 + jnp.exp(-jnp.abs(x)))


def _qkv_kernel(xm_ref, prev_ref, next_ref, cw_ref, cb_ref, wq_ref, wkt_ref, wv_ref, wg_ref, wgt_ref,
                bg_ref, bgt_ref, xc_ref, q_ref, kt_ref, v_ref, gt_ref, gtt_ref):
    i = pl.program_id(0)
    lat = i < LAT_TILES
    first = jnp.where(lat, i % TILES_PER_SEQ == 0, True)
    last = jnp.where(lat, i % TILES_PER_SEQ == TILES_PER_SEQ - 1, True)
    xmb = xm_ref[...]
    xm = xmb.astype(F32)
    prev = jnp.where(first, 0.0, prev_ref[...].astype(F32))
    nxt = jnp.where(last, 0.0, next_ref[...].astype(F32))
    ext = jnp.concatenate([prev[HALO - 8:], xm, nxt[:8]], axis=0)
    acc = jnp.zeros((TM, B_INNER), F32) + cb_ref[...]
    for t in range(CONV_K):
        off = 8 + t - CONV_K // 2
        acc = acc + ext[off:off + TM] * cw_ref[t:t + 1, :]
    xc = acc * jax.nn.sigmoid(acc)
    xcb = xc.astype(BF16)
    xc_ref[...] = xcb
    for h in range(HEADS):
        hs = slice(h * DH, (h + 1) * DH)
        qh = jnp.dot(xcb[:, hs], wq_ref[h], preferred_element_type=F32)
        q_ref[:, h * DQK:(h + 1) * DQK] = (qh * (DQK ** -0.5)).astype(BF16)
        kt_ref[h * DQK:(h + 1) * DQK, :] = lax.dot_general(
            wkt_ref[h], xcb[:, hs], (((1,), (1,)), ((), ())), preferred_element_type=F32).astype(BF16)
        v_ref[:, hs] = jnp.dot(xmb[:, hs], wv_ref[h], preferred_element_type=F32).astype(BF16)
    gts = jnp.dot(xmb, wg_ref[...], preferred_element_type=F32) + bg_ref[...]
    lane = lax.broadcasted_iota(jnp.int32, (TM, LANES), 1)
    gt_ref[...] = jnp.where(((lane >> 3) & 1) == 1, _log_sigmoid(gts), gts)
    gtt = lax.dot_general(wgt_ref[...], xmb, (((1,), (1,)), ((), ())),
                          preferred_element_type=F32) + bgt_ref[...]
    row = lax.broadcasted_iota(jnp.int32, (4 * HEADS, TM), 0)
    gtt_ref[...] = jnp.where(((row >> 3) & 1) == 1, _log_sigmoid(gtt), gtt)


def _qkv(xm, conv_w, conv_b, wq, wkt, wv, wg, wgt, bg, bgt):
    const = lambda i: (0, 0)
    const3 = lambda i: (0, 0, 0)
    hb = TM // HALO
    n_hb = N_ALL // HALO
    return pl.pallas_call(
        _qkv_kernel,
        grid=(ALL_TILES,),
        in_specs=[
            pl.BlockSpec((TM, B_INNER), lambda i: (i, 0)),
            pl.BlockSpec((HALO, B_INNER), lambda i: (jnp.maximum(i * hb - 1, 0), 0)),
            pl.BlockSpec((HALO, B_INNER), lambda i: (jnp.minimum((i + 1) * hb, n_hb - 1), 0)),
            pl.BlockSpec((CONV_K, B_INNER), const),
            pl.BlockSpec((1, B_INNER), const),
            pl.BlockSpec((HEADS, DH, DQK), const3),
            pl.BlockSpec((HEADS, DQK, DH), const3),
            pl.BlockSpec((HEADS, DH, DV), const3),
            pl.BlockSpec((B_INNER, LANES), const),
            pl.BlockSpec((4 * HEADS, B_INNER), const),
            pl.BlockSpec((1, LANES), const),
            pl.BlockSpec((4 * HEADS, 1), const),
        ],
        out_specs=[
            pl.BlockSpec((TM, B_INNER), lambda i: (i, 0)),
            pl.BlockSpec((TM, HEADS * DQK), lambda i: (i, 0)),
            pl.BlockSpec((HEADS * DQK, TM), lambda i: (0, i)),
            pl.BlockSpec((TM, B_INNER), lambda i: (i, 0)),
            pl.BlockSpec((TM, LANES), lambda i: (i, 0)),
            pl.BlockSpec((4 * HEADS, TM), lambda i: (0, i)),
        ],
        out_shape=[
            jax.ShapeDtypeStruct((N_ALL, B_INNER), BF16),
            jax.ShapeDtypeStruct((N_ALL, HEADS * DQK), BF16),
            jax.ShapeDtypeStruct((HEADS * DQK, N_ALL), BF16),
            jax.ShapeDtypeStruct((N_ALL, B_INNER), BF16),
            jax.ShapeDtypeStruct((N_ALL, LANES), F32),
            jax.ShapeDtypeStruct((4 * HEADS, N_ALL), F32),
        ],
        compiler_params=_cparams(("parallel",)),
        name="mlstm_qkv",
    )(xm, xm, xm, conv_w, conv_b, wq, wkt, wv, wg, wgt, bg, bgt)


CTX_CHUNKS = CTX // CHUNK
LAT_CHUNKS = SEQ // CHUNK
SCAN_STEPS = CTX_CHUNKS + LAT_CHUNKS


def _scan_dir(d, q_ref, kt_ref, v_ref, g_ref, gt_ref, o_ref, c_scr, m_scr):
    r = lax.broadcasted_iota(jnp.int32, (CHUNK, CHUNK), 0)
    c = lax.broadcasted_iota(jnp.int32, (CHUNK, CHUNK), 1)
    if d == 0:
        keep = c <= r
    else:
        keep = c >= r
    tri_col = keep.astype(BF16)
    tri_row = (r <= c).astype(BF16) if d == 0 else (r >= c).astype(BF16)
    ones_blk = jnp.ones((CHUNK, LANES), BF16)
    gts = g_ref[...]
    gtt = gt_ref[...]
    bcol_all = _dot_exact_lhs(tri_col, gts)
    rows_all = _dot_exact_rhs(gtt, jnp.concatenate([tri_row, ones_blk], axis=1))
    brow_all = rows_all[:, :CHUNK]
    tot_all = rows_all[:, CHUNK:]
    base = 2 * HEADS * d
    for h in range(HEADS):
        li_r = gtt[base + h:base + h + 1, :]
        b_r = brow_all[base + HEADS + h:base + HEADS + h + 1, :]
        b_last = tot_all[base + HEADS + h:base + HEADS + h + 1, :]
        b_full = jnp.broadcast_to(bcol_all[:, base + HEADS + h:base + HEADS + h + 1], (CHUNK, LANES))
        m_old = m_scr[d, h]
        c_old = c_scr[d, h]
        qh = q_ref[:, h * DQK:(h + 1) * DQK]
        kth = kt_ref[h * DQK:(h + 1) * DQK, :]
        vaug = jnp.concatenate([v_ref[:, h * DV:(h + 1) * DV], ones_blk], axis=1)
        g_r = b_last - b_r + li_r
        m_new = jnp.maximum(b_last + m_old, jnp.max(g_r, axis=-1, keepdims=True))
        decay = jnp.exp(b_last + m_old - m_new)
        wk = jnp.exp(g_r - m_new)
        kwt = (kth.astype(F32) * wk).astype(BF16)
        c_scr[d, h] = jnp.concatenate([decay] * 3, axis=1) * c_old + jnp.dot(
            kwt, vaug, preferred_element_type=F32)
        m_scr[d, h] = m_new

        dmat = jnp.where(keep, b_full - (b_r - li_r), NEG_INF)
        inter = b_full + m_old
        m_t = jnp.maximum(inter, jnp.max(dmat, axis=-1, keepdims=True))
        p = jnp.exp(dmat - m_t)
        s = jnp.dot(qh, kth, preferred_element_type=F32) * p
        a = jnp.exp(inter - m_t)
        num = jnp.concatenate([a] * 3, axis=1) * jnp.dot(
            qh, c_old.astype(BF16), preferred_element_type=F32) + jnp.dot(
                s.astype(BF16), vaug, preferred_element_type=F32)
        inv = 1.0 / jnp.maximum(jnp.abs(num[:, DV:]), jnp.exp(-m_t))
        o_ref[:, h * DV:(h + 1) * DV] = (num[:, :DV] * jnp.concatenate([inv] * 2, axis=1)).astype(BF16)


def _scan_kernel(qf, kf, vf, gf, gtf, qb, kb, vb, gb, gtb, of, ob, c_scr, m_scr):
    j = pl.program_id(1)

    @pl.when(j == 0)
    def _():
        c_scr[...] = jnp.zeros_like(c_scr)
        m_scr[...] = jnp.zeros_like(m_scr)

    _scan_dir(0, qf, kf, vf, gf, gtf, of, c_scr, m_scr)
    _scan_dir(1, qb, kb, vb, gb, gtb, ob, c_scr, m_scr)


def _scan(q, kt, v, gts, gtt):
    lat_blk = N_LAT // CHUNK

    def fwd_in(b, j):
        return jnp.where(j < CTX_CHUNKS, lat_blk + b * CTX_CHUNKS + j, b * LAT_CHUNKS + j - CTX_CHUNKS)

    def bwd_in(b, j):
        return jnp.where(j < CTX_CHUNKS, lat_blk + b * CTX_CHUNKS + (CTX_CHUNKS - 1 - j),
                         b * LAT_CHUNKS + (SCAN_STEPS - 1 - j))

    def fwd_out(b, j):
        return b * LAT_CHUNKS + jnp.maximum(j - CTX_CHUNKS, 0)

    def bwd_out(b, j):
        return b * LAT_CHUNKS + jnp.minimum(SCAN_STEPS - 1 - j, LAT_CHUNKS - 1)

    def specs(fn):
        return [
            pl.BlockSpec((CHUNK, HEADS * DQK), lambda b, j: (fn(b, j), 0)),
            pl.BlockSpec((HEADS * DQK, CHUNK), lambda b, j: (0, fn(b, j))),
            pl.BlockSpec((CHUNK, B_INNER), lambda b, j: (fn(b, j), 0)),
            pl.BlockSpec((CHUNK, LANES), lambda b, j: (fn(b, j), 0)),
            pl.BlockSpec((4 * HEADS, CHUNK), lambda b, j: (0, fn(b, j))),
        ]

    return pl.pallas_call(
        _scan_kernel,
        grid=(BATCH, SCAN_STEPS),
        in_specs=specs(fwd_in) + specs(bwd_in),
        out_specs=[pl.BlockSpec((CHUNK, B_INNER), lambda b, j: (fwd_out(b, j), 0)),
                   pl.BlockSpec((CHUNK, B_INNER), lambda b, j: (bwd_out(b, j), 0))],
        out_shape=[jax.ShapeDtypeStruct((N_LAT, B_INNER), BF16),
                   jax.ShapeDtypeStruct((N_LAT, B_INNER), BF16)],
        scratch_shapes=[pltpu.VMEM((2, HEADS, DQK, DV + LANES), F32),
                        pltpu.VMEM((2, HEADS, 1, LANES), F32)],
        compiler_params=_cparams(("parallel", "arbitrary")),
        name="mlstm_scan",
    )(q, kt, v, gts, gtt, q, kt, v, gts, gtt)


def _combine_kernel(x_ref, mod_ref, hf_ref, hb_ref, xc_ref, op_ref, hg_ref, sk_ref, wout_ref,
                    o_ref, y_scr):
    for h in range(HEADS):
        hs = slice(h * DV, (h + 1) * DV)
        s = hf_ref[:, hs].astype(F32) + hb_ref[:, hs].astype(F32)
        s = s * lax.rsqrt(jnp.mean(s * s, axis=-1, keepdims=True) + EPS)
        y = jax.nn.sigmoid(op_ref[:, hs].astype(F32)) * (
            s * hg_ref[:, hs] + sk_ref[:, hs] * xc_ref[:, hs].astype(F32))
        y_scr[:, hs] = y.astype(BF16)
    out = jnp.dot(y_scr[...], wout_ref[...], preferred_element_type=F32)
    o_ref[...] = x_ref[...] + mod_ref[0, 2:3, :] * out


def _combine(xa, mod, hf, hb, xc, op, head_g, skip, w_out):
    const = lambda i: (0, 0)
    row = lambda i: (i, 0)
    return pl.pallas_call(
        _combine_kernel,
        grid=(LAT_TILES,),
        in_specs=[
            pl.BlockSpec((TM, D), row),
            pl.BlockSpec((1, 6, D), lambda i: (_mod_row(i), 0, 0)),
            pl.BlockSpec((TM, B_INNER), row),
            pl.BlockSpec((TM, B_INNER), row),
            pl.BlockSpec((TM, B_INNER), row),
            pl.BlockSpec((TM, B_INNER), row),
            pl.BlockSpec((1, B_INNER), const),
            pl.BlockSpec((1, B_INNER), const),
            pl.BlockSpec((B_INNER, D), const),
        ],
        out_specs=pl.BlockSpec((TM, D), row),
        out_shape=jax.ShapeDtypeStruct((N_LAT, D), F32),
        scratch_shapes=[pltpu.VMEM((TM, B_INNER), BF16)],
        compiler_params=_cparams(("parallel",)),
        name="mlstm_combine",
    )(xa, mod, hf, hb, xc, op, head_g, skip, w_out)


def _router_weights(w_grp, b_grp, w_exp, b_exp):
    wr = jnp.zeros((LANES, D), F32).at[:N_EXP].set(w_exp.T).at[N_EXP:N_EXP + N_GRP].set(w_grp.T)
    br = jnp.zeros((LANES, 1), F32).at[:N_EXP, 0].set(b_exp).at[N_EXP:N_EXP + N_GRP, 0].set(b_grp)
    return wr, br


PLAN_BLK = 2048
PLAN_PAIRS = 256


def _plan_kernel(meta_ref, cc_ref, cr_ref, pp_ref, plan_ref):
    row = lax.broadcasted_iota(jnp.int32, (LANES, LANES), 0)
    col = lax.broadcasted_iota(jnp.int32, (LANES, LANES), 1)
    cnt_c = cc_ref[...]
    cnt_r = cr_ref[0:1, :]
    starts_c = jnp.sum(jnp.where(col < row, cnt_r, 0.0), axis=1, keepdims=True)
    erow = lax.broadcasted_iota(jnp.int32, (LANES, PLAN_BLK), 0).astype(F32)
    for j in range(2):
        pos = jnp.sum(jnp.where(erow == meta_ref[j:j + 1, :], starts_c, 0.0), axis=0,
                      keepdims=True) + meta_ref[j + 2:j + 3, :]
        pp_ref[j:j + 1, :] = pos.astype(jnp.int32)
    pp_ref[2:8, :] = jnp.zeros((6, PLAN_BLK), jnp.int32)

    @pl.when(pl.program_id(0) == 0)
    def _():
        ends_c = starts_c + cnt_c
        starts_r = jnp.sum(jnp.where(row < col, cnt_c, 0.0), axis=0, keepdims=True)
        ends_r = starts_r + cnt_r

        def tiles(st, en, cn):
            first_tile = jnp.floor(st * (1.0 / TM))
            n = jnp.where(cn > 0.0, jnp.floor((en - 1.0) * (1.0 / TM)) - first_tile + 1.0, 0.0)
            return first_tile, n

        ft_c, pairs_c = tiles(starts_c, ends_c, cnt_c)
        _, pairs_r = tiles(starts_r, ends_r, cnt_r)
        pend_c = jnp.sum(jnp.where(col <= row, pairs_r, 0.0), axis=1, keepdims=True)
        pstart_c = pend_c - pairs_c
        total = jnp.sum(pairs_r, axis=1, keepdims=True)
        q = lax.broadcasted_iota(jnp.int32, (1, PLAN_PAIRS), 1).astype(F32)
        erow_p = lax.broadcasted_iota(jnp.int32, (LANES, PLAN_PAIRS), 0).astype(F32)

        def at(qv):
            qc = jnp.maximum(jnp.minimum(qv, total - 1.0), 0.0)
            e = jnp.sum(jnp.where(qc >= pend_c, 1.0, 0.0), axis=0, keepdims=True)
            oh = erow_p == e
            tile = jnp.sum(jnp.where(oh, ft_c - pstart_c, 0.0), axis=0, keepdims=True) + qc
            return e, oh, tile

        e_q, oh, tile_q = at(q)
        e_p, _, tile_p = at(q - 1.0)
        lo = jnp.sum(jnp.where(oh, starts_c, 0.0), axis=0, keepdims=True) - tile_q * TM
        hi = jnp.sum(jnp.where(oh, ends_c, 0.0), axis=0, keepdims=True) - tile_q * TM
        rows = (tile_q, e_q, jnp.clip(lo, 0.0, TM), jnp.clip(hi, 0.0, TM),
                jnp.where((q == 0.0) | (tile_q != tile_p), 1.0, 0.0),
                jnp.where((q == 0.0) | (e_q != e_p), 1.0, 0.0),
                jnp.where(q < total, 1.0, 0.0),
                jnp.zeros((1, PLAN_PAIRS), F32))
        for j, v in enumerate(rows):
            plan_ref[j:j + 1, :] = v.astype(jnp.int32)


def _route_plan(meta, cnt_c, cnt_r, n_rows):
    n_pairs = 2 * n_rows // TM + N_EXP - 1
    assert n_pairs <= PLAN_PAIRS and n_rows % PLAN_BLK == 0
    const = lambda i: (0, 0)
    pp, plan = pl.pallas_call(
        _plan_kernel,
        grid=(n_rows // PLAN_BLK,),
        in_specs=[pl.BlockSpec((8, PLAN_BLK), lambda i: (0, i)),
                  pl.BlockSpec((LANES, 1), const),
                  pl.BlockSpec((8, LANES), const)],
        out_specs=[pl.BlockSpec((8, PLAN_BLK), lambda i: (0, i)),
                   pl.BlockSpec((8, PLAN_PAIRS), const)],
        out_shape=[jax.ShapeDtypeStruct((8, n_rows), jnp.int32),
                   jax.ShapeDtypeStruct((8, PLAN_PAIRS), jnp.int32)],
        compiler_params=_cparams(("arbitrary",)),
        name="moe_plan",
    )(meta, cnt_c, cnt_r)
    return pp[0], pp[1], tuple(plan[j, :n_pairs] for j in range(7))


def _moe(xa, mod, g2, w_grp, b_grp, w_exp, b_exp, w1, w3, w2, layer, final_g, n_rows, final_norm):
    wr, br = _router_weights(w_grp, b_grp, w_exp, b_exp)
    n_tiles = n_rows // TM
    meta, tmeta, cnt_c, cnt_r = _router(xa, mod, g2, wr, br, n_tiles)
    p1, p2, plan = _route_plan(meta, cnt_c, cnt_r, n_rows)
    xs = _scatter(p1, p2, xa, mod, g2, n_tiles)
    ys = _grouped(plan, xs, w1, w3, w2, layer)
    return _gather(p1, p2, xa, mod, tmeta, final_g.reshape(1, D), ys, n_tiles, final_norm)


def kernel(x, c, ctx, c_ctx, norm_g, w_ada, b_ada, a_w_in, a_b_in, a_g_v, a_w_s, a_b_s, a_w_out,
           b_w_in, b_conv_w, b_conv_b, b_w_q, b_w_k, b_w_v, b_w_gate, b_b_gate, b_head_g, b_skip,
           b_w_out, moe_w_grp, moe_b_grp, moe_w_exp, moe_b_exp, moe_w1, moe_w3, moe_w2, final_g):
    cc = jnp.zeros((MOD_ROWS, D), F32).at[:BATCH].set(c).at[BATCH].set(c_ctx)
    mods = _ada(cc, w_ada, b_ada).reshape(2, MOD_ROWS, 6, D)

    mod = mods[0]
    b_s_full = jnp.repeat(a_b_s[0].T, A_GC, axis=1)
    xa = _gmlp(x.reshape(N_LAT, D), ctx.reshape(N_CTX, D), mod, norm_g[0, 0].reshape(1, D), a_w_in[0].astype(BF16),
               a_b_in[0].reshape(1, -1), a_g_v[0].reshape(1, -1), a_w_s[0].astype(BF16),
               b_s_full, a_w_out[0].astype(BF16))
    xa = _moe(xa, mod, norm_g[0, 1].reshape(1, D), moe_w_grp[0], moe_b_grp[0], moe_w_exp[0],
              moe_b_exp[0], moe_w1, moe_w3, moe_w2, 0, final_g, N_ALL, False)

    mod = mods[1]
    xm, op = _inproj(xa, mod, norm_g[1, 0].reshape(1, D), b_w_in[0].astype(BF16))
    wg = jnp.zeros((B_INNER, LANES), F32).at[:, :4 * HEADS].set(b_w_gate[0]).astype(BF16)
    bg = jnp.zeros((1, LANES), F32).at[0, :4 * HEADS].set(b_b_gate[0])
    xc, q, kt, v, gts, gtt = _qkv(xm, b_conv_w[0], b_conv_b[0].reshape(1, -1),
                                  b_w_q[0].astype(BF16),
                                  jnp.transpose(b_w_k[0], (0, 2, 1)).astype(BF16),
                                  b_w_v[0].astype(BF16), wg, b_w_gate[0].T.astype(BF16), bg,
                                  b_b_gate[0].reshape(-1, 1))
    hf, hb = _scan(q, kt, v, gts, gtt)
    xl = _combine(xa, mod, hf, hb, xc, op, b_head_g[0].reshape(1, -1), b_skip[0].reshape(1, -1),
                  b_w_out[0].astype(BF16))
    out = _moe(xl, mod, norm_g[1, 1].reshape(1, D), moe_w_grp[1], moe_b_grp[1], moe_w_exp[1],
               moe_b_exp[1], moe_w1, moe_w3, moe_w2, 1, final_g, N_LAT, True)
    return out.reshape(BATCH, SEQ, D)
```

```python
import functools

import jax
import jax.numpy as jnp
from jax import lax
from jax.experimental import pallas as pl
from jax.experimental.pallas import tpu as pltpu

F32 = jnp.float32
BF16 = jnp.bfloat16

D = 1024
BATCH = 8
SEQ = 2048
CTX = 256
EPS = 1e-6
NEG_INF = -1e30
N_LAT = BATCH * SEQ
N_CTX = BATCH * CTX
N_ALL = N_LAT + N_CTX

TM = 256
LAT_TILES = N_LAT // TM
ALL_TILES = N_ALL // TM
TILES_PER_SEQ = SEQ // TM
MOD_ROWS = 16

CHUNK = 128
A_HALF = 2048
A_GROUPS = 8
A_GC = A_HALF // A_GROUPS
B_INNER = 2048
HEADS = 8
DH = B_INNER // HEADS
DQK = DH // 2
DV = DH
CONV_K = 5
HALO = 16
N_EXP = 32
N_GRP = 4
EXP_PER_GRP = 8
D_EXP = 512
LANES = 128

VMEM_LIMIT = 56 * 1024 * 1024


def _cparams(sem):
    return pltpu.CompilerParams(dimension_semantics=sem, vmem_limit_bytes=VMEM_LIMIT)


def _mod_row(i):
    return jnp.where(i < LAT_TILES, i // TILES_PER_SEQ, BATCH)


def _norm_mod(x, g, shift, scale):
    y = x * lax.rsqrt(jnp.mean(x * x, axis=-1, keepdims=True) + EPS) * g
    return y * (1.0 + scale) + shift


def _bf16_terms(x, n):
    terms = []
    r = x
    for _ in range(n):
        t = r.astype(BF16)
        terms.append(t)
        r = r - t.astype(F32)
    return terms


def _dot_nt(a, b):
    return lax.dot_general(a, b, (((1,), (1,)), ((), ())), preferred_element_type=F32)


def _dot_nn(a, b):
    return jnp.dot(a, b, preferred_element_type=F32)


def _dot3(a, b, dot=_dot_nn):
    a1, a2 = _bf16_terms(a, 2)
    b1, b2 = _bf16_terms(b, 2)
    return dot(a1, b1) + (dot(a1, b2) + dot(a2, b1))


def _dot_exact_lhs(a_bf16, b, dot=_dot_nn):
    return sum(dot(a_bf16, t) for t in _bf16_terms(b, 3))


def _dot_exact_rhs(a, b_bf16, dot=_dot_nn):
    return sum(dot(t, b_bf16) for t in _bf16_terms(a, 3))


def _gelu_tanh(x):
    return 0.5 * x * (1.0 + jnp.tanh(0.7978845608028654 * (x + 0.044715 * (x * x * x))))


ADA_BN = 1536


def _ada_kernel(c_ref, w_ref, b_ref, o_ref):
    c = c_ref[...]
    a = c * jax.nn.sigmoid(c)
    o_ref[0] = _dot3(a, w_ref[0]) + b_ref[0]


def _ada(cc, w_ada, b_ada):
    depth = w_ada.shape[0]
    return pl.pallas_call(
        _ada_kernel,
        grid=(depth, 6 * D // ADA_BN),
        in_specs=[
            pl.BlockSpec((MOD_ROWS, D), lambda l, j: (0, 0)),
            pl.BlockSpec((1, D, ADA_BN), lambda l, j: (l, 0, j)),
            pl.BlockSpec((1, 1, ADA_BN), lambda l, j: (l, 0, j)),
        ],
        out_specs=pl.BlockSpec((1, MOD_ROWS, ADA_BN), lambda l, j: (l, 0, j)),
        out_shape=jax.ShapeDtypeStruct((depth, MOD_ROWS, 6 * D), F32),
        compiler_params=_cparams(("parallel", "parallel")),
        name="ada",
    )(cc, w_ada, b_ada.reshape(depth, 1, 6 * D))


GM_CH = 512


def _gmlp_kernel(xl_ref, xc_ref, mod_ref, g_ref, win_ref, bin_ref, gv_ref, ws_ref, bs_ref, wout_ref,
                 g2_ref, wr_ref, br_ref, o_ref, meta_ref, tmeta_ref, cnt_ref, cntr_ref,
                 z_scr, y_scr, carry, carry_r):
    x = jnp.where(pl.program_id(0) < LAT_TILES, xl_ref[...], xc_ref[...])
    h = _norm_mod(x, g_ref[...], mod_ref[0, 0:1, :], mod_ref[0, 1:2, :])
    hb = h.astype(BF16)
    s1 = jnp.zeros((TM, 1), F32)
    s2 = jnp.zeros((TM, 1), F32)
    for j in range(2 * A_HALF // GM_CH):
        cs = slice(j * GM_CH, (j + 1) * GM_CH)
        zc = jnp.dot(hb, win_ref[:, cs], preferred_element_type=F32) + bin_ref[:, cs]
        zc = _gelu_tanh(zc)
        z_scr[:, cs] = zc
        if j * GM_CH >= A_HALF:
            s1 = s1 + jnp.sum(zc, axis=-1, keepdims=True)
            s2 = s2 + jnp.sum(zc * zc, axis=-1, keepdims=True)
    mu = s1 * (1.0 / A_HALF)
    rstd = lax.rsqrt(s2 * (1.0 / A_HALF) - mu * mu + EPS)
    for c in range(TM // CHUNK):
        rs = slice(c * CHUNK, (c + 1) * CHUNK)
        for g in range(A_GROUPS):
            cs = slice(g * A_GC, (g + 1) * A_GC)
            vs = slice(A_HALF + g * A_GC, A_HALF + (g + 1) * A_GC)
            v = (z_scr[rs, vs] - mu[rs]) * rstd[rs] * gv_ref[:, cs]
            s = jnp.dot(ws_ref[g], v.astype(BF16), preferred_element_type=F32) + bs_ref[:, cs]
            y_scr[rs, cs] = (z_scr[rs, cs] * s).astype(BF16)
    out = jnp.dot(y_scr[...], wout_ref[...], preferred_element_type=F32)
    xn = x + mod_ref[0, 2:3, :] * out
    o_ref[...] = xn
    _route_tile(xn, mod_ref, g2_ref, wr_ref, br_ref, meta_ref, tmeta_ref, cnt_ref, cntr_ref,
                carry, carry_r)


def _gmlp(xl, xc, mod, g, w_in, b_in, g_v, w_s, b_s_full, w_out, g2, wr, br):
    const = lambda i: (0, 0)
    r_in, r_out, r_shape, r_scratch = _router_specs(ALL_TILES)
    return pl.pallas_call(
        _gmlp_kernel,
        grid=(ALL_TILES,),
        in_specs=[
            pl.BlockSpec((TM, D), lambda i: (jnp.minimum(i, LAT_TILES - 1), 0)),
            pl.BlockSpec((TM, D), lambda i: (jnp.maximum(i - LAT_TILES, 0), 0)),
            pl.BlockSpec((1, 6, D), lambda i: (_mod_row(i), 0, 0)),
            pl.BlockSpec((1, D), const),
            pl.BlockSpec((D, 2 * A_HALF), const),
            pl.BlockSpec((1, 2 * A_HALF), const),
            pl.BlockSpec((1, A_HALF), const),
            pl.BlockSpec((A_GROUPS, CHUNK, CHUNK), lambda i: (0, 0, 0)),
            pl.BlockSpec((CHUNK, A_HALF), const),
            pl.BlockSpec((A_HALF, D), const),
        ] + r_in,
        out_specs=[pl.BlockSpec((TM, D), lambda i: (i, 0))] + r_out,
        out_shape=[jax.ShapeDtypeStruct((N_ALL, D), F32)] + r_shape,
        scratch_shapes=[pltpu.VMEM((TM, 2 * A_HALF), F32), pltpu.VMEM((TM, A_HALF), BF16)] + r_scratch,
        compiler_params=_cparams(("arbitrary",)),
        name="gmlp",
    )(xl, xc, mod, g, w_in, b_in, g_v, w_s, b_s_full, w_out, g2, wr, br)


def _route_tile(x, mod_ref, g_ref, wr_ref, br_ref, meta_ref, tmeta_ref, cnt_ref, cntr_ref,
                carry, carry_r):
    @pl.when(pl.program_id(0) == 0)
    def _():
        carry[...] = jnp.zeros_like(carry)
        carry_r[...] = jnp.zeros_like(carry_r)

    h = _norm_mod(x, g_ref[...], mod_ref[0, 3:4, :], mod_ref[0, 4:5, :])
    lt = _dot_nt(wr_ref[...], h.astype(BF16)) + br_ref[...]
    e_t = lt[0:N_EXP]
    row8 = lax.broadcasted_iota(jnp.int32, (EXP_PER_GRP, TM), 0).astype(F32)
    g_t = jnp.where(row8 < N_GRP, lt[N_EXP:N_EXP + EXP_PER_GRP], -jnp.inf)
    gmax = jnp.max(g_t, axis=0, keepdims=True)
    p_g = 1.0 / jnp.sum(jnp.exp(g_t - gmax), axis=0, keepdims=True)
    g_idx = jnp.min(jnp.where(g_t == gmax, row8, float(EXP_PER_GRP)), axis=0, keepdims=True)
    sel = jnp.zeros((EXP_PER_GRP, TM), F32)
    for g in range(N_GRP):
        sel = sel + jnp.where(g_idx == g, e_t[g * EXP_PER_GRP:(g + 1) * EXP_PER_GRP], 0.0)
    m1 = jnp.max(sel, axis=0, keepdims=True)
    i1 = jnp.min(jnp.where(sel == m1, row8, float(EXP_PER_GRP)), axis=0, keepdims=True)
    sel2 = jnp.where(row8 == i1, -jnp.inf, sel)
    m2 = jnp.max(sel2, axis=0, keepdims=True)
    i2 = jnp.min(jnp.where(sel2 == m2, row8, float(EXP_PER_GRP)), axis=0, keepdims=True)
    e2 = jnp.exp(m2 - m1)
    w1 = p_g / (1.0 + e2)
    w2 = p_g * e2 / (1.0 + e2)
    row = lax.broadcasted_iota(jnp.int32, (LANES, TM), 0).astype(F32)
    id1 = g_idx * EXP_PER_GRP + i1
    id2 = g_idx * EXP_PER_GRP + i2
    oh1 = row == id1
    oh2 = row == id2
    oh = jnp.where(oh1, 1.0, 0.0) + jnp.where(oh2, 1.0, 0.0)
    before = (lax.broadcasted_iota(jnp.int32, (TM, TM), 0)
              < lax.broadcasted_iota(jnp.int32, (TM, TM), 1)).astype(BF16)
    ohb = oh.astype(BF16)
    tot = jnp.dot(ohb, before, preferred_element_type=F32) + carry[...]
    r1 = jnp.sum(jnp.where(oh1, tot, 0.0), axis=0, keepdims=True)
    r2 = jnp.sum(jnp.where(oh2, tot, 0.0), axis=0, keepdims=True)
    carry[...] += jnp.sum(oh, axis=-1, keepdims=True)
    carry_r[...] += _dot_nt(jnp.ones((8, TM), BF16), ohb)
    cnt_ref[...] = carry[...]
    cntr_ref[...] = carry_r[...]
    meta_ref[0:1, :] = id1
    meta_ref[1:2, :] = id2
    meta_ref[2:3, :] = r1
    meta_ref[3:4, :] = r2
    meta_ref[4:8, :] = jnp.zeros((4, TM), F32)
    tmeta_ref[...] = (jnp.where(row == 0.0, w1, 0.0) + jnp.where(row == 1.0, w2, 0.0)).T


def _router_specs(n_tiles):
    const = lambda i: (0, 0)
    n = n_tiles * TM
    in_specs = [pl.BlockSpec((1, D), const), pl.BlockSpec((LANES, D), const),
                pl.BlockSpec((LANES, 1), const)]
    out_specs = [pl.BlockSpec((8, TM), lambda i: (0, i)),
                 pl.BlockSpec((TM, LANES), lambda i: (i, 0)),
                 pl.BlockSpec((LANES, 1), const),
                 pl.BlockSpec((8, LANES), const)]
    out_shape = [jax.ShapeDtypeStruct((8, n), F32), jax.ShapeDtypeStruct((n, LANES), F32),
                 jax.ShapeDtypeStruct((LANES, 1), F32), jax.ShapeDtypeStruct((8, LANES), F32)]
    scratch = [pltpu.VMEM((LANES, 1), F32), pltpu.VMEM((8, LANES), F32)]
    return in_specs, out_specs, out_shape, scratch


ISSUE_UNROLL = 8


def _row_copy_wait(buf_slot, sem_slot):
    pltpu.make_async_copy(buf_slot, buf_slot, sem_slot).wait()


def _scatter_kernel(p1_ref, p2_ref, x_ref, mod_ref, g_ref, xs_ref, buf, sem):
    i = pl.program_id(0)
    n_steps = pl.num_programs(0)
    slot = lax.rem(i, 2)

    def wait_slot(s):
        _row_copy_wait(buf.at[s], sem.at[s])
        _row_copy_wait(buf.at[s], sem.at[s])

    @pl.when(i >= 2)
    def _():
        wait_slot(slot)

    buf[slot] = _norm_mod(x_ref[...], g_ref[...], mod_ref[0, 3:4, :], mod_ref[0, 4:5, :])
    base = i * TM

    def body(r, carry):
        src = buf.at[slot, pl.ds(r, 1), :]
        pltpu.make_async_copy(src, xs_ref.at[pl.ds(p1_ref[base + r], 1), :], sem.at[slot]).start()
        pltpu.make_async_copy(src, xs_ref.at[pl.ds(p2_ref[base + r], 1), :], sem.at[slot]).start()
        return carry

    lax.fori_loop(0, TM, body, 0, unroll=ISSUE_UNROLL)

    @pl.when(i == n_steps - 1)
    def _():
        @pl.when(i >= 1)
        def _():
            wait_slot(1 - slot)

        wait_slot(slot)


def _scatter(p1, p2, xa, mod, g, n_tiles):
    n = n_tiles * TM
    return pl.pallas_call(
        _scatter_kernel,
        grid_spec=pltpu.PrefetchScalarGridSpec(
            num_scalar_prefetch=2,
            grid=(n_tiles,),
            in_specs=[
                pl.BlockSpec((TM, D), lambda i, p1, p2: (i, 0)),
                pl.BlockSpec((1, 6, D), lambda i, p1, p2: (_mod_row(i), 0, 0)),
                pl.BlockSpec((1, D), lambda i, p1, p2: (0, 0)),
            ],
            out_specs=pl.BlockSpec(memory_space=pl.ANY),
            scratch_shapes=[pltpu.VMEM((2, TM, D), F32), pltpu.SemaphoreType.DMA((2,))],
        ),
        out_shape=jax.ShapeDtypeStruct((2 * n, D), F32),
        compiler_params=_cparams(("arbitrary",)),
        name="moe_scatter",
    )(p1, p2, xa, mod, g)


def _grouped_kernel(tile_ref, exp_ref, lo_ref, hi_ref, first_ref, newexp_ref, valid_ref,
                    xs_ref, w1_ref, w3_ref, w2_ref, ys_ref, wb1, wb3, wb2):
    q = pl.program_id(0)

    @pl.when(valid_ref[q] == 1)
    def _():
        @pl.when(newexp_ref[q] == 1)
        def _():
            wb1[...] = w1_ref[0, 0].astype(BF16)
            wb3[...] = w3_ref[0, 0].astype(BF16)
            wb2[...] = w2_ref[0, 0].astype(BF16)

        x = xs_ref[...].astype(BF16)
        a = jnp.dot(x, wb1[...], preferred_element_type=F32)
        b = jnp.dot(x, wb3[...], preferred_element_type=F32)
        rows = lax.broadcasted_iota(jnp.int32, (TM, 1), 0)
        mine = (rows >= lo_ref[q]) & (rows < hi_ref[q])
        act = jnp.where(mine, a * jax.nn.sigmoid(a) * b, 0.0)
        y = jnp.dot(act.astype(BF16), wb2[...], preferred_element_type=F32)

        @pl.when(first_ref[q] == 1)
        def _():
            ys_ref[...] = y

        @pl.when(first_ref[q] == 0)
        def _():
            ys_ref[...] += y


def _grouped(plan, xs, w1, w3, w2, layer):
    n_pairs = plan[0].shape[0]
    tile_map = lambda q, tile, exp, *_: (tile[q], 0)
    exp_map = lambda q, tile, exp, *_: (layer, exp[q], 0, 0)
    return pl.pallas_call(
        _grouped_kernel,
        grid_spec=pltpu.PrefetchScalarGridSpec(
            num_scalar_prefetch=7,
            grid=(n_pairs,),
            in_specs=[
                pl.BlockSpec((TM, D), tile_map),
                pl.BlockSpec((1, 1, D, D_EXP), exp_map),
                pl.BlockSpec((1, 1, D, D_EXP), exp_map),
                pl.BlockSpec((1, 1, D_EXP, D), exp_map),
            ],
            out_specs=pl.BlockSpec((TM, D), tile_map),
            scratch_shapes=[pltpu.VMEM((D, D_EXP), BF16), pltpu.VMEM((D, D_EXP), BF16),
                            pltpu.VMEM((D_EXP, D), BF16)],
        ),
        out_shape=jax.ShapeDtypeStruct(xs.shape, F32),
        compiler_params=_cparams(("arbitrary",)),
        name="moe_grouped",
    )(*plan, xs, w1, w3, w2)


def _gather_kernel(p1_ref, p2_ref, x_ref, mod_ref, tm_ref, fg_ref, ys_ref, o_ref, buf, sem,
                   *, final_norm):
    i = pl.program_id(0)
    n_steps = pl.num_programs(0)
    slot = lax.rem(i, 2)

    def issue(tile, s):
        base = tile * TM

        def body(r, carry):
            pltpu.make_async_copy(ys_ref.at[pl.ds(p1_ref[base + r], 1), :],
                                  buf.at[s, 0, pl.ds(r, 1), :], sem.at[s]).start()
            pltpu.make_async_copy(ys_ref.at[pl.ds(p2_ref[base + r], 1), :],
                                  buf.at[s, 1, pl.ds(r, 1), :], sem.at[s]).start()
            return carry

        lax.fori_loop(0, TM, body, 0, unroll=ISSUE_UNROLL)

    @pl.when(i == 0)
    def _():
        issue(0, 0)

    @pl.when(i + 1 < n_steps)
    def _():
        issue(i + 1, 1 - slot)

    _row_copy_wait(buf.at[slot, 0], sem.at[slot])
    _row_copy_wait(buf.at[slot, 1], sem.at[slot])
    y = tm_ref[:, 0:1] * buf[slot, 0] + tm_ref[:, 1:2] * buf[slot, 1]
    out = x_ref[...] + mod_ref[0, 5:6, :] * y
    if final_norm:
        out = out * lax.rsqrt(jnp.mean(out * out, axis=-1, keepdims=True) + EPS) * fg_ref[...]
    o_ref[...] = out


def _gather(p1, p2, xa, mod, tmeta, final_g, ys, n_tiles, final_norm):
    return pl.pallas_call(
        functools.partial(_gather_kernel, final_norm=final_norm),
        grid_spec=pltpu.PrefetchScalarGridSpec(
            num_scalar_prefetch=2,
            grid=(n_tiles,),
            in_specs=[
                pl.BlockSpec((TM, D), lambda i, p1, p2: (i, 0)),
                pl.BlockSpec((1, 6, D), lambda i, p1, p2: (_mod_row(i), 0, 0)),
                pl.BlockSpec((TM, LANES), lambda i, p1, p2: (i, 0)),
                pl.BlockSpec((1, D), lambda i, p1, p2: (0, 0)),
                pl.BlockSpec(memory_space=pl.ANY),
            ],
            out_specs=pl.BlockSpec((TM, D), lambda i, p1, p2: (i, 0)),
            scratch_shapes=[pltpu.VMEM((2, 2, TM, D), F32), pltpu.SemaphoreType.DMA((2,))],
        ),
        out_shape=jax.ShapeDtypeStruct((n_tiles * TM, D), F32),
        compiler_params=_cparams(("arbitrary",)),
        name="moe_gather",
    )(p1, p2, xa, mod, tmeta, final_g, ys)


def _inproj_kernel(x_ref, mod_ref, g_ref, win_ref, xm_ref, op_ref):
    h = _norm_mod(x_ref[...], g_ref[...], mod_ref[0, 0:1, :], mod_ref[0, 1:2, :])
    hb = h.astype(BF16)
    xm_ref[...] = jnp.dot(hb, win_ref[:, :B_INNER], preferred_element_type=F32).astype(BF16)
    op_ref[...] = jnp.dot(hb, win_ref[:, B_INNER:], preferred_element_type=F32).astype(BF16)


def _inproj(xa, mod, g, w_in):
    const = lambda i: (0, 0)
    return pl.pallas_call(
        _inproj_kernel,
        grid=(ALL_TILES,),
        in_specs=[
            pl.BlockSpec((TM, D), lambda i: (i, 0)),
            pl.BlockSpec((1, 6, D), lambda i: (_mod_row(i), 0, 0)),
            pl.BlockSpec((1, D), const),
            pl.BlockSpec((D, 2 * B_INNER), const),
        ],
        out_specs=[pl.BlockSpec((TM, B_INNER), lambda i: (i, 0)),
                   pl.BlockSpec((TM, B_INNER), lambda i: (i, 0))],
        out_shape=[jax.ShapeDtypeStruct((N_ALL, B_INNER), BF16),
                   jax.ShapeDtypeStruct((N_ALL, B_INNER), BF16)],
        compiler_params=_cparams(("parallel",)),
        name="mlstm_inproj",
    )(xa, mod, g, w_in)


def _log_sigmoid(x):
    return jnp.minimum(x, 0.0) - jnp.log(1.0 + jnp.exp(-jnp.abs(x)))


def _qkv_kernel(xm_ref, prev_ref, next_ref, cw_ref, cb_ref, wq_ref, wkt_ref, wv_ref, wg_ref, wgt_ref,
                bg_ref, bgt_ref, xc_ref, q_ref, kt_ref, v_ref, gt_ref, gtt_ref, tot_ref):
    i = pl.program_id(0)
    lat = i < LAT_TILES
    first = jnp.where(lat, i % TILES_PER_SEQ == 0, True)
    last = jnp.where(lat, i % TILES_PER_SEQ == TILES_PER_SEQ - 1, True)
    xmb = xm_ref[...]
    xm = xmb.astype(F32)
    prev = jnp.where(first, 0.0, prev_ref[...].astype(F32))
    nxt = jnp.where(last, 0.0, next_ref[...].astype(F32))
    ext = jnp.concatenate([prev[HALO - 8:], xm, nxt[:8]], axis=0)
    acc = jnp.zeros((TM, B_INNER), F32) + cb_ref[...]
    for t in range(CONV_K):
        off = 8 + t - CONV_K // 2
        acc = acc + ext[off:off + TM] * cw_ref[t:t + 1, :]
    xc = acc * jax.nn.sigmoid(acc)
    xcb = xc.astype(BF16)
    xc_ref[...] = xcb
    for h in range(HEADS):
        hs = slice(h * DH, (h + 1) * DH)
        qh = jnp.dot(xcb[:, hs], wq_ref[h], preferred_element_type=F32)
        q_ref[:, h * DQK:(h + 1) * DQK] = (qh * (DQK ** -0.5)).astype(BF16)
        kt_ref[h * DQK:(h + 1) * DQK, :] = lax.dot_general(
            wkt_ref[h], xcb[:, hs], (((1,), (1,)), ((), ())), preferred_element_type=F32).astype(BF16)
        v_ref[:, hs] = jnp.dot(xmb[:, hs], wv_ref[h], preferred_element_type=F32).astype(BF16)
    gts = jnp.dot(xmb, wg_ref[...], preferred_element_type=F32) + bg_ref[...]
    lane = lax.broadcasted_iota(jnp.int32, (CHUNK, LANES), 1)
    lane_kind = (lane >> 3) & 3
    gtt = _dot_nt(wgt_ref[...], xmb) + bgt_ref[...]
    row = lax.broadcasted_iota(jnp.int32, (4 * HEADS, CHUNK), 0)
    row_kind = (row >> 3) & 3
    r = lax.broadcasted_iota(jnp.int32, (CHUNK, CHUNK), 0)
    c = lax.broadcasted_iota(jnp.int32, (CHUNK, CHUNK), 1)
    lower = (c <= r).astype(BF16)
    upper = (c >= r).astype(BF16)
    row_sums = jnp.concatenate([upper, lower, jnp.ones((CHUNK, LANES), BF16)], axis=1)
    for ch in range(TM // CHUNK):
        ts = slice(ch * CHUNK, (ch + 1) * CHUNK)
        g = gts[ts]
        g = jnp.where((lane_kind & 1) == 1, _log_sigmoid(g), g)
        terms = _bf16_terms(g, 3)
        pre = sum(_dot_nn(lower, t) for t in terms)
        suf = sum(_dot_nn(upper, t) for t in terms)
        gt_ref[ts, :] = jnp.where(lane_kind == 1, pre, jnp.where(lane_kind == 3, suf, g))
        gt = gtt[:, ts]
        gt = jnp.where((row_kind & 1) == 1, _log_sigmoid(gt), gt)
        sums = _dot_exact_rhs(gt, row_sums)
        gtt_ref[:, ts] = jnp.where(row_kind == 1, sums[:, :CHUNK],
                                   jnp.where(row_kind == 3, sums[:, CHUNK:2 * CHUNK], gt))
        tot_ref[:, ts] = sums[:, 2 * CHUNK:]


def _qkv(xm, conv_w, conv_b, wq, wkt, wv, wg, wgt, bg, bgt):
    const = lambda i: (0, 0)
    const3 = lambda i: (0, 0, 0)
    hb = TM // HALO
    n_hb = N_ALL // HALO
    return pl.pallas_call(
        _qkv_kernel,
        grid=(ALL_TILES,),
        in_specs=[
            pl.BlockSpec((TM, B_INNER), lambda i: (i, 0)),
            pl.BlockSpec((HALO, B_INNER), lambda i: (jnp.maximum(i * hb - 1, 0), 0)),
            pl.BlockSpec((HALO, B_INNER), lambda i: (jnp.minimum((i + 1) * hb, n_hb - 1), 0)),
            pl.BlockSpec((CONV_K, B_INNER), const),
            pl.BlockSpec((1, B_INNER), const),
            pl.BlockSpec((HEADS, DH, DQK), const3),
            pl.BlockSpec((HEADS, DQK, DH), const3),
            pl.BlockSpec((HEADS, DH, DV), const3),
            pl.BlockSpec((B_INNER, LANES), const),
            pl.BlockSpec((4 * HEADS, B_INNER), const),
            pl.BlockSpec((1, LANES), const),
            pl.BlockSpec((4 * HEADS, 1), const),
        ],
        out_specs=[
            pl.BlockSpec((TM, B_INNER), lambda i: (i, 0)),
            pl.BlockSpec((TM, HEADS * DQK), lambda i: (i, 0)),
            pl.BlockSpec((HEADS * DQK, TM), lambda i: (0, i)),
            pl.BlockSpec((TM, B_INNER), lambda i: (i, 0)),
            pl.BlockSpec((TM, LANES), lambda i: (i, 0)),
            pl.BlockSpec((4 * HEADS, TM), lambda i: (0, i)),
            pl.BlockSpec((4 * HEADS, TM), lambda i: (0, i)),
        ],
        out_shape=[
            jax.ShapeDtypeStruct((N_ALL, B_INNER), BF16),
            jax.ShapeDtypeStruct((N_ALL, HEADS * DQK), BF16),
            jax.ShapeDtypeStruct((HEADS * DQK, N_ALL), BF16),
            jax.ShapeDtypeStruct((N_ALL, B_INNER), BF16),
            jax.ShapeDtypeStruct((N_ALL, LANES), F32),
            jax.ShapeDtypeStruct((4 * HEADS, N_ALL), F32),
            jax.ShapeDtypeStruct((4 * HEADS, N_ALL), F32),
        ],
        compiler_params=_cparams(("parallel",)),
        name="mlstm_qkv",
    )(xm, xm, xm, conv_w, conv_b, wq, wkt, wv, wg, wgt, bg, bgt)


CTX_CHUNKS = CTX // CHUNK
LAT_CHUNKS = SEQ // CHUNK
SCAN_STEPS = CTX_CHUNKS + LAT_CHUNKS


def _scan_dir(nb, d, q_ref, kt_ref, v_ref, g_ref, gt_ref, tot_ref, o_ref, c_scr, m_scr):
    r = lax.broadcasted_iota(jnp.int32, (CHUNK, CHUNK), 0)
    c = lax.broadcasted_iota(jnp.int32, (CHUNK, CHUNK), 1)
    if d == 0:
        keep = c <= r
    else:
        keep = c >= r
    ones_blk = jnp.ones((CHUNK, LANES), BF16)
    bcol_all = g_ref[...]
    gtt = gt_ref[...]
    tot_all = tot_ref[...]
    base = 2 * HEADS * d
    for h in range(HEADS):
        li_r = gtt[base + h:base + h + 1, :]
        b_r = gtt[base + HEADS + h:base + HEADS + h + 1, :]
        b_last = tot_all[base + HEADS + h:base + HEADS + h + 1, :]
        b_full = jnp.broadcast_to(bcol_all[:, base + HEADS + h:base + HEADS + h + 1], (CHUNK, LANES))
        m_old = m_scr[nb, d, h]
        c_old = c_scr[nb, d, h]
        qh = q_ref[:, h * DQK:(h + 1) * DQK]
        kth = kt_ref[h * DQK:(h + 1) * DQK, :]
        vaug = jnp.concatenate([v_ref[:, h * DV:(h + 1) * DV], ones_blk], axis=1)
        g_r = b_last - b_r + li_r
        m_new = jnp.maximum(b_last + m_old, jnp.max(g_r, axis=-1, keepdims=True))
        decay = jnp.exp(b_last + m_old - m_new)
        wk = jnp.exp(g_r - m_new)
        kwt = (kth.astype(F32) * wk).astype(BF16)
        c_scr[nb, d, h] = jnp.concatenate([decay] * 3, axis=1) * c_old + jnp.dot(
            kwt, vaug, preferred_element_type=F32)
        m_scr[nb, d, h] = m_new

        dmat = jnp.where(keep, b_full - (b_r - li_r), NEG_INF)
        inter = b_full + m_old
        m_t = jnp.maximum(inter, jnp.max(dmat, axis=-1, keepdims=True))
        p = jnp.exp(dmat - m_t)
        s = jnp.dot(qh, kth, preferred_element_type=F32) * p
        a = jnp.exp(inter - m_t)
        num = jnp.concatenate([a] * 3, axis=1) * jnp.dot(
            qh, c_old.astype(BF16), preferred_element_type=F32) + jnp.dot(
                s.astype(BF16), vaug, preferred_element_type=F32)
        inv = 1.0 / jnp.maximum(jnp.abs(num[:, DV:]), jnp.exp(-m_t))
        o_ref[nb, :, h * DV:(h + 1) * DV] = (
            num[:, :DV] * jnp.concatenate([inv] * 2, axis=1)).astype(BF16)


SCAN_NB = 2
SCAN_IN = 6


def _scan_kernel(*refs):
    n_in = SCAN_NB * 2 * SCAN_IN
    ins, outs = refs[:n_in], refs[n_in:n_in + 2]
    c_scr, m_scr = refs[n_in + 2:]

    @pl.when(pl.program_id(1) == 0)
    def _():
        c_scr[...] = jnp.zeros_like(c_scr)
        m_scr[...] = jnp.zeros_like(m_scr)

    for nb in range(SCAN_NB):
        for d in range(2):
            k = nb * 2 + d
            _scan_dir(nb, d, *ins[k * SCAN_IN:(k + 1) * SCAN_IN], outs[d], c_scr, m_scr)


def _scan(q, kt, v, gts, gtt, tot):
    lat_blk = N_LAT // CHUNK

    def fwd_in(b, j):
        return jnp.where(j < CTX_CHUNKS, lat_blk + b * CTX_CHUNKS + j, b * LAT_CHUNKS + j - CTX_CHUNKS)

    def bwd_in(b, j):
        return jnp.where(j < CTX_CHUNKS, lat_blk + b * CTX_CHUNKS + (CTX_CHUNKS - 1 - j),
                         b * LAT_CHUNKS + (SCAN_STEPS - 1 - j))

    def fwd_out(b, j):
        return b * LAT_CHUNKS + jnp.maximum(j - CTX_CHUNKS, 0)

    def bwd_out(b, j):
        return b * LAT_CHUNKS + jnp.minimum(SCAN_STEPS - 1 - j, LAT_CHUNKS - 1)

    n_g = BATCH // SCAN_NB

    def specs(fn, nb):
        blk = lambda g, j: fn(nb * n_g + g, j)
        return [
            pl.BlockSpec((CHUNK, HEADS * DQK), lambda g, j: (blk(g, j), 0)),
            pl.BlockSpec((HEADS * DQK, CHUNK), lambda g, j: (0, blk(g, j))),
            pl.BlockSpec((CHUNK, B_INNER), lambda g, j: (blk(g, j), 0)),
            pl.BlockSpec((CHUNK, LANES), lambda g, j: (blk(g, j), 0)),
            pl.BlockSpec((4 * HEADS, CHUNK), lambda g, j: (0, blk(g, j))),
            pl.BlockSpec((4 * HEADS, CHUNK), lambda g, j: (0, blk(g, j))),
        ]

    def out_spec(fn):
        return pl.BlockSpec((SCAN_NB, CHUNK, B_INNER), lambda g, j: (0, fn(g, j), 0))

    in_specs, operands = [], []
    for nb in range(SCAN_NB):
        for fn_in in (fwd_in, bwd_in):
            in_specs += specs(fn_in, nb)
            operands += [q, kt, v, gts, gtt, tot]
    h_shape = jax.ShapeDtypeStruct((SCAN_NB, N_LAT // SCAN_NB, B_INNER), BF16)
    hf, hb = pl.pallas_call(
        _scan_kernel,
        grid=(n_g, SCAN_STEPS),
        in_specs=in_specs,
        out_specs=[out_spec(fwd_out), out_spec(bwd_out)],
        out_shape=[h_shape, h_shape],
        scratch_shapes=[pltpu.VMEM((SCAN_NB, 2, HEADS, DQK, DV + LANES), F32),
                        pltpu.VMEM((SCAN_NB, 2, HEADS, 1, LANES), F32)],
        compiler_params=_cparams(("parallel", "arbitrary")),
        name="mlstm_scan",
    )(*operands)
    return hf.reshape(N_LAT, B_INNER), hb.reshape(N_LAT, B_INNER)


def _combine_kernel(x_ref, mod_ref, hf_ref, hb_ref, xc_ref, op_ref, hg_ref, sk_ref, wout_ref,
                    g2_ref, wr_ref, br_ref, o_ref, meta_ref, tmeta_ref, cnt_ref, cntr_ref,
                    y_scr, carry, carry_r):
    for h in range(HEADS):
        hs = slice(h * DV, (h + 1) * DV)
        s = hf_ref[:, hs].astype(F32) + hb_ref[:, hs].astype(F32)
        s = s * lax.rsqrt(jnp.mean(s * s, axis=-1, keepdims=True) + EPS)
        y = jax.nn.sigmoid(op_ref[:, hs].astype(F32)) * (
            s * hg_ref[:, hs] + sk_ref[:, hs] * xc_ref[:, hs].astype(F32))
        y_scr[:, hs] = y.astype(BF16)
    out = jnp.dot(y_scr[...], wout_ref[...], preferred_element_type=F32)
    xn = x_ref[...] + mod_ref[0, 2:3, :] * out
    o_ref[...] = xn
    _route_tile(xn, mod_ref, g2_ref, wr_ref, br_ref, meta_ref, tmeta_ref, cnt_ref, cntr_ref,
                carry, carry_r)


def _combine(xa, mod, hf, hb, xc, op, head_g, skip, w_out, g2, wr, br):
    const = lambda i: (0, 0)
    row = lambda i: (i, 0)
    r_in, r_out, r_shape, r_scratch = _router_specs(LAT_TILES)
    return pl.pallas_call(
        _combine_kernel,
        grid=(LAT_TILES,),
        in_specs=[
            pl.BlockSpec((TM, D), row),
            pl.BlockSpec((1, 6, D), lambda i: (_mod_row(i), 0, 0)),
            pl.BlockSpec((TM, B_INNER), row),
            pl.BlockSpec((TM, B_INNER), row),
            pl.BlockSpec((TM, B_INNER), row),
            pl.BlockSpec((TM, B_INNER), row),
            pl.BlockSpec((1, B_INNER), const),
            pl.BlockSpec((1, B_INNER), const),
            pl.BlockSpec((B_INNER, D), const),
        ] + r_in,
        out_specs=[pl.BlockSpec((TM, D), row)] + r_out,
        out_shape=[jax.ShapeDtypeStruct((N_LAT, D), F32)] + r_shape,
        scratch_shapes=[pltpu.VMEM((TM, B_INNER), BF16)] + r_scratch,
        compiler_params=_cparams(("arbitrary",)),
        name="mlstm_combine",
    )(xa, mod, hf, hb, xc, op, head_g, skip, w_out, g2, wr, br)


def _router_weights(w_grp, b_grp, w_exp, b_exp):
    wr = jnp.zeros((LANES, D), F32).at[:N_EXP].set(w_exp.T).at[N_EXP:N_EXP + N_GRP].set(w_grp.T)
    br = jnp.zeros((LANES, 1), F32).at[:N_EXP, 0].set(b_exp).at[N_EXP:N_EXP + N_GRP, 0].set(b_grp)
    return wr.astype(BF16), br


PLAN_BLK = 2048
PLAN_PAIRS = 256


def _plan_kernel(meta_ref, cc_ref, cr_ref, pp_ref, plan_ref):
    row = lax.broadcasted_iota(jnp.int32, (LANES, LANES), 0)
    col = lax.broadcasted_iota(jnp.int32, (LANES, LANES), 1)
    cnt_c = cc_ref[...]
    cnt_r = cr_ref[0:1, :]
    starts_c = jnp.sum(jnp.where(col < row, cnt_r, 0.0), axis=1, keepdims=True)
    erow = lax.broadcasted_iota(jnp.int32, (LANES, PLAN_BLK), 0).astype(F32)
    for j in range(2):
        pos = jnp.sum(jnp.where(erow == meta_ref[j:j + 1, :], starts_c, 0.0), axis=0,
                      keepdims=True) + meta_ref[j + 2:j + 3, :]
        pp_ref[j:j + 1, :] = pos.astype(jnp.int32)
    pp_ref[2:8, :] = jnp.zeros((6, PLAN_BLK), jnp.int32)

    @pl.when(pl.program_id(0) == 0)
    def _():
        ends_c = starts_c + cnt_c
        starts_r = jnp.sum(jnp.where(row < col, cnt_c, 0.0), axis=0, keepdims=True)
        ends_r = starts_r + cnt_r

        def tiles(st, en, cn):
            first_tile = jnp.floor(st * (1.0 / TM))
            n = jnp.where(cn > 0.0, jnp.floor((en - 1.0) * (1.0 / TM)) - first_tile + 1.0, 0.0)
            return first_tile, n

        ft_c, pairs_c = tiles(starts_c, ends_c, cnt_c)
        _, pairs_r = tiles(starts_r, ends_r, cnt_r)
        pend_c = jnp.sum(jnp.where(col <= row, pairs_r, 0.0), axis=1, keepdims=True)
        pstart_c = pend_c - pairs_c
        total = jnp.sum(pairs_r, axis=1, keepdims=True)
        q = lax.broadcasted_iota(jnp.int32, (1, PLAN_PAIRS), 1).astype(F32)
        erow_p = lax.broadcasted_iota(jnp.int32, (LANES, PLAN_PAIRS), 0).astype(F32)

        def at(qv):
            qc = jnp.maximum(jnp.minimum(qv, total - 1.0), 0.0)
            e = jnp.sum(jnp.where(qc >= pend_c, 1.0, 0.0), axis=0, keepdims=True)
            oh = erow_p == e
            tile = jnp.sum(jnp.where(oh, ft_c - pstart_c, 0.0), axis=0, keepdims=True) + qc
            return e, oh, tile

        e_q, oh, tile_q = at(q)
        e_p, _, tile_p = at(q - 1.0)
        lo = jnp.sum(jnp.where(oh, starts_c, 0.0), axis=0, keepdims=True) - tile_q * TM
        hi = jnp.sum(jnp.where(oh, ends_c, 0.0), axis=0, keepdims=True) - tile_q * TM
        rows = (tile_q, e_q, jnp.clip(lo, 0.0, TM), jnp.clip(hi, 0.0, TM),
                jnp.where((q == 0.0) | (tile_q != tile_p), 1.0, 0.0),
                jnp.where((q == 0.0) | (e_q != e_p), 1.0, 0.0),
                jnp.where(q < total, 1.0, 0.0),
                jnp.zeros((1, PLAN_PAIRS), F32))
        for j, v in enumerate(rows):
            plan_ref[j:j + 1, :] = v.astype(jnp.int32)


def _route_plan(meta, cnt_c, cnt_r, n_rows):
    n_pairs = 2 * n_rows // TM + N_EXP - 1
    assert n_pairs <= PLAN_PAIRS and n_rows % PLAN_BLK == 0
    const = lambda i: (0, 0)
    pp, plan = pl.pallas_call(
        _plan_kernel,
        grid=(n_rows // PLAN_BLK,),
        in_specs=[pl.BlockSpec((8, PLAN_BLK), lambda i: (0, i)),
                  pl.BlockSpec((LANES, 1), const),
                  pl.BlockSpec((8, LANES), const)],
        out_specs=[pl.BlockSpec((8, PLAN_BLK), lambda i: (0, i)),
                   pl.BlockSpec((8, PLAN_PAIRS), const)],
        out_shape=[jax.ShapeDtypeStruct((8, n_rows), jnp.int32),
                   jax.ShapeDtypeStruct((8, PLAN_PAIRS), jnp.int32)],
        compiler_params=_cparams(("arbitrary",)),
        name="moe_plan",
    )(meta, cnt_c, cnt_r)
    return pp[0], pp[1], tuple(plan[j, :n_pairs] for j in range(7))


def _moe(xa, routing, mod, g2, w1, w3, w2, layer, final_g, n_rows, final_norm):
    n_tiles = n_rows // TM
    meta, tmeta, cnt_c, cnt_r = routing
    p1, p2, plan = _route_plan(meta, cnt_c, cnt_r, n_rows)
    xs = _scatter(p1, p2, xa, mod, g2, n_tiles)
    ys = _grouped(plan, xs, w1, w3, w2, layer)
    return _gather(p1, p2, xa, mod, tmeta, final_g.reshape(1, D), ys, n_tiles, final_norm)


def kernel(x, c, ctx, c_ctx, norm_g, w_ada, b_ada, a_w_in, a_b_in, a_g_v, a_w_s, a_b_s, a_w_out,
           b_w_in, b_conv_w, b_conv_b, b_w_q, b_w_k, b_w_v, b_w_gate, b_b_gate, b_head_g, b_skip,
           b_w_out, moe_w_grp, moe_b_grp, moe_w_exp, moe_b_exp, moe_w1, moe_w3, moe_w2, final_g):
    cc = jnp.zeros((MOD_ROWS, D), F32).at[:BATCH].set(c).at[BATCH].set(c_ctx)
    mods = _ada(cc, w_ada, b_ada).reshape(2, MOD_ROWS, 6, D)

    mod = mods[0]
    b_s_full = jnp.repeat(a_b_s[0].T, A_GC, axis=1)
    g2 = norm_g[0, 1].reshape(1, D)
    wr, br = _router_weights(moe_w_grp[0], moe_b_grp[0], moe_w_exp[0], moe_b_exp[0])
    xa, *routing = _gmlp(x.reshape(N_LAT, D), ctx.reshape(N_CTX, D), mod, norm_g[0, 0].reshape(1, D),
                         a_w_in[0].astype(BF16), a_b_in[0].reshape(1, -1), a_g_v[0].reshape(1, -1),
                         a_w_s[0].astype(BF16), b_s_full, a_w_out[0].astype(BF16), g2, wr, br)
    xa = _moe(xa, routing, mod, g2, moe_w1, moe_w3, moe_w2, 0, final_g, N_ALL, False)

    mod = mods[1]
    xm, op = _inproj(xa, mod, norm_g[1, 0].reshape(1, D), b_w_in[0].astype(BF16))
    wg = jnp.zeros((B_INNER, LANES), F32).at[:, :4 * HEADS].set(b_w_gate[0]).astype(BF16)
    bg = jnp.zeros((1, LANES), F32).at[0, :4 * HEADS].set(b_b_gate[0])
    xc, q, kt, v, gts, gtt, tot = _qkv(xm, b_conv_w[0], b_conv_b[0].reshape(1, -1),
                                  b_w_q[0].astype(BF16),
                                  jnp.transpose(b_w_k[0], (0, 2, 1)).astype(BF16),
                                  b_w_v[0].astype(BF16), wg, b_w_gate[0].T.astype(BF16), bg,
                                  b_b_gate[0].reshape(-1, 1))
    hf, hb = _scan(q, kt, v, gts, gtt, tot)
    g2 = norm_g[1, 1].reshape(1, D)
    wr, br = _router_weights(moe_w_grp[1], moe_b_grp[1], moe_w_exp[1], moe_b_exp[1])
    xl, *routing = _combine(xa, mod, hf, hb, xc, op, b_head_g[0].reshape(1, -1),
                            b_skip[0].reshape(1, -1), b_w_out[0].astype(BF16), g2, wr, br)
    out = _moe(xl, routing, mod, g2, moe_w1, moe_w3, moe_w2, 1, final_g, N_LAT, True)
    return out.reshape(BATCH, SEQ, D)
```

```python
import functools

import jax
import jax.numpy as jnp
from jax import lax
from jax.experimental import pallas as pl
from jax.experimental.pallas import tpu as pltpu

F32 = jnp.float32
BF16 = jnp.bfloat16

D = 1024
BATCH = 8
SEQ = 2048
CTX = 256
EPS = 1e-6
NEG_INF = -1e30
N_LAT = BATCH * SEQ
N_CTX = BATCH * CTX
N_ALL = N_LAT + N_CTX

TM = 256
LAT_TILES = N_LAT // TM
ALL_TILES = N_ALL // TM
TILES_PER_SEQ = SEQ // TM
MOD_ROWS = 16

CHUNK = 128
A_HALF = 2048
A_GROUPS = 8
A_GC = A_HALF // A_GROUPS
B_INNER = 2048
HEADS = 8
DH = B_INNER // HEADS
DQK = DH // 2
DV = DH
CONV_K = 5
HALO = 16
N_EXP = 32
N_GRP = 4
EXP_PER_GRP = 8
D_EXP = 512
LANES = 128

VMEM_LIMIT = 56 * 1024 * 1024


def _cparams(sem):
    return pltpu.CompilerParams(dimension_semantics=sem, vmem_limit_bytes=VMEM_LIMIT)


def _mod_row(i):
    return jnp.where(i < LAT_TILES, i // TILES_PER_SEQ, BATCH)


def _norm_mod(x, g, shift, scale):
    y = x * lax.rsqrt(jnp.mean(x * x, axis=-1, keepdims=True) + EPS) * g
    return y * (1.0 + scale) + shift


def _bf16_terms(x, n):
    terms = []
    r = x
    for _ in range(n):
        t = r.astype(BF16)
        terms.append(t)
        r = r - t.astype(F32)
    return terms


def _dot_nt(a, b):
    return lax.dot_general(a, b, (((1,), (1,)), ((), ())), preferred_element_type=F32)


def _dot_nn(a, b):
    return jnp.dot(a, b, preferred_element_type=F32)


def _dot3(a, b, dot=_dot_nn):
    a1, a2 = _bf16_terms(a, 2)
    b1, b2 = _bf16_terms(b, 2)
    return dot(a1, b1) + (dot(a1, b2) + dot(a2, b1))


def _dot_exact_lhs(a_bf16, b, dot=_dot_nn):
    return sum(dot(a_bf16, t) for t in _bf16_terms(b, 3))


def _dot_exact_rhs(a, b_bf16, dot=_dot_nn):
    return sum(dot(t, b_bf16) for t in _bf16_terms(a, 3))


def _gelu_tanh(x):
    return 0.5 * x * (1.0 + jnp.tanh(0.7978845608028654 * (x + 0.044715 * (x * x * x))))


ADA_BN = 1536


def _ada_kernel(c_ref, w_ref, b_ref, o_ref):
    c = c_ref[...]
    a = c * jax.nn.sigmoid(c)
    o_ref[0] = _dot3(a, w_ref[0]) + b_ref[0]


def _ada(cc, w_ada, b_ada):
    depth = w_ada.shape[0]
    return pl.pallas_call(
        _ada_kernel,
        grid=(depth, 6 * D // ADA_BN),
        in_specs=[
            pl.BlockSpec((MOD_ROWS, D), lambda l, j: (0, 0)),
            pl.BlockSpec((1, D, ADA_BN), lambda l, j: (l, 0, j)),
            pl.BlockSpec((1, 1, ADA_BN), lambda l, j: (l, 0, j)),
        ],
        out_specs=pl.BlockSpec((1, MOD_ROWS, ADA_BN), lambda l, j: (l, 0, j)),
        out_shape=jax.ShapeDtypeStruct((depth, MOD_ROWS, 6 * D), F32),
        compiler_params=_cparams(("parallel", "parallel")),
        name="ada",
    )(cc, w_ada, b_ada.reshape(depth, 1, 6 * D))


GM_CH = 512


def _gmlp_kernel(xl_ref, xc_ref, mod_ref, g_ref, win_ref, bin_ref, gv_ref, ws_ref, bs_ref, wout_ref,
                 g2_ref, wr_ref, br_ref, o_ref, meta_ref, tmeta_ref, cnt_ref, cntr_ref,
                 z_scr, y_scr, carry, carry_r):
    x = jnp.where(pl.program_id(0) < LAT_TILES, xl_ref[...], xc_ref[...])
    h = _norm_mod(x, g_ref[...], mod_ref[0, 0:1, :], mod_ref[0, 1:2, :])
    hb = h.astype(BF16)
    s1 = jnp.zeros((TM, 1), F32)
    s2 = jnp.zeros((TM, 1), F32)
    for j in range(2 * A_HALF // GM_CH):
        cs = slice(j * GM_CH, (j + 1) * GM_CH)
        zc = jnp.dot(hb, win_ref[:, cs], preferred_element_type=F32) + bin_ref[:, cs]
        zc = _gelu_tanh(zc)
        z_scr[:, cs] = zc
        if j * GM_CH >= A_HALF:
            s1 = s1 + jnp.sum(zc, axis=-1, keepdims=True)
            s2 = s2 + jnp.sum(zc * zc, axis=-1, keepdims=True)
    mu = s1 * (1.0 / A_HALF)
    rstd = lax.rsqrt(s2 * (1.0 / A_HALF) - mu * mu + EPS)
    for c in range(TM // CHUNK):
        rs = slice(c * CHUNK, (c + 1) * CHUNK)
        for g in range(A_GROUPS):
            cs = slice(g * A_GC, (g + 1) * A_GC)
            vs = slice(A_HALF + g * A_GC, A_HALF + (g + 1) * A_GC)
            v = (z_scr[rs, vs] - mu[rs]) * rstd[rs] * gv_ref[:, cs]
            s = jnp.dot(ws_ref[g], v.astype(BF16), preferred_element_type=F32) + bs_ref[:, cs]
            y_scr[rs, cs] = (z_scr[rs, cs] * s).astype(BF16)
    out = jnp.dot(y_scr[...], wout_ref[...], preferred_element_type=F32)
    xn = x + mod_ref[0, 2:3, :] * out
    o_ref[...] = xn
    _route_tile(xn, mod_ref, g2_ref, wr_ref, br_ref, meta_ref, tmeta_ref, cnt_ref, cntr_ref,
                carry, carry_r)


def _gmlp(xl, xc, mod, g, w_in, b_in, g_v, w_s, b_s_full, w_out, g2, wr, br):
    const = lambda i: (0, 0)
    r_in, r_out, r_shape, r_scratch = _router_specs(ALL_TILES)
    return pl.pallas_call(
        _gmlp_kernel,
        grid=(ALL_TILES,),
        in_specs=[
            pl.BlockSpec((TM, D), lambda i: (jnp.minimum(i, LAT_TILES - 1), 0)),
            pl.BlockSpec((TM, D), lambda i: (jnp.maximum(i - LAT_TILES, 0), 0)),
            pl.BlockSpec((1, 6, D), lambda i: (_mod_row(i), 0, 0)),
            pl.BlockSpec((1, D), const),
            pl.BlockSpec((D, 2 * A_HALF), const),
            pl.BlockSpec((1, 2 * A_HALF), const),
            pl.BlockSpec((1, A_HALF), const),
            pl.BlockSpec((A_GROUPS, CHUNK, CHUNK), lambda i: (0, 0, 0)),
            pl.BlockSpec((CHUNK, A_HALF), const),
            pl.BlockSpec((A_HALF, D), const),
        ] + r_in,
        out_specs=[pl.BlockSpec((TM, D), lambda i: (i, 0))] + r_out,
        out_shape=[jax.ShapeDtypeStruct((N_ALL, D), F32)] + r_shape,
        scratch_shapes=[pltpu.VMEM((TM, 2 * A_HALF), F32), pltpu.VMEM((TM, A_HALF), BF16)] + r_scratch,
        compiler_params=_cparams(("arbitrary",)),
        name="gmlp",
    )(xl, xc, mod, g, w_in, b_in, g_v, w_s, b_s_full, w_out, g2, wr, br)


def _route_tile(x, mod_ref, g_ref, wr_ref, br_ref, meta_ref, tmeta_ref, cnt_ref, cntr_ref,
                carry, carry_r):
    @pl.when(pl.program_id(0) == 0)
    def _():
        carry[...] = jnp.zeros_like(carry)
        carry_r[...] = jnp.zeros_like(carry_r)

    h = _norm_mod(x, g_ref[...], mod_ref[0, 3:4, :], mod_ref[0, 4:5, :])
    lt = _dot_nt(wr_ref[...], h.astype(BF16)) + br_ref[...]
    e_t = lt[0:N_EXP]
    row8 = lax.broadcasted_iota(jnp.int32, (EXP_PER_GRP, TM), 0).astype(F32)
    g_t = jnp.where(row8 < N_GRP, lt[N_EXP:N_EXP + EXP_PER_GRP], -jnp.inf)
    gmax = jnp.max(g_t, axis=0, keepdims=True)
    p_g = 1.0 / jnp.sum(jnp.exp(g_t - gmax), axis=0, keepdims=True)
    g_idx = jnp.min(jnp.where(g_t == gmax, row8, float(EXP_PER_GRP)), axis=0, keepdims=True)
    sel = jnp.zeros((EXP_PER_GRP, TM), F32)
    for g in range(N_GRP):
        sel = sel + jnp.where(g_idx == g, e_t[g * EXP_PER_GRP:(g + 1) * EXP_PER_GRP], 0.0)
    m1 = jnp.max(sel, axis=0, keepdims=True)
    i1 = jnp.min(jnp.where(sel == m1, row8, float(EXP_PER_GRP)), axis=0, keepdims=True)
    sel2 = jnp.where(row8 == i1, -jnp.inf, sel)
    m2 = jnp.max(sel2, axis=0, keepdims=True)
    i2 = jnp.min(jnp.where(sel2 == m2, row8, float(EXP_PER_GRP)), axis=0, keepdims=True)
    e2 = jnp.exp(m2 - m1)
    w1 = p_g / (1.0 + e2)
    w2 = p_g * e2 / (1.0 + e2)
    row = lax.broadcasted_iota(jnp.int32, (LANES, TM), 0).astype(F32)
    id1 = g_idx * EXP_PER_GRP + i1
    id2 = g_idx * EXP_PER_GRP + i2
    oh1 = row == id1
    oh2 = row == id2
    oh = jnp.where(oh1, 1.0, 0.0) + jnp.where(oh2, 1.0, 0.0)
    before = (lax.broadcasted_iota(jnp.int32, (TM, TM), 0)
              < lax.broadcasted_iota(jnp.int32, (TM, TM), 1)).astype(BF16)
    ohb = oh.astype(BF16)
    tot = jnp.dot(ohb, before, preferred_element_type=F32) + carry[...]
    r1 = jnp.sum(jnp.where(oh1, tot, 0.0), axis=0, keepdims=True)
    r2 = jnp.sum(jnp.where(oh2, tot, 0.0), axis=0, keepdims=True)
    carry[...] += jnp.sum(oh, axis=-1, keepdims=True)
    carry_r[...] += _dot_nt(jnp.ones((8, TM), BF16), ohb)
    cnt_ref[...] = carry[...]
    cntr_ref[...] = carry_r[...]
    meta_ref[0:1, :] = id1
    meta_ref[1:2, :] = id2
    meta_ref[2:3, :] = r1
    meta_ref[3:4, :] = r2
    meta_ref[4:8, :] = jnp.zeros((4, TM), F32)
    tmeta_ref[...] = (jnp.where(row == 0.0, w1, 0.0) + jnp.where(row == 1.0, w2, 0.0)).T


def _router_specs(n_tiles):
    const = lambda i: (0, 0)
    n = n_tiles * TM
    in_specs = [pl.BlockSpec((1, D), const), pl.BlockSpec((LANES, D), const),
                pl.BlockSpec((LANES, 1), const)]
    out_specs = [pl.BlockSpec((8, TM), lambda i: (0, i)),
                 pl.BlockSpec((TM, LANES), lambda i: (i, 0)),
                 pl.BlockSpec((LANES, 1), const),
                 pl.BlockSpec((8, LANES), const)]
    out_shape = [jax.ShapeDtypeStruct((8, n), F32), jax.ShapeDtypeStruct((n, LANES), F32),
                 jax.ShapeDtypeStruct((LANES, 1), F32), jax.ShapeDtypeStruct((8, LANES), F32)]
    scratch = [pltpu.VMEM((LANES, 1), F32), pltpu.VMEM((8, LANES), F32)]
    return in_specs, out_specs, out_shape, scratch


ISSUE_UNROLL = 8


def _row_copy_wait(buf_slot, sem_slot):
    pltpu.make_async_copy(buf_slot, buf_slot, sem_slot).wait()


def _scatter_kernel(p1_ref, p2_ref, x_ref, mod_ref, g_ref, xs_ref, buf, sem):
    i = pl.program_id(0)
    n_steps = pl.num_programs(0)
    slot = lax.rem(i, 2)

    def wait_slot(s):
        _row_copy_wait(buf.at[s], sem.at[s])
        _row_copy_wait(buf.at[s], sem.at[s])

    @pl.when(i >= 2)
    def _():
        wait_slot(slot)

    buf[slot] = _norm_mod(x_ref[...], g_ref[...], mod_ref[0, 3:4, :], mod_ref[0, 4:5, :])
    base = i * TM

    def body(r, carry):
        src = buf.at[slot, pl.ds(r, 1), :]
        pltpu.make_async_copy(src, xs_ref.at[pl.ds(p1_ref[base + r], 1), :], sem.at[slot]).start()
        pltpu.make_async_copy(src, xs_ref.at[pl.ds(p2_ref[base + r], 1), :], sem.at[slot]).start()
        return carry

    lax.fori_loop(0, TM, body, 0, unroll=ISSUE_UNROLL)

    @pl.when(i == n_steps - 1)
    def _():
        @pl.when(i >= 1)
        def _():
            wait_slot(1 - slot)

        wait_slot(slot)


def _scatter(p1, p2, xa, mod, g, n_tiles):
    n = n_tiles * TM
    return pl.pallas_call(
        _scatter_kernel,
        grid_spec=pltpu.PrefetchScalarGridSpec(
            num_scalar_prefetch=2,
            grid=(n_tiles,),
            in_specs=[
                pl.BlockSpec((TM, D), lambda i, p1, p2: (i, 0)),
                pl.BlockSpec((1, 6, D), lambda i, p1, p2: (_mod_row(i), 0, 0)),
                pl.BlockSpec((1, D), lambda i, p1, p2: (0, 0)),
            ],
            out_specs=pl.BlockSpec(memory_space=pl.ANY),
            scratch_shapes=[pltpu.VMEM((2, TM, D), F32), pltpu.SemaphoreType.DMA((2,))],
        ),
        out_shape=jax.ShapeDtypeStruct((2 * n, D), F32),
        compiler_params=_cparams(("arbitrary",)),
        name="moe_scatter",
    )(p1, p2, xa, mod, g)


def _grouped_kernel(tile_ref, exp_ref, lo_ref, hi_ref, first_ref, newexp_ref, valid_ref,
                    xs_ref, w1_ref, w3_ref, w2_ref, ys_ref, wb1, wb3, wb2):
    q = pl.program_id(0)

    @pl.when(valid_ref[q] == 1)
    def _():
        @pl.when(newexp_ref[q] == 1)
        def _():
            wb1[...] = w1_ref[0, 0].astype(BF16)
            wb3[...] = w3_ref[0, 0].astype(BF16)
            wb2[...] = w2_ref[0, 0].astype(BF16)

        x = xs_ref[...].astype(BF16)
        a = jnp.dot(x, wb1[...], preferred_element_type=F32)
        b = jnp.dot(x, wb3[...], preferred_element_type=F32)
        rows = lax.broadcasted_iota(jnp.int32, (TM, 1), 0)
        mine = (rows >= lo_ref[q]) & (rows < hi_ref[q])
        act = jnp.where(mine, a * jax.nn.sigmoid(a) * b, 0.0)
        y = jnp.dot(act.astype(BF16), wb2[...], preferred_element_type=F32)

        @pl.when(first_ref[q] == 1)
        def _():
            ys_ref[...] = y

        @pl.when(first_ref[q] == 0)
        def _():
            ys_ref[...] += y


def _grouped(plan, xs, w1, w3, w2, layer):
    n_pairs = plan[0].shape[0]
    tile_map = lambda q, tile, exp, *_: (tile[q], 0)
    exp_map = lambda q, tile, exp, *_: (layer, exp[q], 0, 0)
    return pl.pallas_call(
        _grouped_kernel,
        grid_spec=pltpu.PrefetchScalarGridSpec(
            num_scalar_prefetch=7,
            grid=(n_pairs,),
            in_specs=[
                pl.BlockSpec((TM, D), tile_map),
                pl.BlockSpec((1, 1, D, D_EXP), exp_map),
                pl.BlockSpec((1, 1, D, D_EXP), exp_map),
                pl.BlockSpec((1, 1, D_EXP, D), exp_map),
            ],
            out_specs=pl.BlockSpec((TM, D), tile_map),
            scratch_shapes=[pltpu.VMEM((D, D_EXP), BF16), pltpu.VMEM((D, D_EXP), BF16),
                            pltpu.VMEM((D_EXP, D), BF16)],
        ),
        out_shape=jax.ShapeDtypeStruct(xs.shape, F32),
        compiler_params=_cparams(("arbitrary",)),
        name="moe_grouped",
    )(*plan, xs, w1, w3, w2)


def _gather_kernel(p1_ref, p2_ref, x_ref, mod_ref, tm_ref, fg_ref, ys_ref, o_ref, buf, sem,
                   *, final_norm):
    i = pl.program_id(0)
    n_steps = pl.num_programs(0)
    slot = lax.rem(i, 2)

    def issue(tile, s):
        base = tile * TM

        def body(r, carry):
            pltpu.make_async_copy(ys_ref.at[pl.ds(p1_ref[base + r], 1), :],
                                  buf.at[s, 0, pl.ds(r, 1), :], sem.at[s]).start()
            pltpu.make_async_copy(ys_ref.at[pl.ds(p2_ref[base + r], 1), :],
                                  buf.at[s, 1, pl.ds(r, 1), :], sem.at[s]).start()
            return carry

        lax.fori_loop(0, TM, body, 0, unroll=ISSUE_UNROLL)

    @pl.when(i == 0)
    def _():
        issue(0, 0)

    @pl.when(i + 1 < n_steps)
    def _():
        issue(i + 1, 1 - slot)

    _row_copy_wait(buf.at[slot, 0], sem.at[slot])
    _row_copy_wait(buf.at[slot, 1], sem.at[slot])
    y = tm_ref[:, 0:1] * buf[slot, 0] + tm_ref[:, 1:2] * buf[slot, 1]
    out = x_ref[...] + mod_ref[0, 5:6, :] * y
    if final_norm:
        out = out * lax.rsqrt(jnp.mean(out * out, axis=-1, keepdims=True) + EPS) * fg_ref[...]
    o_ref[...] = out


def _gather(p1, p2, xa, mod, tmeta, final_g, ys, n_tiles, final_norm):
    return pl.pallas_call(
        functools.partial(_gather_kernel, final_norm=final_norm),
        grid_spec=pltpu.PrefetchScalarGridSpec(
            num_scalar_prefetch=2,
            grid=(n_tiles,),
            in_specs=[
                pl.BlockSpec((TM, D), lambda i, p1, p2: (i, 0)),
                pl.BlockSpec((1, 6, D), lambda i, p1, p2: (_mod_row(i), 0, 0)),
                pl.BlockSpec((TM, LANES), lambda i, p1, p2: (i, 0)),
                pl.BlockSpec((1, D), lambda i, p1, p2: (0, 0)),
                pl.BlockSpec(memory_space=pl.ANY),
            ],
            out_specs=pl.BlockSpec((TM, D), lambda i, p1, p2: (i, 0)),
            scratch_shapes=[pltpu.VMEM((2, 2, TM, D), F32), pltpu.SemaphoreType.DMA((2,))],
        ),
        out_shape=jax.ShapeDtypeStruct((n_tiles * TM, D), F32),
        compiler_params=_cparams(("arbitrary",)),
        name="moe_gather",
    )(p1, p2, xa, mod, tmeta, final_g, ys)


def _issue_row_gathers(p1_ref, p2_ref, ys_ref, buf, sem, tile, s, unrolled):
    base = tile * TM

    def one(r):
        pltpu.make_async_copy(ys_ref.at[pl.ds(p1_ref[base + r], 1), :],
                              buf.at[s, 0, pl.ds(r, 1), :], sem.at[s]).start()
        pltpu.make_async_copy(ys_ref.at[pl.ds(p2_ref[base + r], 1), :],
                              buf.at[s, 1, pl.ds(r, 1), :], sem.at[s]).start()

    if unrolled:
        for r in range(TM):
            one(r)
    else:
        def body(r, carry):
            one(r)
            return carry

        lax.fori_loop(0, TM, body, 0, unroll=ISSUE_UNROLL)


def _inproj_kernel(p1_ref, p2_ref, x_ref, mod0_ref, tm_ref, ys_ref, mod_ref, g_ref, win_ref,
                   xn_ref, xm_ref, op_ref, buf, sem):
    i = pl.program_id(0)
    n_steps = pl.num_programs(0)
    slot = lax.rem(i, 2)

    @pl.when(i == 0)
    def _():
        _issue_row_gathers(p1_ref, p2_ref, ys_ref, buf, sem, 0, 0, False)

    _row_copy_wait(buf.at[slot, 0], sem.at[slot])
    _row_copy_wait(buf.at[slot, 1], sem.at[slot])
    y = tm_ref[:, 0:1] * buf[slot, 0] + tm_ref[:, 1:2] * buf[slot, 1]
    xn = x_ref[...] + mod0_ref[0, 5:6, :] * y
    xn_ref[...] = xn
    _issue_row_gathers(p1_ref, p2_ref, ys_ref, buf, sem, jnp.minimum(i + 1, n_steps - 1), 1 - slot,
                       True)
    h = _norm_mod(xn, g_ref[...], mod_ref[0, 0:1, :], mod_ref[0, 1:2, :])
    hb = h.astype(BF16)
    xm_ref[...] = jnp.dot(hb, win_ref[:, :B_INNER], preferred_element_type=F32).astype(BF16)
    op_ref[...] = jnp.dot(hb, win_ref[:, B_INNER:], preferred_element_type=F32).astype(BF16)

    @pl.when(i == n_steps - 1)
    def _():
        _row_copy_wait(buf.at[1 - slot, 0], sem.at[1 - slot])
        _row_copy_wait(buf.at[1 - slot, 1], sem.at[1 - slot])


def _inproj(p1, p2, xa, mod0, tmeta, ys, mod, g, w_in):
    const = lambda i, p1, p2: (0, 0)
    row = lambda i, p1, p2: (i, 0)
    mod_map = lambda i, p1, p2: (_mod_row(i), 0, 0)
    return pl.pallas_call(
        _inproj_kernel,
        grid_spec=pltpu.PrefetchScalarGridSpec(
            num_scalar_prefetch=2,
            grid=(ALL_TILES,),
            in_specs=[
                pl.BlockSpec((TM, D), row),
                pl.BlockSpec((1, 6, D), mod_map),
                pl.BlockSpec((TM, LANES), row),
                pl.BlockSpec(memory_space=pl.ANY),
                pl.BlockSpec((1, 6, D), mod_map),
                pl.BlockSpec((1, D), const),
                pl.BlockSpec((D, 2 * B_INNER), const),
            ],
            out_specs=[pl.BlockSpec((TM, D), row),
                       pl.BlockSpec((TM, B_INNER), row),
                       pl.BlockSpec((TM, B_INNER), row)],
            scratch_shapes=[pltpu.VMEM((2, 2, TM, D), F32), pltpu.SemaphoreType.DMA((2,))],
        ),
        out_shape=[jax.ShapeDtypeStruct((N_ALL, D), F32),
                   jax.ShapeDtypeStruct((N_ALL, B_INNER), BF16),
                   jax.ShapeDtypeStruct((N_ALL, B_INNER), BF16)],
        compiler_params=_cparams(("arbitrary",)),
        name="mlstm_inproj",
    )(p1, p2, xa, mod0, tmeta, ys, mod, g, w_in)


def _log_sigmoid(x):
    return jnp.minimum(x, 0.0) - jnp.log(1.0 + jnp.exp(-jnp.abs(x)))


def _qkv_kernel(xm_ref, prev_ref, next_ref, cw_ref, cb_ref, wq_ref, wkt_ref, wv_ref, wg_ref, wgt_ref,
                bg_ref, bgt_ref, xc_ref, q_ref, kt_ref, v_ref, gt_ref, gtt_ref, tot_ref):
    i = pl.program_id(0)
    lat = i < LAT_TILES
    first = jnp.where(lat, i % TILES_PER_SEQ == 0, True)
    last = jnp.where(lat, i % TILES_PER_SEQ == TILES_PER_SEQ - 1, True)
    xmb = xm_ref[...]
    xm = xmb.astype(F32)
    prev = jnp.where(first, 0.0, prev_ref[...].astype(F32))
    nxt = jnp.where(last, 0.0, next_ref[...].astype(F32))
    ext = jnp.concatenate([prev[HALO - 8:], xm, nxt[:8]], axis=0)
    acc = jnp.zeros((TM, B_INNER), F32) + cb_ref[...]
    for t in range(CONV_K):
        off = 8 + t - CONV_K // 2
        acc = acc + ext[off:off + TM] * cw_ref[t:t + 1, :]
    xc = acc * jax.nn.sigmoid(acc)
    xcb = xc.astype(BF16)
    xc_ref[...] = xcb
    for h in range(HEADS):
        hs = slice(h * DH, (h + 1) * DH)
        qh = jnp.dot(xcb[:, hs], wq_ref[h], preferred_element_type=F32)
        q_ref[:, h * DQK:(h + 1) * DQK] = (qh * (DQK ** -0.5)).astype(BF16)
        kt_ref[h * DQK:(h + 1) * DQK, :] = lax.dot_general(
            wkt_ref[h], xcb[:, hs], (((1,), (1,)), ((), ())), preferred_element_type=F32).astype(BF16)
        v_ref[:, hs] = jnp.dot(xmb[:, hs], wv_ref[h], preferred_element_type=F32).astype(BF16)
    gts = jnp.dot(xmb, wg_ref[...], preferred_element_type=F32) + bg_ref[...]
    lane = lax.broadcasted_iota(jnp.int32, (CHUNK, LANES), 1)
    lane_kind = (lane >> 3) & 3
    gtt = _dot_nt(wgt_ref[...], xmb) + bgt_ref[...]
    row = lax.broadcasted_iota(jnp.int32, (4 * HEADS, CHUNK), 0)
    row_kind = (row >> 3) & 3
    r = lax.broadcasted_iota(jnp.int32, (CHUNK, CHUNK), 0)
    c = lax.broadcasted_iota(jnp.int32, (CHUNK, CHUNK), 1)
    lower = (c <= r).astype(BF16)
    upper = (c >= r).astype(BF16)
    row_sums = jnp.concatenate([upper, lower, jnp.ones((CHUNK, LANES), BF16)], axis=1)
    for ch in range(TM // CHUNK):
        ts = slice(ch * CHUNK, (ch + 1) * CHUNK)
        g = gts[ts]
        g = jnp.where((lane_kind & 1) == 1, _log_sigmoid(g), g)
        terms = _bf16_terms(g, 3)
        pre = sum(_dot_nn(lower, t) for t in terms)
        suf = sum(_dot_nn(upper, t) for t in terms)
        gt_ref[ts, :] = jnp.where(lane_kind == 1, pre, jnp.where(lane_kind == 3, suf, g))
        gt = gtt[:, ts]
        gt = jnp.where((row_kind & 1) == 1, _log_sigmoid(gt), gt)
        sums = _dot_exact_rhs(gt, row_sums)
        gtt_ref[:, ts] = jnp.where(row_kind == 1, sums[:, :CHUNK],
                                   jnp.where(row_kind == 3, sums[:, CHUNK:2 * CHUNK], gt))
        tot_ref[:, ts] = sums[:, 2 * CHUNK:]


def _qkv(xm, conv_w, conv_b, wq, wkt, wv, wg, wgt, bg, bgt):
    const = lambda i: (0, 0)
    const3 = lambda i: (0, 0, 0)
    hb = TM // HALO
    n_hb = N_ALL // HALO
    return pl.pallas_call(
        _qkv_kernel,
        grid=(ALL_TILES,),
        in_specs=[
            pl.BlockSpec((TM, B_INNER), lambda i: (i, 0)),
            pl.BlockSpec((HALO, B_INNER), lambda i: (jnp.maximum(i * hb - 1, 0), 0)),
            pl.BlockSpec((HALO, B_INNER), lambda i: (jnp.minimum((i + 1) * hb, n_hb - 1), 0)),
            pl.BlockSpec((CONV_K, B_INNER), const),
            pl.BlockSpec((1, B_INNER), const),
            pl.BlockSpec((HEADS, DH, DQK), const3),
            pl.BlockSpec((HEADS, DQK, DH), const3),
            pl.BlockSpec((HEADS, DH, DV), const3),
            pl.BlockSpec((B_INNER, LANES), const),
            pl.BlockSpec((4 * HEADS, B_INNER), const),
            pl.BlockSpec((1, LANES), const),
            pl.BlockSpec((4 * HEADS, 1), const),
        ],
        out_specs=[
            pl.BlockSpec((TM, B_INNER), lambda i: (i, 0)),
            pl.BlockSpec((TM, HEADS * DQK), lambda i: (i, 0)),
            pl.BlockSpec((HEADS * DQK, TM), lambda i: (0, i)),
            pl.BlockSpec((TM, B_INNER), lambda i: (i, 0)),
            pl.BlockSpec((TM, LANES), lambda i: (i, 0)),
            pl.BlockSpec((4 * HEADS, TM), lambda i: (0, i)),
            pl.BlockSpec((4 * HEADS, TM), lambda i: (0, i)),
        ],
        out_shape=[
            jax.ShapeDtypeStruct((N_ALL, B_INNER), BF16),
            jax.ShapeDtypeStruct((N_ALL, HEADS * DQK), BF16),
            jax.ShapeDtypeStruct((HEADS * DQK, N_ALL), BF16),
            jax.ShapeDtypeStruct((N_ALL, B_INNER), BF16),
            jax.ShapeDtypeStruct((N_ALL, LANES), F32),
            jax.ShapeDtypeStruct((4 * HEADS, N_ALL), F32),
            jax.ShapeDtypeStruct((4 * HEADS, N_ALL), F32),
        ],
        compiler_params=_cparams(("parallel",)),
        name="mlstm_qkv",
    )(xm, xm, xm, conv_w, conv_b, wq, wkt, wv, wg, wgt, bg, bgt)


CTX_CHUNKS = CTX // CHUNK
LAT_CHUNKS = SEQ // CHUNK
SCAN_STEPS = CTX_CHUNKS + LAT_CHUNKS


def _scan_dir(nb, d, q_ref, kt_ref, v_ref, g_ref, gt_ref, tot_ref, o_ref, c_scr, m_scr):
    r = lax.broadcasted_iota(jnp.int32, (CHUNK, CHUNK), 0)
    c = lax.broadcasted_iota(jnp.int32, (CHUNK, CHUNK), 1)
    if d == 0:
        keep = c <= r
    else:
        keep = c >= r
    ones_blk = jnp.ones((CHUNK, LANES), BF16)
    bcol_all = g_ref[...]
    gtt = gt_ref[...]
    tot_all = tot_ref[...]
    base = 2 * HEADS * d
    for h in range(HEADS):
        li_r = gtt[base + h:base + h + 1, :]
        b_r = gtt[base + HEADS + h:base + HEADS + h + 1, :]
        b_last = tot_all[base + HEADS + h:base + HEADS + h + 1, :]
        b_full = jnp.broadcast_to(bcol_all[:, base + HEADS + h:base + HEADS + h + 1], (CHUNK, LANES))
        m_old = m_scr[nb, d, h]
        c_old = c_scr[nb, d, h]
        qh = q_ref[:, h * DQK:(h + 1) * DQK]
        kth = kt_ref[h * DQK:(h + 1) * DQK, :]
        vaug = jnp.concatenate([v_ref[:, h * DV:(h + 1) * DV], ones_blk], axis=1)
        g_r = b_last - b_r + li_r
        m_new = jnp.maximum(b_last + m_old, jnp.max(g_r, axis=-1, keepdims=True))
        decay = jnp.exp(b_last + m_old - m_new)
        wk = jnp.exp(g_r - m_new)
        kwt = (kth.astype(F32) * wk).astype(BF16)
        c_scr[nb, d, h] = jnp.concatenate([decay] * 3, axis=1) * c_old + jnp.dot(
            kwt, vaug, preferred_element_type=F32)
        m_scr[nb, d, h] = m_new

        dmat = jnp.where(keep, b_full - (b_r - li_r), NEG_INF)
        inter = b_full + m_old
        m_t = jnp.maximum(inter, jnp.max(dmat, axis=-1, keepdims=True))
        p = jnp.exp(dmat - m_t)
        s = jnp.dot(qh, kth, preferred_element_type=F32) * p
        a = jnp.exp(inter - m_t)
        num = jnp.concatenate([a] * 3, axis=1) * jnp.dot(
            qh, c_old.astype(BF16), preferred_element_type=F32) + jnp.dot(
                s.astype(BF16), vaug, preferred_element_type=F32)
        inv = 1.0 / jnp.maximum(jnp.abs(num[:, DV:]), jnp.exp(-m_t))
        o_ref[nb, :, h * DV:(h + 1) * DV] = (
            num[:, :DV] * jnp.concatenate([inv] * 2, axis=1)).astype(BF16)


SCAN_NB = 2
SCAN_IN = 6


def _scan_kernel(*refs):
    n_in = SCAN_NB * 2 * SCAN_IN
    ins, outs = refs[:n_in], refs[n_in:n_in + 2]
    c_scr, m_scr = refs[n_in + 2:]

    @pl.when(pl.program_id(1) == 0)
    def _():
        c_scr[...] = jnp.zeros_like(c_scr)
        m_scr[...] = jnp.zeros_like(m_scr)

    for nb in range(SCAN_NB):
        for d in range(2):
            k = nb * 2 + d
            _scan_dir(nb, d, *ins[k * SCAN_IN:(k + 1) * SCAN_IN], outs[d], c_scr, m_scr)


def _scan(q, kt, v, gts, gtt, tot):
    lat_blk = N_LAT // CHUNK

    def fwd_in(b, j):
        return jnp.where(j < CTX_CHUNKS, lat_blk + b * CTX_CHUNKS + j, b * LAT_CHUNKS + j - CTX_CHUNKS)

    def bwd_in(b, j):
        return jnp.where(j < CTX_CHUNKS, lat_blk + b * CTX_CHUNKS + (CTX_CHUNKS - 1 - j),
                         b * LAT_CHUNKS + (SCAN_STEPS - 1 - j))

    def fwd_out(b, j):
        return b * LAT_CHUNKS + jnp.maximum(j - CTX_CHUNKS, 0)

    def bwd_out(b, j):
        return b * LAT_CHUNKS + jnp.minimum(SCAN_STEPS - 1 - j, LAT_CHUNKS - 1)

    n_g = BATCH // SCAN_NB

    def specs(fn, nb):
        blk = lambda g, j: fn(nb * n_g + g, j)
        return [
            pl.BlockSpec((CHUNK, HEADS * DQK), lambda g, j: (blk(g, j), 0)),
            pl.BlockSpec((HEADS * DQK, CHUNK), lambda g, j: (0, blk(g, j))),
            pl.BlockSpec((CHUNK, B_INNER), lambda g, j: (blk(g, j), 0)),
            pl.BlockSpec((CHUNK, LANES), lambda g, j: (blk(g, j), 0)),
            pl.BlockSpec((4 * HEADS, CHUNK), lambda g, j: (0, blk(g, j))),
            pl.BlockSpec((4 * HEADS, CHUNK), lambda g, j: (0, blk(g, j))),
        ]

    def out_spec(fn):
        return pl.BlockSpec((SCAN_NB, CHUNK, B_INNER), lambda g, j: (0, fn(g, j), 0))

    in_specs, operands = [], []
    for nb in range(SCAN_NB):
        for fn_in in (fwd_in, bwd_in):
            in_specs += specs(fn_in, nb)
            operands += [q, kt, v, gts, gtt, tot]
    h_shape = jax.ShapeDtypeStruct((SCAN_NB, N_LAT // SCAN_NB, B_INNER), BF16)
    hf, hb = pl.pallas_call(
        _scan_kernel,
        grid=(n_g, SCAN_STEPS),
        in_specs=in_specs,
        out_specs=[out_spec(fwd_out), out_spec(bwd_out)],
        out_shape=[h_shape, h_shape],
        scratch_shapes=[pltpu.VMEM((SCAN_NB, 2, HEADS, DQK, DV + LANES), F32),
                        pltpu.VMEM((SCAN_NB, 2, HEADS, 1, LANES), F32)],
        compiler_params=_cparams(("parallel", "arbitrary")),
        name="mlstm_scan",
    )(*operands)
    return hf.reshape(N_LAT, B_INNER), hb.reshape(N_LAT, B_INNER)


def _combine_kernel(x_ref, mod_ref, hf_ref, hb_ref, xc_ref, op_ref, hg_ref, sk_ref, wout_ref,
                    g2_ref, wr_ref, br_ref, o_ref, meta_ref, tmeta_ref, cnt_ref, cntr_ref,
                    y_scr, carry, carry_r):
    for h in range(HEADS):
        hs = slice(h * DV, (h + 1) * DV)
        s = hf_ref[:, hs].astype(F32) + hb_ref[:, hs].astype(F32)
        s = s * lax.rsqrt(jnp.mean(s * s, axis=-1, keepdims=True) + EPS)
        y = jax.nn.sigmoid(op_ref[:, hs].astype(F32)) * (
            s * hg_ref[:, hs] + sk_ref[:, hs] * xc_ref[:, hs].astype(F32))
        y_scr[:, hs] = y.astype(BF16)
    out = jnp.dot(y_scr[...], wout_ref[...], preferred_element_type=F32)
    xn = x_ref[...] + mod_ref[0, 2:3, :] * out
    o_ref[...] = xn
    _route_tile(xn, mod_ref, g2_ref, wr_ref, br_ref, meta_ref, tmeta_ref, cnt_ref, cntr_ref,
                carry, carry_r)


def _combine(xa, mod, hf, hb, xc, op, head_g, skip, w_out, g2, wr, br):
    const = lambda i: (0, 0)
    row = lambda i: (i, 0)
    r_in, r_out, r_shape, r_scratch = _router_specs(LAT_TILES)
    return pl.pallas_call(
        _combine_kernel,
        grid=(LAT_TILES,),
        in_specs=[
            pl.BlockSpec((TM, D), row),
            pl.BlockSpec((1, 6, D), lambda i: (_mod_row(i), 0, 0)),
            pl.BlockSpec((TM, B_INNER), row),
            pl.BlockSpec((TM, B_INNER), row),
            pl.BlockSpec((TM, B_INNER), row),
            pl.BlockSpec((TM, B_INNER), row),
            pl.BlockSpec((1, B_INNER), const),
            pl.BlockSpec((1, B_INNER), const),
            pl.BlockSpec((B_INNER, D), const),
        ] + r_in,
        out_specs=[pl.BlockSpec((TM, D), row)] + r_out,
        out_shape=[jax.ShapeDtypeStruct((N_LAT, D), F32)] + r_shape,
        scratch_shapes=[pltpu.VMEM((TM, B_INNER), BF16)] + r_scratch,
        compiler_params=_cparams(("arbitrary",)),
        name="mlstm_combine",
    )(xa, mod, hf, hb, xc, op, head_g, skip, w_out, g2, wr, br)


def _router_weights(w_grp, b_grp, w_exp, b_exp):
    wr = jnp.zeros((LANES, D), F32).at[:N_EXP].set(w_exp.T).at[N_EXP:N_EXP + N_GRP].set(w_grp.T)
    br = jnp.zeros((LANES, 1), F32).at[:N_EXP, 0].set(b_exp).at[N_EXP:N_EXP + N_GRP, 0].set(b_grp)
    return wr.astype(BF16), br


PLAN_BLK = 2048
PLAN_PAIRS = 256


def _plan_kernel(meta_ref, cc_ref, cr_ref, pp_ref, plan_ref):
    row = lax.broadcasted_iota(jnp.int32, (LANES, LANES), 0)
    col = lax.broadcasted_iota(jnp.int32, (LANES, LANES), 1)
    cnt_c = cc_ref[...]
    cnt_r = cr_ref[0:1, :]
    starts_c = jnp.sum(jnp.where(col < row, cnt_r, 0.0), axis=1, keepdims=True)
    erow = lax.broadcasted_iota(jnp.int32, (LANES, PLAN_BLK), 0).astype(F32)
    for j in range(2):
        pos = jnp.sum(jnp.where(erow == meta_ref[j:j + 1, :], starts_c, 0.0), axis=0,
                      keepdims=True) + meta_ref[j + 2:j + 3, :]
        pp_ref[j:j + 1, :] = pos.astype(jnp.int32)
    pp_ref[2:8, :] = jnp.zeros((6, PLAN_BLK), jnp.int32)

    @pl.when(pl.program_id(0) == 0)
    def _():
        ends_c = starts_c + cnt_c
        starts_r = jnp.sum(jnp.where(row < col, cnt_c, 0.0), axis=0, keepdims=True)
        ends_r = starts_r + cnt_r

        def tiles(st, en, cn):
            first_tile = jnp.floor(st * (1.0 / TM))
            n = jnp.where(cn > 0.0, jnp.floor((en - 1.0) * (1.0 / TM)) - first_tile + 1.0, 0.0)
            return first_tile, n

        ft_c, pairs_c = tiles(starts_c, ends_c, cnt_c)
        _, pairs_r = tiles(starts_r, ends_r, cnt_r)
        pend_c = jnp.sum(jnp.where(col <= row, pairs_r, 0.0), axis=1, keepdims=True)
        pstart_c = pend_c - pairs_c
        total = jnp.sum(pairs_r, axis=1, keepdims=True)
        q = lax.broadcasted_iota(jnp.int32, (1, PLAN_PAIRS), 1).astype(F32)
        erow_p = lax.broadcasted_iota(jnp.int32, (LANES, PLAN_PAIRS), 0).astype(F32)

        def at(qv):
            qc = jnp.maximum(jnp.minimum(qv, total - 1.0), 0.0)
            e = jnp.sum(jnp.where(qc >= pend_c, 1.0, 0.0), axis=0, keepdims=True)
            oh = erow_p == e
            tile = jnp.sum(jnp.where(oh, ft_c - pstart_c, 0.0), axis=0, keepdims=True) + qc
            return e, oh, tile

        e_q, oh, tile_q = at(q)
        e_p, _, tile_p = at(q - 1.0)
        lo = jnp.sum(jnp.where(oh, starts_c, 0.0), axis=0, keepdims=True) - tile_q * TM
        hi = jnp.sum(jnp.where(oh, ends_c, 0.0), axis=0, keepdims=True) - tile_q * TM
        rows = (tile_q, e_q, jnp.clip(lo, 0.0, TM), jnp.clip(hi, 0.0, TM),
                jnp.where((q == 0.0) | (tile_q != tile_p), 1.0, 0.0),
                jnp.where((q == 0.0) | (e_q != e_p), 1.0, 0.0),
                jnp.where(q < total, 1.0, 0.0),
                jnp.zeros((1, PLAN_PAIRS), F32))
        for j, v in enumerate(rows):
            plan_ref[j:j + 1, :] = v.astype(jnp.int32)


def _route_plan(meta, cnt_c, cnt_r, n_rows):
    n_pairs = 2 * n_rows // TM + N_EXP - 1
    assert n_pairs <= PLAN_PAIRS and n_rows % PLAN_BLK == 0
    const = lambda i: (0, 0)
    pp, plan = pl.pallas_call(
        _plan_kernel,
        grid=(n_rows // PLAN_BLK,),
        in_specs=[pl.BlockSpec((8, PLAN_BLK), lambda i: (0, i)),
                  pl.BlockSpec((LANES, 1), const),
                  pl.BlockSpec((8, LANES), const)],
        out_specs=[pl.BlockSpec((8, PLAN_BLK), lambda i: (0, i)),
                   pl.BlockSpec((8, PLAN_PAIRS), const)],
        out_shape=[jax.ShapeDtypeStruct((8, n_rows), jnp.int32),
                   jax.ShapeDtypeStruct((8, PLAN_PAIRS), jnp.int32)],
        compiler_params=_cparams(("arbitrary",)),
        name="moe_plan",
    )(meta, cnt_c, cnt_r)
    return pp[0], pp[1], tuple(plan[j, :n_pairs] for j in range(7))


def _moe_experts(xa, routing, mod, g2, w1, w3, w2, layer, n_rows):
    meta, _, cnt_c, cnt_r = routing
    p1, p2, plan = _route_plan(meta, cnt_c, cnt_r, n_rows)
    xs = _scatter(p1, p2, xa, mod, g2, n_rows // TM)
    return p1, p2, _grouped(plan, xs, w1, w3, w2, layer)


def kernel(x, c, ctx, c_ctx, norm_g, w_ada, b_ada, a_w_in, a_b_in, a_g_v, a_w_s, a_b_s, a_w_out,
           b_w_in, b_conv_w, b_conv_b, b_w_q, b_w_k, b_w_v, b_w_gate, b_b_gate, b_head_g, b_skip,
           b_w_out, moe_w_grp, moe_b_grp, moe_w_exp, moe_b_exp, moe_w1, moe_w3, moe_w2, final_g):
    cc = jnp.zeros((MOD_ROWS, D), F32).at[:BATCH].set(c).at[BATCH].set(c_ctx)
    mods = _ada(cc, w_ada, b_ada).reshape(2, MOD_ROWS, 6, D)

    mod = mods[0]
    b_s_full = jnp.repeat(a_b_s[0].T, A_GC, axis=1)
    g2 = norm_g[0, 1].reshape(1, D)
    wr, br = _router_weights(moe_w_grp[0], moe_b_grp[0], moe_w_exp[0], moe_b_exp[0])
    xa, *routing = _gmlp(x.reshape(N_LAT, D), ctx.reshape(N_CTX, D), mod, norm_g[0, 0].reshape(1, D),
                         a_w_in[0].astype(BF16), a_b_in[0].reshape(1, -1), a_g_v[0].reshape(1, -1),
                         a_w_s[0].astype(BF16), b_s_full, a_w_out[0].astype(BF16), g2, wr, br)
    p1, p2, ys = _moe_experts(xa, routing, mod, g2, moe_w1, moe_w3, moe_w2, 0, N_ALL)

    mod = mods[1]
    xa, xm, op = _inproj(p1, p2, xa, mods[0], routing[1], ys, mod, norm_g[1, 0].reshape(1, D),
                         b_w_in[0].astype(BF16))
    wg = jnp.zeros((B_INNER, LANES), F32).at[:, :4 * HEADS].set(b_w_gate[0]).astype(BF16)
    bg = jnp.zeros((1, LANES), F32).at[0, :4 * HEADS].set(b_b_gate[0])
    xc, q, kt, v, gts, gtt, tot = _qkv(xm, b_conv_w[0], b_conv_b[0].reshape(1, -1),
                                  b_w_q[0].astype(BF16),
                                  jnp.transpose(b_w_k[0], (0, 2, 1)).astype(BF16),
                                  b_w_v[0].astype(BF16), wg, b_w_gate[0].T.astype(BF16), bg,
                                  b_b_gate[0].reshape(-1, 1))
    hf, hb = _scan(q, kt, v, gts, gtt, tot)
    g2 = norm_g[1, 1].reshape(1, D)
    wr, br = _router_weights(moe_w_grp[1], moe_b_grp[1], moe_w_exp[1], moe_b_exp[1])
    xl, *routing = _combine(xa, mod, hf, hb, xc, op, b_head_g[0].reshape(1, -1),
                            b_skip[0].reshape(1, -1), b_w_out[0].astype(BF16), g2, wr, br)
    p1, p2, ys = _moe_experts(xl, routing, mod, g2, moe_w1, moe_w3, moe_w2, 1, N_LAT)
    out = _gather(p1, p2, xl, mod, routing[1], final_g.reshape(1, D), ys, LAT_TILES, True)
    return out.reshape(BATCH, SEQ, D)
```

```python
import functools

import jax
import jax.numpy as jnp
from jax import lax
from jax.experimental import pallas as pl
from jax.experimental.pallas import tpu as pltpu

F32 = jnp.float32
BF16 = jnp.bfloat16

D = 1024
BATCH = 8
SEQ = 2048
CTX = 256
EPS = 1e-6
NEG_INF = -1e30
N_LAT = BATCH * SEQ
N_CTX = BATCH * CTX
N_ALL = N_LAT + N_CTX

TM = 256
LAT_TILES = N_LAT // TM
ALL_TILES = N_ALL // TM
TILES_PER_SEQ = SEQ // TM
MOD_ROWS = 16

CHUNK = 128
A_HALF = 2048
A_GROUPS = 8
A_GC = A_HALF // A_GROUPS
B_INNER = 2048
HEADS = 8
DH = B_INNER // HEADS
DQK = DH // 2
DV = DH
CONV_K = 5
HALO = 16
N_EXP = 32
N_GRP = 4
EXP_PER_GRP = 8
D_EXP = 512
LANES = 128

VMEM_LIMIT = 56 * 1024 * 1024


def _cparams(sem):
    return pltpu.CompilerParams(dimension_semantics=sem, vmem_limit_bytes=VMEM_LIMIT)


def _mod_row(i):
    return jnp.where(i < LAT_TILES, i // TILES_PER_SEQ, BATCH)


def _norm_mod(x, g, shift, scale):
    y = x * lax.rsqrt(jnp.mean(x * x, axis=-1, keepdims=True) + EPS) * g
    return y * (1.0 + scale) + shift


def _bf16_terms(x, n):
    terms = []
    r = x
    for _ in range(n):
        t = r.astype(BF16)
        terms.append(t)
        r = r - t.astype(F32)
    return terms


def _dot_nt(a, b):
    return lax.dot_general(a, b, (((1,), (1,)), ((), ())), preferred_element_type=F32)


def _dot_nn(a, b):
    return jnp.dot(a, b, preferred_element_type=F32)


def _dot3(a, b, dot=_dot_nn):
    a1, a2 = _bf16_terms(a, 2)
    b1, b2 = _bf16_terms(b, 2)
    return dot(a1, b1) + (dot(a1, b2) + dot(a2, b1))


def _dot_exact_lhs(a_bf16, b, dot=_dot_nn):
    return sum(dot(a_bf16, t) for t in _bf16_terms(b, 3))


def _dot_exact_rhs(a, b_bf16, dot=_dot_nn):
    return sum(dot(t, b_bf16) for t in _bf16_terms(a, 3))


def _gelu_tanh(x):
    return 0.5 * x * (1.0 + jnp.tanh(0.7978845608028654 * (x + 0.044715 * (x * x * x))))


ADA_BN = 1536


def _ada_kernel(c_ref, w_ref, b_ref, o_ref):
    c = c_ref[...]
    a = c * jax.nn.sigmoid(c)
    o_ref[0] = _dot3(a, w_ref[0]) + b_ref[0]


def _ada(cc, w_ada, b_ada):
    depth = w_ada.shape[0]
    return pl.pallas_call(
        _ada_kernel,
        grid=(depth, 6 * D // ADA_BN),
        in_specs=[
            pl.BlockSpec((MOD_ROWS, D), lambda l, j: (0, 0)),
            pl.BlockSpec((1, D, ADA_BN), lambda l, j: (l, 0, j)),
            pl.BlockSpec((1, 1, ADA_BN), lambda l, j: (l, 0, j)),
        ],
        out_specs=pl.BlockSpec((1, MOD_ROWS, ADA_BN), lambda l, j: (l, 0, j)),
        out_shape=jax.ShapeDtypeStruct((depth, MOD_ROWS, 6 * D), F32),
        compiler_params=_cparams(("parallel", "parallel")),
        name="ada",
    )(cc, w_ada, b_ada.reshape(depth, 1, 6 * D))


GM_CH = 512


def _gmlp_kernel(xl_ref, xc_ref, mod_ref, g_ref, win_ref, bin_ref, gv_ref, ws_ref, bs_ref, wout_ref,
                 g2_ref, wr_ref, br_ref, o_ref, meta_ref, tmeta_ref, cnt_ref, cntr_ref,
                 z_scr, y_scr, carry, carry_r):
    x = jnp.where(pl.program_id(0) < LAT_TILES, xl_ref[...], xc_ref[...])
    h = _norm_mod(x, g_ref[...], mod_ref[0, 0:1, :], mod_ref[0, 1:2, :])
    hb = h.astype(BF16)
    s1 = jnp.zeros((TM, 1), F32)
    s2 = jnp.zeros((TM, 1), F32)
    for j in range(2 * A_HALF // GM_CH):
        cs = slice(j * GM_CH, (j + 1) * GM_CH)
        zc = jnp.dot(hb, win_ref[:, cs], preferred_element_type=F32) + bin_ref[:, cs]
        zc = _gelu_tanh(zc)
        z_scr[:, cs] = zc
        if j * GM_CH >= A_HALF:
            s1 = s1 + jnp.sum(zc, axis=-1, keepdims=True)
            s2 = s2 + jnp.sum(zc * zc, axis=-1, keepdims=True)
    mu = s1 * (1.0 / A_HALF)
    rstd = lax.rsqrt(s2 * (1.0 / A_HALF) - mu * mu + EPS)
    for c in range(TM // CHUNK):
        rs = slice(c * CHUNK, (c + 1) * CHUNK)
        for g in range(A_GROUPS):
            cs = slice(g * A_GC, (g + 1) * A_GC)
            vs = slice(A_HALF + g * A_GC, A_HALF + (g + 1) * A_GC)
            v = (z_scr[rs, vs] - mu[rs]) * rstd[rs] * gv_ref[:, cs]
            s = jnp.dot(ws_ref[g], v.astype(BF16), preferred_element_type=F32) + bs_ref[:, cs]
            y_scr[rs, cs] = (z_scr[rs, cs] * s).astype(BF16)
    out = jnp.dot(y_scr[...], wout_ref[...], preferred_element_type=F32)
    xn = x + mod_ref[0, 2:3, :] * out
    o_ref[...] = xn
    _route_tile(xn, mod_ref, g2_ref, wr_ref, br_ref, meta_ref, tmeta_ref, cnt_ref, cntr_ref,
                carry, carry_r)


def _gmlp(xl, xc, mod, g, w_in, b_in, g_v, w_s, b_s_full, w_out, g2, wr, br):
    const = lambda i: (0, 0)
    r_in, r_out, r_shape, r_scratch = _router_specs(ALL_TILES)
    return pl.pallas_call(
        _gmlp_kernel,
        grid=(ALL_TILES,),
        in_specs=[
            pl.BlockSpec((TM, D), lambda i: (jnp.minimum(i, LAT_TILES - 1), 0)),
            pl.BlockSpec((TM, D), lambda i: (jnp.maximum(i - LAT_TILES, 0), 0)),
            pl.BlockSpec((1, 6, D), lambda i: (_mod_row(i), 0, 0)),
            pl.BlockSpec((1, D), const),
            pl.BlockSpec((D, 2 * A_HALF), const),
            pl.BlockSpec((1, 2 * A_HALF), const),
            pl.BlockSpec((1, A_HALF), const),
            pl.BlockSpec((A_GROUPS, CHUNK, CHUNK), lambda i: (0, 0, 0)),
            pl.BlockSpec((CHUNK, A_HALF), const),
            pl.BlockSpec((A_HALF, D), const),
        ] + r_in,
        out_specs=[pl.BlockSpec((TM, D), lambda i: (i, 0))] + r_out,
        out_shape=[jax.ShapeDtypeStruct((N_ALL, D), F32)] + r_shape,
        scratch_shapes=[pltpu.VMEM((TM, 2 * A_HALF), F32), pltpu.VMEM((TM, A_HALF), BF16)] + r_scratch,
        compiler_params=_cparams(("arbitrary",)),
        name="gmlp",
    )(xl, xc, mod, g, w_in, b_in, g_v, w_s, b_s_full, w_out, g2, wr, br)


def _route_tile(x, mod_ref, g_ref, wr_ref, br_ref, meta_ref, tmeta_ref, cnt_ref, cntr_ref,
                carry, carry_r):
    @pl.when(pl.program_id(0) == 0)
    def _():
        carry[...] = jnp.zeros_like(carry)
        carry_r[...] = jnp.zeros_like(carry_r)

    h = _norm_mod(x, g_ref[...], mod_ref[0, 3:4, :], mod_ref[0, 4:5, :])
    lt = _dot_nt(wr_ref[...], h.astype(BF16)) + br_ref[...]
    e_t = lt[0:N_EXP]
    row8 = lax.broadcasted_iota(jnp.int32, (EXP_PER_GRP, TM), 0).astype(F32)
    g_t = jnp.where(row8 < N_GRP, lt[N_EXP:N_EXP + EXP_PER_GRP], -jnp.inf)
    gmax = jnp.max(g_t, axis=0, keepdims=True)
    p_g = 1.0 / jnp.sum(jnp.exp(g_t - gmax), axis=0, keepdims=True)
    g_idx = jnp.min(jnp.where(g_t == gmax, row8, float(EXP_PER_GRP)), axis=0, keepdims=True)
    sel = jnp.zeros((EXP_PER_GRP, TM), F32)
    for g in range(N_GRP):
        sel = sel + jnp.where(g_idx == g, e_t[g * EXP_PER_GRP:(g + 1) * EXP_PER_GRP], 0.0)
    m1 = jnp.max(sel, axis=0, keepdims=True)
    i1 = jnp.min(jnp.where(sel == m1, row8, float(EXP_PER_GRP)), axis=0, keepdims=True)
    sel2 = jnp.where(row8 == i1, -jnp.inf, sel)
    m2 = jnp.max(sel2, axis=0, keepdims=True)
    i2 = jnp.min(jnp.where(sel2 == m2, row8, float(EXP_PER_GRP)), axis=0, keepdims=True)
    e2 = jnp.exp(m2 - m1)
    w1 = p_g / (1.0 + e2)
    w2 = p_g * e2 / (1.0 + e2)
    row = lax.broadcasted_iota(jnp.int32, (LANES, TM), 0).astype(F32)
    id1 = g_idx * EXP_PER_GRP + i1
    id2 = g_idx * EXP_PER_GRP + i2
    oh1 = row == id1
    oh2 = row == id2
    oh = jnp.where(oh1, 1.0, 0.0) + jnp.where(oh2, 1.0, 0.0)
    before = (lax.broadcasted_iota(jnp.int32, (TM, TM), 0)
              < lax.broadcasted_iota(jnp.int32, (TM, TM), 1)).astype(BF16)
    ohb = oh.astype(BF16)
    tot = jnp.dot(ohb, before, preferred_element_type=F32) + carry[...]
    r1 = jnp.sum(jnp.where(oh1, tot, 0.0), axis=0, keepdims=True)
    r2 = jnp.sum(jnp.where(oh2, tot, 0.0), axis=0, keepdims=True)
    carry[...] += jnp.sum(oh, axis=-1, keepdims=True)
    carry_r[...] += _dot_nt(jnp.ones((8, TM), BF16), ohb)
    cnt_ref[...] = carry[...]
    cntr_ref[...] = carry_r[...]
    meta_ref[0:1, :] = id1
    meta_ref[1:2, :] = id2
    meta_ref[2:3, :] = r1
    meta_ref[3:4, :] = r2
    meta_ref[4:8, :] = jnp.zeros((4, TM), F32)
    tmeta_ref[...] = (jnp.where(row == 0.0, w1, 0.0) + jnp.where(row == 1.0, w2, 0.0)).T


def _router_specs(n_tiles):
    const = lambda i: (0, 0)
    n = n_tiles * TM
    in_specs = [pl.BlockSpec((1, D), const), pl.BlockSpec((LANES, D), const),
                pl.BlockSpec((LANES, 1), const)]
    out_specs = [pl.BlockSpec((8, TM), lambda i: (0, i)),
                 pl.BlockSpec((TM, LANES), lambda i: (i, 0)),
                 pl.BlockSpec((LANES, 1), const),
                 pl.BlockSpec((8, LANES), const)]
    out_shape = [jax.ShapeDtypeStruct((8, n), F32), jax.ShapeDtypeStruct((n, LANES), F32),
                 jax.ShapeDtypeStruct((LANES, 1), F32), jax.ShapeDtypeStruct((8, LANES), F32)]
    scratch = [pltpu.VMEM((LANES, 1), F32), pltpu.VMEM((8, LANES), F32)]
    return in_specs, out_specs, out_shape, scratch


ISSUE_UNROLL = 8
ROW_TILE = (8, LANES)


def _row_copy_wait(buf_slot, sem_slot):
    pltpu.make_async_copy(buf_slot, buf_slot, sem_slot).wait()


def _scatter_kernel(p1_ref, p2_ref, x_ref, mod_ref, g_ref, xs_ref, buf, sem):
    i = pl.program_id(0)
    n_steps = pl.num_programs(0)
    slot = lax.rem(i, 2)

    def wait_slot(s):
        _row_copy_wait(buf.at[s], sem.at[s])
        _row_copy_wait(buf.at[s], sem.at[s])

    @pl.when(i >= 2)
    def _():
        wait_slot(slot)

    h = _norm_mod(x_ref[...], g_ref[...], mod_ref[0, 3:4, :], mod_ref[0, 4:5, :])
    buf[slot] = h.reshape(TM, *ROW_TILE)
    base = i * TM

    def body(r, carry):
        src = buf.at[slot, r]
        pltpu.make_async_copy(src, xs_ref.at[p1_ref[base + r]], sem.at[slot]).start()
        pltpu.make_async_copy(src, xs_ref.at[p2_ref[base + r]], sem.at[slot]).start()
        return carry

    lax.fori_loop(0, TM, body, 0, unroll=ISSUE_UNROLL)

    @pl.when(i == n_steps - 1)
    def _():
        @pl.when(i >= 1)
        def _():
            wait_slot(1 - slot)

        wait_slot(slot)


def _scatter(p1, p2, xa, mod, g, n_tiles):
    n = n_tiles * TM
    return pl.pallas_call(
        _scatter_kernel,
        grid_spec=pltpu.PrefetchScalarGridSpec(
            num_scalar_prefetch=2,
            grid=(n_tiles,),
            in_specs=[
                pl.BlockSpec((TM, D), lambda i, p1, p2: (i, 0)),
                pl.BlockSpec((1, 6, D), lambda i, p1, p2: (_mod_row(i), 0, 0)),
                pl.BlockSpec((1, D), lambda i, p1, p2: (0, 0)),
            ],
            out_specs=pl.BlockSpec(memory_space=pl.ANY),
            scratch_shapes=[pltpu.VMEM((2, TM) + ROW_TILE, F32), pltpu.SemaphoreType.DMA((2,))],
        ),
        out_shape=jax.ShapeDtypeStruct((2 * n,) + ROW_TILE, F32),
        compiler_params=_cparams(("arbitrary",)),
        name="moe_scatter",
    )(p1, p2, xa, mod, g)


def _grouped_kernel(tile_ref, exp_ref, lo_ref, hi_ref, first_ref, newexp_ref, valid_ref,
                    xs_ref, w1_ref, w3_ref, w2_ref, ys_ref, wb1, wb3, wb2):
    q = pl.program_id(0)

    @pl.when(valid_ref[q] == 1)
    def _():
        @pl.when(newexp_ref[q] == 1)
        def _():
            wb1[...] = w1_ref[0, 0].astype(BF16)
            wb3[...] = w3_ref[0, 0].astype(BF16)
            wb2[...] = w2_ref[0, 0].astype(BF16)

        x = xs_ref[...].reshape(TM, D).astype(BF16)
        a = jnp.dot(x, wb1[...], preferred_element_type=F32)
        b = jnp.dot(x, wb3[...], preferred_element_type=F32)
        rows = lax.broadcasted_iota(jnp.int32, (TM, 1), 0)
        mine = (rows >= lo_ref[q]) & (rows < hi_ref[q])
        act = jnp.where(mine, a * jax.nn.sigmoid(a) * b, 0.0)
        y = jnp.dot(act.astype(BF16), wb2[...], preferred_element_type=F32).reshape(TM, *ROW_TILE)

        @pl.when(first_ref[q] == 1)
        def _():
            ys_ref[...] = y

        @pl.when(first_ref[q] == 0)
        def _():
            ys_ref[...] += y


def _grouped(plan, xs, w1, w3, w2, layer):
    n_pairs = plan[0].shape[0]
    tile_map = lambda q, tile, exp, *_: (tile[q], 0, 0)
    exp_map = lambda q, tile, exp, *_: (layer, exp[q], 0, 0)
    return pl.pallas_call(
        _grouped_kernel,
        grid_spec=pltpu.PrefetchScalarGridSpec(
            num_scalar_prefetch=7,
            grid=(n_pairs,),
            in_specs=[
                pl.BlockSpec((TM,) + ROW_TILE, tile_map),
                pl.BlockSpec((1, 1, D, D_EXP), exp_map),
                pl.BlockSpec((1, 1, D, D_EXP), exp_map),
                pl.BlockSpec((1, 1, D_EXP, D), exp_map),
            ],
            out_specs=pl.BlockSpec((TM,) + ROW_TILE, tile_map),
            scratch_shapes=[pltpu.VMEM((D, D_EXP), BF16), pltpu.VMEM((D, D_EXP), BF16),
                            pltpu.VMEM((D_EXP, D), BF16)],
        ),
        out_shape=jax.ShapeDtypeStruct(xs.shape, F32),
        compiler_params=_cparams(("arbitrary",)),
        name="moe_grouped",
    )(*plan, xs, w1, w3, w2)


def _gather_kernel(p1_ref, p2_ref, x_ref, mod_ref, tm_ref, fg_ref, ys_ref, o_ref, buf, sem,
                   *, final_norm):
    i = pl.program_id(0)
    n_steps = pl.num_programs(0)
    slot = lax.rem(i, 2)

    def issue(tile, s):
        base = tile * TM

        def body(r, carry):
            pltpu.make_async_copy(ys_ref.at[p1_ref[base + r]], buf.at[s, 0, r], sem.at[s]).start()
            pltpu.make_async_copy(ys_ref.at[p2_ref[base + r]], buf.at[s, 1, r], sem.at[s]).start()
            return carry

        lax.fori_loop(0, TM, body, 0, unroll=ISSUE_UNROLL)

    @pl.when(i == 0)
    def _():
        issue(0, 0)

    @pl.when(i + 1 < n_steps)
    def _():
        issue(i + 1, 1 - slot)

    _row_copy_wait(buf.at[slot, 0], sem.at[slot])
    _row_copy_wait(buf.at[slot, 1], sem.at[slot])
    y = (tm_ref[:, 0:1] * buf[slot, 0].reshape(TM, D)
         + tm_ref[:, 1:2] * buf[slot, 1].reshape(TM, D))
    out = x_ref[...] + mod_ref[0, 5:6, :] * y
    if final_norm:
        out = out * lax.rsqrt(jnp.mean(out * out, axis=-1, keepdims=True) + EPS) * fg_ref[...]
    o_ref[...] = out


def _gather(p1, p2, xa, mod, tmeta, final_g, ys, n_tiles, final_norm):
    return pl.pallas_call(
        functools.partial(_gather_kernel, final_norm=final_norm),
        grid_spec=pltpu.PrefetchScalarGridSpec(
            num_scalar_prefetch=2,
            grid=(n_tiles,),
            in_specs=[
                pl.BlockSpec((TM, D), lambda i, p1, p2: (i, 0)),
                pl.BlockSpec((1, 6, D), lambda i, p1, p2: (_mod_row(i), 0, 0)),
                pl.BlockSpec((TM, LANES), lambda i, p1, p2: (i, 0)),
                pl.BlockSpec((1, D), lambda i, p1, p2: (0, 0)),
                pl.BlockSpec(memory_space=pl.ANY),
            ],
            out_specs=pl.BlockSpec((TM, D), lambda i, p1, p2: (i, 0)),
            scratch_shapes=[pltpu.VMEM((2, 2, TM) + ROW_TILE, F32), pltpu.SemaphoreType.DMA((2,))],
        ),
        out_shape=jax.ShapeDtypeStruct((n_tiles * TM, D), F32),
        compiler_params=_cparams(("arbitrary",)),
        name="moe_gather",
    )(p1, p2, xa, mod, tmeta, final_g, ys)


def _issue_row_gathers(p1_ref, p2_ref, ys_ref, buf, sem, tile, s, unrolled):
    base = tile * TM

    def one(r):
        pltpu.make_async_copy(ys_ref.at[p1_ref[base + r]], buf.at[s, 0, r], sem.at[s]).start()
        pltpu.make_async_copy(ys_ref.at[p2_ref[base + r]], buf.at[s, 1, r], sem.at[s]).start()

    if unrolled:
        for r in range(TM):
            one(r)
    else:
        def body(r, carry):
            one(r)
            return carry

        lax.fori_loop(0, TM, body, 0, unroll=ISSUE_UNROLL)


def _inproj_kernel(p1_ref, p2_ref, x_ref, mod0_ref, tm_ref, ys_ref, mod_ref, g_ref, win_ref,
                   xn_ref, xm_ref, op_ref, buf, sem):
    i = pl.program_id(0)
    n_steps = pl.num_programs(0)
    slot = lax.rem(i, 2)

    @pl.when(i == 0)
    def _():
        _issue_row_gathers(p1_ref, p2_ref, ys_ref, buf, sem, 0, 0, False)

    _row_copy_wait(buf.at[slot, 0], sem.at[slot])
    _row_copy_wait(buf.at[slot, 1], sem.at[slot])
    y = (tm_ref[:, 0:1] * buf[slot, 0].reshape(TM, D)
         + tm_ref[:, 1:2] * buf[slot, 1].reshape(TM, D))
    xn = x_ref[...] + mod0_ref[0, 5:6, :] * y
    xn_ref[...] = xn
    _issue_row_gathers(p1_ref, p2_ref, ys_ref, buf, sem, jnp.minimum(i + 1, n_steps - 1), 1 - slot,
                       True)
    h = _norm_mod(xn, g_ref[...], mod_ref[0, 0:1, :], mod_ref[0, 1:2, :])
    hb = h.astype(BF16)
    xm_ref[...] = jnp.dot(hb, win_ref[:, :B_INNER], preferred_element_type=F32).astype(BF16)
    op_ref[...] = jnp.dot(hb, win_ref[:, B_INNER:], preferred_element_type=F32).astype(BF16)

    @pl.when(i == n_steps - 1)
    def _():
        _row_copy_wait(buf.at[1 - slot, 0], sem.at[1 - slot])
        _row_copy_wait(buf.at[1 - slot, 1], sem.at[1 - slot])


def _inproj(p1, p2, xa, mod0, tmeta, ys, mod, g, w_in):
    const = lambda i, p1, p2: (0, 0)
    row = lambda i, p1, p2: (i, 0)
    mod_map = lambda i, p1, p2: (_mod_row(i), 0, 0)
    return pl.pallas_call(
        _inproj_kernel,
        grid_spec=pltpu.PrefetchScalarGridSpec(
            num_scalar_prefetch=2,
            grid=(ALL_TILES,),
            in_specs=[
                pl.BlockSpec((TM, D), row),
                pl.BlockSpec((1, 6, D), mod_map),
                pl.BlockSpec((TM, LANES), row),
                pl.BlockSpec(memory_space=pl.ANY),
                pl.BlockSpec((1, 6, D), mod_map),
                pl.BlockSpec((1, D), const),
                pl.BlockSpec((D, 2 * B_INNER), const),
            ],
            out_specs=[pl.BlockSpec((TM, D), row),
                       pl.BlockSpec((TM, B_INNER), row),
                       pl.BlockSpec((TM, B_INNER), row)],
            scratch_shapes=[pltpu.VMEM((2, 2, TM) + ROW_TILE, F32), pltpu.SemaphoreType.DMA((2,))],
        ),
        out_shape=[jax.ShapeDtypeStruct((N_ALL, D), F32),
                   jax.ShapeDtypeStruct((N_ALL, B_INNER), BF16),
                   jax.ShapeDtypeStruct((N_ALL, B_INNER), BF16)],
        compiler_params=_cparams(("arbitrary",)),
        name="mlstm_inproj",
    )(p1, p2, xa, mod0, tmeta, ys, mod, g, w_in)


def _log_sigmoid(x):
    return jnp.minimum(x, 0.0) - jnp.log(1.0 + jnp.exp(-jnp.abs(x)))


def _qkv_kernel(xm_ref, prev_ref, next_ref, cw_ref, cb_ref, wq_ref, wkt_ref, wv_ref, wg_ref, wgt_ref,
                bg_ref, bgt_ref, xc_ref, q_ref, kt_ref, v_ref, gt_ref, gtt_ref, tot_ref):
    i = pl.program_id(0)
    lat = i < LAT_TILES
    first = jnp.where(lat, i % TILES_PER_SEQ == 0, True)
    last = jnp.where(lat, i % TILES_PER_SEQ == TILES_PER_SEQ - 1, True)
    xmb = xm_ref[...]
    xm = xmb.astype(F32)
    prev = jnp.where(first, 0.0, prev_ref[...].astype(F32))
    nxt = jnp.where(last, 0.0, next_ref[...].astype(F32))
    ext = jnp.concatenate([prev[HALO - 8:], xm, nxt[:8]], axis=0)
    acc = jnp.zeros((TM, B_INNER), F32) + cb_ref[...]
    for t in range(CONV_K):
        off = 8 + t - CONV_K // 2
        acc = acc + ext[off:off + TM] * cw_ref[t:t + 1, :]
    xc = acc * jax.nn.sigmoid(acc)
    xcb = xc.astype(BF16)
    xc_ref[...] = xcb
    for h in range(HEADS):
        hs = slice(h * DH, (h + 1) * DH)
        qh = jnp.dot(xcb[:, hs], wq_ref[h], preferred_element_type=F32)
        q_ref[:, h * DQK:(h + 1) * DQK] = (qh * (DQK ** -0.5)).astype(BF16)
        kt_ref[h * DQK:(h + 1) * DQK, :] = lax.dot_general(
            wkt_ref[h], xcb[:, hs], (((1,), (1,)), ((), ())), preferred_element_type=F32).astype(BF16)
        v_ref[:, hs] = jnp.dot(xmb[:, hs], wv_ref[h], preferred_element_type=F32).astype(BF16)
    gts = jnp.dot(xmb, wg_ref[...], preferred_element_type=F32) + bg_ref[...]
    lane = lax.broadcasted_iota(jnp.int32, (CHUNK, LANES), 1)
    lane_kind = (lane >> 3) & 3
    gtt = _dot_nt(wgt_ref[...], xmb) + bgt_ref[...]
    row = lax.broadcasted_iota(jnp.int32, (4 * HEADS, CHUNK), 0)
    row_kind = (row >> 3) & 3
    r = lax.broadcasted_iota(jnp.int32, (CHUNK, CHUNK), 0)
    c = lax.broadcasted_iota(jnp.int32, (CHUNK, CHUNK), 1)
    lower = (c <= r).astype(BF16)
    upper = (c >= r).astype(BF16)
    row_sums = jnp.concatenate([upper, lower, jnp.ones((CHUNK, LANES), BF16)], axis=1)
    for ch in range(TM // CHUNK):
        ts = slice(ch * CHUNK, (ch + 1) * CHUNK)
        g = gts[ts]
        g = jnp.where((lane_kind & 1) == 1, _log_sigmoid(g), g)
        terms = _bf16_terms(g, 3)
        pre = sum(_dot_nn(lower, t) for t in terms)
        suf = sum(_dot_nn(upper, t) for t in terms)
        gt_ref[ts, :] = jnp.where(lane_kind == 1, pre, jnp.where(lane_kind == 3, suf, g))
        gt = gtt[:, ts]
        gt = jnp.where((row_kind & 1) == 1, _log_sigmoid(gt), gt)
        sums = _dot_exact_rhs(gt, row_sums)
        gtt_ref[:, ts] = jnp.where(row_kind == 1, sums[:, :CHUNK],
                                   jnp.where(row_kind == 3, sums[:, CHUNK:2 * CHUNK], gt))
        tot_ref[:, ts] = sums[:, 2 * CHUNK:]


def _qkv(xm, conv_w, conv_b, wq, wkt, wv, wg, wgt, bg, bgt):
    const = lambda i: (0, 0)
    const3 = lambda i: (0, 0, 0)
    hb = TM // HALO
    n_hb = N_ALL // HALO
    return pl.pallas_call(
        _qkv_kernel,
        grid=(ALL_TILES,),
        in_specs=[
            pl.BlockSpec((TM, B_INNER), lambda i: (i, 0)),
            pl.BlockSpec((HALO, B_INNER), lambda i: (jnp.maximum(i * hb - 1, 0), 0)),
            pl.BlockSpec((HALO, B_INNER), lambda i: (jnp.minimum((i + 1) * hb, n_hb - 1), 0)),
            pl.BlockSpec((CONV_K, B_INNER), const),
            pl.BlockSpec((1, B_INNER), const),
            pl.BlockSpec((HEADS, DH, DQK), const3),
            pl.BlockSpec((HEADS, DQK, DH), const3),
            pl.BlockSpec((HEADS, DH, DV), const3),
            pl.BlockSpec((B_INNER, LANES), const),
            pl.BlockSpec((4 * HEADS, B_INNER), const),
            pl.BlockSpec((1, LANES), const),
            pl.BlockSpec((4 * HEADS, 1), const),
        ],
        out_specs=[
            pl.BlockSpec((TM, B_INNER), lambda i: (i, 0)),
            pl.BlockSpec((TM, HEADS * DQK), lambda i: (i, 0)),
            pl.BlockSpec((HEADS * DQK, TM), lambda i: (0, i)),
            pl.BlockSpec((TM, B_INNER), lambda i: (i, 0)),
            pl.BlockSpec((TM, LANES), lambda i: (i, 0)),
            pl.BlockSpec((4 * HEADS, TM), lambda i: (0, i)),
            pl.BlockSpec((4 * HEADS, TM), lambda i: (0, i)),
        ],
        out_shape=[
            jax.ShapeDtypeStruct((N_ALL, B_INNER), BF16),
            jax.ShapeDtypeStruct((N_ALL, HEADS * DQK), BF16),
            jax.ShapeDtypeStruct((HEADS * DQK, N_ALL), BF16),
            jax.ShapeDtypeStruct((N_ALL, B_INNER), BF16),
            jax.ShapeDtypeStruct((N_ALL, LANES), F32),
            jax.ShapeDtypeStruct((4 * HEADS, N_ALL), F32),
            jax.ShapeDtypeStruct((4 * HEADS, N_ALL), F32),
        ],
        compiler_params=_cparams(("parallel",)),
        name="mlstm_qkv",
    )(xm, xm, xm, conv_w, conv_b, wq, wkt, wv, wg, wgt, bg, bgt)


CTX_CHUNKS = CTX // CHUNK
LAT_CHUNKS = SEQ // CHUNK
SCAN_STEPS = CTX_CHUNKS + LAT_CHUNKS


def _scan_dir(nb, d, q_ref, kt_ref, v_ref, g_ref, gt_ref, tot_ref, o_ref, c_scr, m_scr):
    r = lax.broadcasted_iota(jnp.int32, (CHUNK, CHUNK), 0)
    c = lax.broadcasted_iota(jnp.int32, (CHUNK, CHUNK), 1)
    if d == 0:
        keep = c <= r
    else:
        keep = c >= r
    ones_blk = jnp.ones((CHUNK, LANES), BF16)
    bcol_all = g_ref[...]
    gtt = gt_ref[...]
    tot_all = tot_ref[...]
    base = 2 * HEADS * d
    for h in range(HEADS):
        li_r = gtt[base + h:base + h + 1, :]
        b_r = gtt[base + HEADS + h:base + HEADS + h + 1, :]
        b_last = tot_all[base + HEADS + h:base + HEADS + h + 1, :]
        b_full = jnp.broadcast_to(bcol_all[:, base + HEADS + h:base + HEADS + h + 1], (CHUNK, LANES))
        m_old = m_scr[nb, d, h]
        c_old = c_scr[nb, d, h]
        qh = q_ref[:, h * DQK:(h + 1) * DQK]
        kth = kt_ref[h * DQK:(h + 1) * DQK, :]
        vaug = jnp.concatenate([v_ref[:, h * DV:(h + 1) * DV], ones_blk], axis=1)
        g_r = b_last - b_r + li_r
        m_new = jnp.maximum(b_last + m_old, jnp.max(g_r, axis=-1, keepdims=True))
        decay = jnp.exp(b_last + m_old - m_new)
        wk = jnp.exp(g_r - m_new)
        kwt = (kth.astype(F32) * wk).astype(BF16)
        c_scr[nb, d, h] = jnp.concatenate([decay] * 3, axis=1) * c_old + jnp.dot(
            kwt, vaug, preferred_element_type=F32)
        m_scr[nb, d, h] = m_new

        dmat = jnp.where(keep, b_full - (b_r - li_r), NEG_INF)
        inter = b_full + m_old
        m_t = jnp.maximum(inter, jnp.max(dmat, axis=-1, keepdims=True))
        p = jnp.exp(dmat - m_t)
        s = jnp.dot(qh, kth, preferred_element_type=F32) * p
        a = jnp.exp(inter - m_t)
        num = jnp.concatenate([a] * 3, axis=1) * jnp.dot(
            qh, c_old.astype(BF16), preferred_element_type=F32) + jnp.dot(
                s.astype(BF16), vaug, preferred_element_type=F32)
        inv = 1.0 / jnp.maximum(jnp.abs(num[:, DV:]), jnp.exp(-m_t))
        o_ref[nb, :, h * DV:(h + 1) * DV] = (
            num[:, :DV] * jnp.concatenate([inv] * 2, axis=1)).astype(BF16)


SCAN_NB = 2
SCAN_IN = 6


def _scan_kernel(*refs):
    n_in = SCAN_NB * 2 * SCAN_IN
    ins, outs = refs[:n_in], refs[n_in:n_in + 2]
    c_scr, m_scr = refs[n_in + 2:]

    @pl.when(pl.program_id(1) == 0)
    def _():
        c_scr[...] = jnp.zeros_like(c_scr)
        m_scr[...] = jnp.zeros_like(m_scr)

    for nb in range(SCAN_NB):
        for d in range(2):
            k = nb * 2 + d
            _scan_dir(nb, d, *ins[k * SCAN_IN:(k + 1) * SCAN_IN], outs[d], c_scr, m_scr)


def _scan(q, kt, v, gts, gtt, tot):
    lat_blk = N_LAT // CHUNK

    def fwd_in(b, j):
        return jnp.where(j < CTX_CHUNKS, lat_blk + b * CTX_CHUNKS + j, b * LAT_CHUNKS + j - CTX_CHUNKS)

    def bwd_in(b, j):
        return jnp.where(j < CTX_CHUNKS, lat_blk + b * CTX_CHUNKS + (CTX_CHUNKS - 1 - j),
                         b * LAT_CHUNKS + (SCAN_STEPS - 1 - j))

    def fwd_out(b, j):
        return b * LAT_CHUNKS + jnp.maximum(j - CTX_CHUNKS, 0)

    def bwd_out(b, j):
        return b * LAT_CHUNKS + jnp.minimum(SCAN_STEPS - 1 - j, LAT_CHUNKS - 1)

    n_g = BATCH // SCAN_NB

    def specs(fn, nb):
        blk = lambda g, j: fn(nb * n_g + g, j)
        return [
            pl.BlockSpec((CHUNK, HEADS * DQK), lambda g, j: (blk(g, j), 0)),
            pl.BlockSpec((HEADS * DQK, CHUNK), lambda g, j: (0, blk(g, j))),
            pl.BlockSpec((CHUNK, B_INNER), lambda g, j: (blk(g, j), 0)),
            pl.BlockSpec((CHUNK, LANES), lambda g, j: (blk(g, j), 0)),
            pl.BlockSpec((4 * HEADS, CHUNK), lambda g, j: (0, blk(g, j))),
            pl.BlockSpec((4 * HEADS, CHUNK), lambda g, j: (0, blk(g, j))),
        ]

    def out_spec(fn):
        return pl.BlockSpec((SCAN_NB, CHUNK, B_INNER), lambda g, j: (0, fn(g, j), 0))

    in_specs, operands = [], []
    for nb in range(SCAN_NB):
        for fn_in in (fwd_in, bwd_in):
            in_specs += specs(fn_in, nb)
            operands += [q, kt, v, gts, gtt, tot]
    h_shape = jax.ShapeDtypeStruct((SCAN_NB, N_LAT // SCAN_NB, B_INNER), BF16)
    hf, hb = pl.pallas_call(
        _scan_kernel,
        grid=(n_g, SCAN_STEPS),
        in_specs=in_specs,
        out_specs=[out_spec(fwd_out), out_spec(bwd_out)],
        out_shape=[h_shape, h_shape],
        scratch_shapes=[pltpu.VMEM((SCAN_NB, 2, HEADS, DQK, DV + LANES), F32),
                        pltpu.VMEM((SCAN_NB, 2, HEADS, 1, LANES), F32)],
        compiler_params=_cparams(("parallel", "arbitrary")),
        name="mlstm_scan",
    )(*operands)
    return hf.reshape(N_LAT, B_INNER), hb.reshape(N_LAT, B_INNER)


def _combine_kernel(x_ref, mod_ref, hf_ref, hb_ref, xc_ref, op_ref, hg_ref, sk_ref, wout_ref,
                    g2_ref, wr_ref, br_ref, o_ref, meta_ref, tmeta_ref, cnt_ref, cntr_ref,
                    y_scr, carry, carry_r):
    for h in range(HEADS):
        hs = slice(h * DV, (h + 1) * DV)
        s = hf_ref[:, hs].astype(F32) + hb_ref[:, hs].astype(F32)
        s = s * lax.rsqrt(jnp.mean(s * s, axis=-1, keepdims=True) + EPS)
        y = jax.nn.sigmoid(op_ref[:, hs].astype(F32)) * (
            s * hg_ref[:, hs] + sk_ref[:, hs] * xc_ref[:, hs].astype(F32))
        y_scr[:, hs] = y.astype(BF16)
    out = jnp.dot(y_scr[...], wout_ref[...], preferred_element_type=F32)
    xn = x_ref[...] + mod_ref[0, 2:3, :] * out
    o_ref[...] = xn
    _route_tile(xn, mod_ref, g2_ref, wr_ref, br_ref, meta_ref, tmeta_ref, cnt_ref, cntr_ref,
                carry, carry_r)


def _combine(xa, mod, hf, hb, xc, op, head_g, skip, w_out, g2, wr, br):
    const = lambda i: (0, 0)
    row = lambda i: (i, 0)
    r_in, r_out, r_shape, r_scratch = _router_specs(LAT_TILES)
    return pl.pallas_call(
        _combine_kernel,
        grid=(LAT_TILES,),
        in_specs=[
            pl.BlockSpec((TM, D), row),
            pl.BlockSpec((1, 6, D), lambda i: (_mod_row(i), 0, 0)),
            pl.BlockSpec((TM, B_INNER), row),
            pl.BlockSpec((TM, B_INNER), row),
            pl.BlockSpec((TM, B_INNER), row),
            pl.BlockSpec((TM, B_INNER), row),
            pl.BlockSpec((1, B_INNER), const),
            pl.BlockSpec((1, B_INNER), const),
            pl.BlockSpec((B_INNER, D), const),
        ] + r_in,
        out_specs=[pl.BlockSpec((TM, D), row)] + r_out,
        out_shape=[jax.ShapeDtypeStruct((N_LAT, D), F32)] + r_shape,
        scratch_shapes=[pltpu.VMEM((TM, B_INNER), BF16)] + r_scratch,
        compiler_params=_cparams(("arbitrary",)),
        name="mlstm_combine",
    )(xa, mod, hf, hb, xc, op, head_g, skip, w_out, g2, wr, br)


def _router_weights(w_grp, b_grp, w_exp, b_exp):
    wr = jnp.zeros((LANES, D), F32).at[:N_EXP].set(w_exp.T).at[N_EXP:N_EXP + N_GRP].set(w_grp.T)
    br = jnp.zeros((LANES, 1), F32).at[:N_EXP, 0].set(b_exp).at[N_EXP:N_EXP + N_GRP, 0].set(b_grp)
    return wr.astype(BF16), br


PLAN_BLK = 2048
PLAN_PAIRS = 256


def _plan_kernel(meta_ref, cc_ref, cr_ref, pp_ref, plan_ref):
    row = lax.broadcasted_iota(jnp.int32, (LANES, LANES), 0)
    col = lax.broadcasted_iota(jnp.int32, (LANES, LANES), 1)
    cnt_c = cc_ref[...]
    cnt_r = cr_ref[0:1, :]
    starts_c = jnp.sum(jnp.where(col < row, cnt_r, 0.0), axis=1, keepdims=True)
    erow = lax.broadcasted_iota(jnp.int32, (LANES, PLAN_BLK), 0).astype(F32)
    for j in range(2):
        pos = jnp.sum(jnp.where(erow == meta_ref[j:j + 1, :], starts_c, 0.0), axis=0,
                      keepdims=True) + meta_ref[j + 2:j + 3, :]
        pp_ref[j:j + 1, :] = pos.astype(jnp.int32)
    pp_ref[2:8, :] = jnp.zeros((6, PLAN_BLK), jnp.int32)

    @pl.when(pl.program_id(0) == 0)
    def _():
        ends_c = starts_c + cnt_c
        starts_r = jnp.sum(jnp.where(row < col, cnt_c, 0.0), axis=0, keepdims=True)
        ends_r = starts_r + cnt_r

        def tiles(st, en, cn):
            first_tile = jnp.floor(st * (1.0 / TM))
            n = jnp.where(cn > 0.0, jnp.floor((en - 1.0) * (1.0 / TM)) - first_tile + 1.0, 0.0)
            return first_tile, n

        ft_c, pairs_c = tiles(starts_c, ends_c, cnt_c)
        _, pairs_r = tiles(starts_r, ends_r, cnt_r)
        pend_c = jnp.sum(jnp.where(col <= row, pairs_r, 0.0), axis=1, keepdims=True)
        pstart_c = pend_c - pairs_c
        total = jnp.sum(pairs_r, axis=1, keepdims=True)
        q = lax.broadcasted_iota(jnp.int32, (1, PLAN_PAIRS), 1).astype(F32)
        erow_p = lax.broadcasted_iota(jnp.int32, (LANES, PLAN_PAIRS), 0).astype(F32)

        def at(qv):
            qc = jnp.maximum(jnp.minimum(qv, total - 1.0), 0.0)
            e = jnp.sum(jnp.where(qc >= pend_c, 1.0, 0.0), axis=0, keepdims=True)
            oh = erow_p == e
            tile = jnp.sum(jnp.where(oh, ft_c - pstart_c, 0.0), axis=0, keepdims=True) + qc
            return e, oh, tile

        e_q, oh, tile_q = at(q)
        e_p, _, tile_p = at(q - 1.0)
        lo = jnp.sum(jnp.where(oh, starts_c, 0.0), axis=0, keepdims=True) - tile_q * TM
        hi = jnp.sum(jnp.where(oh, ends_c, 0.0), axis=0, keepdims=True) - tile_q * TM
        rows = (tile_q, e_q, jnp.clip(lo, 0.0, TM), jnp.clip(hi, 0.0, TM),
                jnp.where((q == 0.0) | (tile_q != tile_p), 1.0, 0.0),
                jnp.where((q == 0.0) | (e_q != e_p), 1.0, 0.0),
                jnp.where(q < total, 1.0, 0.0),
                jnp.zeros((1, PLAN_PAIRS), F32))
        for j, v in enumerate(rows):
            plan_ref[j:j + 1, :] = v.astype(jnp.int32)


def _route_plan(meta, cnt_c, cnt_r, n_rows):
    n_pairs = 2 * n_rows // TM + N_EXP - 1
    assert n_pairs <= PLAN_PAIRS and n_rows % PLAN_BLK == 0
    const = lambda i: (0, 0)
    pp, plan = pl.pallas_call(
        _plan_kernel,
        grid=(n_rows // PLAN_BLK,),
        in_specs=[pl.BlockSpec((8, PLAN_BLK), lambda i: (0, i)),
                  pl.BlockSpec((LANES, 1), const),
                  pl.BlockSpec((8, LANES), const)],
        out_specs=[pl.BlockSpec((8, PLAN_BLK), lambda i: (0, i)),
                   pl.BlockSpec((8, PLAN_PAIRS), const)],
        out_shape=[jax.ShapeDtypeStruct((8, n_rows), jnp.int32),
                   jax.ShapeDtypeStruct((8, PLAN_PAIRS), jnp.int32)],
        compiler_params=_cparams(("arbitrary",)),
        name="moe_plan",
    )(meta, cnt_c, cnt_r)
    return pp[0], pp[1], tuple(plan[j, :n_pairs] for j in range(7))


def _moe_experts(xa, routing, mod, g2, w1, w3, w2, layer, n_rows):
    meta, _, cnt_c, cnt_r = routing
    p1, p2, plan = _route_plan(meta, cnt_c, cnt_r, n_rows)
    xs = _scatter(p1, p2, xa, mod, g2, n_rows // TM)
    return p1, p2, _grouped(plan, xs, w1, w3, w2, layer)


def kernel(x, c, ctx, c_ctx, norm_g, w_ada, b_ada, a_w_in, a_b_in, a_g_v, a_w_s, a_b_s, a_w_out,
           b_w_in, b_conv_w, b_conv_b, b_w_q, b_w_k, b_w_v, b_w_gate, b_b_gate, b_head_g, b_skip,
           b_w_out, moe_w_grp, moe_b_grp, moe_w_exp, moe_b_exp, moe_w1, moe_w3, moe_w2, final_g):
    cc = jnp.zeros((MOD_ROWS, D), F32).at[:BATCH].set(c).at[BATCH].set(c_ctx)
    mods = _ada(cc, w_ada, b_ada).reshape(2, MOD_ROWS, 6, D)

    mod = mods[0]
    b_s_full = jnp.repeat(a_b_s[0].T, A_GC, axis=1)
    g2 = norm_g[0, 1].reshape(1, D)
    wr, br = _router_weights(moe_w_grp[0], moe_b_grp[0], moe_w_exp[0], moe_b_exp[0])
    xa, *routing = _gmlp(x.reshape(N_LAT, D), ctx.reshape(N_CTX, D), mod, norm_g[0, 0].reshape(1, D),
                         a_w_in[0].astype(BF16), a_b_in[0].reshape(1, -1), a_g_v[0].reshape(1, -1),
                         a_w_s[0].astype(BF16), b_s_full, a_w_out[0].astype(BF16), g2, wr, br)
    p1, p2, ys = _moe_experts(xa, routing, mod, g2, moe_w1, moe_w3, moe_w2, 0, N_ALL)

    mod = mods[1]
    xa, xm, op = _inproj(p1, p2, xa, mods[0], routing[1], ys, mod, norm_g[1, 0].reshape(1, D),
                         b_w_in[0].astype(BF16))
    wg = jnp.zeros((B_INNER, LANES), F32).at[:, :4 * HEADS].set(b_w_gate[0]).astype(BF16)
    bg = jnp.zeros((1, LANES), F32).at[0, :4 * HEADS].set(b_b_gate[0])
    xc, q, kt, v, gts, gtt, tot = _qkv(xm, b_conv_w[0], b_conv_b[0].reshape(1, -1),
                                  b_w_q[0].astype(BF16),
                                  jnp.transpose(b_w_k[0], (0, 2, 1)).astype(BF16),
                                  b_w_v[0].astype(BF16), wg, b_w_gate[0].T.astype(BF16), bg,
                                  b_b_gate[0].reshape(-1, 1))
    hf, hb = _scan(q, kt, v, gts, gtt, tot)
    g2 = norm_g[1, 1].reshape(1, D)
    wr, br = _router_weights(moe_w_grp[1], moe_b_grp[1], moe_w_exp[1], moe_b_exp[1])
    xl, *routing = _combine(xa, mod, hf, hb, xc, op, b_head_g[0].reshape(1, -1),
                            b_skip[0].reshape(1, -1), b_w_out[0].astype(BF16), g2, wr, br)
    p1, p2, ys = _moe_experts(xl, routing, mod, g2, moe_w1, moe_w3, moe_w2, 1, N_LAT)
    out = _gather(p1, p2, xl, mod, routing[1], final_g.reshape(1, D), ys, LAT_TILES, True)
    return out.reshape(BATCH, SEQ, D)
```

```python
import functools

import jax
import jax.numpy as jnp
from jax import lax
from jax.experimental import pallas as pl
from jax.experimental.pallas import tpu as pltpu

F32 = jnp.float32
BF16 = jnp.bfloat16

D = 1024
BATCH = 8
SEQ = 2048
CTX = 256
EPS = 1e-6
NEG_INF = -1e30
N_LAT = BATCH * SEQ
N_CTX = BATCH * CTX
N_ALL = N_LAT + N_CTX

TM = 256
LAT_TILES = N_LAT // TM
ALL_TILES = N_ALL // TM
TILES_PER_SEQ = SEQ // TM
MOD_ROWS = 16

CHUNK = 128
A_HALF = 2048
A_GROUPS = 8
A_GC = A_HALF // A_GROUPS
B_INNER = 2048
HEADS = 8
DH = B_INNER // HEADS
DQK = DH // 2
DV = DH
CONV_K = 5
HALO = 16
N_EXP = 32
N_GRP = 4
EXP_PER_GRP = 8
D_EXP = 512
LANES = 128

VMEM_LIMIT = 56 * 1024 * 1024


def _cparams(sem):
    return pltpu.CompilerParams(dimension_semantics=sem, vmem_limit_bytes=VMEM_LIMIT)


def _mod_row(i):
    return jnp.where(i < LAT_TILES, i // TILES_PER_SEQ, BATCH)


def _norm_mod(x, g, shift, scale):
    y = x * lax.rsqrt(jnp.mean(x * x, axis=-1, keepdims=True) + EPS) * g
    return y * (1.0 + scale) + shift


def _bf16_terms(x, n):
    terms = []
    r = x
    for _ in range(n):
        t = r.astype(BF16)
        terms.append(t)
        r = r - t.astype(F32)
    return terms


def _dot_nt(a, b):
    return lax.dot_general(a, b, (((1,), (1,)), ((), ())), preferred_element_type=F32)


def _dot_nn(a, b):
    return jnp.dot(a, b, preferred_element_type=F32)


def _dot3(a, b, dot=_dot_nn):
    a1, a2 = _bf16_terms(a, 2)
    b1, b2 = _bf16_terms(b, 2)
    return dot(a1, b1) + (dot(a1, b2) + dot(a2, b1))


def _dot_exact_lhs(a_bf16, b, dot=_dot_nn):
    return sum(dot(a_bf16, t) for t in _bf16_terms(b, 3))


def _dot_exact_rhs(a, b_bf16, dot=_dot_nn):
    return sum(dot(t, b_bf16) for t in _bf16_terms(a, 3))


def _gelu_tanh(x):
    half = 0.5 * x
    t = jnp.tanh(x * (0.7978845608028654 + 0.035677408136300125 * (x * x)))
    return half + half * t


ADA_BN = 1536


def _ada_kernel(c_ref, w_ref, b_ref, o_ref):
    c = c_ref[...]
    a = c * jax.nn.sigmoid(c)
    o_ref[0] = _dot3(a, w_ref[0]) + b_ref[0]


def _ada(cc, w_ada, b_ada):
    depth = w_ada.shape[0]
    return pl.pallas_call(
        _ada_kernel,
        grid=(depth, 6 * D // ADA_BN),
        in_specs=[
            pl.BlockSpec((MOD_ROWS, D), lambda l, j: (0, 0)),
            pl.BlockSpec((1, D, ADA_BN), lambda l, j: (l, 0, j)),
            pl.BlockSpec((1, 1, ADA_BN), lambda l, j: (l, 0, j)),
        ],
        out_specs=pl.BlockSpec((1, MOD_ROWS, ADA_BN), lambda l, j: (l, 0, j)),
        out_shape=jax.ShapeDtypeStruct((depth, MOD_ROWS, 6 * D), F32),
        compiler_params=_cparams(("parallel", "parallel")),
        name="ada",
    )(cc, w_ada, b_ada.reshape(depth, 1, 6 * D))


GM_CH = 512


def _gmlp_kernel(xl_ref, xc_ref, mod_ref, g_ref, win_ref, bin_ref, gv_ref, ws_ref, bs_ref, wout_ref,
                 g2_ref, wr_ref, br_ref, o_ref, meta_ref, tmeta_ref, cnt_ref, cntr_ref,
                 z_scr, y_scr, carry, carry_r):
    x = jnp.where(pl.program_id(0) < LAT_TILES, xl_ref[...], xc_ref[...])
    h = _norm_mod(x, g_ref[...], mod_ref[0, 0:1, :], mod_ref[0, 1:2, :])
    hb = h.astype(BF16)
    s1 = jnp.zeros((TM, 1), F32)
    s2 = jnp.zeros((TM, 1), F32)
    for j in range(2 * A_HALF // GM_CH):
        cs = slice(j * GM_CH, (j + 1) * GM_CH)
        zc = jnp.dot(hb, win_ref[:, cs], preferred_element_type=F32) + bin_ref[:, cs]
        zc = _gelu_tanh(zc)
        z_scr[:, cs] = zc
        if j * GM_CH >= A_HALF:
            s1 = s1 + jnp.sum(zc, axis=-1, keepdims=True)
            s2 = s2 + jnp.sum(zc * zc, axis=-1, keepdims=True)
    mu = s1 * (1.0 / A_HALF)
    rstd = lax.rsqrt(s2 * (1.0 / A_HALF) - mu * mu + EPS)
    for c in range(TM // CHUNK):
        rs = slice(c * CHUNK, (c + 1) * CHUNK)
        for g in range(A_GROUPS):
            cs = slice(g * A_GC, (g + 1) * A_GC)
            vs = slice(A_HALF + g * A_GC, A_HALF + (g + 1) * A_GC)
            v = (z_scr[rs, vs] - mu[rs]) * rstd[rs] * gv_ref[:, cs]
            s = jnp.dot(ws_ref[g], v.astype(BF16), preferred_element_type=F32) + bs_ref[:, cs]
            y_scr[rs, cs] = (z_scr[rs, cs] * s).astype(BF16)
    out = jnp.dot(y_scr[...], wout_ref[...], preferred_element_type=F32)
    xn = x + mod_ref[0, 2:3, :] * out
    o_ref[...] = xn
    _route_tile(xn, mod_ref, g2_ref, wr_ref, br_ref, meta_ref, tmeta_ref, cnt_ref, cntr_ref,
                carry, carry_r)


def _gmlp(xl, xc, mod, g, w_in, b_in, g_v, w_s, b_s_full, w_out, g2, wr, br):
    const = lambda i: (0, 0)
    r_in, r_out, r_shape, r_scratch = _router_specs(ALL_TILES)
    return pl.pallas_call(
        _gmlp_kernel,
        grid=(ALL_TILES,),
        in_specs=[
            pl.BlockSpec((TM, D), lambda i: (jnp.minimum(i, LAT_TILES - 1), 0)),
            pl.BlockSpec((TM, D), lambda i: (jnp.maximum(i - LAT_TILES, 0), 0)),
            pl.BlockSpec((1, 6, D), lambda i: (_mod_row(i), 0, 0)),
            pl.BlockSpec((1, D), const),
            pl.BlockSpec((D, 2 * A_HALF), const),
            pl.BlockSpec((1, 2 * A_HALF), const),
            pl.BlockSpec((1, A_HALF), const),
            pl.BlockSpec((A_GROUPS, CHUNK, CHUNK), lambda i: (0, 0, 0)),
            pl.BlockSpec((CHUNK, A_HALF), const),
            pl.BlockSpec((A_HALF, D), const),
        ] + r_in,
        out_specs=[pl.BlockSpec((TM, D), lambda i: (i, 0))] + r_out,
        out_shape=[jax.ShapeDtypeStruct((N_ALL, D), F32)] + r_shape,
        scratch_shapes=[pltpu.VMEM((TM, 2 * A_HALF), F32), pltpu.VMEM((TM, A_HALF), BF16)] + r_scratch,
        compiler_params=_cparams(("arbitrary",)),
        name="gmlp",
    )(xl, xc, mod, g, w_in, b_in, g_v, w_s, b_s_full, w_out, g2, wr, br)


def _route_tile(x, mod_ref, g_ref, wr_ref, br_ref, meta_ref, tmeta_ref, cnt_ref, cntr_ref,
                carry, carry_r):
    @pl.when(pl.program_id(0) == 0)
    def _():
        carry[...] = jnp.zeros_like(carry)
        carry_r[...] = jnp.zeros_like(carry_r)

    h = _norm_mod(x, g_ref[...], mod_ref[0, 3:4, :], mod_ref[0, 4:5, :])
    lt = _dot_nt(wr_ref[...], h.astype(BF16)) + br_ref[...]
    e_t = lt[0:N_EXP]
    row8 = lax.broadcasted_iota(jnp.int32, (EXP_PER_GRP, TM), 0).astype(F32)
    g_t = jnp.where(row8 < N_GRP, lt[N_EXP:N_EXP + EXP_PER_GRP], -jnp.inf)
    gmax = jnp.max(g_t, axis=0, keepdims=True)
    p_g = 1.0 / jnp.sum(jnp.exp(g_t - gmax), axis=0, keepdims=True)
    g_idx = jnp.min(jnp.where(g_t == gmax, row8, float(EXP_PER_GRP)), axis=0, keepdims=True)
    sel = jnp.zeros((EXP_PER_GRP, TM), F32)
    for g in range(N_GRP):
        sel = sel + jnp.where(g_idx == g, e_t[g * EXP_PER_GRP:(g + 1) * EXP_PER_GRP], 0.0)
    m1 = jnp.max(sel, axis=0, keepdims=True)
    i1 = jnp.min(jnp.where(sel == m1, row8, float(EXP_PER_GRP)), axis=0, keepdims=True)
    sel2 = jnp.where(row8 == i1, -jnp.inf, sel)
    m2 = jnp.max(sel2, axis=0, keepdims=True)
    i2 = jnp.min(jnp.where(sel2 == m2, row8, float(EXP_PER_GRP)), axis=0, keepdims=True)
    e2 = jnp.exp(m2 - m1)
    w1 = p_g / (1.0 + e2)
    w2 = p_g * e2 / (1.0 + e2)
    row = lax.broadcasted_iota(jnp.int32, (LANES, TM), 0).astype(F32)
    id1 = g_idx * EXP_PER_GRP + i1
    id2 = g_idx * EXP_PER_GRP + i2
    oh1 = row == id1
    oh2 = row == id2
    oh = jnp.where(oh1, 1.0, 0.0) + jnp.where(oh2, 1.0, 0.0)
    before = (lax.broadcasted_iota(jnp.int32, (TM, TM), 0)
              < lax.broadcasted_iota(jnp.int32, (TM, TM), 1)).astype(BF16)
    ohb = oh.astype(BF16)
    tot = jnp.dot(ohb, before, preferred_element_type=F32) + carry[...]
    r1 = jnp.sum(jnp.where(oh1, tot, 0.0), axis=0, keepdims=True)
    r2 = jnp.sum(jnp.where(oh2, tot, 0.0), axis=0, keepdims=True)
    carry[...] += jnp.sum(oh, axis=-1, keepdims=True)
    carry_r[...] += _dot_nt(jnp.ones((8, TM), BF16), ohb)
    cnt_ref[...] = carry[...]
    cntr_ref[...] = carry_r[...]
    meta_ref[0:1, :] = id1
    meta_ref[1:2, :] = id2
    meta_ref[2:3, :] = r1
    meta_ref[3:4, :] = r2
    meta_ref[4:8, :] = jnp.zeros((4, TM), F32)
    tmeta_ref[...] = (jnp.where(row == 0.0, w1, 0.0) + jnp.where(row == 1.0, w2, 0.0)).T


def _router_specs(n_tiles):
    const = lambda i: (0, 0)
    n = n_tiles * TM
    in_specs = [pl.BlockSpec((1, D), const), pl.BlockSpec((LANES, D), const),
                pl.BlockSpec((LANES, 1), const)]
    out_specs = [pl.BlockSpec((8, TM), lambda i: (0, i)),
                 pl.BlockSpec((TM, LANES), lambda i: (i, 0)),
                 pl.BlockSpec((LANES, 1), const),
                 pl.BlockSpec((8, LANES), const)]
    out_shape = [jax.ShapeDtypeStruct((8, n), F32), jax.ShapeDtypeStruct((n, LANES), F32),
                 jax.ShapeDtypeStruct((LANES, 1), F32), jax.ShapeDtypeStruct((8, LANES), F32)]
    scratch = [pltpu.VMEM((LANES, 1), F32), pltpu.VMEM((8, LANES), F32)]
    return in_specs, out_specs, out_shape, scratch


ISSUE_UNROLL = 8
ROW_TILE = (8, LANES)


def _row_copy_wait(buf_slot, sem_slot):
    pltpu.make_async_copy(buf_slot, buf_slot, sem_slot).wait()


def _scatter_kernel(p1_ref, p2_ref, x_ref, mod_ref, g_ref, xs_ref, buf, sem):
    i = pl.program_id(0)
    n_steps = pl.num_programs(0)
    slot = lax.rem(i, 2)

    def wait_slot(s):
        _row_copy_wait(buf.at[s], sem.at[s])
        _row_copy_wait(buf.at[s], sem.at[s])

    @pl.when(i >= 2)
    def _():
        wait_slot(slot)

    h = _norm_mod(x_ref[...], g_ref[...], mod_ref[0, 3:4, :], mod_ref[0, 4:5, :])
    buf[slot] = h.reshape(TM, *ROW_TILE)
    base = i * TM

    def body(r, carry):
        src = buf.at[slot, r]
        pltpu.make_async_copy(src, xs_ref.at[p1_ref[base + r]], sem.at[slot]).start()
        pltpu.make_async_copy(src, xs_ref.at[p2_ref[base + r]], sem.at[slot]).start(priority=1)
        return carry

    lax.fori_loop(0, TM, body, 0, unroll=ISSUE_UNROLL)

    @pl.when(i == n_steps - 1)
    def _():
        @pl.when(i >= 1)
        def _():
            wait_slot(1 - slot)

        wait_slot(slot)


def _scatter(p1, p2, xa, mod, g, n_tiles):
    n = n_tiles * TM
    return pl.pallas_call(
        _scatter_kernel,
        grid_spec=pltpu.PrefetchScalarGridSpec(
            num_scalar_prefetch=2,
            grid=(n_tiles,),
            in_specs=[
                pl.BlockSpec((TM, D), lambda i, p1, p2: (i, 0)),
                pl.BlockSpec((1, 6, D), lambda i, p1, p2: (_mod_row(i), 0, 0)),
                pl.BlockSpec((1, D), lambda i, p1, p2: (0, 0)),
            ],
            out_specs=pl.BlockSpec(memory_space=pl.ANY),
            scratch_shapes=[pltpu.VMEM((2, TM) + ROW_TILE, F32), pltpu.SemaphoreType.DMA((2,))],
        ),
        out_shape=jax.ShapeDtypeStruct((2 * n,) + ROW_TILE, F32),
        compiler_params=_cparams(("arbitrary",)),
        name="moe_scatter",
    )(p1, p2, xa, mod, g)


def _grouped_kernel(tile_ref, exp_ref, lo_ref, hi_ref, first_ref, newexp_ref, valid_ref,
                    xs_ref, w1_ref, w3_ref, w2_ref, ys_ref, wb1, wb3, wb2):
    q = pl.program_id(0)

    @pl.when(valid_ref[q] == 1)
    def _():
        @pl.when(newexp_ref[q] == 1)
        def _():
            wb1[...] = w1_ref[0, 0].astype(BF16)
            wb3[...] = w3_ref[0, 0].astype(BF16)
            wb2[...] = w2_ref[0, 0].astype(BF16)

        x = xs_ref[...].reshape(TM, D).astype(BF16)
        a = jnp.dot(x, wb1[...], preferred_element_type=F32)
        b = jnp.dot(x, wb3[...], preferred_element_type=F32)
        rows = lax.broadcasted_iota(jnp.int32, (TM, 1), 0)
        mine = (rows >= lo_ref[q]) & (rows < hi_ref[q])
        act = jnp.where(mine, a * jax.nn.sigmoid(a) * b, 0.0)
        y = jnp.dot(act.astype(BF16), wb2[...], preferred_element_type=F32).reshape(TM, *ROW_TILE)

        @pl.when(first_ref[q] == 1)
        def _():
            ys_ref[...] = y

        @pl.when(first_ref[q] == 0)
        def _():
            ys_ref[...] += y


def _grouped(plan, xs, w1, w3, w2, layer):
    n_pairs = plan[0].shape[0]
    tile_map = lambda q, tile, exp, *_: (tile[q], 0, 0)
    exp_map = lambda q, tile, exp, *_: (layer, exp[q], 0, 0)
    return pl.pallas_call(
        _grouped_kernel,
        grid_spec=pltpu.PrefetchScalarGridSpec(
            num_scalar_prefetch=7,
            grid=(n_pairs,),
            in_specs=[
                pl.BlockSpec((TM,) + ROW_TILE, tile_map),
                pl.BlockSpec((1, 1, D, D_EXP), exp_map),
                pl.BlockSpec((1, 1, D, D_EXP), exp_map),
                pl.BlockSpec((1, 1, D_EXP, D), exp_map),
            ],
            out_specs=pl.BlockSpec((TM,) + ROW_TILE, tile_map),
            scratch_shapes=[pltpu.VMEM((D, D_EXP), BF16), pltpu.VMEM((D, D_EXP), BF16),
                            pltpu.VMEM((D_EXP, D), BF16)],
        ),
        out_shape=jax.ShapeDtypeStruct(xs.shape, F32),
        compiler_params=_cparams(("arbitrary",)),
        name="moe_grouped",
    )(*plan, xs, w1, w3, w2)


def _gather_kernel(p1_ref, p2_ref, x_ref, mod_ref, tm_ref, fg_ref, ys_ref, o_ref, buf, sem,
                   *, final_norm):
    i = pl.program_id(0)
    n_steps = pl.num_programs(0)
    slot = lax.rem(i, 2)

    def issue(tile, s):
        base = tile * TM

        def body(r, carry):
            pltpu.make_async_copy(ys_ref.at[p1_ref[base + r]], buf.at[s, 0, r], sem.at[s]).start()
            pltpu.make_async_copy(ys_ref.at[p2_ref[base + r]], buf.at[s, 1, r],
                                  sem.at[s]).start(priority=1)
            return carry

        lax.fori_loop(0, TM, body, 0, unroll=ISSUE_UNROLL)

    @pl.when(i == 0)
    def _():
        issue(0, 0)

    @pl.when(i + 1 < n_steps)
    def _():
        issue(i + 1, 1 - slot)

    _row_copy_wait(buf.at[slot, 0], sem.at[slot])
    _row_copy_wait(buf.at[slot, 1], sem.at[slot])
    y = (tm_ref[:, 0:1] * buf[slot, 0].reshape(TM, D)
         + tm_ref[:, 1:2] * buf[slot, 1].reshape(TM, D))
    out = x_ref[...] + mod_ref[0, 5:6, :] * y
    if final_norm:
        out = out * lax.rsqrt(jnp.mean(out * out, axis=-1, keepdims=True) + EPS) * fg_ref[...]
    o_ref[...] = out


def _gather(p1, p2, xa, mod, tmeta, final_g, ys, n_tiles, final_norm):
    return pl.pallas_call(
        functools.partial(_gather_kernel, final_norm=final_norm),
        grid_spec=pltpu.PrefetchScalarGridSpec(
            num_scalar_prefetch=2,
            grid=(n_tiles,),
            in_specs=[
                pl.BlockSpec((TM, D), lambda i, p1, p2: (i, 0)),
                pl.BlockSpec((1, 6, D), lambda i, p1, p2: (_mod_row(i), 0, 0)),
                pl.BlockSpec((TM, LANES), lambda i, p1, p2: (i, 0)),
                pl.BlockSpec((1, D), lambda i, p1, p2: (0, 0)),
                pl.BlockSpec(memory_space=pl.ANY),
            ],
            out_specs=pl.BlockSpec((TM, D), lambda i, p1, p2: (i, 0)),
            scratch_shapes=[pltpu.VMEM((2, 2, TM) + ROW_TILE, F32), pltpu.SemaphoreType.DMA((2,))],
        ),
        out_shape=jax.ShapeDtypeStruct((n_tiles * TM, D), F32),
        compiler_params=_cparams(("arbitrary",)),
        name="moe_gather",
    )(p1, p2, xa, mod, tmeta, final_g, ys)


def _issue_row_gathers(p1_ref, p2_ref, ys_ref, buf, sem, tile, s, unrolled):
    base = tile * TM

    def one(r):
        pltpu.make_async_copy(ys_ref.at[p1_ref[base + r]], buf.at[s, 0, r], sem.at[s]).start()
        pltpu.make_async_copy(ys_ref.at[p2_ref[base + r]], buf.at[s, 1, r], sem.at[s]).start()

    if unrolled:
        for r in range(TM):
            one(r)
    else:
        def body(r, carry):
            one(r)
            return carry

        lax.fori_loop(0, TM, body, 0, unroll=ISSUE_UNROLL)


def _inproj_kernel(p1_ref, p2_ref, x_ref, mod0_ref, tm_ref, ys_ref, mod_ref, g_ref, win_ref,
                   xn_ref, xm_ref, op_ref, buf, sem):
    i = pl.program_id(0)
    n_steps = pl.num_programs(0)
    slot = lax.rem(i, 2)

    @pl.when(i == 0)
    def _():
        _issue_row_gathers(p1_ref, p2_ref, ys_ref, buf, sem, 0, 0, False)

    _row_copy_wait(buf.at[slot, 0], sem.at[slot])
    _row_copy_wait(buf.at[slot, 1], sem.at[slot])
    y = (tm_ref[:, 0:1] * buf[slot, 0].reshape(TM, D)
         + tm_ref[:, 1:2] * buf[slot, 1].reshape(TM, D))
    xn = x_ref[...] + mod0_ref[0, 5:6, :] * y
    xn_ref[...] = xn
    _issue_row_gathers(p1_ref, p2_ref, ys_ref, buf, sem, jnp.minimum(i + 1, n_steps - 1), 1 - slot,
                       True)
    h = _norm_mod(xn, g_ref[...], mod_ref[0, 0:1, :], mod_ref[0, 1:2, :])
    hb = h.astype(BF16)
    xm_ref[...] = jnp.dot(hb, win_ref[:, :B_INNER], preferred_element_type=F32).astype(BF16)
    op_ref[...] = jnp.dot(hb, win_ref[:, B_INNER:], preferred_element_type=F32).astype(BF16)

    @pl.when(i == n_steps - 1)
    def _():
        _row_copy_wait(buf.at[1 - slot, 0], sem.at[1 - slot])
        _row_copy_wait(buf.at[1 - slot, 1], sem.at[1 - slot])


def _inproj(p1, p2, xa, mod0, tmeta, ys, mod, g, w_in):
    const = lambda i, p1, p2: (0, 0)
    row = lambda i, p1, p2: (i, 0)
    mod_map = lambda i, p1, p2: (_mod_row(i), 0, 0)
    return pl.pallas_call(
        _inproj_kernel,
        grid_spec=pltpu.PrefetchScalarGridSpec(
            num_scalar_prefetch=2,
            grid=(ALL_TILES,),
            in_specs=[
                pl.BlockSpec((TM, D), row),
                pl.BlockSpec((1, 6, D), mod_map),
                pl.BlockSpec((TM, LANES), row),
                pl.BlockSpec(memory_space=pl.ANY),
                pl.BlockSpec((1, 6, D), mod_map),
                pl.BlockSpec((1, D), const),
                pl.BlockSpec((D, 2 * B_INNER), const),
            ],
            out_specs=[pl.BlockSpec((TM, D), row),
                       pl.BlockSpec((TM, B_INNER), row),
                       pl.BlockSpec((TM, B_INNER), row)],
            scratch_shapes=[pltpu.VMEM((2, 2, TM) + ROW_TILE, F32), pltpu.SemaphoreType.DMA((2,))],
        ),
        out_shape=[jax.ShapeDtypeStruct((N_ALL, D), F32),
                   jax.ShapeDtypeStruct((N_ALL, B_INNER), BF16),
                   jax.ShapeDtypeStruct((N_ALL, B_INNER), BF16)],
        compiler_params=_cparams(("arbitrary",)),
        name="mlstm_inproj",
    )(p1, p2, xa, mod0, tmeta, ys, mod, g, w_in)


def _log_sigmoid(x):
    return jnp.minimum(x, 0.0) - jnp.log(1.0 + jnp.exp(-jnp.abs(x)))


def _qkv_kernel(xm_ref, prev_ref, next_ref, cw_ref, cb_ref, wq_ref, wkt_ref, wv_ref, wg_ref, wgt_ref,
                bg_ref, bgt_ref, xc_ref, q_ref, kt_ref, v_ref, gt_ref, gtt_ref, tot_ref):
    i = pl.program_id(0)
    lat = i < LAT_TILES
    first = jnp.where(lat, i % TILES_PER_SEQ == 0, True)
    last = jnp.where(lat, i % TILES_PER_SEQ == TILES_PER_SEQ - 1, True)
    xmb = xm_ref[...]
    xm = xmb.astype(F32)
    prev = jnp.where(first, 0.0, prev_ref[...].astype(F32))
    nxt = jnp.where(last, 0.0, next_ref[...].astype(F32))
    ext = jnp.concatenate([prev[HALO - 8:], xm, nxt[:8]], axis=0)
    acc = jnp.zeros((TM, B_INNER), F32) + cb_ref[...]
    for t in range(CONV_K):
        off = 8 + t - CONV_K // 2
        acc = acc + ext[off:off + TM] * cw_ref[t:t + 1, :]
    xc = acc * jax.nn.sigmoid(acc)
    xcb = xc.astype(BF16)
    xc_ref[...] = xcb
    for h in range(HEADS):
        hs = slice(h * DH, (h + 1) * DH)
        qh = jnp.dot(xcb[:, hs], wq_ref[h], preferred_element_type=F32)
        q_ref[:, h * DQK:(h + 1) * DQK] = (qh * (DQK ** -0.5)).astype(BF16)
        kt_ref[h * DQK:(h + 1) * DQK, :] = lax.dot_general(
            wkt_ref[h], xcb[:, hs], (((1,), (1,)), ((), ())), preferred_element_type=F32).astype(BF16)
        v_ref[:, hs] = jnp.dot(xmb[:, hs], wv_ref[h], preferred_element_type=F32).astype(BF16)
    gts = jnp.dot(xmb, wg_ref[...], preferred_element_type=F32) + bg_ref[...]
    lane = lax.broadcasted_iota(jnp.int32, (CHUNK, LANES), 1)
    lane_kind = (lane >> 3) & 3
    gtt = _dot_nt(wgt_ref[...], xmb) + bgt_ref[...]
    row = lax.broadcasted_iota(jnp.int32, (4 * HEADS, CHUNK), 0)
    row_kind = (row >> 3) & 3
    r = lax.broadcasted_iota(jnp.int32, (CHUNK, CHUNK), 0)
    c = lax.broadcasted_iota(jnp.int32, (CHUNK, CHUNK), 1)
    lower = (c <= r).astype(BF16)
    upper = (c >= r).astype(BF16)
    row_sums = jnp.concatenate([upper, lower, jnp.ones((CHUNK, LANES), BF16)], axis=1)
    for ch in range(TM // CHUNK):
        ts = slice(ch * CHUNK, (ch + 1) * CHUNK)
        g = gts[ts]
        g = jnp.where((lane_kind & 1) == 1, _log_sigmoid(g), g)
        terms = _bf16_terms(g, 3)
        pre = sum(_dot_nn(lower, t) for t in terms)
        suf = sum(_dot_nn(upper, t) for t in terms)
        gt_ref[ts, :] = jnp.where(lane_kind == 1, pre, jnp.where(lane_kind == 3, suf, g))
        gt = gtt[:, ts]
        gt = jnp.where((row_kind & 1) == 1, _log_sigmoid(gt), gt)
        sums = _dot_exact_rhs(gt, row_sums)
        gtt_ref[:, ts] = jnp.where(row_kind == 1, sums[:, :CHUNK],
                                   jnp.where(row_kind == 3, sums[:, CHUNK:2 * CHUNK], gt))
        tot_ref[:, ts] = sums[:, 2 * CHUNK:]


def _qkv(xm, conv_w, conv_b, wq, wkt, wv, wg, wgt, bg, bgt):
    const = lambda i: (0, 0)
    const3 = lambda i: (0, 0, 0)
    hb = TM // HALO
    n_hb = N_ALL // HALO
    return pl.pallas_call(
        _qkv_kernel,
        grid=(ALL_TILES,),
        in_specs=[
            pl.BlockSpec((TM, B_INNER), lambda i: (i, 0)),
            pl.BlockSpec((HALO, B_INNER), lambda i: (jnp.maximum(i * hb - 1, 0), 0)),
            pl.BlockSpec((HALO, B_INNER), lambda i: (jnp.minimum((i + 1) * hb, n_hb - 1), 0)),
            pl.BlockSpec((CONV_K, B_INNER), const),
            pl.BlockSpec((1, B_INNER), const),
            pl.BlockSpec((HEADS, DH, DQK), const3),
            pl.BlockSpec((HEADS, DQK, DH), const3),
            pl.BlockSpec((HEADS, DH, DV), const3),
            pl.BlockSpec((B_INNER, LANES), const),
            pl.BlockSpec((4 * HEADS, B_INNER), const),
            pl.BlockSpec((1, LANES), const),
            pl.BlockSpec((4 * HEADS, 1), const),
        ],
        out_specs=[
            pl.BlockSpec((TM, B_INNER), lambda i: (i, 0)),
            pl.BlockSpec((TM, HEADS * DQK), lambda i: (i, 0)),
            pl.BlockSpec((HEADS * DQK, TM), lambda i: (0, i)),
            pl.BlockSpec((TM, B_INNER), lambda i: (i, 0)),
            pl.BlockSpec((TM, LANES), lambda i: (i, 0)),
            pl.BlockSpec((4 * HEADS, TM), lambda i: (0, i)),
            pl.BlockSpec((4 * HEADS, TM), lambda i: (0, i)),
        ],
        out_shape=[
            jax.ShapeDtypeStruct((N_ALL, B_INNER), BF16),
            jax.ShapeDtypeStruct((N_ALL, HEADS * DQK), BF16),
            jax.ShapeDtypeStruct((HEADS * DQK, N_ALL), BF16),
            jax.ShapeDtypeStruct((N_ALL, B_INNER), BF16),
            jax.ShapeDtypeStruct((N_ALL, LANES), F32),
            jax.ShapeDtypeStruct((4 * HEADS, N_ALL), F32),
            jax.ShapeDtypeStruct((4 * HEADS, N_ALL), F32),
        ],
        compiler_params=_cparams(("parallel",)),
        name="mlstm_qkv",
    )(xm, xm, xm, conv_w, conv_b, wq, wkt, wv, wg, wgt, bg, bgt)


CTX_CHUNKS = CTX // CHUNK
LAT_CHUNKS = SEQ // CHUNK
SCAN_STEPS = CTX_CHUNKS + LAT_CHUNKS


def _scan_dir(nb, d, q_ref, kt_ref, v_ref, g_ref, gt_ref, tot_ref, o_ref, c_scr, m_scr):
    r = lax.broadcasted_iota(jnp.int32, (CHUNK, CHUNK), 0)
    c = lax.broadcasted_iota(jnp.int32, (CHUNK, CHUNK), 1)
    if d == 0:
        keep = c <= r
    else:
        keep = c >= r
    ones_blk = jnp.ones((CHUNK, LANES), BF16)
    bcol_all = g_ref[...]
    gtt = gt_ref[...]
    tot_all = tot_ref[...]
    base = 2 * HEADS * d
    for h in range(HEADS):
        li_r = gtt[base + h:base + h + 1, :]
        b_r = gtt[base + HEADS + h:base + HEADS + h + 1, :]
        b_last = tot_all[base + HEADS + h:base + HEADS + h + 1, :]
        b_full = jnp.broadcast_to(bcol_all[:, base + HEADS + h:base + HEADS + h + 1], (CHUNK, LANES))
        m_old = m_scr[nb, d, h]
        c_old = c_scr[nb, d, h]
        qh = q_ref[:, h * DQK:(h + 1) * DQK]
        kth = kt_ref[h * DQK:(h + 1) * DQK, :]
        vaug = jnp.concatenate([v_ref[:, h * DV:(h + 1) * DV], ones_blk], axis=1)
        g_r = b_last - b_r + li_r
        m_new = jnp.maximum(b_last + m_old, jnp.max(g_r, axis=-1, keepdims=True))
        decay = jnp.exp(b_last + m_old - m_new)
        wk = jnp.exp(g_r - m_new)
        kwt = (kth.astype(F32) * wk).astype(BF16)
        c_scr[nb, d, h] = jnp.concatenate([decay] * 3, axis=1) * c_old + jnp.dot(
            kwt, vaug, preferred_element_type=F32)
        m_scr[nb, d, h] = m_new

        dmat = jnp.where(keep, b_full - (b_r - li_r), NEG_INF)
        inter = b_full + m_old
        m_t = jnp.maximum(inter, jnp.max(dmat, axis=-1, keepdims=True))
        p = jnp.exp(dmat - m_t)
        s = jnp.dot(qh, kth, preferred_element_type=F32) * p
        a = jnp.exp(inter - m_t)
        num = jnp.concatenate([a] * 3, axis=1) * jnp.dot(
            qh, c_old.astype(BF16), preferred_element_type=F32) + jnp.dot(
                s.astype(BF16), vaug, preferred_element_type=F32)
        inv = 1.0 / jnp.maximum(jnp.abs(num[:, DV:]), jnp.exp(-m_t))
        o_ref[nb, :, h * DV:(h + 1) * DV] = (
            num[:, :DV] * jnp.concatenate([inv] * 2, axis=1)).astype(BF16)


SCAN_NB = 2
SCAN_IN = 6


def _scan_kernel(*refs):
    n_in = SCAN_NB * 2 * SCAN_IN
    ins, outs = refs[:n_in], refs[n_in:n_in + 2]
    c_scr, m_scr = refs[n_in + 2:]

    @pl.when(pl.program_id(1) == 0)
    def _():
        c_scr[...] = jnp.zeros_like(c_scr)
        m_scr[...] = jnp.zeros_like(m_scr)

    for nb in range(SCAN_NB):
        for d in range(2):
            k = nb * 2 + d
            _scan_dir(nb, d, *ins[k * SCAN_IN:(k + 1) * SCAN_IN], outs[d], c_scr, m_scr)


def _scan(q, kt, v, gts, gtt, tot):
    lat_blk = N_LAT // CHUNK

    def fwd_in(b, j):
        return jnp.where(j < CTX_CHUNKS, lat_blk + b * CTX_CHUNKS + j, b * LAT_CHUNKS + j - CTX_CHUNKS)

    def bwd_in(b, j):
        return jnp.where(j < CTX_CHUNKS, lat_blk + b * CTX_CHUNKS + (CTX_CHUNKS - 1 - j),
                         b * LAT_CHUNKS + (SCAN_STEPS - 1 - j))

    def fwd_out(b, j):
        return b * LAT_CHUNKS + jnp.maximum(j - CTX_CHUNKS, 0)

    def bwd_out(b, j):
        return b * LAT_CHUNKS + jnp.minimum(SCAN_STEPS - 1 - j, LAT_CHUNKS - 1)

    n_g = BATCH // SCAN_NB

    def specs(fn, nb):
        blk = lambda g, j: fn(nb * n_g + g, j)
        return [
            pl.BlockSpec((CHUNK, HEADS * DQK), lambda g, j: (blk(g, j), 0)),
            pl.BlockSpec((HEADS * DQK, CHUNK), lambda g, j: (0, blk(g, j))),
            pl.BlockSpec((CHUNK, B_INNER), lambda g, j: (blk(g, j), 0)),
            pl.BlockSpec((CHUNK, LANES), lambda g, j: (blk(g, j), 0)),
            pl.BlockSpec((4 * HEADS, CHUNK), lambda g, j: (0, blk(g, j))),
            pl.BlockSpec((4 * HEADS, CHUNK), lambda g, j: (0, blk(g, j))),
        ]

    def out_spec(fn):
        return pl.BlockSpec((SCAN_NB, CHUNK, B_INNER), lambda g, j: (0, fn(g, j), 0))

    in_specs, operands = [], []
    for nb in range(SCAN_NB):
        for fn_in in (fwd_in, bwd_in):
            in_specs += specs(fn_in, nb)
            operands += [q, kt, v, gts, gtt, tot]
    h_shape = jax.ShapeDtypeStruct((SCAN_NB, N_LAT // SCAN_NB, B_INNER), BF16)
    hf, hb = pl.pallas_call(
        _scan_kernel,
        grid=(n_g, SCAN_STEPS),
        in_specs=in_specs,
        out_specs=[out_spec(fwd_out), out_spec(bwd_out)],
        out_shape=[h_shape, h_shape],
        scratch_shapes=[pltpu.VMEM((SCAN_NB, 2, HEADS, DQK, DV + LANES), F32),
                        pltpu.VMEM((SCAN_NB, 2, HEADS, 1, LANES), F32)],
        compiler_params=_cparams(("parallel", "arbitrary")),
        name="mlstm_scan",
    )(*operands)
    return hf.reshape(N_LAT, B_INNER), hb.reshape(N_LAT, B_INNER)


def _combine_kernel(x_ref, mod_ref, hf_ref, hb_ref, xc_ref, op_ref, hg_ref, sk_ref, wout_ref,
                    g2_ref, wr_ref, br_ref, o_ref, meta_ref, tmeta_ref, cnt_ref, cntr_ref,
                    y_scr, carry, carry_r):
    for h in range(HEADS):
        hs = slice(h * DV, (h + 1) * DV)
        s = hf_ref[:, hs].astype(F32) + hb_ref[:, hs].astype(F32)
        s = s * lax.rsqrt(jnp.mean(s * s, axis=-1, keepdims=True) + EPS)
        y = jax.nn.sigmoid(op_ref[:, hs].astype(F32)) * (
            s * hg_ref[:, hs] + sk_ref[:, hs] * xc_ref[:, hs].astype(F32))
        y_scr[:, hs] = y.astype(BF16)
    out = jnp.dot(y_scr[...], wout_ref[...], preferred_element_type=F32)
    xn = x_ref[...] + mod_ref[0, 2:3, :] * out
    o_ref[...] = xn
    _route_tile(xn, mod_ref, g2_ref, wr_ref, br_ref, meta_ref, tmeta_ref, cnt_ref, cntr_ref,
                carry, carry_r)


def _combine(xa, mod, hf, hb, xc, op, head_g, skip, w_out, g2, wr, br):
    const = lambda i: (0, 0)
    row = lambda i: (i, 0)
    r_in, r_out, r_shape, r_scratch = _router_specs(LAT_TILES)
    return pl.pallas_call(
        _combine_kernel,
        grid=(LAT_TILES,),
        in_specs=[
            pl.BlockSpec((TM, D), row),
            pl.BlockSpec((1, 6, D), lambda i: (_mod_row(i), 0, 0)),
            pl.BlockSpec((TM, B_INNER), row),
            pl.BlockSpec((TM, B_INNER), row),
            pl.BlockSpec((TM, B_INNER), row),
            pl.BlockSpec((TM, B_INNER), row),
            pl.BlockSpec((1, B_INNER), const),
            pl.BlockSpec((1, B_INNER), const),
            pl.BlockSpec((B_INNER, D), const),
        ] + r_in,
        out_specs=[pl.BlockSpec((TM, D), row)] + r_out,
        out_shape=[jax.ShapeDtypeStruct((N_LAT, D), F32)] + r_shape,
        scratch_shapes=[pltpu.VMEM((TM, B_INNER), BF16)] + r_scratch,
        compiler_params=_cparams(("arbitrary",)),
        name="mlstm_combine",
    )(xa, mod, hf, hb, xc, op, head_g, skip, w_out, g2, wr, br)


def _router_weights(w_grp, b_grp, w_exp, b_exp):
    wr = jnp.zeros((LANES, D), F32).at[:N_EXP].set(w_exp.T).at[N_EXP:N_EXP + N_GRP].set(w_grp.T)
    br = jnp.zeros((LANES, 1), F32).at[:N_EXP, 0].set(b_exp).at[N_EXP:N_EXP + N_GRP, 0].set(b_grp)
    return wr.astype(BF16), br


PLAN_BLK = 2048
PLAN_PAIRS = 256


def _plan_kernel(meta_ref, cc_ref, cr_ref, pp_ref, plan_ref):
    row = lax.broadcasted_iota(jnp.int32, (LANES, LANES), 0)
    col = lax.broadcasted_iota(jnp.int32, (LANES, LANES), 1)
    cnt_c = cc_ref[...]
    cnt_r = cr_ref[0:1, :]
    starts_c = jnp.sum(jnp.where(col < row, cnt_r, 0.0), axis=1, keepdims=True)
    erow = lax.broadcasted_iota(jnp.int32, (LANES, PLAN_BLK), 0).astype(F32)
    for j in range(2):
        pos = jnp.sum(jnp.where(erow == meta_ref[j:j + 1, :], starts_c, 0.0), axis=0,
                      keepdims=True) + meta_ref[j + 2:j + 3, :]
        pp_ref[j:j + 1, :] = pos.astype(jnp.int32)
    pp_ref[2:8, :] = jnp.zeros((6, PLAN_BLK), jnp.int32)

    @pl.when(pl.program_id(0) == 0)
    def _():
        ends_c = starts_c + cnt_c
        starts_r = jnp.sum(jnp.where(row < col, cnt_c, 0.0), axis=0, keepdims=True)
        ends_r = starts_r + cnt_r

        def tiles(st, en, cn):
            first_tile = jnp.floor(st * (1.0 / TM))
            n = jnp.where(cn > 0.0, jnp.floor((en - 1.0) * (1.0 / TM)) - first_tile + 1.0, 0.0)
            return first_tile, n

        ft_c, pairs_c = tiles(starts_c, ends_c, cnt_c)
        _, pairs_r = tiles(starts_r, ends_r, cnt_r)
        pend_c = jnp.sum(jnp.where(col <= row, pairs_r, 0.0), axis=1, keepdims=True)
        pstart_c = pend_c - pairs_c
        total = jnp.sum(pairs_r, axis=1, keepdims=True)
        q = lax.broadcasted_iota(jnp.int32, (1, PLAN_PAIRS), 1).astype(F32)
        erow_p = lax.broadcasted_iota(jnp.int32, (LANES, PLAN_PAIRS), 0).astype(F32)

        def at(qv):
            qc = jnp.maximum(jnp.minimum(qv, total - 1.0), 0.0)
            e = jnp.sum(jnp.where(qc >= pend_c, 1.0, 0.0), axis=0, keepdims=True)
            oh = erow_p == e
            tile = jnp.sum(jnp.where(oh, ft_c - pstart_c, 0.0), axis=0, keepdims=True) + qc
            return e, oh, tile

        e_q, oh, tile_q = at(q)
        e_p, _, tile_p = at(q - 1.0)
        lo = jnp.sum(jnp.where(oh, starts_c, 0.0), axis=0, keepdims=True) - tile_q * TM
        hi = jnp.sum(jnp.where(oh, ends_c, 0.0), axis=0, keepdims=True) - tile_q * TM
        rows = (tile_q, e_q, jnp.clip(lo, 0.0, TM), jnp.clip(hi, 0.0, TM),
                jnp.where((q == 0.0) | (tile_q != tile_p), 1.0, 0.0),
                jnp.where((q == 0.0) | (e_q != e_p), 1.0, 0.0),
                jnp.where(q < total, 1.0, 0.0),
                jnp.zeros((1, PLAN_PAIRS), F32))
        for j, v in enumerate(rows):
            plan_ref[j:j + 1, :] = v.astype(jnp.int32)


def _route_plan(meta, cnt_c, cnt_r, n_rows):
    n_pairs = 2 * n_rows // TM + N_EXP - 1
    assert n_pairs <= PLAN_PAIRS and n_rows % PLAN_BLK == 0
    const = lambda i: (0, 0)
    pp, plan = pl.pallas_call(
        _plan_kernel,
        grid=(n_rows // PLAN_BLK,),
        in_specs=[pl.BlockSpec((8, PLAN_BLK), lambda i: (0, i)),
                  pl.BlockSpec((LANES, 1), const),
                  pl.BlockSpec((8, LANES), const)],
        out_specs=[pl.BlockSpec((8, PLAN_BLK), lambda i: (0, i)),
                   pl.BlockSpec((8, PLAN_PAIRS), const)],
        out_shape=[jax.ShapeDtypeStruct((8, n_rows), jnp.int32),
                   jax.ShapeDtypeStruct((8, PLAN_PAIRS), jnp.int32)],
        compiler_params=_cparams(("arbitrary",)),
        name="moe_plan",
    )(meta, cnt_c, cnt_r)
    return pp[0], pp[1], tuple(plan[j, :n_pairs] for j in range(7))


def _moe_experts(xa, routing, mod, g2, w1, w3, w2, layer, n_rows):
    meta, _, cnt_c, cnt_r = routing
    p1, p2, plan = _route_plan(meta, cnt_c, cnt_r, n_rows)
    xs = _scatter(p1, p2, xa, mod, g2, n_rows // TM)
    return p1, p2, _grouped(plan, xs, w1, w3, w2, layer)


def kernel(x, c, ctx, c_ctx, norm_g, w_ada, b_ada, a_w_in, a_b_in, a_g_v, a_w_s, a_b_s, a_w_out,
           b_w_in, b_conv_w, b_conv_b, b_w_q, b_w_k, b_w_v, b_w_gate, b_b_gate, b_head_g, b_skip,
           b_w_out, moe_w_grp, moe_b_grp, moe_w_exp, moe_b_exp, moe_w1, moe_w3, moe_w2, final_g):
    cc = jnp.zeros((MOD_ROWS, D), F32).at[:BATCH].set(c).at[BATCH].set(c_ctx)
    mods = _ada(cc, w_ada, b_ada).reshape(2, MOD_ROWS, 6, D)

    mod = mods[0]
    b_s_full = jnp.repeat(a_b_s[0].T, A_GC, axis=1)
    g2 = norm_g[0, 1].reshape(1, D)
    wr, br = _router_weights(moe_w_grp[0], moe_b_grp[0], moe_w_exp[0], moe_b_exp[0])
    xa, *routing = _gmlp(x.reshape(N_LAT, D), ctx.reshape(N_CTX, D), mod, norm_g[0, 0].reshape(1, D),
                         a_w_in[0].astype(BF16), a_b_in[0].reshape(1, -1), a_g_v[0].reshape(1, -1),
                         a_w_s[0].astype(BF16), b_s_full, a_w_out[0].astype(BF16), g2, wr, br)
    p1, p2, ys = _moe_experts(xa, routing, mod, g2, moe_w1, moe_w3, moe_w2, 0, N_ALL)

    mod = mods[1]
    xa, xm, op = _inproj(p1, p2, xa, mods[0], routing[1], ys, mod, norm_g[1, 0].reshape(1, D),
                         b_w_in[0].astype(BF16))
    wg = jnp.zeros((B_INNER, LANES), F32).at[:, :4 * HEADS].set(b_w_gate[0]).astype(BF16)
    bg = jnp.zeros((1, LANES), F32).at[0, :4 * HEADS].set(b_b_gate[0])
    xc, q, kt, v, gts, gtt, tot = _qkv(xm, b_conv_w[0], b_conv_b[0].reshape(1, -1),
                                  b_w_q[0].astype(BF16),
                                  jnp.transpose(b_w_k[0], (0, 2, 1)).astype(BF16),
                                  b_w_v[0].astype(BF16), wg, b_w_gate[0].T.astype(BF16), bg,
                                  b_b_gate[0].reshape(-1, 1))
    hf, hb = _scan(q, kt, v, gts, gtt, tot)
    g2 = norm_g[1, 1].reshape(1, D)
    wr, br = _router_weights(moe_w_grp[1], moe_b_grp[1], moe_w_exp[1], moe_b_exp[1])
    xl, *routing = _combine(xa, mod, hf, hb, xc, op, b_head_g[0].reshape(1, -1),
                            b_skip[0].reshape(1, -1), b_w_out[0].astype(BF16), g2, wr, br)
    p1, p2, ys = _moe_experts(xl, routing, mod, g2, moe_w1, moe_w3, moe_w2, 1, N_LAT)
    out = _gather(p1, p2, xl, mod, routing[1], final_g.reshape(1, D), ys, LAT_TILES, True)
    return out.reshape(BATCH, SEQ, D)
```

```python
import functools

import jax
import jax.numpy as jnp
from jax import lax
from jax.experimental import pallas as pl
from jax.experimental.pallas import tpu as pltpu

F32 = jnp.float32
BF16 = jnp.bfloat16

D = 1024
BATCH = 8
SEQ = 2048
CTX = 256
EPS = 1e-6
NEG_INF = -1e30
N_LAT = BATCH * SEQ
N_CTX = BATCH * CTX
N_ALL = N_LAT + N_CTX

TM = 256
LAT_TILES = N_LAT // TM
ALL_TILES = N_ALL // TM
TILES_PER_SEQ = SEQ // TM
MOD_ROWS = 16

CHUNK = 128
A_HALF = 2048
A_GROUPS = 8
A_GC = A_HALF // A_GROUPS
B_INNER = 2048
HEADS = 8
DH = B_INNER // HEADS
DQK = DH // 2
DV = DH
CONV_K = 5
HALO = 16
N_EXP = 32
N_GRP = 4
EXP_PER_GRP = 8
D_EXP = 512
LANES = 128

VMEM_LIMIT = 56 * 1024 * 1024


def _cparams(sem):
    return pltpu.CompilerParams(dimension_semantics=sem, vmem_limit_bytes=VMEM_LIMIT)


def _mod_row(i):
    return jnp.where(i < LAT_TILES, i // TILES_PER_SEQ, BATCH)


def _norm_mod(x, g, shift, scale):
    y = x * lax.rsqrt(jnp.mean(x * x, axis=-1, keepdims=True) + EPS) * g
    return y * (1.0 + scale) + shift


def _bf16_terms(x, n):
    terms = []
    r = x
    for _ in range(n):
        t = r.astype(BF16)
        terms.append(t)
        r = r - t.astype(F32)
    return terms


def _dot_nt(a, b):
    return lax.dot_general(a, b, (((1,), (1,)), ((), ())), preferred_element_type=F32)


def _dot_nn(a, b):
    return jnp.dot(a, b, preferred_element_type=F32)


def _dot3(a, b, dot=_dot_nn):
    a1, a2 = _bf16_terms(a, 2)
    b1, b2 = _bf16_terms(b, 2)
    return dot(a1, b1) + (dot(a1, b2) + dot(a2, b1))


def _dot_exact_lhs(a_bf16, b, dot=_dot_nn):
    return sum(dot(a_bf16, t) for t in _bf16_terms(b, 3))


def _dot_exact_rhs(a, b_bf16, dot=_dot_nn):
    return sum(dot(t, b_bf16) for t in _bf16_terms(a, 3))


def _gelu_tanh(x):
    half = 0.5 * x
    t = jnp.tanh(x * (0.7978845608028654 + 0.035677408136300125 * (x * x)))
    return half + half * t


ADA_BN = 1536


def _ada_kernel(c_ref, w_ref, b_ref, o_ref):
    c = c_ref[...]
    a = c * jax.nn.sigmoid(c)
    o_ref[0] = _dot3(a, w_ref[0]) + b_ref[0]


def _ada(cc, w_ada, b_ada):
    depth = w_ada.shape[0]
    return pl.pallas_call(
        _ada_kernel,
        grid=(depth, 6 * D // ADA_BN),
        in_specs=[
            pl.BlockSpec((MOD_ROWS, D), lambda l, j: (0, 0)),
            pl.BlockSpec((1, D, ADA_BN), lambda l, j: (l, 0, j)),
            pl.BlockSpec((1, 1, ADA_BN), lambda l, j: (l, 0, j)),
        ],
        out_specs=pl.BlockSpec((1, MOD_ROWS, ADA_BN), lambda l, j: (l, 0, j)),
        out_shape=jax.ShapeDtypeStruct((depth, MOD_ROWS, 6 * D), F32),
        compiler_params=_cparams(("parallel", "parallel")),
        name="ada",
    )(cc, w_ada, b_ada.reshape(depth, 1, 6 * D))


GM_CH = 512


def _gmlp_kernel(xl_ref, xc_ref, mod_ref, g_ref, win_ref, bin_ref, gv_ref, ws_ref, bs_ref, wout_ref,
                 g2_ref, wr_ref, br_ref, o_ref, meta_ref, tmeta_ref, cnt_ref, cntr_ref,
                 z_scr, y_scr, carry, carry_r):
    x = jnp.where(pl.program_id(0) < LAT_TILES, xl_ref[...], xc_ref[...])
    h = _norm_mod(x, g_ref[...], mod_ref[0, 0:1, :], mod_ref[0, 1:2, :])
    hb = h.astype(BF16)
    s1 = jnp.zeros((TM, 1), F32)
    s2 = jnp.zeros((TM, 1), F32)
    for j in range(2 * A_HALF // GM_CH):
        cs = slice(j * GM_CH, (j + 1) * GM_CH)
        zc = jnp.dot(hb, win_ref[:, cs], preferred_element_type=F32) + bin_ref[:, cs]
        zc = _gelu_tanh(zc)
        z_scr[:, cs] = zc
        if j * GM_CH >= A_HALF:
            s1 = s1 + jnp.sum(zc, axis=-1, keepdims=True)
            s2 = s2 + jnp.sum(zc * zc, axis=-1, keepdims=True)
    mu = s1 * (1.0 / A_HALF)
    rstd = lax.rsqrt(s2 * (1.0 / A_HALF) - mu * mu + EPS)
    for c in range(TM // CHUNK):
        rs = slice(c * CHUNK, (c + 1) * CHUNK)
        for g in range(A_GROUPS):
            cs = slice(g * A_GC, (g + 1) * A_GC)
            vs = slice(A_HALF + g * A_GC, A_HALF + (g + 1) * A_GC)
            v = (z_scr[rs, vs] - mu[rs]) * rstd[rs] * gv_ref[:, cs]
            s = jnp.dot(ws_ref[g], v.astype(BF16), preferred_element_type=F32) + bs_ref[:, cs]
            y_scr[rs, cs] = (z_scr[rs, cs] * s).astype(BF16)
    out = jnp.dot(y_scr[...], wout_ref[...], preferred_element_type=F32)
    xn = x + mod_ref[0, 2:3, :] * out
    o_ref[...] = xn
    _route_tile(xn, mod_ref, g2_ref, wr_ref, br_ref, meta_ref, tmeta_ref, cnt_ref, cntr_ref,
                carry, carry_r)


def _gmlp(xl, xc, mod, g, w_in, b_in, g_v, w_s, b_s_full, w_out, g2, wr, br):
    const = lambda i: (0, 0)
    r_in, r_out, r_shape, r_scratch = _router_specs(ALL_TILES)
    return pl.pallas_call(
        _gmlp_kernel,
        grid=(ALL_TILES,),
        in_specs=[
            pl.BlockSpec((TM, D), lambda i: (jnp.minimum(i, LAT_TILES - 1), 0)),
            pl.BlockSpec((TM, D), lambda i: (jnp.maximum(i - LAT_TILES, 0), 0)),
            pl.BlockSpec((1, 6, D), lambda i: (_mod_row(i), 0, 0)),
            pl.BlockSpec((1, D), const),
            pl.BlockSpec((D, 2 * A_HALF), const),
            pl.BlockSpec((1, 2 * A_HALF), const),
            pl.BlockSpec((1, A_HALF), const),
            pl.BlockSpec((A_GROUPS, CHUNK, CHUNK), lambda i: (0, 0, 0)),
            pl.BlockSpec((CHUNK, A_HALF), const),
            pl.BlockSpec((A_HALF, D), const),
        ] + r_in,
        out_specs=[pl.BlockSpec((TM, D), lambda i: (i, 0))] + r_out,
        out_shape=[jax.ShapeDtypeStruct((N_ALL, D), F32)] + r_shape,
        scratch_shapes=[pltpu.VMEM((TM, 2 * A_HALF), F32), pltpu.VMEM((TM, A_HALF), BF16)] + r_scratch,
        compiler_params=_cparams(("arbitrary",)),
        name="gmlp",
    )(xl, xc, mod, g, w_in, b_in, g_v, w_s, b_s_full, w_out, g2, wr, br)


def _route_tile(x, mod_ref, g_ref, wr_ref, br_ref, meta_ref, tmeta_ref, cnt_ref, cntr_ref,
                carry, carry_r):
    @pl.when(pl.program_id(0) == 0)
    def _():
        carry[...] = jnp.zeros_like(carry)
        carry_r[...] = jnp.zeros_like(carry_r)

    h = _norm_mod(x, g_ref[...], mod_ref[0, 3:4, :], mod_ref[0, 4:5, :])
    lt = _dot_nt(wr_ref[...], h.astype(BF16)) + br_ref[...]
    e_t = lt[0:N_EXP]
    row8 = lax.broadcasted_iota(jnp.int32, (EXP_PER_GRP, TM), 0).astype(F32)
    g_t = jnp.where(row8 < N_GRP, lt[N_EXP:N_EXP + EXP_PER_GRP], -jnp.inf)
    gmax = jnp.max(g_t, axis=0, keepdims=True)
    p_g = 1.0 / jnp.sum(jnp.exp(g_t - gmax), axis=0, keepdims=True)
    g_idx = jnp.min(jnp.where(g_t == gmax, row8, float(EXP_PER_GRP)), axis=0, keepdims=True)
    sel = jnp.zeros((EXP_PER_GRP, TM), F32)
    for g in range(N_GRP):
        sel = sel + jnp.where(g_idx == g, e_t[g * EXP_PER_GRP:(g + 1) * EXP_PER_GRP], 0.0)
    m1 = jnp.max(sel, axis=0, keepdims=True)
    i1 = jnp.min(jnp.where(sel == m1, row8, float(EXP_PER_GRP)), axis=0, keepdims=True)
    sel2 = jnp.where(row8 == i1, -jnp.inf, sel)
    m2 = jnp.max(sel2, axis=0, keepdims=True)
    i2 = jnp.min(jnp.where(sel2 == m2, row8, float(EXP_PER_GRP)), axis=0, keepdims=True)
    e2 = jnp.exp(m2 - m1)
    w1 = p_g / (1.0 + e2)
    w2 = p_g * e2 / (1.0 + e2)
    row = lax.broadcasted_iota(jnp.int32, (LANES, TM), 0).astype(F32)
    id1 = g_idx * EXP_PER_GRP + i1
    id2 = g_idx * EXP_PER_GRP + i2
    oh1 = row == id1
    oh2 = row == id2
    oh = jnp.where(oh1, 1.0, 0.0) + jnp.where(oh2, 1.0, 0.0)
    before = (lax.broadcasted_iota(jnp.int32, (TM, TM), 0)
              < lax.broadcasted_iota(jnp.int32, (TM, TM), 1)).astype(BF16)
    ohb = oh.astype(BF16)
    tot = jnp.dot(ohb, before, preferred_element_type=F32) + carry[...]
    r1 = jnp.sum(jnp.where(oh1, tot, 0.0), axis=0, keepdims=True)
    r2 = jnp.sum(jnp.where(oh2, tot, 0.0), axis=0, keepdims=True)
    carry[...] += jnp.sum(oh, axis=-1, keepdims=True)
    carry_r[...] += _dot_nt(jnp.ones((8, TM), BF16), ohb)
    cnt_ref[...] = carry[...]
    cntr_ref[...] = carry_r[...]
    meta_ref[0:1, :] = id1
    meta_ref[1:2, :] = id2
    meta_ref[2:3, :] = r1
    meta_ref[3:4, :] = r2
    meta_ref[4:8, :] = jnp.zeros((4, TM), F32)
    tmeta_ref[...] = (jnp.where(row == 0.0, w1, 0.0) + jnp.where(row == 1.0, w2, 0.0)).T


def _router_specs(n_tiles):
    const = lambda i: (0, 0)
    n = n_tiles * TM
    in_specs = [pl.BlockSpec((1, D), const), pl.BlockSpec((LANES, D), const),
                pl.BlockSpec((LANES, 1), const)]
    out_specs = [pl.BlockSpec((8, TM), lambda i: (0, i)),
                 pl.BlockSpec((TM, LANES), lambda i: (i, 0)),
                 pl.BlockSpec((LANES, 1), const),
                 pl.BlockSpec((8, LANES), const)]
    out_shape = [jax.ShapeDtypeStruct((8, n), F32), jax.ShapeDtypeStruct((n, LANES), F32),
                 jax.ShapeDtypeStruct((LANES, 1), F32), jax.ShapeDtypeStruct((8, LANES), F32)]
    scratch = [pltpu.VMEM((LANES, 1), F32), pltpu.VMEM((8, LANES), F32)]
    return in_specs, out_specs, out_shape, scratch


ISSUE_UNROLL = 8
ROW_TILE = (8, LANES)


def _row_copy_wait(buf_slot, sem_slot):
    pltpu.make_async_copy(buf_slot, buf_slot, sem_slot).wait()


def _scatter_kernel(p1_ref, p2_ref, x_ref, mod_ref, g_ref, xs_ref, buf, sem):
    i = pl.program_id(0)
    n_steps = pl.num_programs(0)
    slot = lax.rem(i, 2)

    def wait_slot(s):
        _row_copy_wait(buf.at[s], sem.at[s])
        _row_copy_wait(buf.at[s], sem.at[s])

    @pl.when(i >= 2)
    def _():
        wait_slot(slot)

    h = _norm_mod(x_ref[...], g_ref[...], mod_ref[0, 3:4, :], mod_ref[0, 4:5, :])
    buf[slot] = h.reshape(TM, *ROW_TILE)
    base = i * TM

    def body(r, carry):
        src = buf.at[slot, r]
        pltpu.make_async_copy(src, xs_ref.at[p1_ref[base + r]], sem.at[slot]).start()
        pltpu.make_async_copy(src, xs_ref.at[p2_ref[base + r]], sem.at[slot]).start(priority=1)
        return carry

    lax.fori_loop(0, TM, body, 0, unroll=ISSUE_UNROLL)

    @pl.when(i == n_steps - 1)
    def _():
        @pl.when(i >= 1)
        def _():
            wait_slot(1 - slot)

        wait_slot(slot)


def _scatter(p1, p2, xa, mod, g, n_tiles):
    n = n_tiles * TM
    return pl.pallas_call(
        _scatter_kernel,
        grid_spec=pltpu.PrefetchScalarGridSpec(
            num_scalar_prefetch=2,
            grid=(n_tiles,),
            in_specs=[
                pl.BlockSpec((TM, D), lambda i, p1, p2: (i, 0)),
                pl.BlockSpec((1, 6, D), lambda i, p1, p2: (_mod_row(i), 0, 0)),
                pl.BlockSpec((1, D), lambda i, p1, p2: (0, 0)),
            ],
            out_specs=pl.BlockSpec(memory_space=pl.ANY),
            scratch_shapes=[pltpu.VMEM((2, TM) + ROW_TILE, F32), pltpu.SemaphoreType.DMA((2,))],
        ),
        out_shape=jax.ShapeDtypeStruct((2 * n,) + ROW_TILE, F32),
        compiler_params=_cparams(("arbitrary",)),
        name="moe_scatter",
    )(p1, p2, xa, mod, g)


def _grouped_kernel(tile_ref, exp_ref, lo_ref, hi_ref, first_ref, newexp_ref, valid_ref,
                    xs_ref, w1_ref, w3_ref, w2_ref, ys_ref, wb1, wb3, wb2):
    q = pl.program_id(0)

    @pl.when(valid_ref[q] == 1)
    def _():
        @pl.when(newexp_ref[q] == 1)
        def _():
            wb1[...] = w1_ref[0, 0].astype(BF16)
            wb3[...] = w3_ref[0, 0].astype(BF16)
            wb2[...] = w2_ref[0, 0].astype(BF16)

        x = xs_ref[...].reshape(TM, D).astype(BF16)
        a = jnp.dot(x, wb1[...], preferred_element_type=F32)
        b = jnp.dot(x, wb3[...], preferred_element_type=F32)
        rows = lax.broadcasted_iota(jnp.int32, (TM, 1), 0)
        mine = (rows >= lo_ref[q]) & (rows < hi_ref[q])
        act = jnp.where(mine, a * jax.nn.sigmoid(a) * b, 0.0)
        y = jnp.dot(act.astype(BF16), wb2[...], preferred_element_type=F32).reshape(TM, *ROW_TILE)

        @pl.when(first_ref[q] == 1)
        def _():
            ys_ref[...] = y

        @pl.when(first_ref[q] == 0)
        def _():
            ys_ref[...] += y


def _grouped(plan, xs, w1, w3, w2, layer):
    n_pairs = plan[0].shape[0]
    tile_map = lambda q, tile, exp, *_: (tile[q], 0, 0)
    exp_map = lambda q, tile, exp, *_: (layer, exp[q], 0, 0)
    return pl.pallas_call(
        _grouped_kernel,
        grid_spec=pltpu.PrefetchScalarGridSpec(
            num_scalar_prefetch=7,
            grid=(n_pairs,),
            in_specs=[
                pl.BlockSpec((TM,) + ROW_TILE, tile_map),
                pl.BlockSpec((1, 1, D, D_EXP), exp_map),
                pl.BlockSpec((1, 1, D, D_EXP), exp_map),
                pl.BlockSpec((1, 1, D_EXP, D), exp_map),
            ],
            out_specs=pl.BlockSpec((TM,) + ROW_TILE, tile_map),
            scratch_shapes=[pltpu.VMEM((D, D_EXP), BF16), pltpu.VMEM((D, D_EXP), BF16),
                            pltpu.VMEM((D_EXP, D), BF16)],
        ),
        out_shape=jax.ShapeDtypeStruct(xs.shape, F32),
        compiler_params=_cparams(("arbitrary",)),
        name="moe_grouped",
    )(*plan, xs, w1, w3, w2)


def _gather_kernel(p1_ref, p2_ref, x_ref, mod_ref, tm_ref, fg_ref, ys_ref, o_ref, buf, sem,
                   *, final_norm):
    i = pl.program_id(0)
    n_steps = pl.num_programs(0)
    slot = lax.rem(i, 2)

    def issue(tile, s):
        base = tile * TM

        def body(r, carry):
            pltpu.make_async_copy(ys_ref.at[p1_ref[base + r]], buf.at[s, 0, r], sem.at[s]).start()
            pltpu.make_async_copy(ys_ref.at[p2_ref[base + r]], buf.at[s, 1, r],
                                  sem.at[s]).start(priority=1)
            return carry

        lax.fori_loop(0, TM, body, 0, unroll=ISSUE_UNROLL)

    @pl.when(i == 0)
    def _():
        issue(0, 0)

    @pl.when(i + 1 < n_steps)
    def _():
        issue(i + 1, 1 - slot)

    _row_copy_wait(buf.at[slot, 0], sem.at[slot])
    _row_copy_wait(buf.at[slot, 1], sem.at[slot])
    y = (tm_ref[:, 0:1] * buf[slot, 0].reshape(TM, D)
         + tm_ref[:, 1:2] * buf[slot, 1].reshape(TM, D))
    out = x_ref[...] + mod_ref[0, 5:6, :] * y
    if final_norm:
        out = out * lax.rsqrt(jnp.mean(out * out, axis=-1, keepdims=True) + EPS) * fg_ref[...]
    o_ref[...] = out


def _gather(p1, p2, xa, mod, tmeta, final_g, ys, n_tiles, final_norm):
    return pl.pallas_call(
        functools.partial(_gather_kernel, final_norm=final_norm),
        grid_spec=pltpu.PrefetchScalarGridSpec(
            num_scalar_prefetch=2,
            grid=(n_tiles,),
            in_specs=[
                pl.BlockSpec((TM, D), lambda i, p1, p2: (i, 0)),
                pl.BlockSpec((1, 6, D), lambda i, p1, p2: (_mod_row(i), 0, 0)),
                pl.BlockSpec((TM, LANES), lambda i, p1, p2: (i, 0)),
                pl.BlockSpec((1, D), lambda i, p1, p2: (0, 0)),
                pl.BlockSpec(memory_space=pl.ANY),
            ],
            out_specs=pl.BlockSpec((TM, D), lambda i, p1, p2: (i, 0)),
            scratch_shapes=[pltpu.VMEM((2, 2, TM) + ROW_TILE, F32), pltpu.SemaphoreType.DMA((2,))],
        ),
        out_shape=jax.ShapeDtypeStruct((n_tiles * TM, D), F32),
        compiler_params=_cparams(("arbitrary",)),
        name="moe_gather",
    )(p1, p2, xa, mod, tmeta, final_g, ys)


def _issue_row_gathers(p1_ref, p2_ref, ys_ref, buf, sem, tile, s, unrolled):
    base = tile * TM

    def one(r):
        pltpu.make_async_copy(ys_ref.at[p1_ref[base + r]], buf.at[s, 0, r], sem.at[s]).start()
        pltpu.make_async_copy(ys_ref.at[p2_ref[base + r]], buf.at[s, 1, r],
                              sem.at[s]).start(priority=1)

    if unrolled:
        for r in range(TM):
            one(r)
    else:
        def body(r, carry):
            one(r)
            return carry

        lax.fori_loop(0, TM, body, 0, unroll=ISSUE_UNROLL)


def _inproj_kernel(p1_ref, p2_ref, x_ref, mod0_ref, tm_ref, ys_ref, mod_ref, g_ref, win_ref,
                   xn_ref, xm_ref, op_ref, buf, sem):
    i = pl.program_id(0)
    n_steps = pl.num_programs(0)
    slot = lax.rem(i, 2)

    @pl.when(i == 0)
    def _():
        _issue_row_gathers(p1_ref, p2_ref, ys_ref, buf, sem, 0, 0, False)

    _row_copy_wait(buf.at[slot, 0], sem.at[slot])
    _row_copy_wait(buf.at[slot, 1], sem.at[slot])
    y = (tm_ref[:, 0:1] * buf[slot, 0].reshape(TM, D)
         + tm_ref[:, 1:2] * buf[slot, 1].reshape(TM, D))
    xn = x_ref[...] + mod0_ref[0, 5:6, :] * y
    xn_ref[...] = xn
    _issue_row_gathers(p1_ref, p2_ref, ys_ref, buf, sem, jnp.minimum(i + 1, n_steps - 1), 1 - slot,
                       True)
    h = _norm_mod(xn, g_ref[...], mod_ref[0, 0:1, :], mod_ref[0, 1:2, :])
    hb = h.astype(BF16)
    xm_ref[...] = jnp.dot(hb, win_ref[:, :B_INNER], preferred_element_type=F32).astype(BF16)
    op_ref[...] = jnp.dot(hb, win_ref[:, B_INNER:], preferred_element_type=F32).astype(BF16)

    @pl.when(i == n_steps - 1)
    def _():
        _row_copy_wait(buf.at[1 - slot, 0], sem.at[1 - slot])
        _row_copy_wait(buf.at[1 - slot, 1], sem.at[1 - slot])


def _inproj(p1, p2, xa, mod0, tmeta, ys, mod, g, w_in):
    const = lambda i, p1, p2: (0, 0)
    row = lambda i, p1, p2: (i, 0)
    mod_map = lambda i, p1, p2: (_mod_row(i), 0, 0)
    return pl.pallas_call(
        _inproj_kernel,
        grid_spec=pltpu.PrefetchScalarGridSpec(
            num_scalar_prefetch=2,
            grid=(ALL_TILES,),
            in_specs=[
                pl.BlockSpec((TM, D), row),
                pl.BlockSpec((1, 6, D), mod_map),
                pl.BlockSpec((TM, LANES), row),
                pl.BlockSpec(memory_space=pl.ANY),
                pl.BlockSpec((1, 6, D), mod_map),
                pl.BlockSpec((1, D), const),
                pl.BlockSpec((D, 2 * B_INNER), const),
            ],
            out_specs=[pl.BlockSpec((TM, D), row),
                       pl.BlockSpec((TM, B_INNER), row),
                       pl.BlockSpec((TM, B_INNER), row)],
            scratch_shapes=[pltpu.VMEM((2, 2, TM) + ROW_TILE, F32), pltpu.SemaphoreType.DMA((2,))],
        ),
        out_shape=[jax.ShapeDtypeStruct((N_ALL, D), F32),
                   jax.ShapeDtypeStruct((N_ALL, B_INNER), BF16),
                   jax.ShapeDtypeStruct((N_ALL, B_INNER), BF16)],
        compiler_params=_cparams(("arbitrary",)),
        name="mlstm_inproj",
    )(p1, p2, xa, mod0, tmeta, ys, mod, g, w_in)


def _log_sigmoid(x):
    return jnp.minimum(x, 0.0) - jnp.log(1.0 + jnp.exp(-jnp.abs(x)))


def _qkv_kernel(xm_ref, prev_ref, next_ref, cw_ref, cb_ref, wq_ref, wkt_ref, wv_ref, wg_ref, wgt_ref,
                bg_ref, bgt_ref, xc_ref, q_ref, kt_ref, v_ref, gt_ref, gtt_ref, tot_ref):
    i = pl.program_id(0)
    lat = i < LAT_TILES
    first = jnp.where(lat, i % TILES_PER_SEQ == 0, True)
    last = jnp.where(lat, i % TILES_PER_SEQ == TILES_PER_SEQ - 1, True)
    xmb = xm_ref[...]
    xm = xmb.astype(F32)
    prev = jnp.where(first, 0.0, prev_ref[...].astype(F32))
    nxt = jnp.where(last, 0.0, next_ref[...].astype(F32))
    ext = jnp.concatenate([prev[HALO - 8:], xm, nxt[:8]], axis=0)
    acc = jnp.zeros((TM, B_INNER), F32) + cb_ref[...]
    for t in range(CONV_K):
        off = 8 + t - CONV_K // 2
        acc = acc + ext[off:off + TM] * cw_ref[t:t + 1, :]
    xc = acc * jax.nn.sigmoid(acc)
    xcb = xc.astype(BF16)
    xc_ref[...] = xcb
    for h in range(HEADS):
        hs = slice(h * DH, (h + 1) * DH)
        qh = jnp.dot(xcb[:, hs], wq_ref[h], preferred_element_type=F32)
        q_ref[:, h * DQK:(h + 1) * DQK] = (qh * (DQK ** -0.5)).astype(BF16)
        kt_ref[h * DQK:(h + 1) * DQK, :] = lax.dot_general(
            wkt_ref[h], xcb[:, hs], (((1,), (1,)), ((), ())), preferred_element_type=F32).astype(BF16)
        v_ref[:, hs] = jnp.dot(xmb[:, hs], wv_ref[h], preferred_element_type=F32).astype(BF16)
    gts = jnp.dot(xmb, wg_ref[...], preferred_element_type=F32) + bg_ref[...]
    lane = lax.broadcasted_iota(jnp.int32, (CHUNK, LANES), 1)
    lane_kind = (lane >> 3) & 3
    gtt = _dot_nt(wgt_ref[...], xmb) + bgt_ref[...]
    row = lax.broadcasted_iota(jnp.int32, (4 * HEADS, CHUNK), 0)
    row_kind = (row >> 3) & 3
    r = lax.broadcasted_iota(jnp.int32, (CHUNK, CHUNK), 0)
    c = lax.broadcasted_iota(jnp.int32, (CHUNK, CHUNK), 1)
    lower = (c <= r).astype(BF16)
    upper = (c >= r).astype(BF16)
    row_sums = jnp.concatenate([upper, lower, jnp.ones((CHUNK, LANES), BF16)], axis=1)
    for ch in range(TM // CHUNK):
        ts = slice(ch * CHUNK, (ch + 1) * CHUNK)
        g = gts[ts]
        g = jnp.where((lane_kind & 1) == 1, _log_sigmoid(g), g)
        terms = _bf16_terms(g, 3)
        pre = sum(_dot_nn(lower, t) for t in terms)
        suf = sum(_dot_nn(upper, t) for t in terms)
        gt_ref[ts, :] = jnp.where(lane_kind == 1, pre, jnp.where(lane_kind == 3, suf, g))
        gt = gtt[:, ts]
        gt = jnp.where((row_kind & 1) == 1, _log_sigmoid(gt), gt)
        sums = _dot_exact_rhs(gt, row_sums)
        gtt_ref[:, ts] = jnp.where(row_kind == 1, sums[:, :CHUNK],
                                   jnp.where(row_kind == 3, sums[:, CHUNK:2 * CHUNK], gt))
        tot_ref[:, ts] = sums[:, 2 * CHUNK:]


def _qkv(xm, conv_w, conv_b, wq, wkt, wv, wg, wgt, bg, bgt):
    const = lambda i: (0, 0)
    const3 = lambda i: (0, 0, 0)
    hb = TM // HALO
    n_hb = N_ALL // HALO
    return pl.pallas_call(
        _qkv_kernel,
        grid=(ALL_TILES,),
        in_specs=[
            pl.BlockSpec((TM, B_INNER), lambda i: (i, 0)),
            pl.BlockSpec((HALO, B_INNER), lambda i: (jnp.maximum(i * hb - 1, 0), 0)),
            pl.BlockSpec((HALO, B_INNER), lambda i: (jnp.minimum((i + 1) * hb, n_hb - 1), 0)),
            pl.BlockSpec((CONV_K, B_INNER), const),
            pl.BlockSpec((1, B_INNER), const),
            pl.BlockSpec((HEADS, DH, DQK), const3),
            pl.BlockSpec((HEADS, DQK, DH), const3),
            pl.BlockSpec((HEADS, DH, DV), const3),
            pl.BlockSpec((B_INNER, LANES), const),
            pl.BlockSpec((4 * HEADS, B_INNER), const),
            pl.BlockSpec((1, LANES), const),
            pl.BlockSpec((4 * HEADS, 1), const),
        ],
        out_specs=[
            pl.BlockSpec((TM, B_INNER), lambda i: (i, 0)),
            pl.BlockSpec((TM, HEADS * DQK), lambda i: (i, 0)),
            pl.BlockSpec((HEADS * DQK, TM), lambda i: (0, i)),
            pl.BlockSpec((TM, B_INNER), lambda i: (i, 0)),
            pl.BlockSpec((TM, LANES), lambda i: (i, 0)),
            pl.BlockSpec((4 * HEADS, TM), lambda i: (0, i)),
            pl.BlockSpec((4 * HEADS, TM), lambda i: (0, i)),
        ],
        out_shape=[
            jax.ShapeDtypeStruct((N_ALL, B_INNER), BF16),
            jax.ShapeDtypeStruct((N_ALL, HEADS * DQK), BF16),
            jax.ShapeDtypeStruct((HEADS * DQK, N_ALL), BF16),
            jax.ShapeDtypeStruct((N_ALL, B_INNER), BF16),
            jax.ShapeDtypeStruct((N_ALL, LANES), F32),
            jax.ShapeDtypeStruct((4 * HEADS, N_ALL), F32),
            jax.ShapeDtypeStruct((4 * HEADS, N_ALL), F32),
        ],
        compiler_params=_cparams(("parallel",)),
        name="mlstm_qkv",
    )(xm, xm, xm, conv_w, conv_b, wq, wkt, wv, wg, wgt, bg, bgt)


CTX_CHUNKS = CTX // CHUNK
LAT_CHUNKS = SEQ // CHUNK
SCAN_STEPS = CTX_CHUNKS + LAT_CHUNKS


def _scan_dir(nb, d, q_ref, kt_ref, v_ref, g_ref, gt_ref, tot_ref, o_ref, c_scr, m_scr):
    r = lax.broadcasted_iota(jnp.int32, (CHUNK, CHUNK), 0)
    c = lax.broadcasted_iota(jnp.int32, (CHUNK, CHUNK), 1)
    if d == 0:
        keep = c <= r
    else:
        keep = c >= r
    ones_blk = jnp.ones((CHUNK, LANES), BF16)
    bcol_all = g_ref[...]
    gtt = gt_ref[...]
    tot_all = tot_ref[...]
    base = 2 * HEADS * d
    for h in range(HEADS):
        li_r = gtt[base + h:base + h + 1, :]
        b_r = gtt[base + HEADS + h:base + HEADS + h + 1, :]
        b_last = tot_all[base + HEADS + h:base + HEADS + h + 1, :]
        b_full = jnp.broadcast_to(bcol_all[:, base + HEADS + h:base + HEADS + h + 1], (CHUNK, LANES))
        m_old = m_scr[nb, d, h]
        c_old = c_scr[nb, d, h]
        qh = q_ref[:, h * DQK:(h + 1) * DQK]
        kth = kt_ref[h * DQK:(h + 1) * DQK, :]
        vaug = jnp.concatenate([v_ref[:, h * DV:(h + 1) * DV], ones_blk], axis=1)
        g_r = b_last - b_r + li_r
        m_new = jnp.maximum(b_last + m_old, jnp.max(g_r, axis=-1, keepdims=True))
        decay = jnp.exp(b_last + m_old - m_new)
        wk = jnp.exp(g_r - m_new)
        kwt = (kth.astype(F32) * wk).astype(BF16)
        c_scr[nb, d, h] = jnp.concatenate([decay] * 3, axis=1) * c_old + jnp.dot(
            kwt, vaug, preferred_element_type=F32)
        m_scr[nb, d, h] = m_new

        dmat = jnp.where(keep, b_full - (b_r - li_r), NEG_INF)
        inter = b_full + m_old
        m_t = jnp.maximum(inter, jnp.max(dmat, axis=-1, keepdims=True))
        p = jnp.exp(dmat - m_t)
        s = jnp.dot(qh, kth, preferred_element_type=F32) * p
        a = jnp.exp(inter - m_t)
        num = jnp.concatenate([a] * 3, axis=1) * jnp.dot(
            qh, c_old.astype(BF16), preferred_element_type=F32) + jnp.dot(
                s.astype(BF16), vaug, preferred_element_type=F32)
        inv = 1.0 / jnp.maximum(jnp.abs(num[:, DV:]), jnp.exp(-m_t))
        o_ref[nb, :, h * DV:(h + 1) * DV] = (
            num[:, :DV] * jnp.concatenate([inv] * 2, axis=1)).astype(BF16)


SCAN_NB = 2
SCAN_IN = 6


def _scan_kernel(*refs):
    n_in = SCAN_NB * 2 * SCAN_IN
    ins, outs = refs[:n_in], refs[n_in:n_in + 2]
    c_scr, m_scr = refs[n_in + 2:]

    @pl.when(pl.program_id(1) == 0)
    def _():
        c_scr[...] = jnp.zeros_like(c_scr)
        m_scr[...] = jnp.zeros_like(m_scr)

    for nb in range(SCAN_NB):
        for d in range(2):
            k = nb * 2 + d
            _scan_dir(nb, d, *ins[k * SCAN_IN:(k + 1) * SCAN_IN], outs[d], c_scr, m_scr)


def _scan(q, kt, v, gts, gtt, tot):
    lat_blk = N_LAT // CHUNK

    def fwd_in(b, j):
        return jnp.where(j < CTX_CHUNKS, lat_blk + b * CTX_CHUNKS + j, b * LAT_CHUNKS + j - CTX_CHUNKS)

    def bwd_in(b, j):
        return jnp.where(j < CTX_CHUNKS, lat_blk + b * CTX_CHUNKS + (CTX_CHUNKS - 1 - j),
                         b * LAT_CHUNKS + (SCAN_STEPS - 1 - j))

    def fwd_out(b, j):
        return b * LAT_CHUNKS + jnp.maximum(j - CTX_CHUNKS, 0)

    def bwd_out(b, j):
        return b * LAT_CHUNKS + jnp.minimum(SCAN_STEPS - 1 - j, LAT_CHUNKS - 1)

    n_g = BATCH // SCAN_NB

    def specs(fn, nb):
        blk = lambda g, j: fn(nb * n_g + g, j)
        return [
            pl.BlockSpec((CHUNK, HEADS * DQK), lambda g, j: (blk(g, j), 0)),
            pl.BlockSpec((HEADS * DQK, CHUNK), lambda g, j: (0, blk(g, j))),
            pl.BlockSpec((CHUNK, B_INNER), lambda g, j: (blk(g, j), 0)),
            pl.BlockSpec((CHUNK, LANES), lambda g, j: (blk(g, j), 0)),
            pl.BlockSpec((4 * HEADS, CHUNK), lambda g, j: (0, blk(g, j))),
            pl.BlockSpec((4 * HEADS, CHUNK), lambda g, j: (0, blk(g, j))),
        ]

    def out_spec(fn):
        return pl.BlockSpec((SCAN_NB, CHUNK, B_INNER), lambda g, j: (0, fn(g, j), 0))

    in_specs, operands = [], []
    for nb in range(SCAN_NB):
        for fn_in in (fwd_in, bwd_in):
            in_specs += specs(fn_in, nb)
            operands += [q, kt, v, gts, gtt, tot]
    h_shape = jax.ShapeDtypeStruct((SCAN_NB, N_LAT // SCAN_NB, B_INNER), BF16)
    hf, hb = pl.pallas_call(
        _scan_kernel,
        grid=(n_g, SCAN_STEPS),
        in_specs=in_specs,
        out_specs=[out_spec(fwd_out), out_spec(bwd_out)],
        out_shape=[h_shape, h_shape],
        scratch_shapes=[pltpu.VMEM((SCAN_NB, 2, HEADS, DQK, DV + LANES), F32),
                        pltpu.VMEM((SCAN_NB, 2, HEADS, 1, LANES), F32)],
        compiler_params=_cparams(("parallel", "arbitrary")),
        name="mlstm_scan",
    )(*operands)
    return hf.reshape(N_LAT, B_INNER), hb.reshape(N_LAT, B_INNER)


def _combine_kernel(x_ref, mod_ref, hf_ref, hb_ref, xc_ref, op_ref, hg_ref, sk_ref, wout_ref,
                    g2_ref, wr_ref, br_ref, o_ref, meta_ref, tmeta_ref, cnt_ref, cntr_ref,
                    y_scr, carry, carry_r):
    for h in range(HEADS):
        hs = slice(h * DV, (h + 1) * DV)
        s = hf_ref[:, hs].astype(F32) + hb_ref[:, hs].astype(F32)
        s = s * lax.rsqrt(jnp.mean(s * s, axis=-1, keepdims=True) + EPS)
        y = jax.nn.sigmoid(op_ref[:, hs].astype(F32)) * (
            s * hg_ref[:, hs] + sk_ref[:, hs] * xc_ref[:, hs].astype(F32))
        y_scr[:, hs] = y.astype(BF16)
    out = jnp.dot(y_scr[...], wout_ref[...], preferred_element_type=F32)
    xn = x_ref[...] + mod_ref[0, 2:3, :] * out
    o_ref[...] = xn
    _route_tile(xn, mod_ref, g2_ref, wr_ref, br_ref, meta_ref, tmeta_ref, cnt_ref, cntr_ref,
                carry, carry_r)


def _combine(xa, mod, hf, hb, xc, op, head_g, skip, w_out, g2, wr, br):
    const = lambda i: (0, 0)
    row = lambda i: (i, 0)
    r_in, r_out, r_shape, r_scratch = _router_specs(LAT_TILES)
    return pl.pallas_call(
        _combine_kernel,
        grid=(LAT_TILES,),
        in_specs=[
            pl.BlockSpec((TM, D), row),
            pl.BlockSpec((1, 6, D), lambda i: (_mod_row(i), 0, 0)),
            pl.BlockSpec((TM, B_INNER), row),
            pl.BlockSpec((TM, B_INNER), row),
            pl.BlockSpec((TM, B_INNER), row),
            pl.BlockSpec((TM, B_INNER), row),
            pl.BlockSpec((1, B_INNER), const),
            pl.BlockSpec((1, B_INNER), const),
            pl.BlockSpec((B_INNER, D), const),
        ] + r_in,
        out_specs=[pl.BlockSpec((TM, D), row)] + r_out,
        out_shape=[jax.ShapeDtypeStruct((N_LAT, D), F32)] + r_shape,
        scratch_shapes=[pltpu.VMEM((TM, B_INNER), BF16)] + r_scratch,
        compiler_params=_cparams(("arbitrary",)),
        name="mlstm_combine",
    )(xa, mod, hf, hb, xc, op, head_g, skip, w_out, g2, wr, br)


def _router_weights(w_grp, b_grp, w_exp, b_exp):
    wr = jnp.zeros((LANES, D), F32).at[:N_EXP].set(w_exp.T).at[N_EXP:N_EXP + N_GRP].set(w_grp.T)
    br = jnp.zeros((LANES, 1), F32).at[:N_EXP, 0].set(b_exp).at[N_EXP:N_EXP + N_GRP, 0].set(b_grp)
    return wr.astype(BF16), br


PLAN_BLK = 2048
PLAN_PAIRS = 256


def _plan_kernel(meta_ref, cc_ref, cr_ref, pp_ref, plan_ref):
    row = lax.broadcasted_iota(jnp.int32, (LANES, LANES), 0)
    col = lax.broadcasted_iota(jnp.int32, (LANES, LANES), 1)
    cnt_c = cc_ref[...]
    cnt_r = cr_ref[0:1, :]
    starts_c = jnp.sum(jnp.where(col < row, cnt_r, 0.0), axis=1, keepdims=True)
    erow = lax.broadcasted_iota(jnp.int32, (LANES, PLAN_BLK), 0).astype(F32)
    for j in range(2):
        pos = jnp.sum(jnp.where(erow == meta_ref[j:j + 1, :], starts_c, 0.0), axis=0,
                      keepdims=True) + meta_ref[j + 2:j + 3, :]
        pp_ref[j:j + 1, :] = pos.astype(jnp.int32)
    pp_ref[2:8, :] = jnp.zeros((6, PLAN_BLK), jnp.int32)

    @pl.when(pl.program_id(0) == 0)
    def _():
        ends_c = starts_c + cnt_c
        starts_r = jnp.sum(jnp.where(row < col, cnt_c, 0.0), axis=0, keepdims=True)
        ends_r = starts_r + cnt_r

        def tiles(st, en, cn):
            first_tile = jnp.floor(st * (1.0 / TM))
            n = jnp.where(cn > 0.0, jnp.floor((en - 1.0) * (1.0 / TM)) - first_tile + 1.0, 0.0)
            return first_tile, n

        ft_c, pairs_c = tiles(starts_c, ends_c, cnt_c)
        _, pairs_r = tiles(starts_r, ends_r, cnt_r)
        pend_c = jnp.sum(jnp.where(col <= row, pairs_r, 0.0), axis=1, keepdims=True)
        pstart_c = pend_c - pairs_c
        total = jnp.sum(pairs_r, axis=1, keepdims=True)
        q = lax.broadcasted_iota(jnp.int32, (1, PLAN_PAIRS), 1).astype(F32)
        erow_p = lax.broadcasted_iota(jnp.int32, (LANES, PLAN_PAIRS), 0).astype(F32)

        def at(qv):
            qc = jnp.maximum(jnp.minimum(qv, total - 1.0), 0.0)
            e = jnp.sum(jnp.where(qc >= pend_c, 1.0, 0.0), axis=0, keepdims=True)
            oh = erow_p == e
            tile = jnp.sum(jnp.where(oh, ft_c - pstart_c, 0.0), axis=0, keepdims=True) + qc
            return e, oh, tile

        e_q, oh, tile_q = at(q)
        e_p, _, tile_p = at(q - 1.0)
        lo = jnp.sum(jnp.where(oh, starts_c, 0.0), axis=0, keepdims=True) - tile_q * TM
        hi = jnp.sum(jnp.where(oh, ends_c, 0.0), axis=0, keepdims=True) - tile_q * TM
        rows = (tile_q, e_q, jnp.clip(lo, 0.0, TM), jnp.clip(hi, 0.0, TM),
                jnp.where((q == 0.0) | (tile_q != tile_p), 1.0, 0.0),
                jnp.where((q == 0.0) | (e_q != e_p), 1.0, 0.0),
                jnp.where(q < total, 1.0, 0.0),
                jnp.zeros((1, PLAN_PAIRS), F32))
        for j, v in enumerate(rows):
            plan_ref[j:j + 1, :] = v.astype(jnp.int32)


def _route_plan(meta, cnt_c, cnt_r, n_rows):
    n_pairs = 2 * n_rows // TM + N_EXP - 1
    assert n_pairs <= PLAN_PAIRS and n_rows % PLAN_BLK == 0
    const = lambda i: (0, 0)
    pp, plan = pl.pallas_call(
        _plan_kernel,
        grid=(n_rows // PLAN_BLK,),
        in_specs=[pl.BlockSpec((8, PLAN_BLK), lambda i: (0, i)),
                  pl.BlockSpec((LANES, 1), const),
                  pl.BlockSpec((8, LANES), const)],
        out_specs=[pl.BlockSpec((8, PLAN_BLK), lambda i: (0, i)),
                   pl.BlockSpec((8, PLAN_PAIRS), const)],
        out_shape=[jax.ShapeDtypeStruct((8, n_rows), jnp.int32),
                   jax.ShapeDtypeStruct((8, PLAN_PAIRS), jnp.int32)],
        compiler_params=_cparams(("arbitrary",)),
        name="moe_plan",
    )(meta, cnt_c, cnt_r)
    return pp[0], pp[1], tuple(plan[j, :n_pairs] for j in range(7))


def _moe_experts(xa, routing, mod, g2, w1, w3, w2, layer, n_rows):
    meta, _, cnt_c, cnt_r = routing
    p1, p2, plan = _route_plan(meta, cnt_c, cnt_r, n_rows)
    xs = _scatter(p1, p2, xa, mod, g2, n_rows // TM)
    return p1, p2, _grouped(plan, xs, w1, w3, w2, layer)


def kernel(x, c, ctx, c_ctx, norm_g, w_ada, b_ada, a_w_in, a_b_in, a_g_v, a_w_s, a_b_s, a_w_out,
           b_w_in, b_conv_w, b_conv_b, b_w_q, b_w_k, b_w_v, b_w_gate, b_b_gate, b_head_g, b_skip,
           b_w_out, moe_w_grp, moe_b_grp, moe_w_exp, moe_b_exp, moe_w1, moe_w3, moe_w2, final_g):
    cc = jnp.zeros((MOD_ROWS, D), F32).at[:BATCH].set(c).at[BATCH].set(c_ctx)
    mods = _ada(cc, w_ada, b_ada).reshape(2, MOD_ROWS, 6, D)

    mod = mods[0]
    b_s_full = jnp.repeat(a_b_s[0].T, A_GC, axis=1)
    g2 = norm_g[0, 1].reshape(1, D)
    wr, br = _router_weights(moe_w_grp[0], moe_b_grp[0], moe_w_exp[0], moe_b_exp[0])
    xa, *routing = _gmlp(x.reshape(N_LAT, D), ctx.reshape(N_CTX, D), mod, norm_g[0, 0].reshape(1, D),
                         a_w_in[0].astype(BF16), a_b_in[0].reshape(1, -1), a_g_v[0].reshape(1, -1),
                         a_w_s[0].astype(BF16), b_s_full, a_w_out[0].astype(BF16), g2, wr, br)
    p1, p2, ys = _moe_experts(xa, routing, mod, g2, moe_w1, moe_w3, moe_w2, 0, N_ALL)

    mod = mods[1]
    xa, xm, op = _inproj(p1, p2, xa, mods[0], routing[1], ys, mod, norm_g[1, 0].reshape(1, D),
                         b_w_in[0].astype(BF16))
    wg = jnp.zeros((B_INNER, LANES), F32).at[:, :4 * HEADS].set(b_w_gate[0]).astype(BF16)
    bg = jnp.zeros((1, LANES), F32).at[0, :4 * HEADS].set(b_b_gate[0])
    xc, q, kt, v, gts, gtt, tot = _qkv(xm, b_conv_w[0], b_conv_b[0].reshape(1, -1),
                                  b_w_q[0].astype(BF16),
                                  jnp.transpose(b_w_k[0], (0, 2, 1)).astype(BF16),
                                  b_w_v[0].astype(BF16), wg, b_w_gate[0].T.astype(BF16), bg,
                                  b_b_gate[0].reshape(-1, 1))
    hf, hb = _scan(q, kt, v, gts, gtt, tot)
    g2 = norm_g[1, 1].reshape(1, D)
    wr, br = _router_weights(moe_w_grp[1], moe_b_grp[1], moe_w_exp[1], moe_b_exp[1])
    xl, *routing = _combine(xa, mod, hf, hb, xc, op, b_head_g[0].reshape(1, -1),
                            b_skip[0].reshape(1, -1), b_w_out[0].astype(BF16), g2, wr, br)
    p1, p2, ys = _moe_experts(xl, routing, mod, g2, moe_w1, moe_w3, moe_w2, 1, N_LAT)
    out = _gather(p1, p2, xl, mod, routing[1], final_g.reshape(1, D), ys, LAT_TILES, True)
    return out.reshape(BATCH, SEQ, D)
```

```python
import functools

import jax
import jax.numpy as jnp
from jax import lax
from jax.experimental import pallas as pl
from jax.experimental.pallas import tpu as pltpu

F32 = jnp.float32
BF16 = jnp.bfloat16

D = 1024
BATCH = 8
SEQ = 2048
CTX = 256
EPS = 1e-6
NEG_INF = -1e30
N_LAT = BATCH * SEQ
N_CTX = BATCH * CTX
N_ALL = N_LAT + N_CTX

TM = 256
LAT_TILES = N_LAT // TM
ALL_TILES = N_ALL // TM
TILES_PER_SEQ = SEQ // TM
MOD_ROWS = 16

CHUNK = 128
A_HALF = 2048
A_GROUPS = 8
A_GC = A_HALF // A_GROUPS
B_INNER = 2048
HEADS = 8
DH = B_INNER // HEADS
DQK = DH // 2
DV = DH
CONV_K = 5
HALO = 16
N_EXP = 32
N_GRP = 4
EXP_PER_GRP = 8
D_EXP = 512
LANES = 128

VMEM_LIMIT = 56 * 1024 * 1024


def _cparams(sem):
    return pltpu.CompilerParams(dimension_semantics=sem, vmem_limit_bytes=VMEM_LIMIT)


def _mod_row(i):
    return jnp.where(i < LAT_TILES, i // TILES_PER_SEQ, BATCH)


def _norm_mod(x, g, shift, scale):
    y = x * lax.rsqrt(jnp.mean(x * x, axis=-1, keepdims=True) + EPS) * g
    return y * (1.0 + scale) + shift


def _bf16_terms(x, n):
    terms = []
    r = x
    for _ in range(n):
        t = r.astype(BF16)
        terms.append(t)
        r = r - t.astype(F32)
    return terms


def _dot_nt(a, b):
    return lax.dot_general(a, b, (((1,), (1,)), ((), ())), preferred_element_type=F32)


def _dot_nn(a, b):
    return jnp.dot(a, b, preferred_element_type=F32)


def _dot3(a, b, dot=_dot_nn):
    a1, a2 = _bf16_terms(a, 2)
    b1, b2 = _bf16_terms(b, 2)
    return dot(a1, b1) + (dot(a1, b2) + dot(a2, b1))


def _dot_exact_lhs(a_bf16, b, dot=_dot_nn):
    return sum(dot(a_bf16, t) for t in _bf16_terms(b, 3))


def _dot_exact_rhs(a, b_bf16, dot=_dot_nn):
    return sum(dot(t, b_bf16) for t in _bf16_terms(a, 3))


def _gelu_tanh(x):
    half = 0.5 * x
    t = jnp.tanh(x * (0.7978845608028654 + 0.035677408136300125 * (x * x)))
    return half + half * t


ADA_BN = 1536


def _ada_kernel(c_ref, w_ref, b_ref, o_ref):
    c = c_ref[...]
    a = c * jax.nn.sigmoid(c)
    o_ref[0] = _dot3(a, w_ref[0]) + b_ref[0]


def _ada(cc, w_ada, b_ada):
    depth = w_ada.shape[0]
    return pl.pallas_call(
        _ada_kernel,
        grid=(depth, 6 * D // ADA_BN),
        in_specs=[
            pl.BlockSpec((MOD_ROWS, D), lambda l, j: (0, 0)),
            pl.BlockSpec((1, D, ADA_BN), lambda l, j: (l, 0, j)),
            pl.BlockSpec((1, 1, ADA_BN), lambda l, j: (l, 0, j)),
        ],
        out_specs=pl.BlockSpec((1, MOD_ROWS, ADA_BN), lambda l, j: (l, 0, j)),
        out_shape=jax.ShapeDtypeStruct((depth, MOD_ROWS, 6 * D), F32),
        compiler_params=_cparams(("parallel", "parallel")),
        name="ada",
    )(cc, w_ada, b_ada.reshape(depth, 1, 6 * D))


GM_CH = 512


def _gmlp_kernel(xl_ref, xc_ref, mod_ref, g_ref, win_ref, bin_ref, gv_ref, ws_ref, bs_ref, wout_ref,
                 g2_ref, wr_ref, br_ref, o_ref, meta_ref, tmeta_ref, cnt_ref, cntr_ref,
                 z_scr, y_scr, carry, carry_r):
    x = jnp.where(pl.program_id(0) < LAT_TILES, xl_ref[...], xc_ref[...])
    h = _norm_mod(x, g_ref[...], mod_ref[0, 0:1, :], mod_ref[0, 1:2, :])
    hb = h.astype(BF16)
    s1 = jnp.zeros((TM, 1), F32)
    s2 = jnp.zeros((TM, 1), F32)
    for j in range(2 * A_HALF // GM_CH):
        cs = slice(j * GM_CH, (j + 1) * GM_CH)
        zc = jnp.dot(hb, win_ref[:, cs], preferred_element_type=F32) + bin_ref[:, cs]
        zc = _gelu_tanh(zc)
        z_scr[:, cs] = zc
        if j * GM_CH >= A_HALF:
            s1 = s1 + jnp.sum(zc, axis=-1, keepdims=True)
            s2 = s2 + jnp.sum(zc * zc, axis=-1, keepdims=True)
    mu = s1 * (1.0 / A_HALF)
    rstd = lax.rsqrt(s2 * (1.0 / A_HALF) - mu * mu + EPS)
    for c in range(TM // CHUNK):
        rs = slice(c * CHUNK, (c + 1) * CHUNK)
        for g in range(A_GROUPS):
            cs = slice(g * A_GC, (g + 1) * A_GC)
            vs = slice(A_HALF + g * A_GC, A_HALF + (g + 1) * A_GC)
            v = (z_scr[rs, vs] - mu[rs]) * rstd[rs] * gv_ref[:, cs]
            s = jnp.dot(ws_ref[g], v.astype(BF16), preferred_element_type=F32) + bs_ref[:, cs]
            y_scr[rs, cs] = (z_scr[rs, cs] * s).astype(BF16)
    out = jnp.dot(y_scr[...], wout_ref[...], preferred_element_type=F32)
    xn = x + mod_ref[0, 2:3, :] * out
    o_ref[...] = xn
    _route_tile(xn, mod_ref, g2_ref, wr_ref, br_ref, meta_ref, tmeta_ref, cnt_ref, cntr_ref,
                carry, carry_r)


def _gmlp(xl, xc, mod, g, w_in, b_in, g_v, w_s, b_s_full, w_out, g2, wr, br):
    const = lambda i: (0, 0)
    r_in, r_out, r_shape, r_scratch = _router_specs(ALL_TILES)
    return pl.pallas_call(
        _gmlp_kernel,
        grid=(ALL_TILES,),
        in_specs=[
            pl.BlockSpec((TM, D), lambda i: (jnp.minimum(i, LAT_TILES - 1), 0)),
            pl.BlockSpec((TM, D), lambda i: (jnp.maximum(i - LAT_TILES, 0), 0)),
            pl.BlockSpec((1, 6, D), lambda i: (_mod_row(i), 0, 0)),
            pl.BlockSpec((1, D), const),
            pl.BlockSpec((D, 2 * A_HALF), const),
            pl.BlockSpec((1, 2 * A_HALF), const),
            pl.BlockSpec((1, A_HALF), const),
            pl.BlockSpec((A_GROUPS, CHUNK, CHUNK), lambda i: (0, 0, 0)),
            pl.BlockSpec((CHUNK, A_HALF), const),
            pl.BlockSpec((A_HALF, D), const),
        ] + r_in,
        out_specs=[pl.BlockSpec((TM, D), lambda i: (i, 0))] + r_out,
        out_shape=[jax.ShapeDtypeStruct((N_ALL, D), F32)] + r_shape,
        scratch_shapes=[pltpu.VMEM((TM, 2 * A_HALF), F32), pltpu.VMEM((TM, A_HALF), BF16)] + r_scratch,
        compiler_params=_cparams(("arbitrary",)),
        name="gmlp",
    )(xl, xc, mod, g, w_in, b_in, g_v, w_s, b_s_full, w_out, g2, wr, br)


def _route_tile(x, mod_ref, g_ref, wr_ref, br_ref, meta_ref, tmeta_ref, cnt_ref, cntr_ref,
                carry, carry_r):
    @pl.when(pl.program_id(0) == 0)
    def _():
        carry[...] = jnp.zeros_like(carry)
        carry_r[...] = jnp.zeros_like(carry_r)

    h = _norm_mod(x, g_ref[...], mod_ref[0, 3:4, :], mod_ref[0, 4:5, :])
    lt = _dot_nt(wr_ref[...], h.astype(BF16)) + br_ref[...]
    e_t = lt[0:N_EXP]
    row8 = lax.broadcasted_iota(jnp.int32, (EXP_PER_GRP, TM), 0).astype(F32)
    g_t = jnp.where(row8 < N_GRP, lt[N_EXP:N_EXP + EXP_PER_GRP], -jnp.inf)
    gmax = jnp.max(g_t, axis=0, keepdims=True)
    p_g = 1.0 / jnp.sum(jnp.exp(g_t - gmax), axis=0, keepdims=True)
    g_idx = jnp.min(jnp.where(g_t == gmax, row8, float(EXP_PER_GRP)), axis=0, keepdims=True)
    sel = jnp.zeros((EXP_PER_GRP, TM), F32)
    for g in range(N_GRP):
        sel = sel + jnp.where(g_idx == g, e_t[g * EXP_PER_GRP:(g + 1) * EXP_PER_GRP], 0.0)
    m1 = jnp.max(sel, axis=0, keepdims=True)
    i1 = jnp.min(jnp.where(sel == m1, row8, float(EXP_PER_GRP)), axis=0, keepdims=True)
    sel2 = jnp.where(row8 == i1, -jnp.inf, sel)
    m2 = jnp.max(sel2, axis=0, keepdims=True)
    i2 = jnp.min(jnp.where(sel2 == m2, row8, float(EXP_PER_GRP)), axis=0, keepdims=True)
    e2 = jnp.exp(m2 - m1)
    w1 = p_g / (1.0 + e2)
    w2 = p_g * e2 / (1.0 + e2)
    row = lax.broadcasted_iota(jnp.int32, (LANES, TM), 0).astype(F32)
    id1 = g_idx * EXP_PER_GRP + i1
    id2 = g_idx * EXP_PER_GRP + i2
    oh1 = row == id1
    oh2 = row == id2
    oh = jnp.where(oh1, 1.0, 0.0) + jnp.where(oh2, 1.0, 0.0)
    before = (lax.broadcasted_iota(jnp.int32, (TM, TM), 0)
              < lax.broadcasted_iota(jnp.int32, (TM, TM), 1)).astype(BF16)
    ohb = oh.astype(BF16)
    tot = jnp.dot(ohb, before, preferred_element_type=F32) + carry[...]
    r1 = jnp.sum(jnp.where(oh1, tot, 0.0), axis=0, keepdims=True)
    r2 = jnp.sum(jnp.where(oh2, tot, 0.0), axis=0, keepdims=True)
    carry[...] += jnp.sum(oh, axis=-1, keepdims=True)
    carry_r[...] += _dot_nt(jnp.ones((8, TM), BF16), ohb)
    cnt_ref[...] = carry[...]
    cntr_ref[...] = carry_r[...]
    meta_ref[0:1, :] = id1
    meta_ref[1:2, :] = id2
    meta_ref[2:3, :] = r1
    meta_ref[3:4, :] = r2
    meta_ref[4:8, :] = jnp.zeros((4, TM), F32)
    tmeta_ref[...] = (jnp.where(row == 0.0, w1, 0.0) + jnp.where(row == 1.0, w2, 0.0)).T


def _router_specs(n_tiles):
    const = lambda i: (0, 0)
    n = n_tiles * TM
    in_specs = [pl.BlockSpec((1, D), const), pl.BlockSpec((LANES, D), const),
                pl.BlockSpec((LANES, 1), const)]
    out_specs = [pl.BlockSpec((8, TM), lambda i: (0, i)),
                 pl.BlockSpec((TM, LANES), lambda i: (i, 0)),
                 pl.BlockSpec((LANES, 1), const),
                 pl.BlockSpec((8, LANES), const)]
    out_shape = [jax.ShapeDtypeStruct((8, n), F32), jax.ShapeDtypeStruct((n, LANES), F32),
                 jax.ShapeDtypeStruct((LANES, 1), F32), jax.ShapeDtypeStruct((8, LANES), F32)]
    scratch = [pltpu.VMEM((LANES, 1), F32), pltpu.VMEM((8, LANES), F32)]
    return in_specs, out_specs, out_shape, scratch


ISSUE_UNROLL = 8
ROW_TILE = (8, LANES)


def _row_copy_wait(buf_slot, sem_slot):
    pltpu.make_async_copy(buf_slot, buf_slot, sem_slot).wait()


def _scatter_kernel(p1_ref, p2_ref, x_ref, mod_ref, g_ref, xs_ref, buf, sem):
    i = pl.program_id(0)
    n_steps = pl.num_programs(0)
    slot = lax.rem(i, 2)

    def wait_slot(s):
        _row_copy_wait(buf.at[s], sem.at[s])
        _row_copy_wait(buf.at[s], sem.at[s])

    @pl.when(i >= 2)
    def _():
        wait_slot(slot)

    h = _norm_mod(x_ref[...], g_ref[...], mod_ref[0, 3:4, :], mod_ref[0, 4:5, :])
    buf[slot] = h.reshape(TM, *ROW_TILE)
    base = i * TM

    def body(r, carry):
        src = buf.at[slot, r]
        pltpu.make_async_copy(src, xs_ref.at[p1_ref[base + r]], sem.at[slot]).start()
        pltpu.make_async_copy(src, xs_ref.at[p2_ref[base + r]], sem.at[slot]).start(priority=1)
        return carry

    lax.fori_loop(0, TM, body, 0, unroll=ISSUE_UNROLL)

    @pl.when(i == n_steps - 1)
    def _():
        @pl.when(i >= 1)
        def _():
            wait_slot(1 - slot)

        wait_slot(slot)


def _scatter(p1, p2, xa, mod, g, n_tiles):
    n = n_tiles * TM
    return pl.pallas_call(
        _scatter_kernel,
        grid_spec=pltpu.PrefetchScalarGridSpec(
            num_scalar_prefetch=2,
            grid=(n_tiles,),
            in_specs=[
                pl.BlockSpec((TM, D), lambda i, p1, p2: (i, 0)),
                pl.BlockSpec((1, 6, D), lambda i, p1, p2: (_mod_row(i), 0, 0)),
                pl.BlockSpec((1, D), lambda i, p1, p2: (0, 0)),
            ],
            out_specs=pl.BlockSpec(memory_space=pl.ANY),
            scratch_shapes=[pltpu.VMEM((2, TM) + ROW_TILE, F32), pltpu.SemaphoreType.DMA((2,))],
        ),
        out_shape=jax.ShapeDtypeStruct((2 * n,) + ROW_TILE, F32),
        compiler_params=_cparams(("arbitrary",)),
        name="moe_scatter",
    )(p1, p2, xa, mod, g)


def _grouped_kernel(tile_ref, exp_ref, lo_ref, hi_ref, first_ref, newexp_ref, valid_ref,
                    xs_ref, w1_ref, w3_ref, w2_ref, ys_ref, wb1, wb3, wb2):
    q = pl.program_id(0)

    @pl.when(valid_ref[q] == 1)
    def _():
        @pl.when(newexp_ref[q] == 1)
        def _():
            wb1[...] = w1_ref[0, 0].astype(BF16)
            wb3[...] = w3_ref[0, 0].astype(BF16)
            wb2[...] = w2_ref[0, 0].astype(BF16)

        x = xs_ref[...].reshape(TM, D).astype(BF16)
        a = jnp.dot(x, wb1[...], preferred_element_type=F32)
        b = jnp.dot(x, wb3[...], preferred_element_type=F32)
        rows = lax.broadcasted_iota(jnp.int32, (TM, 1), 0)
        mine = (rows >= lo_ref[q]) & (rows < hi_ref[q])
        act = jnp.where(mine, a * jax.nn.sigmoid(a) * b, 0.0)
        y = jnp.dot(act.astype(BF16), wb2[...], preferred_element_type=F32).reshape(TM, *ROW_TILE)

        @pl.when(first_ref[q] == 1)
        def _():
            ys_ref[...] = y

        @pl.when(first_ref[q] == 0)
        def _():
            ys_ref[...] += y


def _grouped(plan, xs, w1, w3, w2, layer):
    n_pairs = plan[0].shape[0]
    tile_map = lambda q, tile, exp, *_: (tile[q], 0, 0)
    exp_map = lambda q, tile, exp, *_: (layer, exp[q], 0, 0)
    return pl.pallas_call(
        _grouped_kernel,
        grid_spec=pltpu.PrefetchScalarGridSpec(
            num_scalar_prefetch=7,
            grid=(n_pairs,),
            in_specs=[
                pl.BlockSpec((TM,) + ROW_TILE, tile_map),
                pl.BlockSpec((1, 1, D, D_EXP), exp_map),
                pl.BlockSpec((1, 1, D, D_EXP), exp_map),
                pl.BlockSpec((1, 1, D_EXP, D), exp_map),
            ],
            out_specs=pl.BlockSpec((TM,) + ROW_TILE, tile_map),
            scratch_shapes=[pltpu.VMEM((D, D_EXP), BF16), pltpu.VMEM((D, D_EXP), BF16),
                            pltpu.VMEM((D_EXP, D), BF16)],
        ),
        out_shape=jax.ShapeDtypeStruct(xs.shape, F32),
        compiler_params=_cparams(("arbitrary",)),
        name="moe_grouped",
    )(*plan, xs, w1, w3, w2)


def _gather_kernel(p1_ref, p2_ref, x_ref, mod_ref, tm_ref, fg_ref, ys_ref, o_ref, buf, sem,
                   *, final_norm):
    i = pl.program_id(0)
    n_steps = pl.num_programs(0)
    slot = lax.rem(i, 2)

    def issue(tile, s):
        base = tile * TM

        def body(r, carry):
            pltpu.make_async_copy(ys_ref.at[p1_ref[base + r]], buf.at[s, 0, r], sem.at[s]).start()
            pltpu.make_async_copy(ys_ref.at[p2_ref[base + r]], buf.at[s, 1, r],
                                  sem.at[s]).start(priority=1)
            return carry

        lax.fori_loop(0, TM, body, 0, unroll=ISSUE_UNROLL)

    @pl.when(i == 0)
    def _():
        issue(0, 0)

    @pl.when(i + 1 < n_steps)
    def _():
        issue(i + 1, 1 - slot)

    _row_copy_wait(buf.at[slot, 0], sem.at[slot])
    _row_copy_wait(buf.at[slot, 1], sem.at[slot])
    y = (tm_ref[:, 0:1] * buf[slot, 0].reshape(TM, D)
         + tm_ref[:, 1:2] * buf[slot, 1].reshape(TM, D))
    out = x_ref[...] + mod_ref[0, 5:6, :] * y
    if final_norm:
        out = out * lax.rsqrt(jnp.mean(out * out, axis=-1, keepdims=True) + EPS) * fg_ref[...]
    o_ref[...] = out


def _gather(p1, p2, xa, mod, tmeta, final_g, ys, n_tiles, final_norm):
    return pl.pallas_call(
        functools.partial(_gather_kernel, final_norm=final_norm),
        grid_spec=pltpu.PrefetchScalarGridSpec(
            num_scalar_prefetch=2,
            grid=(n_tiles,),
            in_specs=[
                pl.BlockSpec((TM, D), lambda i, p1, p2: (i, 0)),
                pl.BlockSpec((1, 6, D), lambda i, p1, p2: (_mod_row(i), 0, 0)),
                pl.BlockSpec((TM, LANES), lambda i, p1, p2: (i, 0)),
                pl.BlockSpec((1, D), lambda i, p1, p2: (0, 0)),
                pl.BlockSpec(memory_space=pl.ANY),
            ],
            out_specs=pl.BlockSpec((TM, D), lambda i, p1, p2: (i, 0)),
            scratch_shapes=[pltpu.VMEM((2, 2, TM) + ROW_TILE, F32), pltpu.SemaphoreType.DMA((2,))],
        ),
        out_shape=jax.ShapeDtypeStruct((n_tiles * TM, D), F32),
        compiler_params=_cparams(("arbitrary",)),
        name="moe_gather",
    )(p1, p2, xa, mod, tmeta, final_g, ys)


def _issue_row_gathers(p1_ref, p2_ref, ys_ref, buf, sem, tile, s, unrolled):
    base = tile * TM

    def one(r):
        pltpu.make_async_copy(ys_ref.at[p1_ref[base + r]], buf.at[s, 0, r], sem.at[s]).start()
        pltpu.make_async_copy(ys_ref.at[p2_ref[base + r]], buf.at[s, 1, r],
                              sem.at[s]).start(priority=1)

    if unrolled:
        for r in range(TM):
            one(r)
    else:
        def body(r, carry):
            one(r)
            return carry

        lax.fori_loop(0, TM, body, 0, unroll=ISSUE_UNROLL)


def _inproj_kernel(p1_ref, p2_ref, x_ref, mod0_ref, tm_ref, ys_ref, mod_ref, g_ref, win_ref,
                   xn_ref, xm_ref, op_ref, buf, sem):
    i = pl.program_id(0)
    n_steps = pl.num_programs(0)
    slot = lax.rem(i, 2)

    @pl.when(i == 0)
    def _():
        _issue_row_gathers(p1_ref, p2_ref, ys_ref, buf, sem, 0, 0, False)

    _row_copy_wait(buf.at[slot, 0], sem.at[slot])
    _row_copy_wait(buf.at[slot, 1], sem.at[slot])
    y = (tm_ref[:, 0:1] * buf[slot, 0].reshape(TM, D)
         + tm_ref[:, 1:2] * buf[slot, 1].reshape(TM, D))
    xn = x_ref[...] + mod0_ref[0, 5:6, :] * y
    xn_ref[...] = xn
    _issue_row_gathers(p1_ref, p2_ref, ys_ref, buf, sem, jnp.minimum(i + 1, n_steps - 1), 1 - slot,
                       True)
    h = _norm_mod(xn, g_ref[...], mod_ref[0, 0:1, :], mod_ref[0, 1:2, :])
    hb = h.astype(BF16)
    xm_ref[...] = jnp.dot(hb, win_ref[:, :B_INNER], preferred_element_type=F32).astype(BF16)
    op_ref[...] = jnp.dot(hb, win_ref[:, B_INNER:], preferred_element_type=F32).astype(BF16)

    @pl.when(i == n_steps - 1)
    def _():
        _row_copy_wait(buf.at[1 - slot, 0], sem.at[1 - slot])
        _row_copy_wait(buf.at[1 - slot, 1], sem.at[1 - slot])


def _inproj(p1, p2, xa, mod0, tmeta, ys, mod, g, w_in):
    const = lambda i, p1, p2: (0, 0)
    row = lambda i, p1, p2: (i, 0)
    mod_map = lambda i, p1, p2: (_mod_row(i), 0, 0)
    return pl.pallas_call(
        _inproj_kernel,
        grid_spec=pltpu.PrefetchScalarGridSpec(
            num_scalar_prefetch=2,
            grid=(ALL_TILES,),
            in_specs=[
                pl.BlockSpec((TM, D), row),
                pl.BlockSpec((1, 6, D), mod_map),
                pl.BlockSpec((TM, LANES), row),
                pl.BlockSpec(memory_space=pl.ANY),
                pl.BlockSpec((1, 6, D), mod_map),
                pl.BlockSpec((1, D), const),
                pl.BlockSpec((D, 2 * B_INNER), const),
            ],
            out_specs=[pl.BlockSpec((TM, D), row),
                       pl.BlockSpec((TM, B_INNER), row),
                       pl.BlockSpec((TM, B_INNER), row)],
            scratch_shapes=[pltpu.VMEM((2, 2, TM) + ROW_TILE, F32), pltpu.SemaphoreType.DMA((2,))],
        ),
        out_shape=[jax.ShapeDtypeStruct((N_ALL, D), F32),
                   jax.ShapeDtypeStruct((N_ALL, B_INNER), BF16),
                   jax.ShapeDtypeStruct((N_ALL, B_INNER), BF16)],
        compiler_params=_cparams(("arbitrary",)),
        name="mlstm_inproj",
    )(p1, p2, xa, mod0, tmeta, ys, mod, g, w_in)


def _log_sigmoid(x):
    return jnp.minimum(x, 0.0) - jnp.log(1.0 + jnp.exp(-jnp.abs(x)))


def _qkv_kernel(xm_ref, prev_ref, next_ref, cw_ref, cb_ref, wq_ref, wkt_ref, wv_ref, wg_ref, wgt_ref,
                bg_ref, bgt_ref, xc_ref, q_ref, kt_ref, v_ref, gt_ref, gtt_ref, tot_ref):
    i = pl.program_id(0)
    lat = i < LAT_TILES
    first = jnp.where(lat, i % TILES_PER_SEQ == 0, True)
    last = jnp.where(lat, i % TILES_PER_SEQ == TILES_PER_SEQ - 1, True)
    xmb = xm_ref[...]
    xm = xmb.astype(F32)
    prev = jnp.where(first, 0.0, prev_ref[...].astype(F32))
    nxt = jnp.where(last, 0.0, next_ref[...].astype(F32))
    ext = jnp.concatenate([prev[HALO - 8:], xm, nxt[:8]], axis=0)
    acc = jnp.zeros((TM, B_INNER), F32) + cb_ref[...]
    for t in range(CONV_K):
        off = 8 + t - CONV_K // 2
        acc = acc + ext[off:off + TM] * cw_ref[t:t + 1, :]
    xc = acc * jax.nn.sigmoid(acc)
    xcb = xc.astype(BF16)
    xc_ref[...] = xcb
    for h in range(HEADS):
        hs = slice(h * DH, (h + 1) * DH)
        qh = jnp.dot(xcb[:, hs], wq_ref[h], preferred_element_type=F32)
        q_ref[:, h * DQK:(h + 1) * DQK] = (qh * (DQK ** -0.5)).astype(BF16)
        kt_ref[h * DQK:(h + 1) * DQK, :] = lax.dot_general(
            wkt_ref[h], xcb[:, hs], (((1,), (1,)), ((), ())), preferred_element_type=F32).astype(BF16)
        v_ref[:, hs] = jnp.dot(xmb[:, hs], wv_ref[h], preferred_element_type=F32).astype(BF16)
    gts = jnp.dot(xmb, wg_ref[...], preferred_element_type=F32) + bg_ref[...]
    lane = lax.broadcasted_iota(jnp.int32, (CHUNK, LANES), 1)
    lane_kind = (lane >> 3) & 3
    gtt = _dot_nt(wgt_ref[...], xmb) + bgt_ref[...]
    row = lax.broadcasted_iota(jnp.int32, (4 * HEADS, CHUNK), 0)
    row_kind = (row >> 3) & 3
    r = lax.broadcasted_iota(jnp.int32, (CHUNK, CHUNK), 0)
    c = lax.broadcasted_iota(jnp.int32, (CHUNK, CHUNK), 1)
    lower = (c <= r).astype(BF16)
    upper = (c >= r).astype(BF16)
    row_sums = jnp.concatenate([upper, lower, jnp.ones((CHUNK, LANES), BF16)], axis=1)
    for ch in range(TM // CHUNK):
        ts = slice(ch * CHUNK, (ch + 1) * CHUNK)
        g = gts[ts]
        g = jnp.where((lane_kind & 1) == 1, _log_sigmoid(g), g)
        terms = _bf16_terms(g, 3)
        pre = sum(_dot_nn(lower, t) for t in terms)
        suf = sum(_dot_nn(upper, t) for t in terms)
        gt_ref[ts, :] = jnp.where(lane_kind == 1, pre, jnp.where(lane_kind == 3, suf, g))
        gt = gtt[:, ts]
        gt = jnp.where((row_kind & 1) == 1, _log_sigmoid(gt), gt)
        sums = _dot_exact_rhs(gt, row_sums)
        gtt_ref[:, ts] = jnp.where(row_kind == 1, sums[:, :CHUNK],
                                   jnp.where(row_kind == 3, sums[:, CHUNK:2 * CHUNK], gt))
        tot_ref[:, ts] = sums[:, 2 * CHUNK:]


def _qkv(xm, conv_w, conv_b, wq, wkt, wv, wg, wgt, bg, bgt):
    const = lambda i: (0, 0)
    const3 = lambda i: (0, 0, 0)
    hb = TM // HALO
    n_hb = N_ALL // HALO
    return pl.pallas_call(
        _qkv_kernel,
        grid=(ALL_TILES,),
        in_specs=[
            pl.BlockSpec((TM, B_INNER), lambda i: (i, 0)),
            pl.BlockSpec((HALO, B_INNER), lambda i: (jnp.maximum(i * hb - 1, 0), 0)),
            pl.BlockSpec((HALO, B_INNER), lambda i: (jnp.minimum((i + 1) * hb, n_hb - 1), 0)),
            pl.BlockSpec((CONV_K, B_INNER), const),
            pl.BlockSpec((1, B_INNER), const),
            pl.BlockSpec((HEADS, DH, DQK), const3),
            pl.BlockSpec((HEADS, DQK, DH), const3),
            pl.BlockSpec((HEADS, DH, DV), const3),
            pl.BlockSpec((B_INNER, LANES), const),
            pl.BlockSpec((4 * HEADS, B_INNER), const),
            pl.BlockSpec((1, LANES), const),
            pl.BlockSpec((4 * HEADS, 1), const),
        ],
        out_specs=[
            pl.BlockSpec((TM, B_INNER), lambda i: (i, 0)),
            pl.BlockSpec((TM, HEADS * DQK), lambda i: (i, 0)),
            pl.BlockSpec((HEADS * DQK, TM), lambda i: (0, i)),
            pl.BlockSpec((TM, B_INNER), lambda i: (i, 0)),
            pl.BlockSpec((TM, LANES), lambda i: (i, 0)),
            pl.BlockSpec((4 * HEADS, TM), lambda i: (0, i)),
            pl.BlockSpec((4 * HEADS, TM), lambda i: (0, i)),
        ],
        out_shape=[
            jax.ShapeDtypeStruct((N_ALL, B_INNER), BF16),
            jax.ShapeDtypeStruct((N_ALL, HEADS * DQK), BF16),
            jax.ShapeDtypeStruct((HEADS * DQK, N_ALL), BF16),
            jax.ShapeDtypeStruct((N_ALL, B_INNER), BF16),
            jax.ShapeDtypeStruct((N_ALL, LANES), F32),
            jax.ShapeDtypeStruct((4 * HEADS, N_ALL), F32),
            jax.ShapeDtypeStruct((4 * HEADS, N_ALL), F32),
        ],
        compiler_params=_cparams(("parallel",)),
        name="mlstm_qkv",
    )(xm, xm, xm, conv_w, conv_b, wq, wkt, wv, wg, wgt, bg, bgt)


CTX_CHUNKS = CTX // CHUNK
LAT_CHUNKS = SEQ // CHUNK
SCAN_STEPS = CTX_CHUNKS + LAT_CHUNKS


def _scan_dir(nb, d, q_ref, kt_ref, v_ref, g_ref, gt_ref, tot_ref, o_ref, c_scr, m_scr):
    r = lax.broadcasted_iota(jnp.int32, (CHUNK, CHUNK), 0)
    c = lax.broadcasted_iota(jnp.int32, (CHUNK, CHUNK), 1)
    if d == 0:
        keep = c <= r
    else:
        keep = c >= r
    ones_blk = jnp.ones((CHUNK, LANES), BF16)
    bcol_all = g_ref[...]
    gtt = gt_ref[...]
    tot_all = tot_ref[...]
    base = 2 * HEADS * d

    def head_first(h):
        li_r = gtt[base + h:base + h + 1, :]
        b_r = gtt[base + HEADS + h:base + HEADS + h + 1, :]
        b_last = tot_all[base + HEADS + h:base + HEADS + h + 1, :]
        m_old = m_scr[nb, d, h]
        c_old = c_scr[nb, d, h]
        qh = q_ref[:, h * DQK:(h + 1) * DQK]
        kth = kt_ref[h * DQK:(h + 1) * DQK, :]
        vaug = jnp.concatenate([v_ref[:, h * DV:(h + 1) * DV], ones_blk], axis=1)
        g_r = b_last - b_r + li_r
        m_new = jnp.maximum(b_last + m_old, jnp.max(g_r, axis=-1, keepdims=True))
        decay = jnp.exp(b_last + m_old - m_new)
        wk = jnp.exp(g_r - m_new)
        kwt = (kth.astype(F32) * wk).astype(BF16)
        qk = jnp.dot(qh, kth, preferred_element_type=F32)
        qc = jnp.dot(qh, c_old.astype(BF16), preferred_element_type=F32)
        c_scr[nb, d, h] = jnp.concatenate([decay] * 3, axis=1) * c_old + jnp.dot(
            kwt, vaug, preferred_element_type=F32)
        m_scr[nb, d, h] = m_new
        return h, li_r, b_r, m_old, vaug, qk, qc

    def head_second(h, li_r, b_r, m_old, vaug, qk, qc):
        b_full = jnp.broadcast_to(bcol_all[:, base + HEADS + h:base + HEADS + h + 1], (CHUNK, LANES))
        dmat = jnp.where(keep, b_full - (b_r - li_r), NEG_INF)
        inter = b_full + m_old
        m_t = jnp.maximum(inter, jnp.max(dmat, axis=-1, keepdims=True))
        p = jnp.exp(dmat - m_t)
        s = qk * p
        a = jnp.exp(inter - m_t)
        num = jnp.concatenate([a] * 3, axis=1) * qc + jnp.dot(
            s.astype(BF16), vaug, preferred_element_type=F32)
        inv = 1.0 / jnp.maximum(jnp.abs(num[:, DV:]), jnp.exp(-m_t))
        o_ref[nb, :, h * DV:(h + 1) * DV] = (
            num[:, :DV] * jnp.concatenate([inv] * 2, axis=1)).astype(BF16)

    return head_first, head_second


SCAN_NB = 2
SCAN_IN = 6
SCAN_SKEW = 1


def _scan_kernel(*refs):
    n_in = SCAN_NB * 2 * SCAN_IN
    ins, outs = refs[:n_in], refs[n_in:n_in + 2]
    c_scr, m_scr = refs[n_in + 2:]

    @pl.when(pl.program_id(1) == 0)
    def _():
        c_scr[...] = jnp.zeros_like(c_scr)
        m_scr[...] = jnp.zeros_like(m_scr)

    units = []
    for nb in range(SCAN_NB):
        for d in range(2):
            k = nb * 2 + d
            first, second = _scan_dir(nb, d, *ins[k * SCAN_IN:(k + 1) * SCAN_IN], outs[d], c_scr, m_scr)
            units += [(first, second, h) for h in range(HEADS)]
    pending = []
    for first, second, h in units:
        pending.append((second, first(h)))
        if len(pending) > SCAN_SKEW:
            fn, old_vals = pending.pop(0)
            fn(*old_vals)
    for fn, old_vals in pending:
        fn(*old_vals)


def _scan(q, kt, v, gts, gtt, tot):
    lat_blk = N_LAT // CHUNK

    def fwd_in(b, j):
        return jnp.where(j < CTX_CHUNKS, lat_blk + b * CTX_CHUNKS + j, b * LAT_CHUNKS + j - CTX_CHUNKS)

    def bwd_in(b, j):
        return jnp.where(j < CTX_CHUNKS, lat_blk + b * CTX_CHUNKS + (CTX_CHUNKS - 1 - j),
                         b * LAT_CHUNKS + (SCAN_STEPS - 1 - j))

    def fwd_out(b, j):
        return b * LAT_CHUNKS + jnp.maximum(j - CTX_CHUNKS, 0)

    def bwd_out(b, j):
        return b * LAT_CHUNKS + jnp.minimum(SCAN_STEPS - 1 - j, LAT_CHUNKS - 1)

    n_g = BATCH // SCAN_NB

    def specs(fn, nb):
        blk = lambda g, j: fn(nb * n_g + g, j)
        return [
            pl.BlockSpec((CHUNK, HEADS * DQK), lambda g, j: (blk(g, j), 0)),
            pl.BlockSpec((HEADS * DQK, CHUNK), lambda g, j: (0, blk(g, j))),
            pl.BlockSpec((CHUNK, B_INNER), lambda g, j: (blk(g, j), 0)),
            pl.BlockSpec((CHUNK, LANES), lambda g, j: (blk(g, j), 0)),
            pl.BlockSpec((4 * HEADS, CHUNK), lambda g, j: (0, blk(g, j))),
            pl.BlockSpec((4 * HEADS, CHUNK), lambda g, j: (0, blk(g, j))),
        ]

    def out_spec(fn):
        return pl.BlockSpec((SCAN_NB, CHUNK, B_INNER), lambda g, j: (0, fn(g, j), 0))

    in_specs, operands = [], []
    for nb in range(SCAN_NB):
        for fn_in in (fwd_in, bwd_in):
            in_specs += specs(fn_in, nb)
            operands += [q, kt, v, gts, gtt, tot]
    h_shape = jax.ShapeDtypeStruct((SCAN_NB, N_LAT // SCAN_NB, B_INNER), BF16)
    hf, hb = pl.pallas_call(
        _scan_kernel,
        grid=(n_g, SCAN_STEPS),
        in_specs=in_specs,
        out_specs=[out_spec(fwd_out), out_spec(bwd_out)],
        out_shape=[h_shape, h_shape],
        scratch_shapes=[pltpu.VMEM((SCAN_NB, 2, HEADS, DQK, DV + LANES), F32),
                        pltpu.VMEM((SCAN_NB, 2, HEADS, 1, LANES), F32)],
        compiler_params=_cparams(("parallel", "arbitrary")),
        name="mlstm_scan",
    )(*operands)
    return hf.reshape(N_LAT, B_INNER), hb.reshape(N_LAT, B_INNER)


def _combine_kernel(x_ref, mod_ref, hf_ref, hb_ref, xc_ref, op_ref, hg_ref, sk_ref, wout_ref,
                    g2_ref, wr_ref, br_ref, o_ref, meta_ref, tmeta_ref, cnt_ref, cntr_ref,
                    y_scr, carry, carry_r):
    for h in range(HEADS):
        hs = slice(h * DV, (h + 1) * DV)
        s = hf_ref[:, hs].astype(F32) + hb_ref[:, hs].astype(F32)
        s = s * lax.rsqrt(jnp.mean(s * s, axis=-1, keepdims=True) + EPS)
        y = jax.nn.sigmoid(op_ref[:, hs].astype(F32)) * (
            s * hg_ref[:, hs] + sk_ref[:, hs] * xc_ref[:, hs].astype(F32))
        y_scr[:, hs] = y.astype(BF16)
    out = jnp.dot(y_scr[...], wout_ref[...], preferred_element_type=F32)
    xn = x_ref[...] + mod_ref[0, 2:3, :] * out
    o_ref[...] = xn
    _route_tile(xn, mod_ref, g2_ref, wr_ref, br_ref, meta_ref, tmeta_ref, cnt_ref, cntr_ref,
                carry, carry_r)


def _combine(xa, mod, hf, hb, xc, op, head_g, skip, w_out, g2, wr, br):
    const = lambda i: (0, 0)
    row = lambda i: (i, 0)
    r_in, r_out, r_shape, r_scratch = _router_specs(LAT_TILES)
    return pl.pallas_call(
        _combine_kernel,
        grid=(LAT_TILES,),
        in_specs=[
            pl.BlockSpec((TM, D), row),
            pl.BlockSpec((1, 6, D), lambda i: (_mod_row(i), 0, 0)),
            pl.BlockSpec((TM, B_INNER), row),
            pl.BlockSpec((TM, B_INNER), row),
            pl.BlockSpec((TM, B_INNER), row),
            pl.BlockSpec((TM, B_INNER), row),
            pl.BlockSpec((1, B_INNER), const),
            pl.BlockSpec((1, B_INNER), const),
            pl.BlockSpec((B_INNER, D), const),
        ] + r_in,
        out_specs=[pl.BlockSpec((TM, D), row)] + r_out,
        out_shape=[jax.ShapeDtypeStruct((N_LAT, D), F32)] + r_shape,
        scratch_shapes=[pltpu.VMEM((TM, B_INNER), BF16)] + r_scratch,
        compiler_params=_cparams(("arbitrary",)),
        name="mlstm_combine",
    )(xa, mod, hf, hb, xc, op, head_g, skip, w_out, g2, wr, br)


def _router_weights(w_grp, b_grp, w_exp, b_exp):
    wr = jnp.zeros((LANES, D), F32).at[:N_EXP].set(w_exp.T).at[N_EXP:N_EXP + N_GRP].set(w_grp.T)
    br = jnp.zeros((LANES, 1), F32).at[:N_EXP, 0].set(b_exp).at[N_EXP:N_EXP + N_GRP, 0].set(b_grp)
    return wr.astype(BF16), br


PLAN_BLK = 2048
PLAN_PAIRS = 256


def _plan_kernel(meta_ref, cc_ref, cr_ref, pp_ref, plan_ref):
    row = lax.broadcasted_iota(jnp.int32, (LANES, LANES), 0)
    col = lax.broadcasted_iota(jnp.int32, (LANES, LANES), 1)
    cnt_c = cc_ref[...]
    cnt_r = cr_ref[0:1, :]
    starts_c = jnp.sum(jnp.where(col < row, cnt_r, 0.0), axis=1, keepdims=True)
    erow = lax.broadcasted_iota(jnp.int32, (LANES, PLAN_BLK), 0).astype(F32)
    for j in range(2):
        pos = jnp.sum(jnp.where(erow == meta_ref[j:j + 1, :], starts_c, 0.0), axis=0,
                      keepdims=True) + meta_ref[j + 2:j + 3, :]
        pp_ref[j:j + 1, :] = pos.astype(jnp.int32)
    pp_ref[2:8, :] = jnp.zeros((6, PLAN_BLK), jnp.int32)

    @pl.when(pl.program_id(0) == 0)
    def _():
        ends_c = starts_c + cnt_c
        starts_r = jnp.sum(jnp.where(row < col, cnt_c, 0.0), axis=0, keepdims=True)
        ends_r = starts_r + cnt_r

        def tiles(st, en, cn):
            first_tile = jnp.floor(st * (1.0 / TM))
            n = jnp.where(cn > 0.0, jnp.floor((en - 1.0) * (1.0 / TM)) - first_tile + 1.0, 0.0)
            return first_tile, n

        ft_c, pairs_c = tiles(starts_c, ends_c, cnt_c)
        _, pairs_r = tiles(starts_r, ends_r, cnt_r)
        pend_c = jnp.sum(jnp.where(col <= row, pairs_r, 0.0), axis=1, keepdims=True)
        pstart_c = pend_c - pairs_c
        total = jnp.sum(pairs_r, axis=1, keepdims=True)
        q = lax.broadcasted_iota(jnp.int32, (1, PLAN_PAIRS), 1).astype(F32)
        erow_p = lax.broadcasted_iota(jnp.int32, (LANES, PLAN_PAIRS), 0).astype(F32)

        def at(qv):
            qc = jnp.maximum(jnp.minimum(qv, total - 1.0), 0.0)
            e = jnp.sum(jnp.where(qc >= pend_c, 1.0, 0.0), axis=0, keepdims=True)
            oh = erow_p == e
            tile = jnp.sum(jnp.where(oh, ft_c - pstart_c, 0.0), axis=0, keepdims=True) + qc
            return e, oh, tile

        e_q, oh, tile_q = at(q)
        e_p, _, tile_p = at(q - 1.0)
        lo = jnp.sum(jnp.where(oh, starts_c, 0.0), axis=0, keepdims=True) - tile_q * TM
        hi = jnp.sum(jnp.where(oh, ends_c, 0.0), axis=0, keepdims=True) - tile_q * TM
        rows = (tile_q, e_q, jnp.clip(lo, 0.0, TM), jnp.clip(hi, 0.0, TM),
                jnp.where((q == 0.0) | (tile_q != tile_p), 1.0, 0.0),
                jnp.where((q == 0.0) | (e_q != e_p), 1.0, 0.0),
                jnp.where(q < total, 1.0, 0.0),
                jnp.zeros((1, PLAN_PAIRS), F32))
        for j, v in enumerate(rows):
            plan_ref[j:j + 1, :] = v.astype(jnp.int32)


def _route_plan(meta, cnt_c, cnt_r, n_rows):
    n_pairs = 2 * n_rows // TM + N_EXP - 1
    assert n_pairs <= PLAN_PAIRS and n_rows % PLAN_BLK == 0
    const = lambda i: (0, 0)
    pp, plan = pl.pallas_call(
        _plan_kernel,
        grid=(n_rows // PLAN_BLK,),
        in_specs=[pl.BlockSpec((8, PLAN_BLK), lambda i: (0, i)),
                  pl.BlockSpec((LANES, 1), const),
                  pl.BlockSpec((8, LANES), const)],
        out_specs=[pl.BlockSpec((8, PLAN_BLK), lambda i: (0, i)),
                   pl.BlockSpec((8, PLAN_PAIRS), const)],
        out_shape=[jax.ShapeDtypeStruct((8, n_rows), jnp.int32),
                   jax.ShapeDtypeStruct((8, PLAN_PAIRS), jnp.int32)],
        compiler_params=_cparams(("arbitrary",)),
        name="moe_plan",
    )(meta, cnt_c, cnt_r)
    return pp[0], pp[1], tuple(plan[j, :n_pairs] for j in range(7))


def _moe_experts(xa, routing, mod, g2, w1, w3, w2, layer, n_rows):
    meta, _, cnt_c, cnt_r = routing
    p1, p2, plan = _route_plan(meta, cnt_c, cnt_r, n_rows)
    xs = _scatter(p1, p2, xa, mod, g2, n_rows // TM)
    return p1, p2, _grouped(plan, xs, w1, w3, w2, layer)


def kernel(x, c, ctx, c_ctx, norm_g, w_ada, b_ada, a_w_in, a_b_in, a_g_v, a_w_s, a_b_s, a_w_out,
           b_w_in, b_conv_w, b_conv_b, b_w_q, b_w_k, b_w_v, b_w_gate, b_b_gate, b_head_g, b_skip,
           b_w_out, moe_w_grp, moe_b_grp, moe_w_exp, moe_b_exp, moe_w1, moe_w3, moe_w2, final_g):
    cc = jnp.zeros((MOD_ROWS, D), F32).at[:BATCH].set(c).at[BATCH].set(c_ctx)
    mods = _ada(cc, w_ada, b_ada).reshape(2, MOD_ROWS, 6, D)

    mod = mods[0]
    b_s_full = jnp.repeat(a_b_s[0].T, A_GC, axis=1)
    g2 = norm_g[0, 1].reshape(1, D)
    wr, br = _router_weights(moe_w_grp[0], moe_b_grp[0], moe_w_exp[0], moe_b_exp[0])
    xa, *routing = _gmlp(x.reshape(N_LAT, D), ctx.reshape(N_CTX, D), mod, norm_g[0, 0].reshape(1, D),
                         a_w_in[0].astype(BF16), a_b_in[0].reshape(1, -1), a_g_v[0].reshape(1, -1),
                         a_w_s[0].astype(BF16), b_s_full, a_w_out[0].astype(BF16), g2, wr, br)
    p1, p2, ys = _moe_experts(xa, routing, mod, g2, moe_w1, moe_w3, moe_w2, 0, N_ALL)

    mod = mods[1]
    xa, xm, op = _inproj(p1, p2, xa, mods[0], routing[1], ys, mod, norm_g[1, 0].reshape(1, D),
                         b_w_in[0].astype(BF16))
    wg = jnp.zeros((B_INNER, LANES), F32).at[:, :4 * HEADS].set(b_w_gate[0]).astype(BF16)
    bg = jnp.zeros((1, LANES), F32).at[0, :4 * HEADS].set(b_b_gate[0])
    xc, q, kt, v, gts, gtt, tot = _qkv(xm, b_conv_w[0], b_conv_b[0].reshape(1, -1),
                                  b_w_q[0].astype(BF16),
                                  jnp.transpose(b_w_k[0], (0, 2, 1)).astype(BF16),
                                  b_w_v[0].astype(BF16), wg, b_w_gate[0].T.astype(BF16), bg,
                                  b_b_gate[0].reshape(-1, 1))
    hf, hb = _scan(q, kt, v, gts, gtt, tot)
    g2 = norm_g[1, 1].reshape(1, D)
    wr, br = _router_weights(moe_w_grp[1], moe_b_grp[1], moe_w_exp[1], moe_b_exp[1])
    xl, *routing = _combine(xa, mod, hf, hb, xc, op, b_head_g[0].reshape(1, -1),
                            b_skip[0].reshape(1, -1), b_w_out[0].astype(BF16), g2, wr, br)
    p1, p2, ys = _moe_experts(xl, routing, mod, g2, moe_w1, moe_w3, moe_w2, 1, N_LAT)
    out = _gather(p1, p2, xl, mod, routing[1], final_g.reshape(1, D), ys, LAT_TILES, True)
    return out.reshape(BATCH, SEQ, D)
```

```python
import functools

import jax
import jax.numpy as jnp
from jax import lax
from jax.experimental import pallas as pl
from jax.experimental.pallas import tpu as pltpu

F32 = jnp.float32
BF16 = jnp.bfloat16

D = 1024
BATCH = 8
SEQ = 2048
CTX = 256
EPS = 1e-6
NEG_INF = -1e30
N_LAT = BATCH * SEQ
N_CTX = BATCH * CTX
N_ALL = N_LAT + N_CTX

TM = 256
LAT_TILES = N_LAT // TM
ALL_TILES = N_ALL // TM
TILES_PER_SEQ = SEQ // TM
MOD_ROWS = 16

CHUNK = 128
A_HALF = 2048
A_GROUPS = 8
A_GC = A_HALF // A_GROUPS
B_INNER = 2048
HEADS = 8
DH = B_INNER // HEADS
DQK = DH // 2
DV = DH
CONV_K = 5
HALO = 16
N_EXP = 32
N_GRP = 4
EXP_PER_GRP = 8
D_EXP = 512
LANES = 128

VMEM_LIMIT = 56 * 1024 * 1024


def _cparams(sem):
    return pltpu.CompilerParams(dimension_semantics=sem, vmem_limit_bytes=VMEM_LIMIT)


def _mod_row(i):
    return jnp.where(i < LAT_TILES, i // TILES_PER_SEQ, BATCH)


def _norm_mod(x, g, shift, scale):
    y = x * lax.rsqrt(jnp.mean(x * x, axis=-1, keepdims=True) + EPS) * g
    return y * (1.0 + scale) + shift


def _bf16_terms(x, n):
    terms = []
    r = x
    for _ in range(n):
        t = r.astype(BF16)
        terms.append(t)
        r = r - t.astype(F32)
    return terms


def _dot_nt(a, b):
    return lax.dot_general(a, b, (((1,), (1,)), ((), ())), preferred_element_type=F32)


def _dot_nn(a, b):
    return jnp.dot(a, b, preferred_element_type=F32)


def _dot3(a, b, dot=_dot_nn):
    a1, a2 = _bf16_terms(a, 2)
    b1, b2 = _bf16_terms(b, 2)
    return dot(a1, b1) + (dot(a1, b2) + dot(a2, b1))


def _dot_exact_lhs(a_bf16, b, dot=_dot_nn):
    return sum(dot(a_bf16, t) for t in _bf16_terms(b, 3))


def _dot_exact_rhs(a, b_bf16, dot=_dot_nn):
    return sum(dot(t, b_bf16) for t in _bf16_terms(a, 3))


def _gelu_tanh(x):
    half = 0.5 * x
    t = jnp.tanh(x * (0.7978845608028654 + 0.035677408136300125 * (x * x)))
    return half + half * t


ADA_BN = 1536


def _ada_kernel(c_ref, w_ref, b_ref, o_ref):
    c = c_ref[...]
    a = c * jax.nn.sigmoid(c)
    o_ref[0] = _dot3(a, w_ref[0]) + b_ref[0]


def _ada(cc, w_ada, b_ada):
    depth = w_ada.shape[0]
    return pl.pallas_call(
        _ada_kernel,
        grid=(depth, 6 * D // ADA_BN),
        in_specs=[
            pl.BlockSpec((MOD_ROWS, D), lambda l, j: (0, 0)),
            pl.BlockSpec((1, D, ADA_BN), lambda l, j: (l, 0, j)),
            pl.BlockSpec((1, 1, ADA_BN), lambda l, j: (l, 0, j)),
        ],
        out_specs=pl.BlockSpec((1, MOD_ROWS, ADA_BN), lambda l, j: (l, 0, j)),
        out_shape=jax.ShapeDtypeStruct((depth, MOD_ROWS, 6 * D), F32),
        compiler_params=_cparams(("parallel", "parallel")),
        name="ada",
    )(cc, w_ada, b_ada.reshape(depth, 1, 6 * D))


GM_CH = 512


def _gmlp_kernel(xl_ref, xc_ref, mod_ref, g_ref, win_ref, bin_ref, gv_ref, ws_ref, bs_ref, wout_ref,
                 g2_ref, wr_ref, br_ref, o_ref, meta_ref, tmeta_ref, cnt_ref, cntr_ref,
                 z_scr, y_scr, carry, carry_r):
    x = jnp.where(pl.program_id(0) < LAT_TILES, xl_ref[...], xc_ref[...])
    h = _norm_mod(x, g_ref[...], mod_ref[0, 0:1, :], mod_ref[0, 1:2, :])
    hb = h.astype(BF16)
    s1 = jnp.zeros((TM, 1), F32)
    s2 = jnp.zeros((TM, 1), F32)
    n_ch = 2 * A_HALF // GM_CH
    for j in list(range(n_ch // 2, n_ch)) + list(range(n_ch // 2)):
        cs = slice(j * GM_CH, (j + 1) * GM_CH)
        zc = jnp.dot(hb, win_ref[:, cs], preferred_element_type=F32) + bin_ref[:, cs]
        zc = _gelu_tanh(zc)
        z_scr[:, cs] = zc
        if j * GM_CH >= A_HALF:
            s1 = s1 + jnp.sum(zc, axis=-1, keepdims=True)
            s2 = s2 + jnp.sum(zc * zc, axis=-1, keepdims=True)
    mu = s1 * (1.0 / A_HALF)
    rstd = lax.rsqrt(s2 * (1.0 / A_HALF) - mu * mu + EPS)
    for c in range(TM // CHUNK):
        rs = slice(c * CHUNK, (c + 1) * CHUNK)
        for g in range(A_GROUPS):
            cs = slice(g * A_GC, (g + 1) * A_GC)
            vs = slice(A_HALF + g * A_GC, A_HALF + (g + 1) * A_GC)
            v = (z_scr[rs, vs] - mu[rs]) * rstd[rs] * gv_ref[:, cs]
            s = jnp.dot(ws_ref[g], v.astype(BF16), preferred_element_type=F32) + bs_ref[:, cs]
            y_scr[rs, cs] = (z_scr[rs, cs] * s).astype(BF16)
    out = jnp.dot(y_scr[...], wout_ref[...], preferred_element_type=F32)
    xn = x + mod_ref[0, 2:3, :] * out
    o_ref[...] = xn
    _route_tile(xn, mod_ref, g2_ref, wr_ref, br_ref, meta_ref, tmeta_ref, cnt_ref, cntr_ref,
                carry, carry_r)


def _gmlp(xl, xc, mod, g, w_in, b_in, g_v, w_s, b_s_full, w_out, g2, wr, br):
    const = lambda i: (0, 0)
    r_in, r_out, r_shape, r_scratch = _router_specs(ALL_TILES)
    return pl.pallas_call(
        _gmlp_kernel,
        grid=(ALL_TILES,),
        in_specs=[
            pl.BlockSpec((TM, D), lambda i: (jnp.minimum(i, LAT_TILES - 1), 0)),
            pl.BlockSpec((TM, D), lambda i: (jnp.maximum(i - LAT_TILES, 0), 0)),
            pl.BlockSpec((1, 6, D), lambda i: (_mod_row(i), 0, 0)),
            pl.BlockSpec((1, D), const),
            pl.BlockSpec((D, 2 * A_HALF), const),
            pl.BlockSpec((1, 2 * A_HALF), const),
            pl.BlockSpec((1, A_HALF), const),
            pl.BlockSpec((A_GROUPS, CHUNK, CHUNK), lambda i: (0, 0, 0)),
            pl.BlockSpec((CHUNK, A_HALF), const),
            pl.BlockSpec((A_HALF, D), const),
        ] + r_in,
        out_specs=[pl.BlockSpec((TM, D), lambda i: (i, 0))] + r_out,
        out_shape=[jax.ShapeDtypeStruct((N_ALL, D), F32)] + r_shape,
        scratch_shapes=[pltpu.VMEM((TM, 2 * A_HALF), F32), pltpu.VMEM((TM, A_HALF), BF16)] + r_scratch,
        compiler_params=_cparams(("arbitrary",)),
        name="gmlp",
    )(xl, xc, mod, g, w_in, b_in, g_v, w_s, b_s_full, w_out, g2, wr, br)


def _route_tile(x, mod_ref, g_ref, wr_ref, br_ref, meta_ref, tmeta_ref, cnt_ref, cntr_ref,
                carry, carry_r):
    @pl.when(pl.program_id(0) == 0)
    def _():
        carry[...] = jnp.zeros_like(carry)
        carry_r[...] = jnp.zeros_like(carry_r)

    h = _norm_mod(x, g_ref[...], mod_ref[0, 3:4, :], mod_ref[0, 4:5, :])
    lt = _dot_nt(wr_ref[...], h.astype(BF16)) + br_ref[...]
    e_t = lt[0:N_EXP]
    row8 = lax.broadcasted_iota(jnp.int32, (EXP_PER_GRP, TM), 0).astype(F32)
    g_t = jnp.where(row8 < N_GRP, lt[N_EXP:N_EXP + EXP_PER_GRP], -jnp.inf)
    gmax = jnp.max(g_t, axis=0, keepdims=True)
    p_g = 1.0 / jnp.sum(jnp.exp(g_t - gmax), axis=0, keepdims=True)
    g_idx = jnp.min(jnp.where(g_t == gmax, row8, float(EXP_PER_GRP)), axis=0, keepdims=True)
    sel = jnp.zeros((EXP_PER_GRP, TM), F32)
    for g in range(N_GRP):
        sel = sel + jnp.where(g_idx == g, e_t[g * EXP_PER_GRP:(g + 1) * EXP_PER_GRP], 0.0)
    m1 = jnp.max(sel, axis=0, keepdims=True)
    i1 = jnp.min(jnp.where(sel == m1, row8, float(EXP_PER_GRP)), axis=0, keepdims=True)
    sel2 = jnp.where(row8 == i1, -jnp.inf, sel)
    m2 = jnp.max(sel2, axis=0, keepdims=True)
    i2 = jnp.min(jnp.where(sel2 == m2, row8, float(EXP_PER_GRP)), axis=0, keepdims=True)
    e2 = jnp.exp(m2 - m1)
    w1 = p_g / (1.0 + e2)
    w2 = p_g * e2 / (1.0 + e2)
    row = lax.broadcasted_iota(jnp.int32, (LANES, TM), 0).astype(F32)
    id1 = g_idx * EXP_PER_GRP + i1
    id2 = g_idx * EXP_PER_GRP + i2
    oh1 = row == id1
    oh2 = row == id2
    oh = jnp.where(oh1, 1.0, 0.0) + jnp.where(oh2, 1.0, 0.0)
    before = (lax.broadcasted_iota(jnp.int32, (TM, TM), 0)
              < lax.broadcasted_iota(jnp.int32, (TM, TM), 1)).astype(BF16)
    ohb = oh.astype(BF16)
    tot = jnp.dot(ohb, before, preferred_element_type=F32) + carry[...]
    r1 = jnp.sum(jnp.where(oh1, tot, 0.0), axis=0, keepdims=True)
    r2 = jnp.sum(jnp.where(oh2, tot, 0.0), axis=0, keepdims=True)
    carry[...] += jnp.sum(oh, axis=-1, keepdims=True)
    carry_r[...] += _dot_nt(jnp.ones((8, TM), BF16), ohb)
    cnt_ref[...] = carry[...]
    cntr_ref[...] = carry_r[...]
    meta_ref[0:1, :] = id1
    meta_ref[1:2, :] = id2
    meta_ref[2:3, :] = r1
    meta_ref[3:4, :] = r2
    meta_ref[4:8, :] = jnp.zeros((4, TM), F32)
    tmeta_ref[...] = (jnp.where(row == 0.0, w1, 0.0) + jnp.where(row == 1.0, w2, 0.0)).T


def _router_specs(n_tiles):
    const = lambda i: (0, 0)
    n = n_tiles * TM
    in_specs = [pl.BlockSpec((1, D), const), pl.BlockSpec((LANES, D), const),
                pl.BlockSpec((LANES, 1), const)]
    out_specs = [pl.BlockSpec((8, TM), lambda i: (0, i)),
                 pl.BlockSpec((TM, LANES), lambda i: (i, 0)),
                 pl.BlockSpec((LANES, 1), const),
                 pl.BlockSpec((8, LANES), const)]
    out_shape = [jax.ShapeDtypeStruct((8, n), F32), jax.ShapeDtypeStruct((n, LANES), F32),
                 jax.ShapeDtypeStruct((LANES, 1), F32), jax.ShapeDtypeStruct((8, LANES), F32)]
    scratch = [pltpu.VMEM((LANES, 1), F32), pltpu.VMEM((8, LANES), F32)]
    return in_specs, out_specs, out_shape, scratch


ISSUE_UNROLL = 8
ROW_TILE = (8, LANES)


def _row_copy_wait(buf_slot, sem_slot):
    pltpu.make_async_copy(buf_slot, buf_slot, sem_slot).wait()


def _scatter_kernel(p1_ref, p2_ref, x_ref, mod_ref, g_ref, xs_ref, buf, sem):
    i = pl.program_id(0)
    n_steps = pl.num_programs(0)
    slot = lax.rem(i, 2)

    def wait_slot(s):
        _row_copy_wait(buf.at[s], sem.at[s])
        _row_copy_wait(buf.at[s], sem.at[s])

    @pl.when(i >= 2)
    def _():
        wait_slot(slot)

    h = _norm_mod(x_ref[...], g_ref[...], mod_ref[0, 3:4, :], mod_ref[0, 4:5, :])
    buf[slot] = h.reshape(TM, *ROW_TILE)
    base = i * TM

    def body(r, carry):
        src = buf.at[slot, r]
        pltpu.make_async_copy(src, xs_ref.at[p1_ref[base + r]], sem.at[slot]).start()
        pltpu.make_async_copy(src, xs_ref.at[p2_ref[base + r]], sem.at[slot]).start(priority=1)
        return carry

    lax.fori_loop(0, TM, body, 0, unroll=ISSUE_UNROLL)

    @pl.when(i == n_steps - 1)
    def _():
        @pl.when(i >= 1)
        def _():
            wait_slot(1 - slot)

        wait_slot(slot)


def _scatter(p1, p2, xa, mod, g, n_tiles):
    n = n_tiles * TM
    return pl.pallas_call(
        _scatter_kernel,
        grid_spec=pltpu.PrefetchScalarGridSpec(
            num_scalar_prefetch=2,
            grid=(n_tiles,),
            in_specs=[
                pl.BlockSpec((TM, D), lambda i, p1, p2: (i, 0)),
                pl.BlockSpec((1, 6, D), lambda i, p1, p2: (_mod_row(i), 0, 0)),
                pl.BlockSpec((1, D), lambda i, p1, p2: (0, 0)),
            ],
            out_specs=pl.BlockSpec(memory_space=pl.ANY),
            scratch_shapes=[pltpu.VMEM((2, TM) + ROW_TILE, F32), pltpu.SemaphoreType.DMA((2,))],
        ),
        out_shape=jax.ShapeDtypeStruct((2 * n,) + ROW_TILE, F32),
        compiler_params=_cparams(("arbitrary",)),
        name="moe_scatter",
    )(p1, p2, xa, mod, g)


def _grouped_kernel(tile_ref, exp_ref, lo_ref, hi_ref, first_ref, newexp_ref, valid_ref,
                    xs_ref, w1_ref, w3_ref, w2_ref, ys_ref, wb1, wb3, wb2):
    q = pl.program_id(0)

    @pl.when(valid_ref[q] == 1)
    def _():
        @pl.when(newexp_ref[q] == 1)
        def _():
            wb1[...] = w1_ref[0, 0].astype(BF16)
            wb3[...] = w3_ref[0, 0].astype(BF16)
            wb2[...] = w2_ref[0, 0].astype(BF16)

        x = xs_ref[...].reshape(TM, D).astype(BF16)
        a = jnp.dot(x, wb1[...], preferred_element_type=F32)
        b = jnp.dot(x, wb3[...], preferred_element_type=F32)
        rows = lax.broadcasted_iota(jnp.int32, (TM, 1), 0)
        mine = (rows >= lo_ref[q]) & (rows < hi_ref[q])
        act = jnp.where(mine, a * jax.nn.sigmoid(a) * b, 0.0)
        y = jnp.dot(act.astype(BF16), wb2[...], preferred_element_type=F32).reshape(TM, *ROW_TILE)

        @pl.when(first_ref[q] == 1)
        def _():
            ys_ref[...] = y

        @pl.when(first_ref[q] == 0)
        def _():
            ys_ref[...] += y


def _grouped(plan, xs, w1, w3, w2, layer):
    n_pairs = plan[0].shape[0]
    tile_map = lambda q, tile, exp, *_: (tile[q], 0, 0)
    exp_map = lambda q, tile, exp, *_: (layer, exp[q], 0, 0)
    return pl.pallas_call(
        _grouped_kernel,
        grid_spec=pltpu.PrefetchScalarGridSpec(
            num_scalar_prefetch=7,
            grid=(n_pairs,),
            in_specs=[
                pl.BlockSpec((TM,) + ROW_TILE, tile_map),
                pl.BlockSpec((1, 1, D, D_EXP), exp_map),
                pl.BlockSpec((1, 1, D, D_EXP), exp_map),
                pl.BlockSpec((1, 1, D_EXP, D), exp_map),
            ],
            out_specs=pl.BlockSpec((TM,) + ROW_TILE, tile_map),
            scratch_shapes=[pltpu.VMEM((D, D_EXP), BF16), pltpu.VMEM((D, D_EXP), BF16),
                            pltpu.VMEM((D_EXP, D), BF16)],
        ),
        out_shape=jax.ShapeDtypeStruct(xs.shape, F32),
        compiler_params=_cparams(("arbitrary",)),
        name="moe_grouped",
    )(*plan, xs, w1, w3, w2)


def _gather_kernel(p1_ref, p2_ref, x_ref, mod_ref, tm_ref, fg_ref, ys_ref, o_ref, buf, sem,
                   *, final_norm):
    i = pl.program_id(0)
    n_steps = pl.num_programs(0)
    slot = lax.rem(i, 2)

    def issue(tile, s):
        base = tile * TM

        def body(r, carry):
            pltpu.make_async_copy(ys_ref.at[p1_ref[base + r]], buf.at[s, 0, r], sem.at[s]).start()
            pltpu.make_async_copy(ys_ref.at[p2_ref[base + r]], buf.at[s, 1, r],
                                  sem.at[s]).start(priority=1)
            return carry

        lax.fori_loop(0, TM, body, 0, unroll=ISSUE_UNROLL)

    @pl.when(i == 0)
    def _():
        issue(0, 0)

    @pl.when(i + 1 < n_steps)
    def _():
        issue(i + 1, 1 - slot)

    _row_copy_wait(buf.at[slot, 0], sem.at[slot])
    _row_copy_wait(buf.at[slot, 1], sem.at[slot])
    y = (tm_ref[:, 0:1] * buf[slot, 0].reshape(TM, D)
         + tm_ref[:, 1:2] * buf[slot, 1].reshape(TM, D))
    out = x_ref[...] + mod_ref[0, 5:6, :] * y
    if final_norm:
        out = out * lax.rsqrt(jnp.mean(out * out, axis=-1, keepdims=True) + EPS) * fg_ref[...]
    o_ref[...] = out


def _gather(p1, p2, xa, mod, tmeta, final_g, ys, n_tiles, final_norm):
    return pl.pallas_call(
        functools.partial(_gather_kernel, final_norm=final_norm),
        grid_spec=pltpu.PrefetchScalarGridSpec(
            num_scalar_prefetch=2,
            grid=(n_tiles,),
            in_specs=[
                pl.BlockSpec((TM, D), lambda i, p1, p2: (i, 0)),
                pl.BlockSpec((1, 6, D), lambda i, p1, p2: (_mod_row(i), 0, 0)),
                pl.BlockSpec((TM, LANES), lambda i, p1, p2: (i, 0)),
                pl.BlockSpec((1, D), lambda i, p1, p2: (0, 0)),
                pl.BlockSpec(memory_space=pl.ANY),
            ],
            out_specs=pl.BlockSpec((TM, D), lambda i, p1, p2: (i, 0)),
            scratch_shapes=[pltpu.VMEM((2, 2, TM) + ROW_TILE, F32), pltpu.SemaphoreType.DMA((2,))],
        ),
        out_shape=jax.ShapeDtypeStruct((n_tiles * TM, D), F32),
        compiler_params=_cparams(("arbitrary",)),
        name="moe_gather",
    )(p1, p2, xa, mod, tmeta, final_g, ys)


def _issue_row_gathers(p1_ref, p2_ref, ys_ref, buf, sem, tile, s, unrolled):
    base = tile * TM

    def one(r):
        pltpu.make_async_copy(ys_ref.at[p1_ref[base + r]], buf.at[s, 0, r], sem.at[s]).start()
        pltpu.make_async_copy(ys_ref.at[p2_ref[base + r]], buf.at[s, 1, r],
                              sem.at[s]).start(priority=1)

    if unrolled:
        for r in range(TM):
            one(r)
    else:
        def body(r, carry):
            one(r)
            return carry

        lax.fori_loop(0, TM, body, 0, unroll=ISSUE_UNROLL)


def _inproj_kernel(p1_ref, p2_ref, x_ref, mod0_ref, tm_ref, ys_ref, mod_ref, g_ref, win_ref,
                   xn_ref, xm_ref, op_ref, buf, sem):
    i = pl.program_id(0)
    n_steps = pl.num_programs(0)
    slot = lax.rem(i, 2)

    @pl.when(i == 0)
    def _():
        _issue_row_gathers(p1_ref, p2_ref, ys_ref, buf, sem, 0, 0, False)

    _row_copy_wait(buf.at[slot, 0], sem.at[slot])
    _row_copy_wait(buf.at[slot, 1], sem.at[slot])
    y = (tm_ref[:, 0:1] * buf[slot, 0].reshape(TM, D)
         + tm_ref[:, 1:2] * buf[slot, 1].reshape(TM, D))
    xn = x_ref[...] + mod0_ref[0, 5:6, :] * y
    xn_ref[...] = xn
    _issue_row_gathers(p1_ref, p2_ref, ys_ref, buf, sem, jnp.minimum(i + 1, n_steps - 1), 1 - slot,
                       True)
    h = _norm_mod(xn, g_ref[...], mod_ref[0, 0:1, :], mod_ref[0, 1:2, :])
    hb = h.astype(BF16)
    xm_ref[...] = jnp.dot(hb, win_ref[:, :B_INNER], preferred_element_type=F32).astype(BF16)
    op_ref[...] = jnp.dot(hb, win_ref[:, B_INNER:], preferred_element_type=F32).astype(BF16)

    @pl.when(i == n_steps - 1)
    def _():
        _row_copy_wait(buf.at[1 - slot, 0], sem.at[1 - slot])
        _row_copy_wait(buf.at[1 - slot, 1], sem.at[1 - slot])


def _inproj(p1, p2, xa, mod0, tmeta, ys, mod, g, w_in):
    const = lambda i, p1, p2: (0, 0)
    row = lambda i, p1, p2: (i, 0)
    mod_map = lambda i, p1, p2: (_mod_row(i), 0, 0)
    return pl.pallas_call(
        _inproj_kernel,
        grid_spec=pltpu.PrefetchScalarGridSpec(
            num_scalar_prefetch=2,
            grid=(ALL_TILES,),
            in_specs=[
                pl.BlockSpec((TM, D), row),
                pl.BlockSpec((1, 6, D), mod_map),
                pl.BlockSpec((TM, LANES), row),
                pl.BlockSpec(memory_space=pl.ANY),
                pl.BlockSpec((1, 6, D), mod_map),
                pl.BlockSpec((1, D), const),
                pl.BlockSpec((D, 2 * B_INNER), const),
            ],
            out_specs=[pl.BlockSpec((TM, D), row),
                       pl.BlockSpec((TM, B_INNER), row),
                       pl.BlockSpec((TM, B_INNER), row)],
            scratch_shapes=[pltpu.VMEM((2, 2, TM) + ROW_TILE, F32), pltpu.SemaphoreType.DMA((2,))],
        ),
        out_shape=[jax.ShapeDtypeStruct((N_ALL, D), F32),
                   jax.ShapeDtypeStruct((N_ALL, B_INNER), BF16),
                   jax.ShapeDtypeStruct((N_ALL, B_INNER), BF16)],
        compiler_params=_cparams(("arbitrary",)),
        name="mlstm_inproj",
    )(p1, p2, xa, mod0, tmeta, ys, mod, g, w_in)


def _log_sigmoid(x):
    return jnp.minimum(x, 0.0) - jnp.log(1.0 + jnp.exp(-jnp.abs(x)))


def _qkv_kernel(xm_ref, prev_ref, next_ref, cw_ref, cb_ref, wq_ref, wkt_ref, wv_ref, wg_ref, wgt_ref,
                bg_ref, bgt_ref, xc_ref, q_ref, kt_ref, v_ref, gt_ref, gtt_ref, tot_ref):
    i = pl.program_id(0)
    lat = i < LAT_TILES
    first = jnp.where(lat, i % TILES_PER_SEQ == 0, True)
    last = jnp.where(lat, i % TILES_PER_SEQ == TILES_PER_SEQ - 1, True)
    xmb = xm_ref[...]
    xm = xmb.astype(F32)
    prev = jnp.where(first, 0.0, prev_ref[...].astype(F32))
    nxt = jnp.where(last, 0.0, next_ref[...].astype(F32))
    ext = jnp.concatenate([prev[HALO - 8:], xm, nxt[:8]], axis=0)
    acc = jnp.zeros((TM, B_INNER), F32) + cb_ref[...]
    for t in range(CONV_K):
        off = 8 + t - CONV_K // 2
        acc = acc + ext[off:off + TM] * cw_ref[t:t + 1, :]
    xc = acc * jax.nn.sigmoid(acc)
    xcb = xc.astype(BF16)
    xc_ref[...] = xcb
    for h in range(HEADS):
        hs = slice(h * DH, (h + 1) * DH)
        qh = jnp.dot(xcb[:, hs], wq_ref[h], preferred_element_type=F32)
        q_ref[:, h * DQK:(h + 1) * DQK] = (qh * (DQK ** -0.5)).astype(BF16)
        kt_ref[h * DQK:(h + 1) * DQK, :] = lax.dot_general(
            wkt_ref[h], xcb[:, hs], (((1,), (1,)), ((), ())), preferred_element_type=F32).astype(BF16)
        v_ref[:, hs] = jnp.dot(xmb[:, hs], wv_ref[h], preferred_element_type=F32).astype(BF16)
    gts = jnp.dot(xmb, wg_ref[...], preferred_element_type=F32) + bg_ref[...]
    lane = lax.broadcasted_iota(jnp.int32, (CHUNK, LANES), 1)
    lane_kind = (lane >> 3) & 3
    gtt = _dot_nt(wgt_ref[...], xmb) + bgt_ref[...]
    row = lax.broadcasted_iota(jnp.int32, (4 * HEADS, CHUNK), 0)
    row_kind = (row >> 3) & 3
    r = lax.broadcasted_iota(jnp.int32, (CHUNK, CHUNK), 0)
    c = lax.broadcasted_iota(jnp.int32, (CHUNK, CHUNK), 1)
    lower = (c <= r).astype(BF16)
    upper = (c >= r).astype(BF16)
    row_sums = jnp.concatenate([upper, lower, jnp.ones((CHUNK, LANES), BF16)], axis=1)
    for ch in range(TM // CHUNK):
        ts = slice(ch * CHUNK, (ch + 1) * CHUNK)
        g = gts[ts]
        g = jnp.where((lane_kind & 1) == 1, _log_sigmoid(g), g)
        terms = _bf16_terms(g, 3)
        pre = sum(_dot_nn(lower, t) for t in terms)
        suf = sum(_dot_nn(upper, t) for t in terms)
        gt_ref[ts, :] = jnp.where(lane_kind == 1, pre, jnp.where(lane_kind == 3, suf, g))
        gt = gtt[:, ts]
        gt = jnp.where((row_kind & 1) == 1, _log_sigmoid(gt), gt)
        sums = _dot_exact_rhs(gt, row_sums)
        gtt_ref[:, ts] = jnp.where(row_kind == 1, sums[:, :CHUNK],
                                   jnp.where(row_kind == 3, sums[:, CHUNK:2 * CHUNK], gt))
        tot_ref[:, ts] = sums[:, 2 * CHUNK:]


def _qkv(xm, conv_w, conv_b, wq, wkt, wv, wg, wgt, bg, bgt):
    const = lambda i: (0, 0)
    const3 = lambda i: (0, 0, 0)
    hb = TM // HALO
    n_hb = N_ALL // HALO
    return pl.pallas_call(
        _qkv_kernel,
        grid=(ALL_TILES,),
        in_specs=[
            pl.BlockSpec((TM, B_INNER), lambda i: (i, 0)),
            pl.BlockSpec((HALO, B_INNER), lambda i: (jnp.maximum(i * hb - 1, 0), 0)),
            pl.BlockSpec((HALO, B_INNER), lambda i: (jnp.minimum((i + 1) * hb, n_hb - 1), 0)),
            pl.BlockSpec((CONV_K, B_INNER), const),
            pl.BlockSpec((1, B_INNER), const),
            pl.BlockSpec((HEADS, DH, DQK), const3),
            pl.BlockSpec((HEADS, DQK, DH), const3),
            pl.BlockSpec((HEADS, DH, DV), const3),
            pl.BlockSpec((B_INNER, LANES), const),
            pl.BlockSpec((4 * HEADS, B_INNER), const),
            pl.BlockSpec((1, LANES), const),
            pl.BlockSpec((4 * HEADS, 1), const),
        ],
        out_specs=[
            pl.BlockSpec((TM, B_INNER), lambda i: (i, 0)),
            pl.BlockSpec((TM, HEADS * DQK), lambda i: (i, 0)),
            pl.BlockSpec((HEADS * DQK, TM), lambda i: (0, i)),
            pl.BlockSpec((TM, B_INNER), lambda i: (i, 0)),
            pl.BlockSpec((TM, LANES), lambda i: (i, 0)),
            pl.BlockSpec((4 * HEADS, TM), lambda i: (0, i)),
            pl.BlockSpec((4 * HEADS, TM), lambda i: (0, i)),
        ],
        out_shape=[
            jax.ShapeDtypeStruct((N_ALL, B_INNER), BF16),
            jax.ShapeDtypeStruct((N_ALL, HEADS * DQK), BF16),
            jax.ShapeDtypeStruct((HEADS * DQK, N_ALL), BF16),
            jax.ShapeDtypeStruct((N_ALL, B_INNER), BF16),
            jax.ShapeDtypeStruct((N_ALL, LANES), F32),
            jax.ShapeDtypeStruct((4 * HEADS, N_ALL), F32),
            jax.ShapeDtypeStruct((4 * HEADS, N_ALL), F32),
        ],
        compiler_params=_cparams(("parallel",)),
        name="mlstm_qkv",
    )(xm, xm, xm, conv_w, conv_b, wq, wkt, wv, wg, wgt, bg, bgt)


CTX_CHUNKS = CTX // CHUNK
LAT_CHUNKS = SEQ // CHUNK
SCAN_STEPS = CTX_CHUNKS + LAT_CHUNKS


def _scan_dir(nb, d, q_ref, kt_ref, v_ref, g_ref, gt_ref, tot_ref, o_ref, c_scr, m_scr):
    r = lax.broadcasted_iota(jnp.int32, (CHUNK, CHUNK), 0)
    c = lax.broadcasted_iota(jnp.int32, (CHUNK, CHUNK), 1)
    if d == 0:
        keep = c <= r
    else:
        keep = c >= r
    ones_blk = jnp.ones((CHUNK, LANES), BF16)
    bcol_all = g_ref[...]
    gtt = gt_ref[...]
    tot_all = tot_ref[...]
    base = 2 * HEADS * d

    def head_first(h):
        li_r = gtt[base + h:base + h + 1, :]
        b_r = gtt[base + HEADS + h:base + HEADS + h + 1, :]
        b_last = tot_all[base + HEADS + h:base + HEADS + h + 1, :]
        m_old = m_scr[nb, d, h]
        c_old = c_scr[nb, d, h]
        qh = q_ref[:, h * DQK:(h + 1) * DQK]
        kth = kt_ref[h * DQK:(h + 1) * DQK, :]
        vaug = jnp.concatenate([v_ref[:, h * DV:(h + 1) * DV], ones_blk], axis=1)
        g_r = b_last - b_r + li_r
        m_new = jnp.maximum(b_last + m_old, jnp.max(g_r, axis=-1, keepdims=True))
        decay = jnp.exp(b_last + m_old - m_new)
        wk = jnp.exp(g_r - m_new)
        kwt = (kth.astype(F32) * wk).astype(BF16)
        qk = jnp.dot(qh, kth, preferred_element_type=F32)
        qc = jnp.dot(qh, c_old.astype(BF16), preferred_element_type=F32)
        c_scr[nb, d, h] = jnp.concatenate([decay] * 3, axis=1) * c_old + jnp.dot(
            kwt, vaug, preferred_element_type=F32)
        m_scr[nb, d, h] = m_new
        return h, li_r, b_r, m_old, vaug, qk, qc

    def head_second(h, li_r, b_r, m_old, vaug, qk, qc):
        b_full = jnp.broadcast_to(bcol_all[:, base + HEADS + h:base + HEADS + h + 1], (CHUNK, LANES))
        dmat = jnp.where(keep, b_full - (b_r - li_r), NEG_INF)
        inter = b_full + m_old
        m_t = jnp.maximum(inter, jnp.max(dmat, axis=-1, keepdims=True))
        p = jnp.exp(dmat - m_t)
        s = qk * p
        a = jnp.exp(inter - m_t)
        num = jnp.concatenate([a] * 3, axis=1) * qc + jnp.dot(
            s.astype(BF16), vaug, preferred_element_type=F32)
        inv = 1.0 / jnp.maximum(jnp.abs(num[:, DV:]), jnp.exp(-m_t))
        o_ref[nb, :, h * DV:(h + 1) * DV] = (
            num[:, :DV] * jnp.concatenate([inv] * 2, axis=1)).astype(BF16)

    return head_first, head_second


SCAN_NB = 2
SCAN_IN = 6
SCAN_SKEW = 1


def _scan_kernel(*refs):
    n_in = SCAN_NB * 2 * SCAN_IN
    ins, outs = refs[:n_in], refs[n_in:n_in + 2]
    c_scr, m_scr = refs[n_in + 2:]

    @pl.when(pl.program_id(1) == 0)
    def _():
        c_scr[...] = jnp.zeros_like(c_scr)
        m_scr[...] = jnp.zeros_like(m_scr)

    units = []
    for nb in range(SCAN_NB):
        for d in range(2):
            k = nb * 2 + d
            first, second = _scan_dir(nb, d, *ins[k * SCAN_IN:(k + 1) * SCAN_IN], outs[d], c_scr, m_scr)
            units += [(first, second, h) for h in range(HEADS)]
    pending = []
    for first, second, h in units:
        pending.append((second, first(h)))
        if len(pending) > SCAN_SKEW:
            fn, old_vals = pending.pop(0)
            fn(*old_vals)
    for fn, old_vals in pending:
        fn(*old_vals)


def _scan(q, kt, v, gts, gtt, tot):
    lat_blk = N_LAT // CHUNK

    def fwd_in(b, j):
        return jnp.where(j < CTX_CHUNKS, lat_blk + b * CTX_CHUNKS + j, b * LAT_CHUNKS + j - CTX_CHUNKS)

    def bwd_in(b, j):
        return jnp.where(j < CTX_CHUNKS, lat_blk + b * CTX_CHUNKS + (CTX_CHUNKS - 1 - j),
                         b * LAT_CHUNKS + (SCAN_STEPS - 1 - j))

    def fwd_out(b, j):
        return b * LAT_CHUNKS + jnp.maximum(j - CTX_CHUNKS, 0)

    def bwd_out(b, j):
        return b * LAT_CHUNKS + jnp.minimum(SCAN_STEPS - 1 - j, LAT_CHUNKS - 1)

    n_g = BATCH // SCAN_NB

    def specs(fn, nb):
        blk = lambda g, j: fn(nb * n_g + g, j)
        return [
            pl.BlockSpec((CHUNK, HEADS * DQK), lambda g, j: (blk(g, j), 0)),
            pl.BlockSpec((HEADS * DQK, CHUNK), lambda g, j: (0, blk(g, j))),
            pl.BlockSpec((CHUNK, B_INNER), lambda g, j: (blk(g, j), 0)),
            pl.BlockSpec((CHUNK, LANES), lambda g, j: (blk(g, j), 0)),
            pl.BlockSpec((4 * HEADS, CHUNK), lambda g, j: (0, blk(g, j))),
            pl.BlockSpec((4 * HEADS, CHUNK), lambda g, j: (0, blk(g, j))),
        ]

    def out_spec(fn):
        return pl.BlockSpec((SCAN_NB, CHUNK, B_INNER), lambda g, j: (0, fn(g, j), 0))

    in_specs, operands = [], []
    for nb in range(SCAN_NB):
        for fn_in in (fwd_in, bwd_in):
            in_specs += specs(fn_in, nb)
            operands += [q, kt, v, gts, gtt, tot]
    h_shape = jax.ShapeDtypeStruct((SCAN_NB, N_LAT // SCAN_NB, B_INNER), BF16)
    hf, hb = pl.pallas_call(
        _scan_kernel,
        grid=(n_g, SCAN_STEPS),
        in_specs=in_specs,
        out_specs=[out_spec(fwd_out), out_spec(bwd_out)],
        out_shape=[h_shape, h_shape],
        scratch_shapes=[pltpu.VMEM((SCAN_NB, 2, HEADS, DQK, DV + LANES), F32),
                        pltpu.VMEM((SCAN_NB, 2, HEADS, 1, LANES), F32)],
        compiler_params=_cparams(("parallel", "arbitrary")),
        name="mlstm_scan",
    )(*operands)
    return hf.reshape(N_LAT, B_INNER), hb.reshape(N_LAT, B_INNER)


def _combine_kernel(x_ref, mod_ref, hf_ref, hb_ref, xc_ref, op_ref, hg_ref, sk_ref, wout_ref,
                    g2_ref, wr_ref, br_ref, o_ref, meta_ref, tmeta_ref, cnt_ref, cntr_ref,
                    y_scr, carry, carry_r):
    for h in range(HEADS):
        hs = slice(h * DV, (h + 1) * DV)
        s = hf_ref[:, hs].astype(F32) + hb_ref[:, hs].astype(F32)
        s = s * lax.rsqrt(jnp.mean(s * s, axis=-1, keepdims=True) + EPS)
        y = jax.nn.sigmoid(op_ref[:, hs].astype(F32)) * (
            s * hg_ref[:, hs] + sk_ref[:, hs] * xc_ref[:, hs].astype(F32))
        y_scr[:, hs] = y.astype(BF16)
        if h % 2 == 1:
            ks = slice((h - 1) * DV, (h + 1) * DV)
            part = jnp.dot(y_scr[:, ks], wout_ref[ks, :], preferred_element_type=F32)
            out = part if h == 1 else out + part
    xn = x_ref[...] + mod_ref[0, 2:3, :] * out
    o_ref[...] = xn
    _route_tile(xn, mod_ref, g2_ref, wr_ref, br_ref, meta_ref, tmeta_ref, cnt_ref, cntr_ref,
                carry, carry_r)


def _combine(xa, mod, hf, hb, xc, op, head_g, skip, w_out, g2, wr, br):
    const = lambda i: (0, 0)
    row = lambda i: (i, 0)
    r_in, r_out, r_shape, r_scratch = _router_specs(LAT_TILES)
    return pl.pallas_call(
        _combine_kernel,
        grid=(LAT_TILES,),
        in_specs=[
            pl.BlockSpec((TM, D), row),
            pl.BlockSpec((1, 6, D), lambda i: (_mod_row(i), 0, 0)),
            pl.BlockSpec((TM, B_INNER), row),
            pl.BlockSpec((TM, B_INNER), row),
            pl.BlockSpec((TM, B_INNER), row),
            pl.BlockSpec((TM, B_INNER), row),
            pl.BlockSpec((1, B_INNER), const),
            pl.BlockSpec((1, B_INNER), const),
            pl.BlockSpec((B_INNER, D), const),
        ] + r_in,
        out_specs=[pl.BlockSpec((TM, D), row)] + r_out,
        out_shape=[jax.ShapeDtypeStruct((N_LAT, D), F32)] + r_shape,
        scratch_shapes=[pltpu.VMEM((TM, B_INNER), BF16)] + r_scratch,
        compiler_params=_cparams(("arbitrary",)),
        name="mlstm_combine",
    )(xa, mod, hf, hb, xc, op, head_g, skip, w_out, g2, wr, br)


def _router_weights(w_grp, b_grp, w_exp, b_exp):
    wr = jnp.zeros((LANES, D), F32).at[:N_EXP].set(w_exp.T).at[N_EXP:N_EXP + N_GRP].set(w_grp.T)
    br = jnp.zeros((LANES, 1), F32).at[:N_EXP, 0].set(b_exp).at[N_EXP:N_EXP + N_GRP, 0].set(b_grp)
    return wr.astype(BF16), br


PLAN_BLK = 2048
PLAN_PAIRS = 256


def _plan_kernel(meta_ref, cc_ref, cr_ref, pp_ref, plan_ref):
    row = lax.broadcasted_iota(jnp.int32, (LANES, LANES), 0)
    col = lax.broadcasted_iota(jnp.int32, (LANES, LANES), 1)
    cnt_c = cc_ref[...]
    cnt_r = cr_ref[0:1, :]
    starts_c = jnp.sum(jnp.where(col < row, cnt_r, 0.0), axis=1, keepdims=True)
    erow = lax.broadcasted_iota(jnp.int32, (LANES, PLAN_BLK), 0).astype(F32)
    for j in range(2):
        pos = jnp.sum(jnp.where(erow == meta_ref[j:j + 1, :], starts_c, 0.0), axis=0,
                      keepdims=True) + meta_ref[j + 2:j + 3, :]
        pp_ref[j:j + 1, :] = pos.astype(jnp.int32)
    pp_ref[2:8, :] = jnp.zeros((6, PLAN_BLK), jnp.int32)

    @pl.when(pl.program_id(0) == 0)
    def _():
        ends_c = starts_c + cnt_c
        starts_r = jnp.sum(jnp.where(row < col, cnt_c, 0.0), axis=0, keepdims=True)
        ends_r = starts_r + cnt_r

        def tiles(st, en, cn):
            first_tile = jnp.floor(st * (1.0 / TM))
            n = jnp.where(cn > 0.0, jnp.floor((en - 1.0) * (1.0 / TM)) - first_tile + 1.0, 0.0)
            return first_tile, n

        ft_c, pairs_c = tiles(starts_c, ends_c, cnt_c)
        _, pairs_r = tiles(starts_r, ends_r, cnt_r)
        pend_c = jnp.sum(jnp.where(col <= row, pairs_r, 0.0), axis=1, keepdims=True)
        pstart_c = pend_c - pairs_c
        total = jnp.sum(pairs_r, axis=1, keepdims=True)
        q = lax.broadcasted_iota(jnp.int32, (1, PLAN_PAIRS), 1).astype(F32)
        erow_p = lax.broadcasted_iota(jnp.int32, (LANES, PLAN_PAIRS), 0).astype(F32)

        def at(qv):
            qc = jnp.maximum(jnp.minimum(qv, total - 1.0), 0.0)
            e = jnp.sum(jnp.where(qc >= pend_c, 1.0, 0.0), axis=0, keepdims=True)
            oh = erow_p == e
            tile = jnp.sum(jnp.where(oh, ft_c - pstart_c, 0.0), axis=0, keepdims=True) + qc
            return e, oh, tile

        e_q, oh, tile_q = at(q)
        e_p, _, tile_p = at(q - 1.0)
        lo = jnp.sum(jnp.where(oh, starts_c, 0.0), axis=0, keepdims=True) - tile_q * TM
        hi = jnp.sum(jnp.where(oh, ends_c, 0.0), axis=0, keepdims=True) - tile_q * TM
        rows = (tile_q, e_q, jnp.clip(lo, 0.0, TM), jnp.clip(hi, 0.0, TM),
                jnp.where((q == 0.0) | (tile_q != tile_p), 1.0, 0.0),
                jnp.where((q == 0.0) | (e_q != e_p), 1.0, 0.0),
                jnp.where(q < total, 1.0, 0.0),
                jnp.zeros((1, PLAN_PAIRS), F32))
        for j, v in enumerate(rows):
            plan_ref[j:j + 1, :] = v.astype(jnp.int32)


def _route_plan(meta, cnt_c, cnt_r, n_rows):
    n_pairs = 2 * n_rows // TM + N_EXP - 1
    assert n_pairs <= PLAN_PAIRS and n_rows % PLAN_BLK == 0
    const = lambda i: (0, 0)
    pp, plan = pl.pallas_call(
        _plan_kernel,
        grid=(n_rows // PLAN_BLK,),
        in_specs=[pl.BlockSpec((8, PLAN_BLK), lambda i: (0, i)),
                  pl.BlockSpec((LANES, 1), const),
                  pl.BlockSpec((8, LANES), const)],
        out_specs=[pl.BlockSpec((8, PLAN_BLK), lambda i: (0, i)),
                   pl.BlockSpec((8, PLAN_PAIRS), const)],
        out_shape=[jax.ShapeDtypeStruct((8, n_rows), jnp.int32),
                   jax.ShapeDtypeStruct((8, PLAN_PAIRS), jnp.int32)],
        compiler_params=_cparams(("arbitrary",)),
        name="moe_plan",
    )(meta, cnt_c, cnt_r)
    return pp[0], pp[1], tuple(plan[j, :n_pairs] for j in range(7))


def _moe_experts(xa, routing, mod, g2, w1, w3, w2, layer, n_rows):
    meta, _, cnt_c, cnt_r = routing
    p1, p2, plan = _route_plan(meta, cnt_c, cnt_r, n_rows)
    xs = _scatter(p1, p2, xa, mod, g2, n_rows // TM)
    return p1, p2, _grouped(plan, xs, w1, w3, w2, layer)


def kernel(x, c, ctx, c_ctx, norm_g, w_ada, b_ada, a_w_in, a_b_in, a_g_v, a_w_s, a_b_s, a_w_out,
           b_w_in, b_conv_w, b_conv_b, b_w_q, b_w_k, b_w_v, b_w_gate, b_b_gate, b_head_g, b_skip,
           b_w_out, moe_w_grp, moe_b_grp, moe_w_exp, moe_b_exp, moe_w1, moe_w3, moe_w2, final_g):
    cc = jnp.zeros((MOD_ROWS, D), F32).at[:BATCH].set(c).at[BATCH].set(c_ctx)
    mods = _ada(cc, w_ada, b_ada).reshape(2, MOD_ROWS, 6, D)

    mod = mods[0]
    b_s_full = jnp.repeat(a_b_s[0].T, A_GC, axis=1)
    g2 = norm_g[0, 1].reshape(1, D)
    wr, br = _router_weights(moe_w_grp[0], moe_b_grp[0], moe_w_exp[0], moe_b_exp[0])
    xa, *routing = _gmlp(x.reshape(N_LAT, D), ctx.reshape(N_CTX, D), mod, norm_g[0, 0].reshape(1, D),
                         a_w_in[0].astype(BF16), a_b_in[0].reshape(1, -1), a_g_v[0].reshape(1, -1),
                         a_w_s[0].astype(BF16), b_s_full, a_w_out[0].astype(BF16), g2, wr, br)
    p1, p2, ys = _moe_experts(xa, routing, mod, g2, moe_w1, moe_w3, moe_w2, 0, N_ALL)

    mod = mods[1]
    xa, xm, op = _inproj(p1, p2, xa, mods[0], routing[1], ys, mod, norm_g[1, 0].reshape(1, D),
                         b_w_in[0].astype(BF16))
    wg = jnp.zeros((B_INNER, LANES), F32).at[:, :4 * HEADS].set(b_w_gate[0]).astype(BF16)
    bg = jnp.zeros((1, LANES), F32).at[0, :4 * HEADS].set(b_b_gate[0])
    xc, q, kt, v, gts, gtt, tot = _qkv(xm, b_conv_w[0], b_conv_b[0].reshape(1, -1),
                                  b_w_q[0].astype(BF16),
                                  jnp.transpose(b_w_k[0], (0, 2, 1)).astype(BF16),
                                  b_w_v[0].astype(BF16), wg, b_w_gate[0].T.astype(BF16), bg,
                                  b_b_gate[0].reshape(-1, 1))
    hf, hb = _scan(q, kt, v, gts, gtt, tot)
    g2 = norm_g[1, 1].reshape(1, D)
    wr, br = _router_weights(moe_w_grp[1], moe_b_grp[1], moe_w_exp[1], moe_b_exp[1])
    xl, *routing = _combine(xa, mod, hf, hb, xc, op, b_head_g[0].reshape(1, -1),
                            b_skip[0].reshape(1, -1), b_w_out[0].astype(BF16), g2, wr, br)
    p1, p2, ys = _moe_experts(xl, routing, mod, g2, moe_w1, moe_w3, moe_w2, 1, N_LAT)
    out = _gather(p1, p2, xl, mod, routing[1], final_g.reshape(1, D), ys, LAT_TILES, True)
    return out.reshape(BATCH, SEQ, D)
```

```python
import functools

import jax
import jax.numpy as jnp
from jax import lax
from jax.experimental import pallas as pl
from jax.experimental.pallas import tpu as pltpu

F32 = jnp.float32
BF16 = jnp.bfloat16

D = 1024
BATCH = 8
SEQ = 2048
CTX = 256
EPS = 1e-6
NEG_INF = -1e30
N_LAT = BATCH * SEQ
N_CTX = BATCH * CTX
N_ALL = N_LAT + N_CTX

TM = 256
LAT_TILES = N_LAT // TM
ALL_TILES = N_ALL // TM
TILES_PER_SEQ = SEQ // TM
MOD_ROWS = 16

CHUNK = 128
A_HALF = 2048
A_GROUPS = 8
A_GC = A_HALF // A_GROUPS
B_INNER = 2048
HEADS = 8
DH = B_INNER // HEADS
DQK = DH // 2
DV = DH
CONV_K = 5
HALO = 16
N_EXP = 32
N_GRP = 4
EXP_PER_GRP = 8
D_EXP = 512
LANES = 128

VMEM_LIMIT = 56 * 1024 * 1024


def _cparams(sem):
    return pltpu.CompilerParams(dimension_semantics=sem, vmem_limit_bytes=VMEM_LIMIT)


def _mod_row(i):
    return jnp.where(i < LAT_TILES, i // TILES_PER_SEQ, BATCH)


def _norm_mod(x, g, shift, scale):
    y = x * lax.rsqrt(jnp.mean(x * x, axis=-1, keepdims=True) + EPS) * g
    return y * (1.0 + scale) + shift


def _bf16_terms(x, n):
    terms = []
    r = x
    for _ in range(n):
        t = r.astype(BF16)
        terms.append(t)
        r = r - t.astype(F32)
    return terms


def _dot_nt(a, b):
    return lax.dot_general(a, b, (((1,), (1,)), ((), ())), preferred_element_type=F32)


def _dot_nn(a, b):
    return jnp.dot(a, b, preferred_element_type=F32)


def _dot3(a, b, dot=_dot_nn):
    a1, a2 = _bf16_terms(a, 2)
    b1, b2 = _bf16_terms(b, 2)
    return dot(a1, b1) + (dot(a1, b2) + dot(a2, b1))


def _dot_exact_lhs(a_bf16, b, dot=_dot_nn):
    return sum(dot(a_bf16, t) for t in _bf16_terms(b, 3))


def _dot_exact_rhs(a, b_bf16, dot=_dot_nn):
    return sum(dot(t, b_bf16) for t in _bf16_terms(a, 3))


def _gelu_tanh(x):
    half = 0.5 * x
    t = jnp.tanh(x * (0.7978845608028654 + 0.035677408136300125 * (x * x)))
    return half + half * t


ADA_BN = 1536


def _ada_kernel(c_ref, w_ref, b_ref, o_ref):
    c = c_ref[...]
    a = c * jax.nn.sigmoid(c)
    o_ref[0] = _dot3(a, w_ref[0]) + b_ref[0]


def _ada(cc, w_ada, b_ada):
    depth = w_ada.shape[0]
    return pl.pallas_call(
        _ada_kernel,
        grid=(depth, 6 * D // ADA_BN),
        in_specs=[
            pl.BlockSpec((MOD_ROWS, D), lambda l, j: (0, 0)),
            pl.BlockSpec((1, D, ADA_BN), lambda l, j: (l, 0, j)),
            pl.BlockSpec((1, 1, ADA_BN), lambda l, j: (l, 0, j)),
        ],
        out_specs=pl.BlockSpec((1, MOD_ROWS, ADA_BN), lambda l, j: (l, 0, j)),
        out_shape=jax.ShapeDtypeStruct((depth, MOD_ROWS, 6 * D), F32),
        compiler_params=_cparams(("parallel", "parallel")),
        name="ada",
    )(cc, w_ada, b_ada.reshape(depth, 1, 6 * D))


GM_CH = 512


def _gmlp_kernel(xl_ref, xc_ref, mod_ref, g_ref, win_ref, bin_ref, gv_ref, ws_ref, bs_ref, wout_ref,
                 g2_ref, wr_ref, br_ref, o_ref, meta_ref, tmeta_ref, cnt_ref, cntr_ref,
                 z_scr, y_scr, carry, carry_r):
    x = jnp.where(pl.program_id(0) < LAT_TILES, xl_ref[...], xc_ref[...])
    h = _norm_mod(x, g_ref[...], mod_ref[0, 0:1, :], mod_ref[0, 1:2, :])
    hb = h.astype(BF16)
    s1 = jnp.zeros((TM, 1), F32)
    s2 = jnp.zeros((TM, 1), F32)
    n_ch = 2 * A_HALF // GM_CH
    for j in list(range(n_ch // 2, n_ch)) + list(range(n_ch // 2)):
        cs = slice(j * GM_CH, (j + 1) * GM_CH)
        zc = jnp.dot(hb, win_ref[:, cs], preferred_element_type=F32) + bin_ref[:, cs]
        zc = _gelu_tanh(zc)
        z_scr[:, cs] = zc
        if j * GM_CH >= A_HALF:
            s1 = s1 + jnp.sum(zc, axis=-1, keepdims=True)
            s2 = s2 + jnp.sum(zc * zc, axis=-1, keepdims=True)
    mu = s1 * (1.0 / A_HALF)
    rstd = lax.rsqrt(s2 * (1.0 / A_HALF) - mu * mu + EPS)
    for c in range(TM // CHUNK):
        rs = slice(c * CHUNK, (c + 1) * CHUNK)
        for g in range(A_GROUPS):
            cs = slice(g * A_GC, (g + 1) * A_GC)
            vs = slice(A_HALF + g * A_GC, A_HALF + (g + 1) * A_GC)
            v = (z_scr[rs, vs] - mu[rs]) * rstd[rs] * gv_ref[:, cs]
            s = jnp.dot(ws_ref[g], v.astype(BF16), preferred_element_type=F32) + bs_ref[:, cs]
            y_scr[rs, cs] = (z_scr[rs, cs] * s).astype(BF16)
    out = jnp.dot(y_scr[...], wout_ref[...], preferred_element_type=F32)
    xn = x + mod_ref[0, 2:3, :] * out
    o_ref[...] = xn
    _route_tile(xn, mod_ref, g2_ref, wr_ref, br_ref, meta_ref, tmeta_ref, cnt_ref, cntr_ref,
                carry, carry_r)


def _gmlp(xl, xc, mod, g, w_in, b_in, g_v, w_s, b_s_full, w_out, g2, wr, br):
    const = lambda i: (0, 0)
    r_in, r_out, r_shape, r_scratch = _router_specs(ALL_TILES)
    return pl.pallas_call(
        _gmlp_kernel,
        grid=(ALL_TILES,),
        in_specs=[
            pl.BlockSpec((TM, D), lambda i: (jnp.minimum(i, LAT_TILES - 1), 0)),
            pl.BlockSpec((TM, D), lambda i: (jnp.maximum(i - LAT_TILES, 0), 0)),
            pl.BlockSpec((1, 6, D), lambda i: (_mod_row(i), 0, 0)),
            pl.BlockSpec((1, D), const),
            pl.BlockSpec((D, 2 * A_HALF), const),
            pl.BlockSpec((1, 2 * A_HALF), const),
            pl.BlockSpec((1, A_HALF), const),
            pl.BlockSpec((A_GROUPS, CHUNK, CHUNK), lambda i: (0, 0, 0)),
            pl.BlockSpec((CHUNK, A_HALF), const),
            pl.BlockSpec((A_HALF, D), const),
        ] + r_in,
        out_specs=[pl.BlockSpec((TM, D), lambda i: (i, 0))] + r_out,
        out_shape=[jax.ShapeDtypeStruct((N_ALL, D), F32)] + r_shape,
        scratch_shapes=[pltpu.VMEM((TM, 2 * A_HALF), F32), pltpu.VMEM((TM, A_HALF), BF16)] + r_scratch,
        compiler_params=_cparams(("arbitrary",)),
        name="gmlp",
    )(xl, xc, mod, g, w_in, b_in, g_v, w_s, b_s_full, w_out, g2, wr, br)


def _route_tile(x, mod_ref, g_ref, wr_ref, br_ref, meta_ref, tmeta_ref, cnt_ref, cntr_ref,
                carry, carry_r):
    @pl.when(pl.program_id(0) == 0)
    def _():
        carry[...] = jnp.zeros_like(carry)
        carry_r[...] = jnp.zeros_like(carry_r)

    h = _norm_mod(x, g_ref[...], mod_ref[0, 3:4, :], mod_ref[0, 4:5, :])
    lt = _dot_nt(wr_ref[...], h.astype(BF16)) + br_ref[...]
    e_t = lt[0:N_EXP]
    row8 = lax.broadcasted_iota(jnp.int32, (EXP_PER_GRP, TM), 0).astype(F32)
    g_t = jnp.where(row8 < N_GRP, lt[N_EXP:N_EXP + EXP_PER_GRP], -jnp.inf)
    gmax = jnp.max(g_t, axis=0, keepdims=True)
    p_g = 1.0 / jnp.sum(jnp.exp(g_t - gmax), axis=0, keepdims=True)
    g_idx = jnp.min(jnp.where(g_t == gmax, row8, float(EXP_PER_GRP)), axis=0, keepdims=True)
    sel = jnp.zeros((EXP_PER_GRP, TM), F32)
    for g in range(N_GRP):
        sel = sel + jnp.where(g_idx == g, e_t[g * EXP_PER_GRP:(g + 1) * EXP_PER_GRP], 0.0)
    m1 = jnp.max(sel, axis=0, keepdims=True)
    i1 = jnp.min(jnp.where(sel == m1, row8, float(EXP_PER_GRP)), axis=0, keepdims=True)
    sel2 = jnp.where(row8 == i1, -jnp.inf, sel)
    m2 = jnp.max(sel2, axis=0, keepdims=True)
    i2 = jnp.min(jnp.where(sel2 == m2, row8, float(EXP_PER_GRP)), axis=0, keepdims=True)
    e2 = jnp.exp(m2 - m1)
    w1 = p_g / (1.0 + e2)
    w2 = p_g * e2 / (1.0 + e2)
    row = lax.broadcasted_iota(jnp.int32, (LANES, TM), 0).astype(F32)
    id1 = g_idx * EXP_PER_GRP + i1
    id2 = g_idx * EXP_PER_GRP + i2
    oh1 = row == id1
    oh2 = row == id2
    oh = jnp.where(oh1, 1.0, 0.0) + jnp.where(oh2, 1.0, 0.0)
    before = (lax.broadcasted_iota(jnp.int32, (TM, TM), 0)
              < lax.broadcasted_iota(jnp.int32, (TM, TM), 1)).astype(BF16)
    ohb = oh.astype(BF16)
    tot = jnp.dot(ohb, before, preferred_element_type=F32) + carry[...]
    r1 = jnp.sum(jnp.where(oh1, tot, 0.0), axis=0, keepdims=True)
    r2 = jnp.sum(jnp.where(oh2, tot, 0.0), axis=0, keepdims=True)
    carry[...] += jnp.sum(oh, axis=-1, keepdims=True)
    carry_r[...] += _dot_nt(jnp.ones((8, TM), BF16), ohb)
    cnt_ref[...] = carry[...]
    cntr_ref[...] = carry_r[...]
    meta_ref[0:1, :] = id1
    meta_ref[1:2, :] = id2
    meta_ref[2:3, :] = r1
    meta_ref[3:4, :] = r2
    meta_ref[4:8, :] = jnp.zeros((4, TM), F32)
    tmeta_ref[...] = (jnp.where(row == 0.0, w1, 0.0) + jnp.where(row == 1.0, w2, 0.0)).T


def _router_specs(n_tiles):
    const = lambda i: (0, 0)
    n = n_tiles * TM
    in_specs = [pl.BlockSpec((1, D), const), pl.BlockSpec((LANES, D), const),
                pl.BlockSpec((LANES, 1), const)]
    out_specs = [pl.BlockSpec((8, TM), lambda i: (0, i)),
                 pl.BlockSpec((TM, LANES), lambda i: (i, 0)),
                 pl.BlockSpec((LANES, 1), const),
                 pl.BlockSpec((8, LANES), const)]
    out_shape = [jax.ShapeDtypeStruct((8, n), F32), jax.ShapeDtypeStruct((n, LANES), F32),
                 jax.ShapeDtypeStruct((LANES, 1), F32), jax.ShapeDtypeStruct((8, LANES), F32)]
    scratch = [pltpu.VMEM((LANES, 1), F32), pltpu.VMEM((8, LANES), F32)]
    return in_specs, out_specs, out_shape, scratch


ISSUE_UNROLL = 8
ROW_TILE = (8, LANES)


def _row_copy_wait(buf_slot, sem_slot):
    pltpu.make_async_copy(buf_slot, buf_slot, sem_slot).wait()


def _scatter_kernel(p1_ref, p2_ref, x_ref, mod_ref, g_ref, xs_ref, buf, sem):
    i = pl.program_id(0)
    n_steps = pl.num_programs(0)
    slot = lax.rem(i, 2)

    def wait_slot(s):
        _row_copy_wait(buf.at[s], sem.at[s])
        _row_copy_wait(buf.at[s], sem.at[s])

    @pl.when(i >= 2)
    def _():
        wait_slot(slot)

    h = _norm_mod(x_ref[...], g_ref[...], mod_ref[0, 3:4, :], mod_ref[0, 4:5, :])
    buf[slot] = h.reshape(TM, *ROW_TILE)
    base = i * TM

    def body(r, carry):
        src = buf.at[slot, r]
        pltpu.make_async_copy(src, xs_ref.at[p1_ref[base + r]], sem.at[slot]).start()
        pltpu.make_async_copy(src, xs_ref.at[p2_ref[base + r]], sem.at[slot]).start(priority=1)
        return carry

    lax.fori_loop(0, TM, body, 0, unroll=ISSUE_UNROLL)

    @pl.when(i == n_steps - 1)
    def _():
        @pl.when(i >= 1)
        def _():
            wait_slot(1 - slot)

        wait_slot(slot)


def _scatter(p1, p2, xa, mod, g, n_tiles):
    n = n_tiles * TM
    return pl.pallas_call(
        _scatter_kernel,
        grid_spec=pltpu.PrefetchScalarGridSpec(
            num_scalar_prefetch=2,
            grid=(n_tiles,),
            in_specs=[
                pl.BlockSpec((TM, D), lambda i, p1, p2: (i, 0)),
                pl.BlockSpec((1, 6, D), lambda i, p1, p2: (_mod_row(i), 0, 0)),
                pl.BlockSpec((1, D), lambda i, p1, p2: (0, 0)),
            ],
            out_specs=pl.BlockSpec(memory_space=pl.ANY),
            scratch_shapes=[pltpu.VMEM((2, TM) + ROW_TILE, F32), pltpu.SemaphoreType.DMA((2,))],
        ),
        out_shape=jax.ShapeDtypeStruct((2 * n,) + ROW_TILE, F32),
        compiler_params=_cparams(("arbitrary",)),
        name="moe_scatter",
    )(p1, p2, xa, mod, g)


def _grouped_kernel(tile_ref, exp_ref, lo_ref, hi_ref, first_ref, newexp_ref, valid_ref,
                    xs_ref, w1_ref, w3_ref, w2_ref, ys_ref, wb1, wb3, wb2):
    q = pl.program_id(0)

    @pl.when(valid_ref[q] == 1)
    def _():
        @pl.when(newexp_ref[q] == 1)
        def _():
            wb1[...] = w1_ref[0, 0].astype(BF16)
            wb3[...] = w3_ref[0, 0].astype(BF16)
            wb2[...] = w2_ref[0, 0].astype(BF16)

        x = xs_ref[...].reshape(TM, D).astype(BF16)
        a = jnp.dot(x, wb1[...], preferred_element_type=F32)
        b = jnp.dot(x, wb3[...], preferred_element_type=F32)
        rows = lax.broadcasted_iota(jnp.int32, (TM, 1), 0)
        mine = (rows >= lo_ref[q]) & (rows < hi_ref[q])
        act = jnp.where(mine, a * jax.nn.sigmoid(a) * b, 0.0)
        y = jnp.dot(act.astype(BF16), wb2[...], preferred_element_type=F32).reshape(TM, *ROW_TILE)

        @pl.when(first_ref[q] == 1)
        def _():
            ys_ref[...] = y

        @pl.when(first_ref[q] == 0)
        def _():
            ys_ref[...] += y


def _grouped(plan, xs, w1, w3, w2, layer):
    n_pairs = plan[0].shape[0]
    tile_map = lambda q, tile, exp, *_: (tile[q], 0, 0)
    exp_map = lambda q, tile, exp, *_: (layer, exp[q], 0, 0)
    return pl.pallas_call(
        _grouped_kernel,
        grid_spec=pltpu.PrefetchScalarGridSpec(
            num_scalar_prefetch=7,
            grid=(n_pairs,),
            in_specs=[
                pl.BlockSpec((TM,) + ROW_TILE, tile_map),
                pl.BlockSpec((1, 1, D, D_EXP), exp_map),
                pl.BlockSpec((1, 1, D, D_EXP), exp_map),
                pl.BlockSpec((1, 1, D_EXP, D), exp_map),
            ],
            out_specs=pl.BlockSpec((TM,) + ROW_TILE, tile_map),
            scratch_shapes=[pltpu.VMEM((D, D_EXP), BF16), pltpu.VMEM((D, D_EXP), BF16),
                            pltpu.VMEM((D_EXP, D), BF16)],
        ),
        out_shape=jax.ShapeDtypeStruct(xs.shape, F32),
        compiler_params=_cparams(("arbitrary",)),
        name="moe_grouped",
    )(*plan, xs, w1, w3, w2)


def _gather_kernel(p1_ref, p2_ref, x_ref, mod_ref, tm_ref, fg_ref, ys_ref, o_ref, buf, sem,
                   *, final_norm):
    i = pl.program_id(0)
    n_steps = pl.num_programs(0)
    slot = lax.rem(i, 2)

    def issue(tile, s):
        base = tile * TM

        def body(r, carry):
            pltpu.make_async_copy(ys_ref.at[p1_ref[base + r]], buf.at[s, 0, r], sem.at[s]).start()
            pltpu.make_async_copy(ys_ref.at[p2_ref[base + r]], buf.at[s, 1, r],
                                  sem.at[s]).start(priority=1)
            return carry

        lax.fori_loop(0, TM, body, 0, unroll=ISSUE_UNROLL)

    @pl.when(i == 0)
    def _():
        issue(0, 0)

    @pl.when(i + 1 < n_steps)
    def _():
        issue(i + 1, 1 - slot)

    _row_copy_wait(buf.at[slot, 0], sem.at[slot])
    _row_copy_wait(buf.at[slot, 1], sem.at[slot])
    y = (tm_ref[:, 0:1] * buf[slot, 0].reshape(TM, D)
         + tm_ref[:, 1:2] * buf[slot, 1].reshape(TM, D))
    out = x_ref[...] + mod_ref[0, 5:6, :] * y
    if final_norm:
        out = out * lax.rsqrt(jnp.mean(out * out, axis=-1, keepdims=True) + EPS) * fg_ref[...]
    o_ref[...] = out


def _gather(p1, p2, xa, mod, tmeta, final_g, ys, n_tiles, final_norm):
    return pl.pallas_call(
        functools.partial(_gather_kernel, final_norm=final_norm),
        grid_spec=pltpu.PrefetchScalarGridSpec(
            num_scalar_prefetch=2,
            grid=(n_tiles,),
            in_specs=[
                pl.BlockSpec((TM, D), lambda i, p1, p2: (i, 0)),
                pl.BlockSpec((1, 6, D), lambda i, p1, p2: (_mod_row(i), 0, 0)),
                pl.BlockSpec((TM, LANES), lambda i, p1, p2: (i, 0)),
                pl.BlockSpec((1, D), lambda i, p1, p2: (0, 0)),
                pl.BlockSpec(memory_space=pl.ANY),
            ],
            out_specs=pl.BlockSpec((TM, D), lambda i, p1, p2: (i, 0)),
            scratch_shapes=[pltpu.VMEM((2, 2, TM) + ROW_TILE, F32), pltpu.SemaphoreType.DMA((2,))],
        ),
        out_shape=jax.ShapeDtypeStruct((n_tiles * TM, D), F32),
        compiler_params=_cparams(("arbitrary",)),
        name="moe_gather",
    )(p1, p2, xa, mod, tmeta, final_g, ys)


def _issue_row_gathers(p1_ref, p2_ref, ys_ref, buf, sem, tile, s, unrolled):
    base = tile * TM

    def one(r):
        pltpu.make_async_copy(ys_ref.at[p1_ref[base + r]], buf.at[s, 0, r], sem.at[s]).start()
        pltpu.make_async_copy(ys_ref.at[p2_ref[base + r]], buf.at[s, 1, r],
                              sem.at[s]).start(priority=1)

    if unrolled:
        for r in range(TM):
            one(r)
    else:
        def body(r, carry):
            one(r)
            return carry

        lax.fori_loop(0, TM, body, 0, unroll=ISSUE_UNROLL)


def _inproj_kernel(p1_ref, p2_ref, x_ref, mod0_ref, tm_ref, ys_ref, mod_ref, g_ref, win_ref,
                   xn_ref, xm_ref, op_ref, buf, sem):
    i = pl.program_id(0)
    n_steps = pl.num_programs(0)
    slot = lax.rem(i, 2)

    @pl.when(i == 0)
    def _():
        _issue_row_gathers(p1_ref, p2_ref, ys_ref, buf, sem, 0, 0, False)

    _row_copy_wait(buf.at[slot, 0], sem.at[slot])
    _row_copy_wait(buf.at[slot, 1], sem.at[slot])
    y = (tm_ref[:, 0:1] * buf[slot, 0].reshape(TM, D)
         + tm_ref[:, 1:2] * buf[slot, 1].reshape(TM, D))
    xn = x_ref[...] + mod0_ref[0, 5:6, :] * y
    xn_ref[...] = xn
    _issue_row_gathers(p1_ref, p2_ref, ys_ref, buf, sem, jnp.minimum(i + 1, n_steps - 1), 1 - slot,
                       True)
    h = _norm_mod(xn, g_ref[...], mod_ref[0, 0:1, :], mod_ref[0, 1:2, :])
    hb = h.astype(BF16)
    xm_ref[...] = jnp.dot(hb, win_ref[:, :B_INNER], preferred_element_type=F32).astype(BF16)
    op_ref[...] = jnp.dot(hb, win_ref[:, B_INNER:], preferred_element_type=F32).astype(BF16)

    @pl.when(i == n_steps - 1)
    def _():
        _row_copy_wait(buf.at[1 - slot, 0], sem.at[1 - slot])
        _row_copy_wait(buf.at[1 - slot, 1], sem.at[1 - slot])


def _inproj(p1, p2, xa, mod0, tmeta, ys, mod, g, w_in):
    const = lambda i, p1, p2: (0, 0)
    row = lambda i, p1, p2: (i, 0)
    mod_map = lambda i, p1, p2: (_mod_row(i), 0, 0)
    return pl.pallas_call(
        _inproj_kernel,
        grid_spec=pltpu.PrefetchScalarGridSpec(
            num_scalar_prefetch=2,
            grid=(ALL_TILES,),
            in_specs=[
                pl.BlockSpec((TM, D), row),
                pl.BlockSpec((1, 6, D), mod_map),
                pl.BlockSpec((TM, LANES), row),
                pl.BlockSpec(memory_space=pl.ANY),
                pl.BlockSpec((1, 6, D), mod_map),
                pl.BlockSpec((1, D), const),
                pl.BlockSpec((D, 2 * B_INNER), const),
            ],
            out_specs=[pl.BlockSpec((TM, D), row),
                       pl.BlockSpec((TM, B_INNER), row),
                       pl.BlockSpec((TM, B_INNER), row)],
            scratch_shapes=[pltpu.VMEM((2, 2, TM) + ROW_TILE, F32), pltpu.SemaphoreType.DMA((2,))],
        ),
        out_shape=[jax.ShapeDtypeStruct((N_ALL, D), F32),
                   jax.ShapeDtypeStruct((N_ALL, B_INNER), BF16),
                   jax.ShapeDtypeStruct((N_ALL, B_INNER), BF16)],
        compiler_params=_cparams(("arbitrary",)),
        name="mlstm_inproj",
    )(p1, p2, xa, mod0, tmeta, ys, mod, g, w_in)


def _log_sigmoid(x):
    return jnp.minimum(x, 0.0) - jnp.log(1.0 + jnp.exp(-jnp.abs(x)))


def _qkv_kernel(xm_ref, prev_ref, next_ref, cw_ref, cb_ref, wq_ref, wkt_ref, wv_ref, wg_ref, wgt_ref,
                bg_ref, bgt_ref, xc_ref, q_ref, kt_ref, v_ref, gt_ref, gtt_ref, tot_ref):
    i = pl.program_id(0)
    lat = i < LAT_TILES
    first = jnp.where(lat, i % TILES_PER_SEQ == 0, True)
    last = jnp.where(lat, i % TILES_PER_SEQ == TILES_PER_SEQ - 1, True)
    xmb = xm_ref[...]
    prev = jnp.where(first, 0.0, prev_ref[HALO - 8:, :].astype(F32))
    nxt = jnp.where(last, 0.0, next_ref[0:8, :].astype(F32))
    half = CONV_K // 2

    rr = lax.broadcasted_iota(jnp.int32, (TM, TM), 0)
    cc = lax.broadcasted_iota(jnp.int32, (TM, TM), 1)
    shifts = {t: (cc == rr + (t - half)).astype(BF16) for t in range(CONV_K) if t != half}

    def conv_head(h):
        hs = slice(h * DH, (h + 1) * DH)
        xh = xmb[:, hs]

        def conv_rows(ext):
            out = jnp.zeros((8, DH), F32) + cb_ref[:, hs]
            for t in range(CONV_K):
                out = out + ext[8 + t - half:16 + t - half] * cw_ref[t:t + 1, hs]
            return out

        acc = xh.astype(F32) * cw_ref[half:half + 1, hs] + cb_ref[:, hs]
        for t, shift in shifts.items():
            acc = acc + jnp.dot(shift, xh, preferred_element_type=F32) * cw_ref[t:t + 1, hs]
        top = conv_rows(jnp.concatenate([prev[:, hs], xh[0:16].astype(F32)], axis=0))
        bot = conv_rows(jnp.concatenate([xh[TM - 16:].astype(F32), nxt[:, hs]], axis=0))
        acc = jnp.concatenate([top, acc[8:TM - 8], bot], axis=0)
        xcb = (acc * jax.nn.sigmoid(acc)).astype(BF16)
        xc_ref[:, hs] = xcb
        return xcb

    def project(h, xcb):
        qh = jnp.dot(xcb, wq_ref[h], preferred_element_type=F32)
        q_ref[:, h * DQK:(h + 1) * DQK] = (qh * (DQK ** -0.5)).astype(BF16)
        kt_ref[h * DQK:(h + 1) * DQK, :] = _dot_nt(wkt_ref[h], xcb).astype(BF16)

    gts = jnp.dot(xmb, wg_ref[...], preferred_element_type=F32) + bg_ref[...]
    gtt = _dot_nt(wgt_ref[...], xmb) + bgt_ref[...]

    pending = None
    for h in range(HEADS):
        hs = slice(h * DH, (h + 1) * DH)
        xcb = conv_head(h)
        v_ref[:, hs] = jnp.dot(xmb[:, hs], wv_ref[h], preferred_element_type=F32).astype(BF16)
        if pending is not None:
            project(*pending)
        pending = (h, xcb)
    project(*pending)
    lane = lax.broadcasted_iota(jnp.int32, (CHUNK, LANES), 1)
    lane_kind = (lane >> 3) & 3
    row = lax.broadcasted_iota(jnp.int32, (4 * HEADS, CHUNK), 0)
    row_kind = (row >> 3) & 3
    r = lax.broadcasted_iota(jnp.int32, (CHUNK, CHUNK), 0)
    c = lax.broadcasted_iota(jnp.int32, (CHUNK, CHUNK), 1)
    lower = (c <= r).astype(BF16)
    upper = (c >= r).astype(BF16)
    row_sums = jnp.concatenate([upper, lower, jnp.ones((CHUNK, LANES), BF16)], axis=1)
    for ch in range(TM // CHUNK):
        ts = slice(ch * CHUNK, (ch + 1) * CHUNK)
        g = gts[ts]
        g = jnp.where((lane_kind & 1) == 1, _log_sigmoid(g), g)
        terms = _bf16_terms(g, 3)
        pre = sum(_dot_nn(lower, t) for t in terms)
        suf = sum(_dot_nn(upper, t) for t in terms)
        gt_ref[ts, :] = jnp.where(lane_kind == 1, pre, jnp.where(lane_kind == 3, suf, g))
        gt = gtt[:, ts]
        gt = jnp.where((row_kind & 1) == 1, _log_sigmoid(gt), gt)
        sums = _dot_exact_rhs(gt, row_sums)
        gtt_ref[:, ts] = jnp.where(row_kind == 1, sums[:, :CHUNK],
                                   jnp.where(row_kind == 3, sums[:, CHUNK:2 * CHUNK], gt))
        tot_ref[:, ts] = sums[:, 2 * CHUNK:]


def _qkv(xm, conv_w, conv_b, wq, wkt, wv, wg, wgt, bg, bgt):
    const = lambda i: (0, 0)
    const3 = lambda i: (0, 0, 0)
    hb = TM // HALO
    n_hb = N_ALL // HALO
    return pl.pallas_call(
        _qkv_kernel,
        grid=(ALL_TILES,),
        in_specs=[
            pl.BlockSpec((TM, B_INNER), lambda i: (i, 0)),
            pl.BlockSpec((HALO, B_INNER), lambda i: (jnp.maximum(i * hb - 1, 0), 0)),
            pl.BlockSpec((HALO, B_INNER), lambda i: (jnp.minimum((i + 1) * hb, n_hb - 1), 0)),
            pl.BlockSpec((CONV_K, B_INNER), const),
            pl.BlockSpec((1, B_INNER), const),
            pl.BlockSpec((HEADS, DH, DQK), const3),
            pl.BlockSpec((HEADS, DQK, DH), const3),
            pl.BlockSpec((HEADS, DH, DV), const3),
            pl.BlockSpec((B_INNER, LANES), const),
            pl.BlockSpec((4 * HEADS, B_INNER), const),
            pl.BlockSpec((1, LANES), const),
            pl.BlockSpec((4 * HEADS, 1), const),
        ],
        out_specs=[
            pl.BlockSpec((TM, B_INNER), lambda i: (i, 0)),
            pl.BlockSpec((TM, HEADS * DQK), lambda i: (i, 0)),
            pl.BlockSpec((HEADS * DQK, TM), lambda i: (0, i)),
            pl.BlockSpec((TM, B_INNER), lambda i: (i, 0)),
            pl.BlockSpec((TM, LANES), lambda i: (i, 0)),
            pl.BlockSpec((4 * HEADS, TM), lambda i: (0, i)),
            pl.BlockSpec((4 * HEADS, TM), lambda i: (0, i)),
        ],
        out_shape=[
            jax.ShapeDtypeStruct((N_ALL, B_INNER), BF16),
            jax.ShapeDtypeStruct((N_ALL, HEADS * DQK), BF16),
            jax.ShapeDtypeStruct((HEADS * DQK, N_ALL), BF16),
            jax.ShapeDtypeStruct((N_ALL, B_INNER), BF16),
            jax.ShapeDtypeStruct((N_ALL, LANES), F32),
            jax.ShapeDtypeStruct((4 * HEADS, N_ALL), F32),
            jax.ShapeDtypeStruct((4 * HEADS, N_ALL), F32),
        ],
        compiler_params=_cparams(("parallel",)),
        name="mlstm_qkv",
    )(xm, xm, xm, conv_w, conv_b, wq, wkt, wv, wg, wgt, bg, bgt)


CTX_CHUNKS = CTX // CHUNK
LAT_CHUNKS = SEQ // CHUNK
SCAN_STEPS = CTX_CHUNKS + LAT_CHUNKS


def _scan_dir(nb, d, q_ref, kt_ref, v_ref, g_ref, gt_ref, tot_ref, o_ref, c_scr, m_scr):
    r = lax.broadcasted_iota(jnp.int32, (CHUNK, CHUNK), 0)
    c = lax.broadcasted_iota(jnp.int32, (CHUNK, CHUNK), 1)
    if d == 0:
        keep = c <= r
    else:
        keep = c >= r
    ones_blk = jnp.ones((CHUNK, LANES), BF16)
    bcol_all = g_ref[...]
    gtt = gt_ref[...]
    tot_all = tot_ref[...]
    base = 2 * HEADS * d

    def head_first(h):
        li_r = gtt[base + h:base + h + 1, :]
        b_r = gtt[base + HEADS + h:base + HEADS + h + 1, :]
        b_last = tot_all[base + HEADS + h:base + HEADS + h + 1, :]
        m_old = m_scr[nb, d, h]
        c_old = c_scr[nb, d, h]
        qh = q_ref[:, h * DQK:(h + 1) * DQK]
        kth = kt_ref[h * DQK:(h + 1) * DQK, :]
        vaug = jnp.concatenate([v_ref[:, h * DV:(h + 1) * DV], ones_blk], axis=1)
        g_r = b_last - b_r + li_r
        m_new = jnp.maximum(b_last + m_old, jnp.max(g_r, axis=-1, keepdims=True))
        decay = jnp.exp(b_last + m_old - m_new)
        wk = jnp.exp(g_r - m_new)
        kwt = (kth.astype(F32) * wk).astype(BF16)
        qk = jnp.dot(qh, kth, preferred_element_type=F32)
        qc = jnp.dot(qh, c_old.astype(BF16), preferred_element_type=F32)
        c_scr[nb, d, h] = jnp.concatenate([decay] * 3, axis=1) * c_old + jnp.dot(
            kwt, vaug, preferred_element_type=F32)
        m_scr[nb, d, h] = m_new
        return h, li_r, b_r, m_old, vaug, qk, qc

    def head_second(h, li_r, b_r, m_old, vaug, qk, qc):
        b_full = jnp.broadcast_to(bcol_all[:, base + HEADS + h:base + HEADS + h + 1], (CHUNK, LANES))
        dmat = jnp.where(keep, b_full - (b_r - li_r), NEG_INF)
        inter = b_full + m_old
        m_t = jnp.maximum(inter, jnp.max(dmat, axis=-1, keepdims=True))
        p = jnp.exp(dmat - m_t)
        s = qk * p
        a = jnp.exp(inter - m_t)
        num = jnp.concatenate([a] * 3, axis=1) * qc + jnp.dot(
            s.astype(BF16), vaug, preferred_element_type=F32)
        inv = 1.0 / jnp.maximum(jnp.abs(num[:, DV:]), jnp.exp(-m_t))
        o_ref[nb, :, h * DV:(h + 1) * DV] = (
            num[:, :DV] * jnp.concatenate([inv] * 2, axis=1)).astype(BF16)

    return head_first, head_second


SCAN_NB = 2
SCAN_IN = 6
SCAN_SKEW = 1


def _scan_kernel(*refs):
    n_in = SCAN_NB * 2 * SCAN_IN
    ins, outs = refs[:n_in], refs[n_in:n_in + 2]
    c_scr, m_scr = refs[n_in + 2:]

    @pl.when(pl.program_id(1) == 0)
    def _():
        c_scr[...] = jnp.zeros_like(c_scr)
        m_scr[...] = jnp.zeros_like(m_scr)

    units = []
    for nb in range(SCAN_NB):
        for d in range(2):
            k = nb * 2 + d
            first, second = _scan_dir(nb, d, *ins[k * SCAN_IN:(k + 1) * SCAN_IN], outs[d], c_scr, m_scr)
            units += [(first, second, h) for h in range(HEADS)]
    pending = []
    for first, second, h in units:
        pending.append((second, first(h)))
        if len(pending) > SCAN_SKEW:
            fn, old_vals = pending.pop(0)
            fn(*old_vals)
    for fn, old_vals in pending:
        fn(*old_vals)


def _scan(q, kt, v, gts, gtt, tot):
    lat_blk = N_LAT // CHUNK

    def fwd_in(b, j):
        return jnp.where(j < CTX_CHUNKS, lat_blk + b * CTX_CHUNKS + j, b * LAT_CHUNKS + j - CTX_CHUNKS)

    def bwd_in(b, j):
        return jnp.where(j < CTX_CHUNKS, lat_blk + b * CTX_CHUNKS + (CTX_CHUNKS - 1 - j),
                         b * LAT_CHUNKS + (SCAN_STEPS - 1 - j))

    def fwd_out(b, j):
        return b * LAT_CHUNKS + jnp.maximum(j - CTX_CHUNKS, 0)

    def bwd_out(b, j):
        return b * LAT_CHUNKS + jnp.minimum(SCAN_STEPS - 1 - j, LAT_CHUNKS - 1)

    n_g = BATCH // SCAN_NB

    def specs(fn, nb):
        blk = lambda g, j: fn(nb * n_g + g, j)
        return [
            pl.BlockSpec((CHUNK, HEADS * DQK), lambda g, j: (blk(g, j), 0)),
            pl.BlockSpec((HEADS * DQK, CHUNK), lambda g, j: (0, blk(g, j))),
            pl.BlockSpec((CHUNK, B_INNER), lambda g, j: (blk(g, j), 0)),
            pl.BlockSpec((CHUNK, LANES), lambda g, j: (blk(g, j), 0)),
            pl.BlockSpec((4 * HEADS, CHUNK), lambda g, j: (0, blk(g, j))),
            pl.BlockSpec((4 * HEADS, CHUNK), lambda g, j: (0, blk(g, j))),
        ]

    def out_spec(fn):
        return pl.BlockSpec((SCAN_NB, CHUNK, B_INNER), lambda g, j: (0, fn(g, j), 0))

    in_specs, operands = [], []
    for nb in range(SCAN_NB):
        for fn_in in (fwd_in, bwd_in):
            in_specs += specs(fn_in, nb)
            operands += [q, kt, v, gts, gtt, tot]
    h_shape = jax.ShapeDtypeStruct((SCAN_NB, N_LAT // SCAN_NB, B_INNER), BF16)
    hf, hb = pl.pallas_call(
        _scan_kernel,
        grid=(n_g, SCAN_STEPS),
        in_specs=in_specs,
        out_specs=[out_spec(fwd_out), out_spec(bwd_out)],
        out_shape=[h_shape, h_shape],
        scratch_shapes=[pltpu.VMEM((SCAN_NB, 2, HEADS, DQK, DV + LANES), F32),
                        pltpu.VMEM((SCAN_NB, 2, HEADS, 1, LANES), F32)],
        compiler_params=_cparams(("parallel", "arbitrary")),
        name="mlstm_scan",
    )(*operands)
    return hf.reshape(N_LAT, B_INNER), hb.reshape(N_LAT, B_INNER)


def _combine_kernel(x_ref, mod_ref, hf_ref, hb_ref, xc_ref, op_ref, hg_ref, sk_ref, wout_ref,
                    g2_ref, wr_ref, br_ref, o_ref, meta_ref, tmeta_ref, cnt_ref, cntr_ref,
                    y_scr, carry, carry_r):
    for h in range(HEADS):
        hs = slice(h * DV, (h + 1) * DV)
        s = hf_ref[:, hs].astype(F32) + hb_ref[:, hs].astype(F32)
        s = s * lax.rsqrt(jnp.mean(s * s, axis=-1, keepdims=True) + EPS)
        y = jax.nn.sigmoid(op_ref[:, hs].astype(F32)) * (
            s * hg_ref[:, hs] + sk_ref[:, hs] * xc_ref[:, hs].astype(F32))
        y_scr[:, hs] = y.astype(BF16)
        if h % 2 == 1:
            ks = slice((h - 1) * DV, (h + 1) * DV)
            part = jnp.dot(y_scr[:, ks], wout_ref[ks, :], preferred_element_type=F32)
            out = part if h == 1 else out + part
    xn = x_ref[...] + mod_ref[0, 2:3, :] * out
    o_ref[...] = xn
    _route_tile(xn, mod_ref, g2_ref, wr_ref, br_ref, meta_ref, tmeta_ref, cnt_ref, cntr_ref,
                carry, carry_r)


def _combine(xa, mod, hf, hb, xc, op, head_g, skip, w_out, g2, wr, br):
    const = lambda i: (0, 0)
    row = lambda i: (i, 0)
    r_in, r_out, r_shape, r_scratch = _router_specs(LAT_TILES)
    return pl.pallas_call(
        _combine_kernel,
        grid=(LAT_TILES,),
        in_specs=[
            pl.BlockSpec((TM, D), row),
            pl.BlockSpec((1, 6, D), lambda i: (_mod_row(i), 0, 0)),
            pl.BlockSpec((TM, B_INNER), row),
            pl.BlockSpec((TM, B_INNER), row),
            pl.BlockSpec((TM, B_INNER), row),
            pl.BlockSpec((TM, B_INNER), row),
            pl.BlockSpec((1, B_INNER), const),
            pl.BlockSpec((1, B_INNER), const),
            pl.BlockSpec((B_INNER, D), const),
        ] + r_in,
        out_specs=[pl.BlockSpec((TM, D), row)] + r_out,
        out_shape=[jax.ShapeDtypeStruct((N_LAT, D), F32)] + r_shape,
        scratch_shapes=[pltpu.VMEM((TM, B_INNER), BF16)] + r_scratch,
        compiler_params=_cparams(("arbitrary",)),
        name="mlstm_combine",
    )(xa, mod, hf, hb, xc, op, head_g, skip, w_out, g2, wr, br)


def _router_weights(w_grp, b_grp, w_exp, b_exp):
    wr = jnp.zeros((LANES, D), F32).at[:N_EXP].set(w_exp.T).at[N_EXP:N_EXP + N_GRP].set(w_grp.T)
    br = jnp.zeros((LANES, 1), F32).at[:N_EXP, 0].set(b_exp).at[N_EXP:N_EXP + N_GRP, 0].set(b_grp)
    return wr.astype(BF16), br


PLAN_BLK = 2048
PLAN_PAIRS = 256


def _plan_kernel(meta_ref, cc_ref, cr_ref, pp_ref, plan_ref):
    row = lax.broadcasted_iota(jnp.int32, (LANES, LANES), 0)
    col = lax.broadcasted_iota(jnp.int32, (LANES, LANES), 1)
    cnt_c = cc_ref[...]
    cnt_r = cr_ref[0:1, :]
    starts_c = jnp.sum(jnp.where(col < row, cnt_r, 0.0), axis=1, keepdims=True)
    erow = lax.broadcasted_iota(jnp.int32, (LANES, PLAN_BLK), 0).astype(F32)
    for j in range(2):
        pos = jnp.sum(jnp.where(erow == meta_ref[j:j + 1, :], starts_c, 0.0), axis=0,
                      keepdims=True) + meta_ref[j + 2:j + 3, :]
        pp_ref[j:j + 1, :] = pos.astype(jnp.int32)
    pp_ref[2:8, :] = jnp.zeros((6, PLAN_BLK), jnp.int32)

    @pl.when(pl.program_id(0) == 0)
    def _():
        ends_c = starts_c + cnt_c
        starts_r = jnp.sum(jnp.where(row < col, cnt_c, 0.0), axis=0, keepdims=True)
        ends_r = starts_r + cnt_r

        def tiles(st, en, cn):
            first_tile = jnp.floor(st * (1.0 / TM))
            n = jnp.where(cn > 0.0, jnp.floor((en - 1.0) * (1.0 / TM)) - first_tile + 1.0, 0.0)
            return first_tile, n

        ft_c, pairs_c = tiles(starts_c, ends_c, cnt_c)
        _, pairs_r = tiles(starts_r, ends_r, cnt_r)
        pend_c = jnp.sum(jnp.where(col <= row, pairs_r, 0.0), axis=1, keepdims=True)
        pstart_c = pend_c - pairs_c
        total = jnp.sum(pairs_r, axis=1, keepdims=True)
        q = lax.broadcasted_iota(jnp.int32, (1, PLAN_PAIRS), 1).astype(F32)
        erow_p = lax.broadcasted_iota(jnp.int32, (LANES, PLAN_PAIRS), 0).astype(F32)

        def at(qv):
            qc = jnp.maximum(jnp.minimum(qv, total - 1.0), 0.0)
            e = jnp.sum(jnp.where(qc >= pend_c, 1.0, 0.0), axis=0, keepdims=True)
            oh = erow_p == e
            tile = jnp.sum(jnp.where(oh, ft_c - pstart_c, 0.0), axis=0, keepdims=True) + qc
            return e, oh, tile

        e_q, oh, tile_q = at(q)
        e_p, _, tile_p = at(q - 1.0)
        lo = jnp.sum(jnp.where(oh, starts_c, 0.0), axis=0, keepdims=True) - tile_q * TM
        hi = jnp.sum(jnp.where(oh, ends_c, 0.0), axis=0, keepdims=True) - tile_q * TM
        rows = (tile_q, e_q, jnp.clip(lo, 0.0, TM), jnp.clip(hi, 0.0, TM),
                jnp.where((q == 0.0) | (tile_q != tile_p), 1.0, 0.0),
                jnp.where((q == 0.0) | (e_q != e_p), 1.0, 0.0),
                jnp.where(q < total, 1.0, 0.0),
                jnp.zeros((1, PLAN_PAIRS), F32))
        for j, v in enumerate(rows):
            plan_ref[j:j + 1, :] = v.astype(jnp.int32)


def _route_plan(meta, cnt_c, cnt_r, n_rows):
    n_pairs = 2 * n_rows // TM + N_EXP - 1
    assert n_pairs <= PLAN_PAIRS and n_rows % PLAN_BLK == 0
    const = lambda i: (0, 0)
    pp, plan = pl.pallas_call(
        _plan_kernel,
        grid=(n_rows // PLAN_BLK,),
        in_specs=[pl.BlockSpec((8, PLAN_BLK), lambda i: (0, i)),
                  pl.BlockSpec((LANES, 1), const),
                  pl.BlockSpec((8, LANES), const)],
        out_specs=[pl.BlockSpec((8, PLAN_BLK), lambda i: (0, i)),
                   pl.BlockSpec((8, PLAN_PAIRS), const)],
        out_shape=[jax.ShapeDtypeStruct((8, n_rows), jnp.int32),
                   jax.ShapeDtypeStruct((8, PLAN_PAIRS), jnp.int32)],
        compiler_params=_cparams(("arbitrary",)),
        name="moe_plan",
    )(meta, cnt_c, cnt_r)
    return pp[0], pp[1], tuple(plan[j, :n_pairs] for j in range(7))


def _moe_experts(xa, routing, mod, g2, w1, w3, w2, layer, n_rows):
    meta, _, cnt_c, cnt_r = routing
    p1, p2, plan = _route_plan(meta, cnt_c, cnt_r, n_rows)
    xs = _scatter(p1, p2, xa, mod, g2, n_rows // TM)
    return p1, p2, _grouped(plan, xs, w1, w3, w2, layer)


def kernel(x, c, ctx, c_ctx, norm_g, w_ada, b_ada, a_w_in, a_b_in, a_g_v, a_w_s, a_b_s, a_w_out,
           b_w_in, b_conv_w, b_conv_b, b_w_q, b_w_k, b_w_v, b_w_gate, b_b_gate, b_head_g, b_skip,
           b_w_out, moe_w_grp, moe_b_grp, moe_w_exp, moe_b_exp, moe_w1, moe_w3, moe_w2, final_g):
    cc = jnp.zeros((MOD_ROWS, D), F32).at[:BATCH].set(c).at[BATCH].set(c_ctx)
    mods = _ada(cc, w_ada, b_ada).reshape(2, MOD_ROWS, 6, D)

    mod = mods[0]
    b_s_full = jnp.repeat(a_b_s[0].T, A_GC, axis=1)
    g2 = norm_g[0, 1].reshape(1, D)
    wr, br = _router_weights(moe_w_grp[0], moe_b_grp[0], moe_w_exp[0], moe_b_exp[0])
    xa, *routing = _gmlp(x.reshape(N_LAT, D), ctx.reshape(N_CTX, D), mod, norm_g[0, 0].reshape(1, D),
                         a_w_in[0].astype(BF16), a_b_in[0].reshape(1, -1), a_g_v[0].reshape(1, -1),
                         a_w_s[0].astype(BF16), b_s_full, a_w_out[0].astype(BF16), g2, wr, br)
    p1, p2, ys = _moe_experts(xa, routing, mod, g2, moe_w1, moe_w3, moe_w2, 0, N_ALL)

    mod = mods[1]
    xa, xm, op = _inproj(p1, p2, xa, mods[0], routing[1], ys, mod, norm_g[1, 0].reshape(1, D),
                         b_w_in[0].astype(BF16))
    wg = jnp.zeros((B_INNER, LANES), F32).at[:, :4 * HEADS].set(b_w_gate[0]).astype(BF16)
    bg = jnp.zeros((1, LANES), F32).at[0, :4 * HEADS].set(b_b_gate[0])
    xc, q, kt, v, gts, gtt, tot = _qkv(xm, b_conv_w[0], b_conv_b[0].reshape(1, -1),
                                  b_w_q[0].astype(BF16),
                                  jnp.transpose(b_w_k[0], (0, 2, 1)).astype(BF16),
                                  b_w_v[0].astype(BF16), wg, b_w_gate[0].T.astype(BF16), bg,
                                  b_b_gate[0].reshape(-1, 1))
    hf, hb = _scan(q, kt, v, gts, gtt, tot)
    g2 = norm_g[1, 1].reshape(1, D)
    wr, br = _router_weights(moe_w_grp[1], moe_b_grp[1], moe_w_exp[1], moe_b_exp[1])
    xl, *routing = _combine(xa, mod, hf, hb, xc, op, b_head_g[0].reshape(1, -1),
                            b_skip[0].reshape(1, -1), b_w_out[0].astype(BF16), g2, wr, br)
    p1, p2, ys = _moe_experts(xl, routing, mod, g2, moe_w1, moe_w3, moe_w2, 1, N_LAT)
    out = _gather(p1, p2, xl, mod, routing[1], final_g.reshape(1, D), ys, LAT_TILES, True)
    return out.reshape(BATCH, SEQ, D)
```

```python
import functools

import jax
import jax.numpy as jnp
from jax import lax
from jax.experimental import pallas as pl
from jax.experimental.pallas import tpu as pltpu

F32 = jnp.float32
BF16 = jnp.bfloat16

D = 1024
BATCH = 8
SEQ = 2048
CTX = 256
EPS = 1e-6
NEG_INF = -1e30
N_LAT = BATCH * SEQ
N_CTX = BATCH * CTX
N_ALL = N_LAT + N_CTX

TM = 256
LAT_TILES = N_LAT // TM
ALL_TILES = N_ALL // TM
TILES_PER_SEQ = SEQ // TM
MOD_ROWS = 16

CHUNK = 128
A_HALF = 2048
A_GROUPS = 8
A_GC = A_HALF // A_GROUPS
B_INNER = 2048
HEADS = 8
DH = B_INNER // HEADS
DQK = DH // 2
DV = DH
CONV_K = 5
HALO = 16
N_EXP = 32
N_GRP = 4
EXP_PER_GRP = 8
D_EXP = 512
LANES = 128

VMEM_LIMIT = 56 * 1024 * 1024


def _cparams(sem):
    return pltpu.CompilerParams(dimension_semantics=sem, vmem_limit_bytes=VMEM_LIMIT)


def _mod_row(i):
    return jnp.where(i < LAT_TILES, i // TILES_PER_SEQ, BATCH)


def _norm_mod(x, g, shift, scale):
    y = x * lax.rsqrt(jnp.mean(x * x, axis=-1, keepdims=True) + EPS) * g
    return y * (1.0 + scale) + shift


def _bf16_terms(x, n):
    terms = []
    r = x
    for _ in range(n):
        t = r.astype(BF16)
        terms.append(t)
        r = r - t.astype(F32)
    return terms


def _dot_nt(a, b):
    return lax.dot_general(a, b, (((1,), (1,)), ((), ())), preferred_element_type=F32)


def _dot_nn(a, b):
    return jnp.dot(a, b, preferred_element_type=F32)


def _dot3(a, b, dot=_dot_nn):
    a1, a2 = _bf16_terms(a, 2)
    b1, b2 = _bf16_terms(b, 2)
    return dot(a1, b1) + (dot(a1, b2) + dot(a2, b1))


def _dot_exact_lhs(a_bf16, b, dot=_dot_nn):
    return sum(dot(a_bf16, t) for t in _bf16_terms(b, 3))


def _dot_exact_rhs(a, b_bf16, dot=_dot_nn):
    return sum(dot(t, b_bf16) for t in _bf16_terms(a, 3))


def _gelu_tanh(x):
    half = 0.5 * x
    t = jnp.tanh(x * (0.7978845608028654 + 0.035677408136300125 * (x * x)))
    return half + half * t


ADA_BN = 1536


def _ada_kernel(c_ref, w_ref, b_ref, o_ref):
    c = c_ref[...]
    a = c * jax.nn.sigmoid(c)
    o_ref[0] = _dot3(a, w_ref[0]) + b_ref[0]


def _ada(cc, w_ada, b_ada):
    depth = w_ada.shape[0]
    return pl.pallas_call(
        _ada_kernel,
        grid=(depth, 6 * D // ADA_BN),
        in_specs=[
            pl.BlockSpec((MOD_ROWS, D), lambda l, j: (0, 0)),
            pl.BlockSpec((1, D, ADA_BN), lambda l, j: (l, 0, j)),
            pl.BlockSpec((1, 1, ADA_BN), lambda l, j: (l, 0, j)),
        ],
        out_specs=pl.BlockSpec((1, MOD_ROWS, ADA_BN), lambda l, j: (l, 0, j)),
        out_shape=jax.ShapeDtypeStruct((depth, MOD_ROWS, 6 * D), F32),
        compiler_params=_cparams(("parallel", "parallel")),
        name="ada",
    )(cc, w_ada, b_ada.reshape(depth, 1, 6 * D))


GM_CH = 512


def _gmlp_kernel(xl_ref, xc_ref, mod_ref, g_ref, win_ref, bin_ref, gv_ref, ws_ref, bs_ref, wout_ref,
                 g2_ref, wr_ref, br_ref, o_ref, meta_ref, tmeta_ref, cnt_ref, cntr_ref,
                 z_scr, y_scr, carry, carry_r):
    x = jnp.where(pl.program_id(0) < LAT_TILES, xl_ref[...], xc_ref[...])
    h = _norm_mod(x, g_ref[...], mod_ref[0, 0:1, :], mod_ref[0, 1:2, :])
    hb = h.astype(BF16)
    s1 = jnp.zeros((TM, 1), F32)
    s2 = jnp.zeros((TM, 1), F32)
    n_ch = 2 * A_HALF // GM_CH
    for j in list(range(n_ch // 2, n_ch)) + list(range(n_ch // 2)):
        cs = slice(j * GM_CH, (j + 1) * GM_CH)
        zc = jnp.dot(hb, win_ref[:, cs], preferred_element_type=F32) + bin_ref[:, cs]
        zc = _gelu_tanh(zc)
        z_scr[:, cs] = zc
        if j * GM_CH >= A_HALF:
            s1 = s1 + jnp.sum(zc, axis=-1, keepdims=True)
            s2 = s2 + jnp.sum(zc * zc, axis=-1, keepdims=True)
    mu = s1 * (1.0 / A_HALF)
    rstd = lax.rsqrt(s2 * (1.0 / A_HALF) - mu * mu + EPS)
    for c in range(TM // CHUNK):
        rs = slice(c * CHUNK, (c + 1) * CHUNK)
        for g in range(A_GROUPS):
            cs = slice(g * A_GC, (g + 1) * A_GC)
            vs = slice(A_HALF + g * A_GC, A_HALF + (g + 1) * A_GC)
            v = (z_scr[rs, vs] - mu[rs]) * rstd[rs] * gv_ref[:, cs]
            s = jnp.dot(ws_ref[g], v.astype(BF16), preferred_element_type=F32) + bs_ref[:, cs]
            y_scr[rs, cs] = (z_scr[rs, cs] * s).astype(BF16)
    out = jnp.dot(y_scr[...], wout_ref[...], preferred_element_type=F32)
    xn = x + mod_ref[0, 2:3, :] * out
    o_ref[...] = xn
    _route_tile(xn, mod_ref, g2_ref, wr_ref, br_ref, meta_ref, tmeta_ref, cnt_ref, cntr_ref,
                carry, carry_r)


def _gmlp(xl, xc, mod, g, w_in, b_in, g_v, w_s, b_s_full, w_out, g2, wr, br):
    const = lambda i: (0, 0)
    r_in, r_out, r_shape, r_scratch = _router_specs(ALL_TILES)
    return pl.pallas_call(
        _gmlp_kernel,
        grid=(ALL_TILES,),
        in_specs=[
            pl.BlockSpec((TM, D), lambda i: (jnp.minimum(i, LAT_TILES - 1), 0)),
            pl.BlockSpec((TM, D), lambda i: (jnp.maximum(i - LAT_TILES, 0), 0)),
            pl.BlockSpec((1, 6, D), lambda i: (_mod_row(i), 0, 0)),
            pl.BlockSpec((1, D), const),
            pl.BlockSpec((D, 2 * A_HALF), const),
            pl.BlockSpec((1, 2 * A_HALF), const),
            pl.BlockSpec((1, A_HALF), const),
            pl.BlockSpec((A_GROUPS, CHUNK, CHUNK), lambda i: (0, 0, 0)),
            pl.BlockSpec((CHUNK, A_HALF), const),
            pl.BlockSpec((A_HALF, D), const),
        ] + r_in,
        out_specs=[pl.BlockSpec((TM, D), lambda i: (i, 0))] + r_out,
        out_shape=[jax.ShapeDtypeStruct((N_ALL, D), F32)] + r_shape,
        scratch_shapes=[pltpu.VMEM((TM, 2 * A_HALF), F32), pltpu.VMEM((TM, A_HALF), BF16)] + r_scratch,
        compiler_params=_cparams(("arbitrary",)),
        name="gmlp",
    )(xl, xc, mod, g, w_in, b_in, g_v, w_s, b_s_full, w_out, g2, wr, br)


def _route_tile(x, mod_ref, g_ref, wr_ref, br_ref, meta_ref, tmeta_ref, cnt_ref, cntr_ref,
                carry, carry_r):
    @pl.when(pl.program_id(0) == 0)
    def _():
        carry[...] = jnp.zeros_like(carry)
        carry_r[...] = jnp.zeros_like(carry_r)

    h = _norm_mod(x, g_ref[...], mod_ref[0, 3:4, :], mod_ref[0, 4:5, :])
    lt = _dot_nt(wr_ref[...], h.astype(BF16)) + br_ref[...]
    e_t = lt[0:N_EXP]
    row8 = lax.broadcasted_iota(jnp.int32, (EXP_PER_GRP, TM), 0).astype(F32)
    g_t = jnp.where(row8 < N_GRP, lt[N_EXP:N_EXP + EXP_PER_GRP], -jnp.inf)
    gmax = jnp.max(g_t, axis=0, keepdims=True)
    p_g = 1.0 / jnp.sum(jnp.exp(g_t - gmax), axis=0, keepdims=True)
    g_idx = jnp.min(jnp.where(g_t == gmax, row8, float(EXP_PER_GRP)), axis=0, keepdims=True)
    sel = jnp.zeros((EXP_PER_GRP, TM), F32)
    for g in range(N_GRP):
        sel = sel + jnp.where(g_idx == g, e_t[g * EXP_PER_GRP:(g + 1) * EXP_PER_GRP], 0.0)
    m1 = jnp.max(sel, axis=0, keepdims=True)
    i1 = jnp.min(jnp.where(sel == m1, row8, float(EXP_PER_GRP)), axis=0, keepdims=True)
    sel2 = jnp.where(row8 == i1, -jnp.inf, sel)
    m2 = jnp.max(sel2, axis=0, keepdims=True)
    i2 = jnp.min(jnp.where(sel2 == m2, row8, float(EXP_PER_GRP)), axis=0, keepdims=True)
    e2 = jnp.exp(m2 - m1)
    w1 = p_g / (1.0 + e2)
    w2 = p_g * e2 / (1.0 + e2)
    row = lax.broadcasted_iota(jnp.int32, (LANES, TM), 0).astype(F32)
    id1 = g_idx * EXP_PER_GRP + i1
    id2 = g_idx * EXP_PER_GRP + i2
    oh1 = row == id1
    oh2 = row == id2
    oh = jnp.where(oh1, 1.0, 0.0) + jnp.where(oh2, 1.0, 0.0)
    before = (lax.broadcasted_iota(jnp.int32, (TM, TM), 0)
              < lax.broadcasted_iota(jnp.int32, (TM, TM), 1)).astype(BF16)
    ohb = oh.astype(BF16)
    tot = jnp.dot(ohb, before, preferred_element_type=F32) + carry[...]
    r1 = jnp.sum(jnp.where(oh1, tot, 0.0), axis=0, keepdims=True)
    r2 = jnp.sum(jnp.where(oh2, tot, 0.0), axis=0, keepdims=True)
    carry[...] += jnp.sum(oh, axis=-1, keepdims=True)
    carry_r[...] += _dot_nt(jnp.ones((8, TM), BF16), ohb)
    cnt_ref[...] = carry[...]
    cntr_ref[...] = carry_r[...]
    meta_ref[0:1, :] = id1
    meta_ref[1:2, :] = id2
    meta_ref[2:3, :] = r1
    meta_ref[3:4, :] = r2
    meta_ref[4:8, :] = jnp.zeros((4, TM), F32)
    tmeta_ref[...] = (jnp.where(row == 0.0, w1, 0.0) + jnp.where(row == 1.0, w2, 0.0)).T


def _router_specs(n_tiles):
    const = lambda i: (0, 0)
    n = n_tiles * TM
    in_specs = [pl.BlockSpec((1, D), const), pl.BlockSpec((LANES, D), const),
                pl.BlockSpec((LANES, 1), const)]
    out_specs = [pl.BlockSpec((8, TM), lambda i: (0, i)),
                 pl.BlockSpec((TM, LANES), lambda i: (i, 0)),
                 pl.BlockSpec((LANES, 1), const),
                 pl.BlockSpec((8, LANES), const)]
    out_shape = [jax.ShapeDtypeStruct((8, n), F32), jax.ShapeDtypeStruct((n, LANES), F32),
                 jax.ShapeDtypeStruct((LANES, 1), F32), jax.ShapeDtypeStruct((8, LANES), F32)]
    scratch = [pltpu.VMEM((LANES, 1), F32), pltpu.VMEM((8, LANES), F32)]
    return in_specs, out_specs, out_shape, scratch


ISSUE_UNROLL = 8
ROW_TILE = (8, LANES)
MXU_N = 256


def _row_copy_wait(buf_slot, sem_slot):
    pltpu.make_async_copy(buf_slot, buf_slot, sem_slot).wait()


def _scatter_kernel(p1_ref, p2_ref, x_ref, mod_ref, g_ref, xs_ref, buf, sem):
    i = pl.program_id(0)
    n_steps = pl.num_programs(0)
    slot = lax.rem(i, 2)

    def wait_slot(s):
        _row_copy_wait(buf.at[s], sem.at[s])
        _row_copy_wait(buf.at[s], sem.at[s])

    @pl.when(i >= 2)
    def _():
        wait_slot(slot)

    h = _norm_mod(x_ref[...], g_ref[...], mod_ref[0, 3:4, :], mod_ref[0, 4:5, :])
    buf[slot] = h.reshape(TM, *ROW_TILE)
    base = i * TM

    def body(r, carry):
        src = buf.at[slot, r]
        pltpu.make_async_copy(src, xs_ref.at[p1_ref[base + r]], sem.at[slot]).start()
        pltpu.make_async_copy(src, xs_ref.at[p2_ref[base + r]], sem.at[slot]).start(priority=1)
        return carry

    lax.fori_loop(0, TM, body, 0, unroll=ISSUE_UNROLL)

    @pl.when(i == n_steps - 1)
    def _():
        @pl.when(i >= 1)
        def _():
            wait_slot(1 - slot)

        wait_slot(slot)


def _scatter(p1, p2, xa, mod, g, n_tiles):
    n = n_tiles * TM
    return pl.pallas_call(
        _scatter_kernel,
        grid_spec=pltpu.PrefetchScalarGridSpec(
            num_scalar_prefetch=2,
            grid=(n_tiles,),
            in_specs=[
                pl.BlockSpec((TM, D), lambda i, p1, p2: (i, 0)),
                pl.BlockSpec((1, 6, D), lambda i, p1, p2: (_mod_row(i), 0, 0)),
                pl.BlockSpec((1, D), lambda i, p1, p2: (0, 0)),
            ],
            out_specs=pl.BlockSpec(memory_space=pl.ANY),
            scratch_shapes=[pltpu.VMEM((2, TM) + ROW_TILE, F32), pltpu.SemaphoreType.DMA((2,))],
        ),
        out_shape=jax.ShapeDtypeStruct((2 * n,) + ROW_TILE, F32),
        compiler_params=_cparams(("arbitrary",)),
        name="moe_scatter",
    )(p1, p2, xa, mod, g)


def _grouped_kernel(tile_ref, exp_ref, lo_ref, hi_ref, first_ref, newexp_ref, valid_ref,
                    xs_ref, w1_ref, w3_ref, w2_ref, ys_ref, wb1, wb3, wb2):
    q = pl.program_id(0)

    @pl.when(valid_ref[q] == 1)
    def _():
        @pl.when(newexp_ref[q] == 1)
        def _():
            wb1[...] = w1_ref[0, 0].astype(BF16)
            wb3[...] = w3_ref[0, 0].astype(BF16)
            wb2[...] = w2_ref[0, 0].astype(BF16)

        x = xs_ref[...].reshape(TM, D).astype(BF16)
        rows = lax.broadcasted_iota(jnp.int32, (TM, 1), 0)
        mine = (rows >= lo_ref[q]) & (rows < hi_ref[q])
        acts = []
        for j in range(D_EXP // MXU_N):
            cs = slice(j * MXU_N, (j + 1) * MXU_N)
            a = jnp.dot(x, wb1[:, cs], preferred_element_type=F32)
            b = jnp.dot(x, wb3[:, cs], preferred_element_type=F32)
            acts.append(jnp.where(mine, a * jax.nn.sigmoid(a) * b, 0.0).astype(BF16))
        act = jnp.concatenate(acts, axis=1)
        y = jnp.dot(act, wb2[...], preferred_element_type=F32).reshape(TM, *ROW_TILE)

        @pl.when(first_ref[q] == 1)
        def _():
            ys_ref[...] = y

        @pl.when(first_ref[q] == 0)
        def _():
            ys_ref[...] += y


def _grouped(plan, xs, w1, w3, w2, layer):
    n_pairs = plan[0].shape[0]
    tile_map = lambda q, tile, exp, *_: (tile[q], 0, 0)
    exp_map = lambda q, tile, exp, *_: (layer, exp[q], 0, 0)
    return pl.pallas_call(
        _grouped_kernel,
        grid_spec=pltpu.PrefetchScalarGridSpec(
            num_scalar_prefetch=7,
            grid=(n_pairs,),
            in_specs=[
                pl.BlockSpec((TM,) + ROW_TILE, tile_map),
                pl.BlockSpec((1, 1, D, D_EXP), exp_map),
                pl.BlockSpec((1, 1, D, D_EXP), exp_map),
                pl.BlockSpec((1, 1, D_EXP, D), exp_map),
            ],
            out_specs=pl.BlockSpec((TM,) + ROW_TILE, tile_map),
            scratch_shapes=[pltpu.VMEM((D, D_EXP), BF16), pltpu.VMEM((D, D_EXP), BF16),
                            pltpu.VMEM((D_EXP, D), BF16)],
        ),
        out_shape=jax.ShapeDtypeStruct(xs.shape, F32),
        compiler_params=_cparams(("arbitrary",)),
        name="moe_grouped",
    )(*plan, xs, w1, w3, w2)


def _gather_kernel(p1_ref, p2_ref, x_ref, mod_ref, tm_ref, fg_ref, ys_ref, o_ref, buf, sem,
                   *, final_norm):
    i = pl.program_id(0)
    n_steps = pl.num_programs(0)
    slot = lax.rem(i, 2)

    def issue(tile, s):
        base = tile * TM

        def body(r, carry):
            pltpu.make_async_copy(ys_ref.at[p1_ref[base + r]], buf.at[s, 0, r], sem.at[s]).start()
            pltpu.make_async_copy(ys_ref.at[p2_ref[base + r]], buf.at[s, 1, r],
                                  sem.at[s]).start(priority=1)
            return carry

        lax.fori_loop(0, TM, body, 0, unroll=ISSUE_UNROLL)

    @pl.when(i == 0)
    def _():
        issue(0, 0)

    @pl.when(i + 1 < n_steps)
    def _():
        issue(i + 1, 1 - slot)

    _row_copy_wait(buf.at[slot, 0], sem.at[slot])
    _row_copy_wait(buf.at[slot, 1], sem.at[slot])
    y = (tm_ref[:, 0:1] * buf[slot, 0].reshape(TM, D)
         + tm_ref[:, 1:2] * buf[slot, 1].reshape(TM, D))
    out = x_ref[...] + mod_ref[0, 5:6, :] * y
    if final_norm:
        out = out * lax.rsqrt(jnp.mean(out * out, axis=-1, keepdims=True) + EPS) * fg_ref[...]
    o_ref[...] = out


def _gather(p1, p2, xa, mod, tmeta, final_g, ys, n_tiles, final_norm):
    return pl.pallas_call(
        functools.partial(_gather_kernel, final_norm=final_norm),
        grid_spec=pltpu.PrefetchScalarGridSpec(
            num_scalar_prefetch=2,
            grid=(n_tiles,),
            in_specs=[
                pl.BlockSpec((TM, D), lambda i, p1, p2: (i, 0)),
                pl.BlockSpec((1, 6, D), lambda i, p1, p2: (_mod_row(i), 0, 0)),
                pl.BlockSpec((TM, LANES), lambda i, p1, p2: (i, 0)),
                pl.BlockSpec((1, D), lambda i, p1, p2: (0, 0)),
                pl.BlockSpec(memory_space=pl.ANY),
            ],
            out_specs=pl.BlockSpec((TM, D), lambda i, p1, p2: (i, 0)),
            scratch_shapes=[pltpu.VMEM((2, 2, TM) + ROW_TILE, F32), pltpu.SemaphoreType.DMA((2,))],
        ),
        out_shape=jax.ShapeDtypeStruct((n_tiles * TM, D), F32),
        compiler_params=_cparams(("arbitrary",)),
        name="moe_gather",
    )(p1, p2, xa, mod, tmeta, final_g, ys)


def _issue_row_gathers(p1_ref, p2_ref, ys_ref, buf, sem, tile, s, unrolled):
    base = tile * TM

    def one(r):
        pltpu.make_async_copy(ys_ref.at[p1_ref[base + r]], buf.at[s, 0, r], sem.at[s]).start()
        pltpu.make_async_copy(ys_ref.at[p2_ref[base + r]], buf.at[s, 1, r],
                              sem.at[s]).start(priority=1)

    if unrolled:
        for r in range(TM):
            one(r)
    else:
        def body(r, carry):
            one(r)
            return carry

        lax.fori_loop(0, TM, body, 0, unroll=ISSUE_UNROLL)


def _inproj_kernel(p1_ref, p2_ref, x_ref, mod0_ref, tm_ref, ys_ref, mod_ref, g_ref, win_ref,
                   xn_ref, xm_ref, op_ref, buf, sem):
    i = pl.program_id(0)
    n_steps = pl.num_programs(0)
    slot = lax.rem(i, 2)

    @pl.when(i == 0)
    def _():
        _issue_row_gathers(p1_ref, p2_ref, ys_ref, buf, sem, 0, 0, False)

    _row_copy_wait(buf.at[slot, 0], sem.at[slot])
    _row_copy_wait(buf.at[slot, 1], sem.at[slot])
    y = (tm_ref[:, 0:1] * buf[slot, 0].reshape(TM, D)
         + tm_ref[:, 1:2] * buf[slot, 1].reshape(TM, D))
    xn = x_ref[...] + mod0_ref[0, 5:6, :] * y
    xn_ref[...] = xn
    _issue_row_gathers(p1_ref, p2_ref, ys_ref, buf, sem, jnp.minimum(i + 1, n_steps - 1), 1 - slot,
                       True)
    h = _norm_mod(xn, g_ref[...], mod_ref[0, 0:1, :], mod_ref[0, 1:2, :])
    hb = h.astype(BF16)
    xm_ref[...] = jnp.dot(hb, win_ref[:, :B_INNER], preferred_element_type=F32).astype(BF16)
    op_ref[...] = jnp.dot(hb, win_ref[:, B_INNER:], preferred_element_type=F32).astype(BF16)

    @pl.when(i == n_steps - 1)
    def _():
        _row_copy_wait(buf.at[1 - slot, 0], sem.at[1 - slot])
        _row_copy_wait(buf.at[1 - slot, 1], sem.at[1 - slot])


def _inproj(p1, p2, xa, mod0, tmeta, ys, mod, g, w_in):
    const = lambda i, p1, p2: (0, 0)
    row = lambda i, p1, p2: (i, 0)
    mod_map = lambda i, p1, p2: (_mod_row(i), 0, 0)
    return pl.pallas_call(
        _inproj_kernel,
        grid_spec=pltpu.PrefetchScalarGridSpec(
            num_scalar_prefetch=2,
            grid=(ALL_TILES,),
            in_specs=[
                pl.BlockSpec((TM, D), row),
                pl.BlockSpec((1, 6, D), mod_map),
                pl.BlockSpec((TM, LANES), row),
                pl.BlockSpec(memory_space=pl.ANY),
                pl.BlockSpec((1, 6, D), mod_map),
                pl.BlockSpec((1, D), const),
                pl.BlockSpec((D, 2 * B_INNER), const),
            ],
            out_specs=[pl.BlockSpec((TM, D), row),
                       pl.BlockSpec((TM, B_INNER), row),
                       pl.BlockSpec((TM, B_INNER), row)],
            scratch_shapes=[pltpu.VMEM((2, 2, TM) + ROW_TILE, F32), pltpu.SemaphoreType.DMA((2,))],
        ),
        out_shape=[jax.ShapeDtypeStruct((N_ALL, D), F32),
                   jax.ShapeDtypeStruct((N_ALL, B_INNER), BF16),
                   jax.ShapeDtypeStruct((N_ALL, B_INNER), BF16)],
        compiler_params=_cparams(("arbitrary",)),
        name="mlstm_inproj",
    )(p1, p2, xa, mod0, tmeta, ys, mod, g, w_in)


def _log_sigmoid(x):
    return jnp.minimum(x, 0.0) - jnp.log(1.0 + jnp.exp(-jnp.abs(x)))


def _qkv_kernel(xm_ref, prev_ref, next_ref, cw_ref, cb_ref, wq_ref, wkt_ref, wv_ref, wg_ref, wgt_ref,
                bg_ref, bgt_ref, xc_ref, q_ref, kt_ref, v_ref, gt_ref, gtt_ref, tot_ref):
    i = pl.program_id(0)
    lat = i < LAT_TILES
    first = jnp.where(lat, i % TILES_PER_SEQ == 0, True)
    last = jnp.where(lat, i % TILES_PER_SEQ == TILES_PER_SEQ - 1, True)
    xmb = xm_ref[...]
    prev = jnp.where(first, 0.0, prev_ref[HALO - 8:, :].astype(F32))
    nxt = jnp.where(last, 0.0, next_ref[0:8, :].astype(F32))
    half = CONV_K // 2

    rr = lax.broadcasted_iota(jnp.int32, (TM, TM), 0)
    cc = lax.broadcasted_iota(jnp.int32, (TM, TM), 1)
    shifts = {t: (cc == rr + (t - half)).astype(BF16) for t in range(CONV_K) if t != half}

    def conv_head(h):
        hs = slice(h * DH, (h + 1) * DH)
        xh = xmb[:, hs]

        def conv_rows(ext):
            out = jnp.zeros((8, DH), F32) + cb_ref[:, hs]
            for t in range(CONV_K):
                out = out + ext[8 + t - half:16 + t - half] * cw_ref[t:t + 1, hs]
            return out

        acc = xh.astype(F32) * cw_ref[half:half + 1, hs] + cb_ref[:, hs]
        for t, shift in shifts.items():
            acc = acc + jnp.dot(shift, xh, preferred_element_type=F32) * cw_ref[t:t + 1, hs]
        top = conv_rows(jnp.concatenate([prev[:, hs], xh[0:16].astype(F32)], axis=0))
        bot = conv_rows(jnp.concatenate([xh[TM - 16:].astype(F32), nxt[:, hs]], axis=0))
        acc = jnp.concatenate([top, acc[8:TM - 8], bot], axis=0)
        xcb = (acc * jax.nn.sigmoid(acc)).astype(BF16)
        xc_ref[:, hs] = xcb
        return xcb

    def project(h, xcb):
        qh = jnp.dot(xcb, wq_ref[h], preferred_element_type=F32)
        q_ref[:, h * DQK:(h + 1) * DQK] = (qh * (DQK ** -0.5)).astype(BF16)
        kt_ref[h * DQK:(h + 1) * DQK, :] = _dot_nt(wkt_ref[h], xcb).astype(BF16)

    gts = jnp.dot(xmb, wg_ref[...], preferred_element_type=F32) + bg_ref[...]
    gtt = _dot_nt(wgt_ref[...], xmb) + bgt_ref[...]

    pending = None
    for h in range(HEADS):
        hs = slice(h * DH, (h + 1) * DH)
        xcb = conv_head(h)
        v_ref[:, hs] = jnp.dot(xmb[:, hs], wv_ref[h], preferred_element_type=F32).astype(BF16)
        if pending is not None:
            project(*pending)
        pending = (h, xcb)
    project(*pending)
    lane = lax.broadcasted_iota(jnp.int32, (CHUNK, LANES), 1)
    lane_kind = (lane >> 3) & 3
    row = lax.broadcasted_iota(jnp.int32, (4 * HEADS, CHUNK), 0)
    row_kind = (row >> 3) & 3
    r = lax.broadcasted_iota(jnp.int32, (CHUNK, CHUNK), 0)
    c = lax.broadcasted_iota(jnp.int32, (CHUNK, CHUNK), 1)
    lower = (c <= r).astype(BF16)
    upper = (c >= r).astype(BF16)
    row_sums = jnp.concatenate([upper, lower, jnp.ones((CHUNK, LANES), BF16)], axis=1)
    for ch in range(TM // CHUNK):
        ts = slice(ch * CHUNK, (ch + 1) * CHUNK)
        g = gts[ts]
        g = jnp.where((lane_kind & 1) == 1, _log_sigmoid(g), g)
        terms = _bf16_terms(g, 3)
        pre = sum(_dot_nn(lower, t) for t in terms)
        suf = sum(_dot_nn(upper, t) for t in terms)
        gt_ref[ts, :] = jnp.where(lane_kind == 1, pre, jnp.where(lane_kind == 3, suf, g))
        gt = gtt[:, ts]
        gt = jnp.where((row_kind & 1) == 1, _log_sigmoid(gt), gt)
        sums = _dot_exact_rhs(gt, row_sums)
        gtt_ref[:, ts] = jnp.where(row_kind == 1, sums[:, :CHUNK],
                                   jnp.where(row_kind == 3, sums[:, CHUNK:2 * CHUNK], gt))
        tot_ref[:, ts] = sums[:, 2 * CHUNK:]


def _qkv(xm, conv_w, conv_b, wq, wkt, wv, wg, wgt, bg, bgt):
    const = lambda i: (0, 0)
    const3 = lambda i: (0, 0, 0)
    hb = TM // HALO
    n_hb = N_ALL // HALO
    return pl.pallas_call(
        _qkv_kernel,
        grid=(ALL_TILES,),
        in_specs=[
            pl.BlockSpec((TM, B_INNER), lambda i: (i, 0)),
            pl.BlockSpec((HALO, B_INNER), lambda i: (jnp.maximum(i * hb - 1, 0), 0)),
            pl.BlockSpec((HALO, B_INNER), lambda i: (jnp.minimum((i + 1) * hb, n_hb - 1), 0)),
            pl.BlockSpec((CONV_K, B_INNER), const),
            pl.BlockSpec((1, B_INNER), const),
            pl.BlockSpec((HEADS, DH, DQK), const3),
            pl.BlockSpec((HEADS, DQK, DH), const3),
            pl.BlockSpec((HEADS, DH, DV), const3),
            pl.BlockSpec((B_INNER, LANES), const),
            pl.BlockSpec((4 * HEADS, B_INNER), const),
            pl.BlockSpec((1, LANES), const),
            pl.BlockSpec((4 * HEADS, 1), const),
        ],
        out_specs=[
            pl.BlockSpec((TM, B_INNER), lambda i: (i, 0)),
            pl.BlockSpec((TM, HEADS * DQK), lambda i: (i, 0)),
            pl.BlockSpec((HEADS * DQK, TM), lambda i: (0, i)),
            pl.BlockSpec((TM, B_INNER), lambda i: (i, 0)),
            pl.BlockSpec((TM, LANES), lambda i: (i, 0)),
            pl.BlockSpec((4 * HEADS, TM), lambda i: (0, i)),
            pl.BlockSpec((4 * HEADS, TM), lambda i: (0, i)),
        ],
        out_shape=[
            jax.ShapeDtypeStruct((N_ALL, B_INNER), BF16),
            jax.ShapeDtypeStruct((N_ALL, HEADS * DQK), BF16),
            jax.ShapeDtypeStruct((HEADS * DQK, N_ALL), BF16),
            jax.ShapeDtypeStruct((N_ALL, B_INNER), BF16),
            jax.ShapeDtypeStruct((N_ALL, LANES), F32),
            jax.ShapeDtypeStruct((4 * HEADS, N_ALL), F32),
            jax.ShapeDtypeStruct((4 * HEADS, N_ALL), F32),
        ],
        compiler_params=_cparams(("parallel",)),
        name="mlstm_qkv",
    )(xm, xm, xm, conv_w, conv_b, wq, wkt, wv, wg, wgt, bg, bgt)


CTX_CHUNKS = CTX // CHUNK
LAT_CHUNKS = SEQ // CHUNK
SCAN_STEPS = CTX_CHUNKS + LAT_CHUNKS


def _scan_dir(nb, d, q_ref, kt_ref, v_ref, g_ref, gt_ref, tot_ref, o_ref, c_scr, m_scr):
    r = lax.broadcasted_iota(jnp.int32, (CHUNK, CHUNK), 0)
    c = lax.broadcasted_iota(jnp.int32, (CHUNK, CHUNK), 1)
    if d == 0:
        keep = c <= r
    else:
        keep = c >= r
    ones_blk = jnp.ones((CHUNK, LANES), BF16)
    bcol_all = g_ref[...]
    gtt = gt_ref[...]
    tot_all = tot_ref[...]
    base = 2 * HEADS * d

    def head_first(h):
        li_r = gtt[base + h:base + h + 1, :]
        b_r = gtt[base + HEADS + h:base + HEADS + h + 1, :]
        b_last = tot_all[base + HEADS + h:base + HEADS + h + 1, :]
        m_old = m_scr[nb, d, h]
        c_old = c_scr[nb, d, h]
        qh = q_ref[:, h * DQK:(h + 1) * DQK]
        kth = kt_ref[h * DQK:(h + 1) * DQK, :]
        vaug = jnp.concatenate([v_ref[:, h * DV:(h + 1) * DV], ones_blk], axis=1)
        g_r = b_last - b_r + li_r
        m_new = jnp.maximum(b_last + m_old, jnp.max(g_r, axis=-1, keepdims=True))
        decay = jnp.exp(b_last + m_old - m_new)
        wk = jnp.exp(g_r - m_new)
        kwt = (kth.astype(F32) * wk).astype(BF16)
        qk = jnp.dot(qh, kth, preferred_element_type=F32)
        qc = jnp.dot(qh, c_old.astype(BF16), preferred_element_type=F32)
        c_scr[nb, d, h] = jnp.concatenate([decay] * 3, axis=1) * c_old + jnp.dot(
            kwt, vaug, preferred_element_type=F32)
        m_scr[nb, d, h] = m_new
        return h, li_r, b_r, m_old, vaug, qk, qc

    def head_second(h, li_r, b_r, m_old, vaug, qk, qc):
        b_full = jnp.broadcast_to(bcol_all[:, base + HEADS + h:base + HEADS + h + 1], (CHUNK, LANES))
        dmat = jnp.where(keep, b_full - (b_r - li_r), NEG_INF)
        inter = b_full + m_old
        m_t = jnp.maximum(inter, jnp.max(dmat, axis=-1, keepdims=True))
        p = jnp.exp(dmat - m_t)
        s = qk * p
        a = jnp.exp(inter - m_t)
        num = jnp.concatenate([a] * 3, axis=1) * qc + jnp.dot(
            s.astype(BF16), vaug, preferred_element_type=F32)
        inv = 1.0 / jnp.maximum(jnp.abs(num[:, DV:]), jnp.exp(-m_t))
        o_ref[nb, :, h * DV:(h + 1) * DV] = (
            num[:, :DV] * jnp.concatenate([inv] * 2, axis=1)).astype(BF16)

    return head_first, head_second


SCAN_NB = 2
SCAN_IN = 6
SCAN_SKEW = 1


def _scan_kernel(*refs):
    n_in = SCAN_NB * 2 * SCAN_IN
    ins, outs = refs[:n_in], refs[n_in:n_in + 2]
    c_scr, m_scr = refs[n_in + 2:]

    @pl.when(pl.program_id(1) == 0)
    def _():
        c_scr[...] = jnp.zeros_like(c_scr)
        m_scr[...] = jnp.zeros_like(m_scr)

    units = []
    for nb in range(SCAN_NB):
        for d in range(2):
            k = nb * 2 + d
            first, second = _scan_dir(nb, d, *ins[k * SCAN_IN:(k + 1) * SCAN_IN], outs[d], c_scr, m_scr)
            units += [(first, second, h) for h in range(HEADS)]
    pending = []
    for first, second, h in units:
        pending.append((second, first(h)))
        if len(pending) > SCAN_SKEW:
            fn, old_vals = pending.pop(0)
            fn(*old_vals)
    for fn, old_vals in pending:
        fn(*old_vals)


def _scan(q, kt, v, gts, gtt, tot):
    lat_blk = N_LAT // CHUNK

    def fwd_in(b, j):
        return jnp.where(j < CTX_CHUNKS, lat_blk + b * CTX_CHUNKS + j, b * LAT_CHUNKS + j - CTX_CHUNKS)

    def bwd_in(b, j):
        return jnp.where(j < CTX_CHUNKS, lat_blk + b * CTX_CHUNKS + (CTX_CHUNKS - 1 - j),
                         b * LAT_CHUNKS + (SCAN_STEPS - 1 - j))

    def fwd_out(b, j):
        return b * LAT_CHUNKS + jnp.maximum(j - CTX_CHUNKS, 0)

    def bwd_out(b, j):
        return b * LAT_CHUNKS + jnp.minimum(SCAN_STEPS - 1 - j, LAT_CHUNKS - 1)

    n_g = BATCH // SCAN_NB

    def specs(fn, nb):
        blk = lambda g, j: fn(nb * n_g + g, j)
        return [
            pl.BlockSpec((CHUNK, HEADS * DQK), lambda g, j: (blk(g, j), 0)),
            pl.BlockSpec((HEADS * DQK, CHUNK), lambda g, j: (0, blk(g, j))),
            pl.BlockSpec((CHUNK, B_INNER), lambda g, j: (blk(g, j), 0)),
            pl.BlockSpec((CHUNK, LANES), lambda g, j: (blk(g, j), 0)),
            pl.BlockSpec((4 * HEADS, CHUNK), lambda g, j: (0, blk(g, j))),
            pl.BlockSpec((4 * HEADS, CHUNK), lambda g, j: (0, blk(g, j))),
        ]

    def out_spec(fn):
        return pl.BlockSpec((SCAN_NB, CHUNK, B_INNER), lambda g, j: (0, fn(g, j), 0))

    in_specs, operands = [], []
    for nb in range(SCAN_NB):
        for fn_in in (fwd_in, bwd_in):
            in_specs += specs(fn_in, nb)
            operands += [q, kt, v, gts, gtt, tot]
    h_shape = jax.ShapeDtypeStruct((SCAN_NB, N_LAT // SCAN_NB, B_INNER), BF16)
    hf, hb = pl.pallas_call(
        _scan_kernel,
        grid=(n_g, SCAN_STEPS),
        in_specs=in_specs,
        out_specs=[out_spec(fwd_out), out_spec(bwd_out)],
        out_shape=[h_shape, h_shape],
        scratch_shapes=[pltpu.VMEM((SCAN_NB, 2, HEADS, DQK, DV + LANES), F32),
                        pltpu.VMEM((SCAN_NB, 2, HEADS, 1, LANES), F32)],
        compiler_params=_cparams(("parallel", "arbitrary")),
        name="mlstm_scan",
    )(*operands)
    return hf.reshape(N_LAT, B_INNER), hb.reshape(N_LAT, B_INNER)


def _combine_kernel(x_ref, mod_ref, hf_ref, hb_ref, xc_ref, op_ref, hg_ref, sk_ref, wout_ref,
                    g2_ref, wr_ref, br_ref, o_ref, meta_ref, tmeta_ref, cnt_ref, cntr_ref,
                    y_scr, carry, carry_r):
    for h in range(HEADS):
        hs = slice(h * DV, (h + 1) * DV)
        s = hf_ref[:, hs].astype(F32) + hb_ref[:, hs].astype(F32)
        s = s * lax.rsqrt(jnp.mean(s * s, axis=-1, keepdims=True) + EPS)
        y = jax.nn.sigmoid(op_ref[:, hs].astype(F32)) * (
            s * hg_ref[:, hs] + sk_ref[:, hs] * xc_ref[:, hs].astype(F32))
        y_scr[:, hs] = y.astype(BF16)
        if h % 2 == 1:
            ks = slice((h - 1) * DV, (h + 1) * DV)
            part = jnp.dot(y_scr[:, ks], wout_ref[ks, :], preferred_element_type=F32)
            out = part if h == 1 else out + part
    xn = x_ref[...] + mod_ref[0, 2:3, :] * out
    o_ref[...] = xn
    _route_tile(xn, mod_ref, g2_ref, wr_ref, br_ref, meta_ref, tmeta_ref, cnt_ref, cntr_ref,
                carry, carry_r)


def _combine(xa, mod, hf, hb, xc, op, head_g, skip, w_out, g2, wr, br):
    const = lambda i: (0, 0)
    row = lambda i: (i, 0)
    r_in, r_out, r_shape, r_scratch = _router_specs(LAT_TILES)
    return pl.pallas_call(
        _combine_kernel,
        grid=(LAT_TILES,),
        in_specs=[
            pl.BlockSpec((TM, D), row),
            pl.BlockSpec((1, 6, D), lambda i: (_mod_row(i), 0, 0)),
            pl.BlockSpec((TM, B_INNER), row),
            pl.BlockSpec((TM, B_INNER), row),
            pl.BlockSpec((TM, B_INNER), row),
            pl.BlockSpec((TM, B_INNER), row),
            pl.BlockSpec((1, B_INNER), const),
            pl.BlockSpec((1, B_INNER), const),
            pl.BlockSpec((B_INNER, D), const),
        ] + r_in,
        out_specs=[pl.BlockSpec((TM, D), row)] + r_out,
        out_shape=[jax.ShapeDtypeStruct((N_LAT, D), F32)] + r_shape,
        scratch_shapes=[pltpu.VMEM((TM, B_INNER), BF16)] + r_scratch,
        compiler_params=_cparams(("arbitrary",)),
        name="mlstm_combine",
    )(xa, mod, hf, hb, xc, op, head_g, skip, w_out, g2, wr, br)


def _router_weights(w_grp, b_grp, w_exp, b_exp):
    wr = jnp.zeros((LANES, D), F32).at[:N_EXP].set(w_exp.T).at[N_EXP:N_EXP + N_GRP].set(w_grp.T)
    br = jnp.zeros((LANES, 1), F32).at[:N_EXP, 0].set(b_exp).at[N_EXP:N_EXP + N_GRP, 0].set(b_grp)
    return wr.astype(BF16), br


PLAN_BLK = 2048
PLAN_PAIRS = 256


def _plan_kernel(meta_ref, cc_ref, cr_ref, pp_ref, plan_ref):
    row = lax.broadcasted_iota(jnp.int32, (LANES, LANES), 0)
    col = lax.broadcasted_iota(jnp.int32, (LANES, LANES), 1)
    cnt_c = cc_ref[...]
    cnt_r = cr_ref[0:1, :]
    starts_c = jnp.sum(jnp.where(col < row, cnt_r, 0.0), axis=1, keepdims=True)
    erow = lax.broadcasted_iota(jnp.int32, (LANES, PLAN_BLK), 0).astype(F32)
    for j in range(2):
        pos = jnp.sum(jnp.where(erow == meta_ref[j:j + 1, :], starts_c, 0.0), axis=0,
                      keepdims=True) + meta_ref[j + 2:j + 3, :]
        pp_ref[j:j + 1, :] = pos.astype(jnp.int32)
    pp_ref[2:8, :] = jnp.zeros((6, PLAN_BLK), jnp.int32)

    @pl.when(pl.program_id(0) == 0)
    def _():
        ends_c = starts_c + cnt_c
        starts_r = jnp.sum(jnp.where(row < col, cnt_c, 0.0), axis=0, keepdims=True)
        ends_r = starts_r + cnt_r

        def tiles(st, en, cn):
            first_tile = jnp.floor(st * (1.0 / TM))
            n = jnp.where(cn > 0.0, jnp.floor((en - 1.0) * (1.0 / TM)) - first_tile + 1.0, 0.0)
            return first_tile, n

        ft_c, pairs_c = tiles(starts_c, ends_c, cnt_c)
        _, pairs_r = tiles(starts_r, ends_r, cnt_r)
        pend_c = jnp.sum(jnp.where(col <= row, pairs_r, 0.0), axis=1, keepdims=True)
        pstart_c = pend_c - pairs_c
        total = jnp.sum(pairs_r, axis=1, keepdims=True)
        q = lax.broadcasted_iota(jnp.int32, (1, PLAN_PAIRS), 1).astype(F32)
        erow_p = lax.broadcasted_iota(jnp.int32, (LANES, PLAN_PAIRS), 0).astype(F32)

        def at(qv):
            qc = jnp.maximum(jnp.minimum(qv, total - 1.0), 0.0)
            e = jnp.sum(jnp.where(qc >= pend_c, 1.0, 0.0), axis=0, keepdims=True)
            oh = erow_p == e
            tile = jnp.sum(jnp.where(oh, ft_c - pstart_c, 0.0), axis=0, keepdims=True) + qc
            return e, oh, tile

        e_q, oh, tile_q = at(q)
        e_p, _, tile_p = at(q - 1.0)
        lo = jnp.sum(jnp.where(oh, starts_c, 0.0), axis=0, keepdims=True) - tile_q * TM
        hi = jnp.sum(jnp.where(oh, ends_c, 0.0), axis=0, keepdims=True) - tile_q * TM
        rows = (tile_q, e_q, jnp.clip(lo, 0.0, TM), jnp.clip(hi, 0.0, TM),
                jnp.where((q == 0.0) | (tile_q != tile_p), 1.0, 0.0),
                jnp.where((q == 0.0) | (e_q != e_p), 1.0, 0.0),
                jnp.where(q < total, 1.0, 0.0),
                jnp.zeros((1, PLAN_PAIRS), F32))
        for j, v in enumerate(rows):
            plan_ref[j:j + 1, :] = v.astype(jnp.int32)


def _route_plan(meta, cnt_c, cnt_r, n_rows):
    n_pairs = 2 * n_rows // TM + N_EXP - 1
    assert n_pairs <= PLAN_PAIRS and n_rows % PLAN_BLK == 0
    const = lambda i: (0, 0)
    pp, plan = pl.pallas_call(
        _plan_kernel,
        grid=(n_rows // PLAN_BLK,),
        in_specs=[pl.BlockSpec((8, PLAN_BLK), lambda i: (0, i)),
                  pl.BlockSpec((LANES, 1), const),
                  pl.BlockSpec((8, LANES), const)],
        out_specs=[pl.BlockSpec((8, PLAN_BLK), lambda i: (0, i)),
                   pl.BlockSpec((8, PLAN_PAIRS), const)],
        out_shape=[jax.ShapeDtypeStruct((8, n_rows), jnp.int32),
                   jax.ShapeDtypeStruct((8, PLAN_PAIRS), jnp.int32)],
        compiler_params=_cparams(("arbitrary",)),
        name="moe_plan",
    )(meta, cnt_c, cnt_r)
    return pp[0], pp[1], tuple(plan[j, :n_pairs] for j in range(7))


def _moe_experts(xa, routing, mod, g2, w1, w3, w2, layer, n_rows):
    meta, _, cnt_c, cnt_r = routing
    p1, p2, plan = _route_plan(meta, cnt_c, cnt_r, n_rows)
    xs = _scatter(p1, p2, xa, mod, g2, n_rows // TM)
    return p1, p2, _grouped(plan, xs, w1, w3, w2, layer)


def kernel(x, c, ctx, c_ctx, norm_g, w_ada, b_ada, a_w_in, a_b_in, a_g_v, a_w_s, a_b_s, a_w_out,
           b_w_in, b_conv_w, b_conv_b, b_w_q, b_w_k, b_w_v, b_w_gate, b_b_gate, b_head_g, b_skip,
           b_w_out, moe_w_grp, moe_b_grp, moe_w_exp, moe_b_exp, moe_w1, moe_w3, moe_w2, final_g):
    cc = jnp.zeros((MOD_ROWS, D), F32).at[:BATCH].set(c).at[BATCH].set(c_ctx)
    mods = _ada(cc, w_ada, b_ada).reshape(2, MOD_ROWS, 6, D)

    mod = mods[0]
    b_s_full = jnp.repeat(a_b_s[0].T, A_GC, axis=1)
    g2 = norm_g[0, 1].reshape(1, D)
    wr, br = _router_weights(moe_w_grp[0], moe_b_grp[0], moe_w_exp[0], moe_b_exp[0])
    xa, *routing = _gmlp(x.reshape(N_LAT, D), ctx.reshape(N_CTX, D), mod, norm_g[0, 0].reshape(1, D),
                         a_w_in[0].astype(BF16), a_b_in[0].reshape(1, -1), a_g_v[0].reshape(1, -1),
                         a_w_s[0].astype(BF16), b_s_full, a_w_out[0].astype(BF16), g2, wr, br)
    p1, p2, ys = _moe_experts(xa, routing, mod, g2, moe_w1, moe_w3, moe_w2, 0, N_ALL)

    mod = mods[1]
    xa, xm, op = _inproj(p1, p2, xa, mods[0], routing[1], ys, mod, norm_g[1, 0].reshape(1, D),
                         b_w_in[0].astype(BF16))
    wg = jnp.zeros((B_INNER, LANES), F32).at[:, :4 * HEADS].set(b_w_gate[0]).astype(BF16)
    bg = jnp.zeros((1, LANES), F32).at[0, :4 * HEADS].set(b_b_gate[0])
    xc, q, kt, v, gts, gtt, tot = _qkv(xm, b_conv_w[0], b_conv_b[0].reshape(1, -1),
                                  b_w_q[0].astype(BF16),
                                  jnp.transpose(b_w_k[0], (0, 2, 1)).astype(BF16),
                                  b_w_v[0].astype(BF16), wg, b_w_gate[0].T.astype(BF16), bg,
                                  b_b_gate[0].reshape(-1, 1))
    hf, hb = _scan(q, kt, v, gts, gtt, tot)
    g2 = norm_g[1, 1].reshape(1, D)
    wr, br = _router_weights(moe_w_grp[1], moe_b_grp[1], moe_w_exp[1], moe_b_exp[1])
    xl, *routing = _combine(xa, mod, hf, hb, xc, op, b_head_g[0].reshape(1, -1),
                            b_skip[0].reshape(1, -1), b_w_out[0].astype(BF16), g2, wr, br)
    p1, p2, ys = _moe_experts(xl, routing, mod, g2, moe_w1, moe_w3, moe_w2, 1, N_LAT)
    out = _gather(p1, p2, xl, mod, routing[1], final_g.reshape(1, D), ys, LAT_TILES, True)
    return out.reshape(BATCH, SEQ, D)
```

```python
import functools

import jax
import jax.numpy as jnp
from jax import lax
from jax.experimental import pallas as pl
from jax.experimental.pallas import tpu as pltpu

F32 = jnp.float32
BF16 = jnp.bfloat16

D = 1024
BATCH = 8
SEQ = 2048
CTX = 256
EPS = 1e-6
NEG_INF = -1e30
N_LAT = BATCH * SEQ
N_CTX = BATCH * CTX
N_ALL = N_LAT + N_CTX

TM = 256
LAT_TILES = N_LAT // TM
ALL_TILES = N_ALL // TM
TILES_PER_SEQ = SEQ // TM
MOD_ROWS = 16

CHUNK = 128
A_HALF = 2048
A_GROUPS = 8
A_GC = A_HALF // A_GROUPS
B_INNER = 2048
HEADS = 8
DH = B_INNER // HEADS
DQK = DH // 2
DV = DH
CONV_K = 5
HALO = 16
N_EXP = 32
N_GRP = 4
EXP_PER_GRP = 8
D_EXP = 512
LANES = 128

VMEM_LIMIT = 56 * 1024 * 1024


def _cparams(sem):
    return pltpu.CompilerParams(dimension_semantics=sem, vmem_limit_bytes=VMEM_LIMIT)


def _mod_row(i):
    return jnp.where(i < LAT_TILES, i // TILES_PER_SEQ, BATCH)


def _norm_mod(x, g, shift, scale):
    y = x * lax.rsqrt(jnp.mean(x * x, axis=-1, keepdims=True) + EPS) * g
    return y * (1.0 + scale) + shift


def _bf16_terms(x, n):
    terms = []
    r = x
    for _ in range(n):
        t = r.astype(BF16)
        terms.append(t)
        r = r - t.astype(F32)
    return terms


def _dot_nt(a, b):
    return lax.dot_general(a, b, (((1,), (1,)), ((), ())), preferred_element_type=F32)


def _dot_nn(a, b):
    return jnp.dot(a, b, preferred_element_type=F32)


def _dot3(a, b, dot=_dot_nn):
    a1, a2 = _bf16_terms(a, 2)
    b1, b2 = _bf16_terms(b, 2)
    return dot(a1, b1) + (dot(a1, b2) + dot(a2, b1))


def _dot_exact_lhs(a_bf16, b, dot=_dot_nn):
    return sum(dot(a_bf16, t) for t in _bf16_terms(b, 3))


def _dot_exact_rhs(a, b_bf16, dot=_dot_nn):
    return sum(dot(t, b_bf16) for t in _bf16_terms(a, 3))


def _gelu_tanh(x):
    half = 0.5 * x
    t = jnp.tanh(x * (0.7978845608028654 + 0.035677408136300125 * (x * x)))
    return half + half * t


ADA_BN = 1536


def _ada_kernel(c_ref, w_ref, b_ref, o_ref):
    c = c_ref[...]
    a = c * jax.nn.sigmoid(c)
    o_ref[0] = _dot3(a, w_ref[0]) + b_ref[0]


def _ada(cc, w_ada, b_ada):
    depth = w_ada.shape[0]
    return pl.pallas_call(
        _ada_kernel,
        grid=(depth, 6 * D // ADA_BN),
        in_specs=[
            pl.BlockSpec((MOD_ROWS, D), lambda l, j: (0, 0)),
            pl.BlockSpec((1, D, ADA_BN), lambda l, j: (l, 0, j)),
            pl.BlockSpec((1, 1, ADA_BN), lambda l, j: (l, 0, j)),
        ],
        out_specs=pl.BlockSpec((1, MOD_ROWS, ADA_BN), lambda l, j: (l, 0, j)),
        out_shape=jax.ShapeDtypeStruct((depth, MOD_ROWS, 6 * D), F32),
        compiler_params=_cparams(("parallel", "parallel")),
        name="ada",
    )(cc, w_ada, b_ada.reshape(depth, 1, 6 * D))


GM_CH = 512


def _gmlp_kernel(xl_ref, xc_ref, mod_ref, g_ref, win_ref, bin_ref, gv_ref, ws_ref, bs_ref, wout_ref,
                 g2_ref, wr_ref, br_ref, o_ref, meta_ref, tmeta_ref, cnt_ref, cntr_ref,
                 z_scr, y_scr, carry, carry_r):
    x = jnp.where(pl.program_id(0) < LAT_TILES, xl_ref[...], xc_ref[...])
    h = _norm_mod(x, g_ref[...], mod_ref[0, 0:1, :], mod_ref[0, 1:2, :])
    hb = h.astype(BF16)
    s1 = jnp.zeros((TM, 1), F32)
    s2 = jnp.zeros((TM, 1), F32)
    n_ch = 2 * A_HALF // GM_CH
    for j in list(range(n_ch // 2, n_ch)) + list(range(n_ch // 2)):
        cs = slice(j * GM_CH, (j + 1) * GM_CH)
        zc = jnp.dot(hb, win_ref[:, cs], preferred_element_type=F32) + bin_ref[:, cs]
        zc = _gelu_tanh(zc)
        z_scr[:, cs] = zc
        if j * GM_CH >= A_HALF:
            s1 = s1 + jnp.sum(zc, axis=-1, keepdims=True)
            s2 = s2 + jnp.sum(zc * zc, axis=-1, keepdims=True)
    mu = s1 * (1.0 / A_HALF)
    rstd = lax.rsqrt(s2 * (1.0 / A_HALF) - mu * mu + EPS)
    for c in range(TM // CHUNK):
        rs = slice(c * CHUNK, (c + 1) * CHUNK)
        for g in range(A_GROUPS):
            cs = slice(g * A_GC, (g + 1) * A_GC)
            vs = slice(A_HALF + g * A_GC, A_HALF + (g + 1) * A_GC)
            v = (z_scr[rs, vs] - mu[rs]) * rstd[rs] * gv_ref[:, cs]
            s = jnp.dot(ws_ref[g], v.astype(BF16), preferred_element_type=F32) + bs_ref[:, cs]
            y_scr[rs, cs] = (z_scr[rs, cs] * s).astype(BF16)
    out = jnp.dot(y_scr[...], wout_ref[...], preferred_element_type=F32)
    xn = x + mod_ref[0, 2:3, :] * out
    o_ref[...] = xn
    _route_tile(xn, mod_ref, g2_ref, wr_ref, br_ref, meta_ref, tmeta_ref, cnt_ref, cntr_ref,
                carry, carry_r)


def _gmlp(xl, xc, mod, g, w_in, b_in, g_v, w_s, b_s_full, w_out, g2, wr, br):
    const = lambda i: (0, 0)
    r_in, r_out, r_shape, r_scratch = _router_specs(ALL_TILES)
    return pl.pallas_call(
        _gmlp_kernel,
        grid=(ALL_TILES,),
        in_specs=[
            pl.BlockSpec((TM, D), lambda i: (jnp.minimum(i, LAT_TILES - 1), 0)),
            pl.BlockSpec((TM, D), lambda i: (jnp.maximum(i - LAT_TILES, 0), 0)),
            pl.BlockSpec((1, 6, D), lambda i: (_mod_row(i), 0, 0)),
            pl.BlockSpec((1, D), const),
            pl.BlockSpec((D, 2 * A_HALF), const),
            pl.BlockSpec((1, 2 * A_HALF), const),
            pl.BlockSpec((1, A_HALF), const),
            pl.BlockSpec((A_GROUPS, CHUNK, CHUNK), lambda i: (0, 0, 0)),
            pl.BlockSpec((CHUNK, A_HALF), const),
            pl.BlockSpec((A_HALF, D), const),
        ] + r_in,
        out_specs=[pl.BlockSpec((TM, D), lambda i: (i, 0))] + r_out,
        out_shape=[jax.ShapeDtypeStruct((N_ALL, D), F32)] + r_shape,
        scratch_shapes=[pltpu.VMEM((TM, 2 * A_HALF), F32), pltpu.VMEM((TM, A_HALF), BF16)] + r_scratch,
        compiler_params=_cparams(("arbitrary",)),
        name="gmlp",
    )(xl, xc, mod, g, w_in, b_in, g_v, w_s, b_s_full, w_out, g2, wr, br)


def _route_tile(x, mod_ref, g_ref, wr_ref, br_ref, meta_ref, tmeta_ref, cnt_ref, cntr_ref,
                carry, carry_r):
    @pl.when(pl.program_id(0) == 0)
    def _():
        carry[...] = jnp.zeros_like(carry)
        carry_r[...] = jnp.zeros_like(carry_r)

    h = _norm_mod(x, g_ref[...], mod_ref[0, 3:4, :], mod_ref[0, 4:5, :])
    lt = _dot_nt(wr_ref[...], h.astype(BF16)) + br_ref[...]
    e_t = lt[0:N_EXP]
    row8 = lax.broadcasted_iota(jnp.int32, (EXP_PER_GRP, TM), 0).astype(F32)
    g_t = jnp.where(row8 < N_GRP, lt[N_EXP:N_EXP + EXP_PER_GRP], -jnp.inf)
    gmax = jnp.max(g_t, axis=0, keepdims=True)
    p_g = 1.0 / jnp.sum(jnp.exp(g_t - gmax), axis=0, keepdims=True)
    g_idx = jnp.min(jnp.where(g_t == gmax, row8, float(EXP_PER_GRP)), axis=0, keepdims=True)
    sel = jnp.zeros((EXP_PER_GRP, TM), F32)
    for g in range(N_GRP):
        sel = sel + jnp.where(g_idx == g, e_t[g * EXP_PER_GRP:(g + 1) * EXP_PER_GRP], 0.0)
    m1 = jnp.max(sel, axis=0, keepdims=True)
    i1 = jnp.min(jnp.where(sel == m1, row8, float(EXP_PER_GRP)), axis=0, keepdims=True)
    sel2 = jnp.where(row8 == i1, -jnp.inf, sel)
    m2 = jnp.max(sel2, axis=0, keepdims=True)
    i2 = jnp.min(jnp.where(sel2 == m2, row8, float(EXP_PER_GRP)), axis=0, keepdims=True)
    e2 = jnp.exp(m2 - m1)
    w1 = p_g / (1.0 + e2)
    w2 = p_g * e2 / (1.0 + e2)
    row = lax.broadcasted_iota(jnp.int32, (LANES, TM), 0).astype(F32)
    id1 = g_idx * EXP_PER_GRP + i1
    id2 = g_idx * EXP_PER_GRP + i2
    oh1 = row == id1
    oh2 = row == id2
    oh = jnp.where(oh1, 1.0, 0.0) + jnp.where(oh2, 1.0, 0.0)
    before = (lax.broadcasted_iota(jnp.int32, (TM, TM), 0)
              < lax.broadcasted_iota(jnp.int32, (TM, TM), 1)).astype(BF16)
    ohb = oh.astype(BF16)
    tot = jnp.dot(ohb, before, preferred_element_type=F32) + carry[...]
    r1 = jnp.sum(jnp.where(oh1, tot, 0.0), axis=0, keepdims=True)
    r2 = jnp.sum(jnp.where(oh2, tot, 0.0), axis=0, keepdims=True)
    carry[...] += jnp.sum(oh, axis=-1, keepdims=True)
    carry_r[...] += _dot_nt(jnp.ones((8, TM), BF16), ohb)
    cnt_ref[...] = carry[...]
    cntr_ref[...] = carry_r[...]
    meta_ref[0:1, :] = id1
    meta_ref[1:2, :] = id2
    meta_ref[2:3, :] = r1
    meta_ref[3:4, :] = r2
    meta_ref[4:8, :] = jnp.zeros((4, TM), F32)
    tmeta_ref[...] = (jnp.where(row == 0.0, w1, 0.0) + jnp.where(row == 1.0, w2, 0.0)).T


def _router_specs(n_tiles):
    const = lambda i: (0, 0)
    n = n_tiles * TM
    in_specs = [pl.BlockSpec((1, D), const), pl.BlockSpec((LANES, D), const),
                pl.BlockSpec((LANES, 1), const)]
    out_specs = [pl.BlockSpec((8, TM), lambda i: (0, i)),
                 pl.BlockSpec((TM, LANES), lambda i: (i, 0)),
                 pl.BlockSpec((LANES, 1), const),
                 pl.BlockSpec((8, LANES), const)]
    out_shape = [jax.ShapeDtypeStruct((8, n), F32), jax.ShapeDtypeStruct((n, LANES), F32),
                 jax.ShapeDtypeStruct((LANES, 1), F32), jax.ShapeDtypeStruct((8, LANES), F32)]
    scratch = [pltpu.VMEM((LANES, 1), F32), pltpu.VMEM((8, LANES), F32)]
    return in_specs, out_specs, out_shape, scratch


ISSUE_UNROLL = 8
ROW_TILE = (8, LANES)
MXU_N = 256


def _row_copy_wait(buf_slot, sem_slot):
    pltpu.make_async_copy(buf_slot, buf_slot, sem_slot).wait()


def _scatter_kernel(p1_ref, p2_ref, x_ref, mod_ref, g_ref, xs_ref, buf, sem):
    i = pl.program_id(0)
    n_steps = pl.num_programs(0)
    slot = lax.rem(i, 2)

    def wait_slot(s):
        _row_copy_wait(buf.at[s], sem.at[s])
        _row_copy_wait(buf.at[s], sem.at[s])

    @pl.when(i >= 2)
    def _():
        wait_slot(slot)

    h = _norm_mod(x_ref[...], g_ref[...], mod_ref[0, 3:4, :], mod_ref[0, 4:5, :])
    buf[slot] = h.reshape(TM, *ROW_TILE)
    base = i * TM

    def body(r, carry):
        src = buf.at[slot, r]
        pltpu.make_async_copy(src, xs_ref.at[p1_ref[base + r]], sem.at[slot]).start()
        pltpu.make_async_copy(src, xs_ref.at[p2_ref[base + r]], sem.at[slot]).start(priority=1)
        return carry

    lax.fori_loop(0, TM, body, 0, unroll=ISSUE_UNROLL)

    @pl.when(i == n_steps - 1)
    def _():
        @pl.when(i >= 1)
        def _():
            wait_slot(1 - slot)

        wait_slot(slot)


def _scatter(p1, p2, xa, mod, g, n_tiles):
    n = n_tiles * TM
    return pl.pallas_call(
        _scatter_kernel,
        grid_spec=pltpu.PrefetchScalarGridSpec(
            num_scalar_prefetch=2,
            grid=(n_tiles,),
            in_specs=[
                pl.BlockSpec((TM, D), lambda i, p1, p2: (i, 0)),
                pl.BlockSpec((1, 6, D), lambda i, p1, p2: (_mod_row(i), 0, 0)),
                pl.BlockSpec((1, D), lambda i, p1, p2: (0, 0)),
            ],
            out_specs=pl.BlockSpec(memory_space=pl.ANY),
            scratch_shapes=[pltpu.VMEM((2, TM) + ROW_TILE, F32), pltpu.SemaphoreType.DMA((2,))],
        ),
        out_shape=jax.ShapeDtypeStruct((2 * n,) + ROW_TILE, F32),
        compiler_params=_cparams(("arbitrary",)),
        name="moe_scatter",
    )(p1, p2, xa, mod, g)


def _grouped_kernel(tile_ref, exp_ref, lo_ref, hi_ref, first_ref, newexp_ref, valid_ref, par_ref,
                    expa_ref, expb_ref, xs_ref, w1a_ref, w3a_ref, w2a_ref, w1b_ref, w3b_ref, w2b_ref,
                    ys_ref, wb1, wb3, wb2):
    q = pl.program_id(0)

    @pl.when(valid_ref[q] == 1)
    def _():
        for parity, (w1_ref, w3_ref, w2_ref) in enumerate(((w1a_ref, w3a_ref, w2a_ref),
                                                           (w1b_ref, w3b_ref, w2b_ref))):
            @pl.when((newexp_ref[q] == 1) & (par_ref[q] == parity))
            def _():
                wb1[...] = w1_ref[0, 0].astype(BF16)
                wb3[...] = w3_ref[0, 0].astype(BF16)
                wb2[...] = w2_ref[0, 0].astype(BF16)

        x = xs_ref[...].reshape(TM, D).astype(BF16)
        rows = lax.broadcasted_iota(jnp.int32, (TM, 1), 0)
        mine = (rows >= lo_ref[q]) & (rows < hi_ref[q])
        acts = []
        for j in range(D_EXP // MXU_N):
            cs = slice(j * MXU_N, (j + 1) * MXU_N)
            a = jnp.dot(x, wb1[:, cs], preferred_element_type=F32)
            b = jnp.dot(x, wb3[:, cs], preferred_element_type=F32)
            acts.append(jnp.where(mine, a * jax.nn.sigmoid(a) * b, 0.0).astype(BF16))
        act = jnp.concatenate(acts, axis=1)
        y = jnp.dot(act, wb2[...], preferred_element_type=F32).reshape(TM, *ROW_TILE)

        @pl.when(first_ref[q] == 1)
        def _():
            ys_ref[...] = y

        @pl.when(first_ref[q] == 0)
        def _():
            ys_ref[...] += y


def _grouped(plan, xs, w1, w3, w2, layer):
    n_pairs = plan[0].shape[0]
    tile_map = lambda q, tile, *_: (tile[q], 0, 0)
    a_map = lambda q, *plan: (layer, plan[8][q], 0, 0)
    b_map = lambda q, *plan: (layer, plan[9][q], 0, 0)
    return pl.pallas_call(
        _grouped_kernel,
        grid_spec=pltpu.PrefetchScalarGridSpec(
            num_scalar_prefetch=PLAN_FIELDS,
            grid=(n_pairs,),
            in_specs=[
                pl.BlockSpec((TM,) + ROW_TILE, tile_map),
                pl.BlockSpec((1, 1, D, D_EXP), a_map),
                pl.BlockSpec((1, 1, D, D_EXP), a_map),
                pl.BlockSpec((1, 1, D_EXP, D), a_map),
                pl.BlockSpec((1, 1, D, D_EXP), b_map),
                pl.BlockSpec((1, 1, D, D_EXP), b_map),
                pl.BlockSpec((1, 1, D_EXP, D), b_map),
            ],
            out_specs=pl.BlockSpec((TM,) + ROW_TILE, tile_map),
            scratch_shapes=[pltpu.VMEM((D, D_EXP), BF16), pltpu.VMEM((D, D_EXP), BF16),
                            pltpu.VMEM((D_EXP, D), BF16)],
        ),
        out_shape=jax.ShapeDtypeStruct(xs.shape, F32),
        compiler_params=_cparams(("arbitrary",)),
        name="moe_grouped",
    )(*plan, xs, w1, w3, w2, w1, w3, w2)


def _gather_kernel(p1_ref, p2_ref, x_ref, mod_ref, tm_ref, fg_ref, ys_ref, o_ref, buf, sem,
                   *, final_norm):
    i = pl.program_id(0)
    n_steps = pl.num_programs(0)
    slot = lax.rem(i, 2)

    def issue(tile, s):
        base = tile * TM

        def body(r, carry):
            pltpu.make_async_copy(ys_ref.at[p1_ref[base + r]], buf.at[s, 0, r], sem.at[s]).start()
            pltpu.make_async_copy(ys_ref.at[p2_ref[base + r]], buf.at[s, 1, r],
                                  sem.at[s]).start(priority=1)
            return carry

        lax.fori_loop(0, TM, body, 0, unroll=ISSUE_UNROLL)

    @pl.when(i == 0)
    def _():
        issue(0, 0)

    @pl.when(i + 1 < n_steps)
    def _():
        issue(i + 1, 1 - slot)

    _row_copy_wait(buf.at[slot, 0], sem.at[slot])
    _row_copy_wait(buf.at[slot, 1], sem.at[slot])
    y = (tm_ref[:, 0:1] * buf[slot, 0].reshape(TM, D)
         + tm_ref[:, 1:2] * buf[slot, 1].reshape(TM, D))
    out = x_ref[...] + mod_ref[0, 5:6, :] * y
    if final_norm:
        out = out * lax.rsqrt(jnp.mean(out * out, axis=-1, keepdims=True) + EPS) * fg_ref[...]
    o_ref[...] = out


def _gather(p1, p2, xa, mod, tmeta, final_g, ys, n_tiles, final_norm):
    return pl.pallas_call(
        functools.partial(_gather_kernel, final_norm=final_norm),
        grid_spec=pltpu.PrefetchScalarGridSpec(
            num_scalar_prefetch=2,
            grid=(n_tiles,),
            in_specs=[
                pl.BlockSpec((TM, D), lambda i, p1, p2: (i, 0)),
                pl.BlockSpec((1, 6, D), lambda i, p1, p2: (_mod_row(i), 0, 0)),
                pl.BlockSpec((TM, LANES), lambda i, p1, p2: (i, 0)),
                pl.BlockSpec((1, D), lambda i, p1, p2: (0, 0)),
                pl.BlockSpec(memory_space=pl.ANY),
            ],
            out_specs=pl.BlockSpec((TM, D), lambda i, p1, p2: (i, 0)),
            scratch_shapes=[pltpu.VMEM((2, 2, TM) + ROW_TILE, F32), pltpu.SemaphoreType.DMA((2,))],
        ),
        out_shape=jax.ShapeDtypeStruct((n_tiles * TM, D), F32),
        compiler_params=_cparams(("arbitrary",)),
        name="moe_gather",
    )(p1, p2, xa, mod, tmeta, final_g, ys)


def _issue_row_gathers(p1_ref, p2_ref, ys_ref, buf, sem, tile, s, unrolled):
    base = tile * TM

    def one(r):
        pltpu.make_async_copy(ys_ref.at[p1_ref[base + r]], buf.at[s, 0, r], sem.at[s]).start()
        pltpu.make_async_copy(ys_ref.at[p2_ref[base + r]], buf.at[s, 1, r],
                              sem.at[s]).start(priority=1)

    if unrolled:
        for r in range(TM):
            one(r)
    else:
        def body(r, carry):
            one(r)
            return carry

        lax.fori_loop(0, TM, body, 0, unroll=ISSUE_UNROLL)


def _inproj_kernel(p1_ref, p2_ref, x_ref, mod0_ref, tm_ref, ys_ref, mod_ref, g_ref, win_ref,
                   xn_ref, xm_ref, op_ref, buf, sem):
    i = pl.program_id(0)
    n_steps = pl.num_programs(0)
    slot = lax.rem(i, 2)

    @pl.when(i == 0)
    def _():
        _issue_row_gathers(p1_ref, p2_ref, ys_ref, buf, sem, 0, 0, False)

    _row_copy_wait(buf.at[slot, 0], sem.at[slot])
    _row_copy_wait(buf.at[slot, 1], sem.at[slot])
    y = (tm_ref[:, 0:1] * buf[slot, 0].reshape(TM, D)
         + tm_ref[:, 1:2] * buf[slot, 1].reshape(TM, D))
    xn = x_ref[...] + mod0_ref[0, 5:6, :] * y
    xn_ref[...] = xn
    _issue_row_gathers(p1_ref, p2_ref, ys_ref, buf, sem, jnp.minimum(i + 1, n_steps - 1), 1 - slot,
                       True)
    h = _norm_mod(xn, g_ref[...], mod_ref[0, 0:1, :], mod_ref[0, 1:2, :])
    hb = h.astype(BF16)
    xm_ref[...] = jnp.dot(hb, win_ref[:, :B_INNER], preferred_element_type=F32).astype(BF16)
    op_ref[...] = jnp.dot(hb, win_ref[:, B_INNER:], preferred_element_type=F32).astype(BF16)

    @pl.when(i == n_steps - 1)
    def _():
        _row_copy_wait(buf.at[1 - slot, 0], sem.at[1 - slot])
        _row_copy_wait(buf.at[1 - slot, 1], sem.at[1 - slot])


def _inproj(p1, p2, xa, mod0, tmeta, ys, mod, g, w_in):
    const = lambda i, p1, p2: (0, 0)
    row = lambda i, p1, p2: (i, 0)
    mod_map = lambda i, p1, p2: (_mod_row(i), 0, 0)
    return pl.pallas_call(
        _inproj_kernel,
        grid_spec=pltpu.PrefetchScalarGridSpec(
            num_scalar_prefetch=2,
            grid=(ALL_TILES,),
            in_specs=[
                pl.BlockSpec((TM, D), row),
                pl.BlockSpec((1, 6, D), mod_map),
                pl.BlockSpec((TM, LANES), row),
                pl.BlockSpec(memory_space=pl.ANY),
                pl.BlockSpec((1, 6, D), mod_map),
                pl.BlockSpec((1, D), const),
                pl.BlockSpec((D, 2 * B_INNER), const),
            ],
            out_specs=[pl.BlockSpec((TM, D), row),
                       pl.BlockSpec((TM, B_INNER), row),
                       pl.BlockSpec((TM, B_INNER), row)],
            scratch_shapes=[pltpu.VMEM((2, 2, TM) + ROW_TILE, F32), pltpu.SemaphoreType.DMA((2,))],
        ),
        out_shape=[jax.ShapeDtypeStruct((N_ALL, D), F32),
                   jax.ShapeDtypeStruct((N_ALL, B_INNER), BF16),
                   jax.ShapeDtypeStruct((N_ALL, B_INNER), BF16)],
        compiler_params=_cparams(("arbitrary",)),
        name="mlstm_inproj",
    )(p1, p2, xa, mod0, tmeta, ys, mod, g, w_in)


def _log_sigmoid(x):
    return jnp.minimum(x, 0.0) - jnp.log(1.0 + jnp.exp(-jnp.abs(x)))


def _qkv_kernel(xm_ref, prev_ref, next_ref, cw_ref, cb_ref, wq_ref, wkt_ref, wv_ref, wg_ref, wgt_ref,
                bg_ref, bgt_ref, xc_ref, q_ref, kt_ref, v_ref, gt_ref, gtt_ref, tot_ref):
    i = pl.program_id(0)
    lat = i < LAT_TILES
    first = jnp.where(lat, i % TILES_PER_SEQ == 0, True)
    last = jnp.where(lat, i % TILES_PER_SEQ == TILES_PER_SEQ - 1, True)
    xmb = xm_ref[...]
    prev = jnp.where(first, 0.0, prev_ref[HALO - 8:, :].astype(F32))
    nxt = jnp.where(last, 0.0, next_ref[0:8, :].astype(F32))
    half = CONV_K // 2

    rr = lax.broadcasted_iota(jnp.int32, (TM, TM), 0)
    cc = lax.broadcasted_iota(jnp.int32, (TM, TM), 1)
    shifts = {t: (cc == rr + (t - half)).astype(BF16) for t in range(CONV_K) if t != half}

    def conv_head(h):
        hs = slice(h * DH, (h + 1) * DH)
        xh = xmb[:, hs]

        def conv_rows(ext):
            out = jnp.zeros((8, DH), F32) + cb_ref[:, hs]
            for t in range(CONV_K):
                out = out + ext[8 + t - half:16 + t - half] * cw_ref[t:t + 1, hs]
            return out

        acc = xh.astype(F32) * cw_ref[half:half + 1, hs] + cb_ref[:, hs]
        for t, shift in shifts.items():
            acc = acc + jnp.dot(shift, xh, preferred_element_type=F32) * cw_ref[t:t + 1, hs]
        top = conv_rows(jnp.concatenate([prev[:, hs], xh[0:16].astype(F32)], axis=0))
        bot = conv_rows(jnp.concatenate([xh[TM - 16:].astype(F32), nxt[:, hs]], axis=0))
        acc = jnp.concatenate([top, acc[8:TM - 8], bot], axis=0)
        xcb = (acc * jax.nn.sigmoid(acc)).astype(BF16)
        xc_ref[:, hs] = xcb
        return xcb

    def project(h, xcb):
        qh = jnp.dot(xcb, wq_ref[h], preferred_element_type=F32)
        q_ref[:, h * DQK:(h + 1) * DQK] = (qh * (DQK ** -0.5)).astype(BF16)
        kt_ref[h * DQK:(h + 1) * DQK, :] = _dot_nt(wkt_ref[h], xcb).astype(BF16)

    gts = jnp.dot(xmb, wg_ref[...], preferred_element_type=F32) + bg_ref[...]
    gtt = _dot_nt(wgt_ref[...], xmb) + bgt_ref[...]

    pending = None
    for h in range(HEADS):
        hs = slice(h * DH, (h + 1) * DH)
        xcb = conv_head(h)
        v_ref[:, hs] = jnp.dot(xmb[:, hs], wv_ref[h], preferred_element_type=F32).astype(BF16)
        if pending is not None:
            project(*pending)
        pending = (h, xcb)
    project(*pending)
    lane = lax.broadcasted_iota(jnp.int32, (CHUNK, LANES), 1)
    lane_kind = (lane >> 3) & 3
    row = lax.broadcasted_iota(jnp.int32, (4 * HEADS, CHUNK), 0)
    row_kind = (row >> 3) & 3
    r = lax.broadcasted_iota(jnp.int32, (CHUNK, CHUNK), 0)
    c = lax.broadcasted_iota(jnp.int32, (CHUNK, CHUNK), 1)
    lower = (c <= r).astype(BF16)
    upper = (c >= r).astype(BF16)
    row_sums = jnp.concatenate([upper, lower, jnp.ones((CHUNK, LANES), BF16)], axis=1)
    for ch in range(TM // CHUNK):
        ts = slice(ch * CHUNK, (ch + 1) * CHUNK)
        g = gts[ts]
        g = jnp.where((lane_kind & 1) == 1, _log_sigmoid(g), g)
        terms = _bf16_terms(g, 3)
        pre = sum(_dot_nn(lower, t) for t in terms)
        suf = sum(_dot_nn(upper, t) for t in terms)
        gt_ref[ts, :] = jnp.where(lane_kind == 1, pre, jnp.where(lane_kind == 3, suf, g))
        gt = gtt[:, ts]
        gt = jnp.where((row_kind & 1) == 1, _log_sigmoid(gt), gt)
        sums = _dot_exact_rhs(gt, row_sums)
        gtt_ref[:, ts] = jnp.where(row_kind == 1, sums[:, :CHUNK],
                                   jnp.where(row_kind == 3, sums[:, CHUNK:2 * CHUNK], gt))
        tot_ref[:, ts] = sums[:, 2 * CHUNK:]


def _qkv(xm, conv_w, conv_b, wq, wkt, wv, wg, wgt, bg, bgt):
    const = lambda i: (0, 0)
    const3 = lambda i: (0, 0, 0)
    hb = TM // HALO
    n_hb = N_ALL // HALO
    return pl.pallas_call(
        _qkv_kernel,
        grid=(ALL_TILES,),
        in_specs=[
            pl.BlockSpec((TM, B_INNER), lambda i: (i, 0)),
            pl.BlockSpec((HALO, B_INNER), lambda i: (jnp.maximum(i * hb - 1, 0), 0)),
            pl.BlockSpec((HALO, B_INNER), lambda i: (jnp.minimum((i + 1) * hb, n_hb - 1), 0)),
            pl.BlockSpec((CONV_K, B_INNER), const),
            pl.BlockSpec((1, B_INNER), const),
            pl.BlockSpec((HEADS, DH, DQK), const3),
            pl.BlockSpec((HEADS, DQK, DH), const3),
            pl.BlockSpec((HEADS, DH, DV), const3),
            pl.BlockSpec((B_INNER, LANES), const),
            pl.BlockSpec((4 * HEADS, B_INNER), const),
            pl.BlockSpec((1, LANES), const),
            pl.BlockSpec((4 * HEADS, 1), const),
        ],
        out_specs=[
            pl.BlockSpec((TM, B_INNER), lambda i: (i, 0)),
            pl.BlockSpec((TM, HEADS * DQK), lambda i: (i, 0)),
            pl.BlockSpec((HEADS * DQK, TM), lambda i: (0, i)),
            pl.BlockSpec((TM, B_INNER), lambda i: (i, 0)),
            pl.BlockSpec((TM, LANES), lambda i: (i, 0)),
            pl.BlockSpec((4 * HEADS, TM), lambda i: (0, i)),
            pl.BlockSpec((4 * HEADS, TM), lambda i: (0, i)),
        ],
        out_shape=[
            jax.ShapeDtypeStruct((N_ALL, B_INNER), BF16),
            jax.ShapeDtypeStruct((N_ALL, HEADS * DQK), BF16),
            jax.ShapeDtypeStruct((HEADS * DQK, N_ALL), BF16),
            jax.ShapeDtypeStruct((N_ALL, B_INNER), BF16),
            jax.ShapeDtypeStruct((N_ALL, LANES), F32),
            jax.ShapeDtypeStruct((4 * HEADS, N_ALL), F32),
            jax.ShapeDtypeStruct((4 * HEADS, N_ALL), F32),
        ],
        compiler_params=_cparams(("parallel",)),
        name="mlstm_qkv",
    )(xm, xm, xm, conv_w, conv_b, wq, wkt, wv, wg, wgt, bg, bgt)


CTX_CHUNKS = CTX // CHUNK
LAT_CHUNKS = SEQ // CHUNK
SCAN_STEPS = CTX_CHUNKS + LAT_CHUNKS


def _scan_dir(nb, d, q_ref, kt_ref, v_ref, g_ref, gt_ref, tot_ref, o_ref, c_scr, m_scr):
    r = lax.broadcasted_iota(jnp.int32, (CHUNK, CHUNK), 0)
    c = lax.broadcasted_iota(jnp.int32, (CHUNK, CHUNK), 1)
    if d == 0:
        keep = c <= r
    else:
        keep = c >= r
    ones_blk = jnp.ones((CHUNK, LANES), BF16)
    bcol_all = g_ref[...]
    gtt = gt_ref[...]
    tot_all = tot_ref[...]
    base = 2 * HEADS * d

    def head_first(h):
        li_r = gtt[base + h:base + h + 1, :]
        b_r = gtt[base + HEADS + h:base + HEADS + h + 1, :]
        b_last = tot_all[base + HEADS + h:base + HEADS + h + 1, :]
        m_old = m_scr[nb, d, h]
        c_old = c_scr[nb, d, h]
        qh = q_ref[:, h * DQK:(h + 1) * DQK]
        kth = kt_ref[h * DQK:(h + 1) * DQK, :]
        vaug = jnp.concatenate([v_ref[:, h * DV:(h + 1) * DV], ones_blk], axis=1)
        g_r = b_last - b_r + li_r
        m_new = jnp.maximum(b_last + m_old, jnp.max(g_r, axis=-1, keepdims=True))
        decay = jnp.exp(b_last + m_old - m_new)
        wk = jnp.exp(g_r - m_new)
        kwt = (kth.astype(F32) * wk).astype(BF16)
        qk = jnp.dot(qh, kth, preferred_element_type=F32)
        qc = jnp.dot(qh, c_old.astype(BF16), preferred_element_type=F32)
        c_scr[nb, d, h] = jnp.concatenate([decay] * 3, axis=1) * c_old + jnp.dot(
            kwt, vaug, preferred_element_type=F32)
        m_scr[nb, d, h] = m_new
        return h, li_r, b_r, m_old, vaug, qk, qc

    def head_second(h, li_r, b_r, m_old, vaug, qk, qc):
        b_full = jnp.broadcast_to(bcol_all[:, base + HEADS + h:base + HEADS + h + 1], (CHUNK, LANES))
        dmat = jnp.where(keep, b_full - (b_r - li_r), NEG_INF)
        inter = b_full + m_old
        m_t = jnp.maximum(inter, jnp.max(dmat, axis=-1, keepdims=True))
        p = jnp.exp(dmat - m_t)
        s = qk * p
        a = jnp.exp(inter - m_t)
        num = jnp.concatenate([a] * 3, axis=1) * qc + jnp.dot(
            s.astype(BF16), vaug, preferred_element_type=F32)
        inv = 1.0 / jnp.maximum(jnp.abs(num[:, DV:]), jnp.exp(-m_t))
        o_ref[nb, :, h * DV:(h + 1) * DV] = (
            num[:, :DV] * jnp.concatenate([inv] * 2, axis=1)).astype(BF16)

    return head_first, head_second


SCAN_NB = 2
SCAN_IN = 6
SCAN_SKEW = 1


def _scan_kernel(*refs):
    n_in = SCAN_NB * 2 * SCAN_IN
    ins, outs = refs[:n_in], refs[n_in:n_in + 2]
    c_scr, m_scr = refs[n_in + 2:]

    @pl.when(pl.program_id(1) == 0)
    def _():
        c_scr[...] = jnp.zeros_like(c_scr)
        m_scr[...] = jnp.zeros_like(m_scr)

    units = []
    for nb in range(SCAN_NB):
        for d in range(2):
            k = nb * 2 + d
            first, second = _scan_dir(nb, d, *ins[k * SCAN_IN:(k + 1) * SCAN_IN], outs[d], c_scr, m_scr)
            units += [(first, second, h) for h in range(HEADS)]
    pending = []
    for first, second, h in units:
        pending.append((second, first(h)))
        if len(pending) > SCAN_SKEW:
            fn, old_vals = pending.pop(0)
            fn(*old_vals)
    for fn, old_vals in pending:
        fn(*old_vals)


def _scan(q, kt, v, gts, gtt, tot):
    lat_blk = N_LAT // CHUNK

    def fwd_in(b, j):
        return jnp.where(j < CTX_CHUNKS, lat_blk + b * CTX_CHUNKS + j, b * LAT_CHUNKS + j - CTX_CHUNKS)

    def bwd_in(b, j):
        return jnp.where(j < CTX_CHUNKS, lat_blk + b * CTX_CHUNKS + (CTX_CHUNKS - 1 - j),
                         b * LAT_CHUNKS + (SCAN_STEPS - 1 - j))

    def fwd_out(b, j):
        return b * LAT_CHUNKS + jnp.maximum(j - CTX_CHUNKS, 0)

    def bwd_out(b, j):
        return b * LAT_CHUNKS + jnp.minimum(SCAN_STEPS - 1 - j, LAT_CHUNKS - 1)

    n_g = BATCH // SCAN_NB

    def specs(fn, nb):
        blk = lambda g, j: fn(nb * n_g + g, j)
        return [
            pl.BlockSpec((CHUNK, HEADS * DQK), lambda g, j: (blk(g, j), 0)),
            pl.BlockSpec((HEADS * DQK, CHUNK), lambda g, j: (0, blk(g, j))),
            pl.BlockSpec((CHUNK, B_INNER), lambda g, j: (blk(g, j), 0)),
            pl.BlockSpec((CHUNK, LANES), lambda g, j: (blk(g, j), 0)),
            pl.BlockSpec((4 * HEADS, CHUNK), lambda g, j: (0, blk(g, j))),
            pl.BlockSpec((4 * HEADS, CHUNK), lambda g, j: (0, blk(g, j))),
        ]

    def out_spec(fn):
        return pl.BlockSpec((SCAN_NB, CHUNK, B_INNER), lambda g, j: (0, fn(g, j), 0))

    in_specs, operands = [], []
    for nb in range(SCAN_NB):
        for fn_in in (fwd_in, bwd_in):
            in_specs += specs(fn_in, nb)
            operands += [q, kt, v, gts, gtt, tot]
    h_shape = jax.ShapeDtypeStruct((SCAN_NB, N_LAT // SCAN_NB, B_INNER), BF16)
    hf, hb = pl.pallas_call(
        _scan_kernel,
        grid=(n_g, SCAN_STEPS),
        in_specs=in_specs,
        out_specs=[out_spec(fwd_out), out_spec(bwd_out)],
        out_shape=[h_shape, h_shape],
        scratch_shapes=[pltpu.VMEM((SCAN_NB, 2, HEADS, DQK, DV + LANES), F32),
                        pltpu.VMEM((SCAN_NB, 2, HEADS, 1, LANES), F32)],
        compiler_params=_cparams(("parallel", "arbitrary")),
        name="mlstm_scan",
    )(*operands)
    return hf.reshape(N_LAT, B_INNER), hb.reshape(N_LAT, B_INNER)


def _combine_kernel(x_ref, mod_ref, hf_ref, hb_ref, xc_ref, op_ref, hg_ref, sk_ref, wout_ref,
                    g2_ref, wr_ref, br_ref, o_ref, meta_ref, tmeta_ref, cnt_ref, cntr_ref,
                    y_scr, carry, carry_r):
    for h in range(HEADS):
        hs = slice(h * DV, (h + 1) * DV)
        s = hf_ref[:, hs].astype(F32) + hb_ref[:, hs].astype(F32)
        s = s * lax.rsqrt(jnp.mean(s * s, axis=-1, keepdims=True) + EPS)
        y = jax.nn.sigmoid(op_ref[:, hs].astype(F32)) * (
            s * hg_ref[:, hs] + sk_ref[:, hs] * xc_ref[:, hs].astype(F32))
        y_scr[:, hs] = y.astype(BF16)
        if h % 2 == 1:
            ks = slice((h - 1) * DV, (h + 1) * DV)
            part = jnp.dot(y_scr[:, ks], wout_ref[ks, :], preferred_element_type=F32)
            out = part if h == 1 else out + part
    xn = x_ref[...] + mod_ref[0, 2:3, :] * out
    o_ref[...] = xn
    _route_tile(xn, mod_ref, g2_ref, wr_ref, br_ref, meta_ref, tmeta_ref, cnt_ref, cntr_ref,
                carry, carry_r)


def _combine(xa, mod, hf, hb, xc, op, head_g, skip, w_out, g2, wr, br):
    const = lambda i: (0, 0)
    row = lambda i: (i, 0)
    r_in, r_out, r_shape, r_scratch = _router_specs(LAT_TILES)
    return pl.pallas_call(
        _combine_kernel,
        grid=(LAT_TILES,),
        in_specs=[
            pl.BlockSpec((TM, D), row),
            pl.BlockSpec((1, 6, D), lambda i: (_mod_row(i), 0, 0)),
            pl.BlockSpec((TM, B_INNER), row),
            pl.BlockSpec((TM, B_INNER), row),
            pl.BlockSpec((TM, B_INNER), row),
            pl.BlockSpec((TM, B_INNER), row),
            pl.BlockSpec((1, B_INNER), const),
            pl.BlockSpec((1, B_INNER), const),
            pl.BlockSpec((B_INNER, D), const),
        ] + r_in,
        out_specs=[pl.BlockSpec((TM, D), row)] + r_out,
        out_shape=[jax.ShapeDtypeStruct((N_LAT, D), F32)] + r_shape,
        scratch_shapes=[pltpu.VMEM((TM, B_INNER), BF16)] + r_scratch,
        compiler_params=_cparams(("arbitrary",)),
        name="mlstm_combine",
    )(xa, mod, hf, hb, xc, op, head_g, skip, w_out, g2, wr, br)


def _router_weights(w_grp, b_grp, w_exp, b_exp):
    wr = jnp.zeros((LANES, D), F32).at[:N_EXP].set(w_exp.T).at[N_EXP:N_EXP + N_GRP].set(w_grp.T)
    br = jnp.zeros((LANES, 1), F32).at[:N_EXP, 0].set(b_exp).at[N_EXP:N_EXP + N_GRP, 0].set(b_grp)
    return wr.astype(BF16), br


PLAN_BLK = 2048
PLAN_PAIRS = 256
PLAN_ROWS = 16
PLAN_FIELDS = 10


def _plan_kernel(meta_ref, cc_ref, cr_ref, pp_ref, plan_ref):
    row = lax.broadcasted_iota(jnp.int32, (LANES, LANES), 0)
    col = lax.broadcasted_iota(jnp.int32, (LANES, LANES), 1)
    cnt_c = cc_ref[...]
    cnt_r = cr_ref[0:1, :]
    starts_c = jnp.sum(jnp.where(col < row, cnt_r, 0.0), axis=1, keepdims=True)
    erow = lax.broadcasted_iota(jnp.int32, (LANES, PLAN_BLK), 0).astype(F32)
    for j in range(2):
        pos = jnp.sum(jnp.where(erow == meta_ref[j:j + 1, :], starts_c, 0.0), axis=0,
                      keepdims=True) + meta_ref[j + 2:j + 3, :]
        pp_ref[j:j + 1, :] = pos.astype(jnp.int32)
    pp_ref[2:8, :] = jnp.zeros((6, PLAN_BLK), jnp.int32)

    @pl.when(pl.program_id(0) == 0)
    def _():
        ends_c = starts_c + cnt_c
        starts_r = jnp.sum(jnp.where(row < col, cnt_c, 0.0), axis=0, keepdims=True)
        ends_r = starts_r + cnt_r

        def tiles(st, en, cn):
            first_tile = jnp.floor(st * (1.0 / TM))
            n = jnp.where(cn > 0.0, jnp.floor((en - 1.0) * (1.0 / TM)) - first_tile + 1.0, 0.0)
            return first_tile, n

        ft_c, pairs_c = tiles(starts_c, ends_c, cnt_c)
        _, pairs_r = tiles(starts_r, ends_r, cnt_r)
        pend_c = jnp.sum(jnp.where(col <= row, pairs_r, 0.0), axis=1, keepdims=True)
        pstart_c = pend_c - pairs_c
        total = jnp.sum(pairs_r, axis=1, keepdims=True)
        q = lax.broadcasted_iota(jnp.int32, (1, PLAN_PAIRS), 1).astype(F32)
        erow_p = lax.broadcasted_iota(jnp.int32, (LANES, PLAN_PAIRS), 0).astype(F32)

        def at(qv):
            qc = jnp.maximum(jnp.minimum(qv, total - 1.0), 0.0)
            e = jnp.sum(jnp.where(qc >= pend_c, 1.0, 0.0), axis=0, keepdims=True)
            oh = erow_p == e
            tile = jnp.sum(jnp.where(oh, ft_c - pstart_c, 0.0), axis=0, keepdims=True) + qc
            return e, oh, tile

        e_q, oh, tile_q = at(q)
        e_p, _, tile_p = at(q - 1.0)
        lo = jnp.sum(jnp.where(oh, starts_c, 0.0), axis=0, keepdims=True) - tile_q * TM
        hi = jnp.sum(jnp.where(oh, ends_c, 0.0), axis=0, keepdims=True) - tile_q * TM
        rowf, colf = row.astype(F32), col.astype(F32)
        used_r = pairs_r > 0.0
        order_c = jnp.sum(jnp.where((col < row) & used_r, 1.0, 0.0), axis=1, keepdims=True)
        parity_c = order_c - 2.0 * jnp.floor(order_c * 0.5)
        next_c = jnp.min(jnp.where((col > row) & used_r, colf, float(LANES)), axis=1, keepdims=True)
        next_c = jnp.where(next_c >= float(LANES), rowf[:, 0:1], next_c)
        parity_q = jnp.sum(jnp.where(oh, parity_c, 0.0), axis=0, keepdims=True)
        next_q = jnp.sum(jnp.where(oh, next_c, 0.0), axis=0, keepdims=True)
        rows = (tile_q, e_q, jnp.clip(lo, 0.0, TM), jnp.clip(hi, 0.0, TM),
                jnp.where((q == 0.0) | (tile_q != tile_p), 1.0, 0.0),
                jnp.where((q == 0.0) | (e_q != e_p), 1.0, 0.0),
                jnp.where(q < total, 1.0, 0.0),
                parity_q,
                jnp.where(parity_q == 0.0, e_q, next_q),
                jnp.where(parity_q == 1.0, e_q, next_q))
        for j, v in enumerate(rows):
            plan_ref[j:j + 1, :] = v.astype(jnp.int32)
        plan_ref[len(rows):, :] = jnp.zeros((PLAN_ROWS - len(rows), PLAN_PAIRS), jnp.int32)


def _route_plan(meta, cnt_c, cnt_r, n_rows):
    n_pairs = 2 * n_rows // TM + N_EXP - 1
    assert n_pairs <= PLAN_PAIRS and n_rows % PLAN_BLK == 0
    const = lambda i: (0, 0)
    pp, plan = pl.pallas_call(
        _plan_kernel,
        grid=(n_rows // PLAN_BLK,),
        in_specs=[pl.BlockSpec((8, PLAN_BLK), lambda i: (0, i)),
                  pl.BlockSpec((LANES, 1), const),
                  pl.BlockSpec((8, LANES), const)],
        out_specs=[pl.BlockSpec((8, PLAN_BLK), lambda i: (0, i)),
                   pl.BlockSpec((PLAN_ROWS, PLAN_PAIRS), const)],
        out_shape=[jax.ShapeDtypeStruct((8, n_rows), jnp.int32),
                   jax.ShapeDtypeStruct((PLAN_ROWS, PLAN_PAIRS), jnp.int32)],
        compiler_params=_cparams(("arbitrary",)),
        name="moe_plan",
    )(meta, cnt_c, cnt_r)
    return pp[0], pp[1], tuple(plan[j, :n_pairs] for j in range(PLAN_FIELDS))


def _moe_experts(xa, routing, mod, g2, w1, w3, w2, layer, n_rows):
    meta, _, cnt_c, cnt_r = routing
    p1, p2, plan = _route_plan(meta, cnt_c, cnt_r, n_rows)
    xs = _scatter(p1, p2, xa, mod, g2, n_rows // TM)
    return p1, p2, _grouped(plan, xs, w1, w3, w2, layer)


def kernel(x, c, ctx, c_ctx, norm_g, w_ada, b_ada, a_w_in, a_b_in, a_g_v, a_w_s, a_b_s, a_w_out,
           b_w_in, b_conv_w, b_conv_b, b_w_q, b_w_k, b_w_v, b_w_gate, b_b_gate, b_head_g, b_skip,
           b_w_out, moe_w_grp, moe_b_grp, moe_w_exp, moe_b_exp, moe_w1, moe_w3, moe_w2, final_g):
    cc = jnp.zeros((MOD_ROWS, D), F32).at[:BATCH].set(c).at[BATCH].set(c_ctx)
    mods = _ada(cc, w_ada, b_ada).reshape(2, MOD_ROWS, 6, D)

    mod = mods[0]
    b_s_full = jnp.repeat(a_b_s[0].T, A_GC, axis=1)
    g2 = norm_g[0, 1].reshape(1, D)
    wr, br = _router_weights(moe_w_grp[0], moe_b_grp[0], moe_w_exp[0], moe_b_exp[0])
    xa, *routing = _gmlp(x.reshape(N_LAT, D), ctx.reshape(N_CTX, D), mod, norm_g[0, 0].reshape(1, D),
                         a_w_in[0].astype(BF16), a_b_in[0].reshape(1, -1), a_g_v[0].reshape(1, -1),
                         a_w_s[0].astype(BF16), b_s_full, a_w_out[0].astype(BF16), g2, wr, br)
    p1, p2, ys = _moe_experts(xa, routing, mod, g2, moe_w1, moe_w3, moe_w2, 0, N_ALL)

    mod = mods[1]
    xa, xm, op = _inproj(p1, p2, xa, mods[0], routing[1], ys, mod, norm_g[1, 0].reshape(1, D),
                         b_w_in[0].astype(BF16))
    wg = jnp.zeros((B_INNER, LANES), F32).at[:, :4 * HEADS].set(b_w_gate[0]).astype(BF16)
    bg = jnp.zeros((1, LANES), F32).at[0, :4 * HEADS].set(b_b_gate[0])
    xc, q, kt, v, gts, gtt, tot = _qkv(xm, b_conv_w[0], b_conv_b[0].reshape(1, -1),
                                  b_w_q[0].astype(BF16),
                                  jnp.transpose(b_w_k[0], (0, 2, 1)).astype(BF16),
                                  b_w_v[0].astype(BF16), wg, b_w_gate[0].T.astype(BF16), bg,
                                  b_b_gate[0].reshape(-1, 1))
    hf, hb = _scan(q, kt, v, gts, gtt, tot)
    g2 = norm_g[1, 1].reshape(1, D)
    wr, br = _router_weights(moe_w_grp[1], moe_b_grp[1], moe_w_exp[1], moe_b_exp[1])
    xl, *routing = _combine(xa, mod, hf, hb, xc, op, b_head_g[0].reshape(1, -1),
                            b_skip[0].reshape(1, -1), b_w_out[0].astype(BF16), g2, wr, br)
    p1, p2, ys = _moe_experts(xl, routing, mod, g2, moe_w1, moe_w3, moe_w2, 1, N_LAT)
    out = _gather(p1, p2, xl, mod, routing[1], final_g.reshape(1, D), ys, LAT_TILES, True)
    return out.reshape(BATCH, SEQ, D)
```

```python
import functools

import jax
import jax.numpy as jnp
from jax import lax
from jax.experimental import pallas as pl
from jax.experimental.pallas import tpu as pltpu

F32 = jnp.float32
BF16 = jnp.bfloat16

D = 1024
BATCH = 8
SEQ = 2048
CTX = 256
EPS = 1e-6
NEG_INF = -1e30
N_LAT = BATCH * SEQ
N_CTX = BATCH * CTX
N_ALL = N_LAT + N_CTX

TM = 256
LAT_TILES = N_LAT // TM
ALL_TILES = N_ALL // TM
TILES_PER_SEQ = SEQ // TM
MOD_ROWS = 16

CHUNK = 128
A_HALF = 2048
A_GROUPS = 8
A_GC = A_HALF // A_GROUPS
B_INNER = 2048
HEADS = 8
DH = B_INNER // HEADS
DQK = DH // 2
DV = DH
CONV_K = 5
HALO = 16
N_EXP = 32
N_GRP = 4
EXP_PER_GRP = 8
D_EXP = 512
LANES = 128

VMEM_LIMIT = 56 * 1024 * 1024


def _cparams(sem):
    return pltpu.CompilerParams(dimension_semantics=sem, vmem_limit_bytes=VMEM_LIMIT)


def _mod_row(i):
    return jnp.where(i < LAT_TILES, i // TILES_PER_SEQ, BATCH)


def _norm_mod(x, g, shift, scale):
    y = x * lax.rsqrt(jnp.mean(x * x, axis=-1, keepdims=True) + EPS) * g
    return y * (1.0 + scale) + shift


def _bf16_terms(x, n):
    terms = []
    r = x
    for _ in range(n):
        t = r.astype(BF16)
        terms.append(t)
        r = r - t.astype(F32)
    return terms


def _dot_nt(a, b):
    return lax.dot_general(a, b, (((1,), (1,)), ((), ())), preferred_element_type=F32)


def _dot_nn(a, b):
    return jnp.dot(a, b, preferred_element_type=F32)


def _dot3(a, b, dot=_dot_nn):
    a1, a2 = _bf16_terms(a, 2)
    b1, b2 = _bf16_terms(b, 2)
    return dot(a1, b1) + (dot(a1, b2) + dot(a2, b1))


def _dot_exact_lhs(a_bf16, b, dot=_dot_nn):
    return sum(dot(a_bf16, t) for t in _bf16_terms(b, 3))


def _dot_exact_rhs(a, b_bf16, dot=_dot_nn):
    return sum(dot(t, b_bf16) for t in _bf16_terms(a, 3))


def _gelu_tanh(x):
    half = 0.5 * x
    t = jnp.tanh(x * (0.7978845608028654 + 0.035677408136300125 * (x * x)))
    return half + half * t


ADA_BN = 1536


def _ada_kernel(c_ref, w_ref, b_ref, o_ref):
    c = c_ref[...]
    a = c * jax.nn.sigmoid(c)
    o_ref[0] = _dot3(a, w_ref[0]) + b_ref[0]


def _ada(cc, w_ada, b_ada):
    depth = w_ada.shape[0]
    return pl.pallas_call(
        _ada_kernel,
        grid=(depth, 6 * D // ADA_BN),
        in_specs=[
            pl.BlockSpec((MOD_ROWS, D), lambda l, j: (0, 0)),
            pl.BlockSpec((1, D, ADA_BN), lambda l, j: (l, 0, j)),
            pl.BlockSpec((1, 1, ADA_BN), lambda l, j: (l, 0, j)),
        ],
        out_specs=pl.BlockSpec((1, MOD_ROWS, ADA_BN), lambda l, j: (l, 0, j)),
        out_shape=jax.ShapeDtypeStruct((depth, MOD_ROWS, 6 * D), F32),
        compiler_params=_cparams(("parallel", "parallel")),
        name="ada",
    )(cc, w_ada, b_ada.reshape(depth, 1, 6 * D))


GM_CH = 512


def _gmlp_kernel(xl_ref, xc_ref, mod_ref, g_ref, win_ref, bin_ref, gv_ref, ws_ref, bs_ref, wout_ref,
                 g2_ref, wr_ref, br_ref, o_ref, meta_ref, tmeta_ref, cnt_ref, cntr_ref,
                 z_scr, y_scr, carry, carry_r):
    x = jnp.where(pl.program_id(0) < LAT_TILES, xl_ref[...], xc_ref[...])
    h = _norm_mod(x, g_ref[...], mod_ref[0, 0:1, :], mod_ref[0, 1:2, :])
    hb = h.astype(BF16)
    s1 = jnp.zeros((TM, 1), F32)
    s2 = jnp.zeros((TM, 1), F32)
    n_ch = 2 * A_HALF // GM_CH
    for j in list(range(n_ch // 2, n_ch)) + list(range(n_ch // 2)):
        cs = slice(j * GM_CH, (j + 1) * GM_CH)
        zc = jnp.dot(hb, win_ref[:, cs], preferred_element_type=F32) + bin_ref[:, cs]
        zc = _gelu_tanh(zc)
        z_scr[:, cs] = zc
        if j * GM_CH >= A_HALF:
            s1 = s1 + jnp.sum(zc, axis=-1, keepdims=True)
            s2 = s2 + jnp.sum(zc * zc, axis=-1, keepdims=True)
    mu = s1 * (1.0 / A_HALF)
    rstd = lax.rsqrt(s2 * (1.0 / A_HALF) - mu * mu + EPS)
    for c in range(TM // CHUNK):
        rs = slice(c * CHUNK, (c + 1) * CHUNK)
        for g in range(A_GROUPS):
            cs = slice(g * A_GC, (g + 1) * A_GC)
            vs = slice(A_HALF + g * A_GC, A_HALF + (g + 1) * A_GC)
            v = (z_scr[rs, vs] - mu[rs]) * rstd[rs] * gv_ref[:, cs]
            s = jnp.dot(ws_ref[g], v.astype(BF16), preferred_element_type=F32) + bs_ref[:, cs]
            y_scr[rs, cs] = (z_scr[rs, cs] * s).astype(BF16)
    out = jnp.dot(y_scr[...], wout_ref[...], preferred_element_type=F32)
    xn = x + mod_ref[0, 2:3, :] * out
    o_ref[...] = xn
    _route_tile(xn, mod_ref, g2_ref, wr_ref, br_ref, meta_ref, tmeta_ref, cnt_ref, cntr_ref,
                carry, carry_r)


def _gmlp(xl, xc, mod, g, w_in, b_in, g_v, w_s, b_s_full, w_out, g2, wr, br):
    const = lambda i: (0, 0)
    r_in, r_out, r_shape, r_scratch = _router_specs(ALL_TILES)
    return pl.pallas_call(
        _gmlp_kernel,
        grid=(ALL_TILES,),
        in_specs=[
            pl.BlockSpec((TM, D), lambda i: (jnp.minimum(i, LAT_TILES - 1), 0)),
            pl.BlockSpec((TM, D), lambda i: (jnp.maximum(i - LAT_TILES, 0), 0)),
            pl.BlockSpec((1, 6, D), lambda i: (_mod_row(i), 0, 0)),
            pl.BlockSpec((1, D), const),
            pl.BlockSpec((D, 2 * A_HALF), const),
            pl.BlockSpec((1, 2 * A_HALF), const),
            pl.BlockSpec((1, A_HALF), const),
            pl.BlockSpec((A_GROUPS, CHUNK, CHUNK), lambda i: (0, 0, 0)),
            pl.BlockSpec((CHUNK, A_HALF), const),
            pl.BlockSpec((A_HALF, D), const),
        ] + r_in,
        out_specs=[pl.BlockSpec((TM, D), lambda i: (i, 0))] + r_out,
        out_shape=[jax.ShapeDtypeStruct((N_ALL, D), F32)] + r_shape,
        scratch_shapes=[pltpu.VMEM((TM, 2 * A_HALF), F32), pltpu.VMEM((TM, A_HALF), BF16)] + r_scratch,
        compiler_params=_cparams(("arbitrary",)),
        name="gmlp",
    )(xl, xc, mod, g, w_in, b_in, g_v, w_s, b_s_full, w_out, g2, wr, br)


def _route_tile(x, mod_ref, g_ref, wr_ref, br_ref, meta_ref, tmeta_ref, cnt_ref, cntr_ref,
                carry, carry_r):
    @pl.when(pl.program_id(0) == 0)
    def _():
        carry[...] = jnp.zeros_like(carry)
        carry_r[...] = jnp.zeros_like(carry_r)

    h = _norm_mod(x, g_ref[...], mod_ref[0, 3:4, :], mod_ref[0, 4:5, :])
    lt = _dot_nt(wr_ref[...], h.astype(BF16)) + br_ref[...]
    e_t = lt[0:N_EXP]
    row8 = lax.broadcasted_iota(jnp.int32, (EXP_PER_GRP, TM), 0).astype(F32)
    g_t = jnp.where(row8 < N_GRP, lt[N_EXP:N_EXP + EXP_PER_GRP], -jnp.inf)
    gmax = jnp.max(g_t, axis=0, keepdims=True)
    p_g = 1.0 / jnp.sum(jnp.exp(g_t - gmax), axis=0, keepdims=True)
    g_idx = jnp.min(jnp.where(g_t == gmax, row8, float(EXP_PER_GRP)), axis=0, keepdims=True)
    sel = jnp.zeros((EXP_PER_GRP, TM), F32)
    for g in range(N_GRP):
        sel = sel + jnp.where(g_idx == g, e_t[g * EXP_PER_GRP:(g + 1) * EXP_PER_GRP], 0.0)
    m1 = jnp.max(sel, axis=0, keepdims=True)
    i1 = jnp.min(jnp.where(sel == m1, row8, float(EXP_PER_GRP)), axis=0, keepdims=True)
    sel2 = jnp.where(row8 == i1, -jnp.inf, sel)
    m2 = jnp.max(sel2, axis=0, keepdims=True)
    i2 = jnp.min(jnp.where(sel2 == m2, row8, float(EXP_PER_GRP)), axis=0, keepdims=True)
    e2 = jnp.exp(m2 - m1)
    w1 = p_g / (1.0 + e2)
    w2 = p_g * e2 / (1.0 + e2)
    row = lax.broadcasted_iota(jnp.int32, (LANES, TM), 0).astype(F32)
    id1 = g_idx * EXP_PER_GRP + i1
    id2 = g_idx * EXP_PER_GRP + i2
    oh1 = row == id1
    oh2 = row == id2
    oh = jnp.where(oh1, 1.0, 0.0) + jnp.where(oh2, 1.0, 0.0)
    before = (lax.broadcasted_iota(jnp.int32, (TM, TM), 0)
              < lax.broadcasted_iota(jnp.int32, (TM, TM), 1)).astype(BF16)
    ohb = oh.astype(BF16)
    tot = jnp.dot(ohb, before, preferred_element_type=F32) + carry[...]
    r1 = jnp.sum(jnp.where(oh1, tot, 0.0), axis=0, keepdims=True)
    r2 = jnp.sum(jnp.where(oh2, tot, 0.0), axis=0, keepdims=True)
    carry[...] += jnp.sum(oh, axis=-1, keepdims=True)
    carry_r[...] += _dot_nt(jnp.ones((8, TM), BF16), ohb)
    cnt_ref[...] = carry[...]
    cntr_ref[...] = carry_r[...]
    meta_ref[0:1, :] = id1
    meta_ref[1:2, :] = id2
    meta_ref[2:3, :] = r1
    meta_ref[3:4, :] = r2
    meta_ref[4:8, :] = jnp.zeros((4, TM), F32)
    tmeta_ref[...] = (jnp.where(row == 0.0, w1, 0.0) + jnp.where(row == 1.0, w2, 0.0)).T


def _router_specs(n_tiles):
    const = lambda i: (0, 0)
    n = n_tiles * TM
    in_specs = [pl.BlockSpec((1, D), const), pl.BlockSpec((LANES, D), const),
                pl.BlockSpec((LANES, 1), const)]
    out_specs = [pl.BlockSpec((8, TM), lambda i: (0, i)),
                 pl.BlockSpec((TM, LANES), lambda i: (i, 0)),
                 pl.BlockSpec((LANES, 1), const),
                 pl.BlockSpec((8, LANES), const)]
    out_shape = [jax.ShapeDtypeStruct((8, n), F32), jax.ShapeDtypeStruct((n, LANES), F32),
                 jax.ShapeDtypeStruct((LANES, 1), F32), jax.ShapeDtypeStruct((8, LANES), F32)]
    scratch = [pltpu.VMEM((LANES, 1), F32), pltpu.VMEM((8, LANES), F32)]
    return in_specs, out_specs, out_shape, scratch


ISSUE_UNROLL = 8
ROW_TILE = (8, LANES)
MXU_N = 256


def _row_copy_wait(buf_slot, sem_slot):
    pltpu.make_async_copy(buf_slot, buf_slot, sem_slot).wait()


def _scatter_kernel(p1_ref, p2_ref, x_ref, mod_ref, g_ref, xs_ref, buf, sem):
    i = pl.program_id(0)
    n_steps = pl.num_programs(0)
    slot = lax.rem(i, 2)

    def wait_slot(s):
        _row_copy_wait(buf.at[s], sem.at[s])
        _row_copy_wait(buf.at[s], sem.at[s])

    @pl.when(i >= 2)
    def _():
        wait_slot(slot)

    h = _norm_mod(x_ref[...], g_ref[...], mod_ref[0, 3:4, :], mod_ref[0, 4:5, :])
    buf[slot] = h.reshape(TM, *ROW_TILE)
    base = i * TM

    def body(r, carry):
        src = buf.at[slot, r]
        pltpu.make_async_copy(src, xs_ref.at[p1_ref[base + r]], sem.at[slot]).start()
        pltpu.make_async_copy(src, xs_ref.at[p2_ref[base + r]], sem.at[slot]).start(priority=1)
        return carry

    lax.fori_loop(0, TM, body, 0, unroll=ISSUE_UNROLL)

    @pl.when(i == n_steps - 1)
    def _():
        @pl.when(i >= 1)
        def _():
            wait_slot(1 - slot)

        wait_slot(slot)


def _scatter(p1, p2, xa, mod, g, n_tiles):
    n = n_tiles * TM
    return pl.pallas_call(
        _scatter_kernel,
        grid_spec=pltpu.PrefetchScalarGridSpec(
            num_scalar_prefetch=2,
            grid=(n_tiles,),
            in_specs=[
                pl.BlockSpec((TM, D), lambda i, p1, p2: (i, 0)),
                pl.BlockSpec((1, 6, D), lambda i, p1, p2: (_mod_row(i), 0, 0)),
                pl.BlockSpec((1, D), lambda i, p1, p2: (0, 0)),
            ],
            out_specs=pl.BlockSpec(memory_space=pl.ANY),
            scratch_shapes=[pltpu.VMEM((2, TM) + ROW_TILE, F32), pltpu.SemaphoreType.DMA((2,))],
        ),
        out_shape=jax.ShapeDtypeStruct((2 * n,) + ROW_TILE, F32),
        compiler_params=_cparams(("arbitrary",)),
        name="moe_scatter",
    )(p1, p2, xa, mod, g)


def _grouped_kernel(tile_ref, exp_ref, lo_ref, hi_ref, first_ref, newexp_ref, valid_ref,
                    xs_ref, w1_ref, w3_ref, w2_ref, ys_ref, wb1, wb3, wb2):
    q = pl.program_id(0)

    @pl.when(valid_ref[q] == 1)
    def _():
        @pl.when(newexp_ref[q] == 1)
        def _():
            wb1[...] = w1_ref[0, 0].astype(BF16)
            wb3[...] = w3_ref[0, 0].astype(BF16)
            wb2[...] = w2_ref[0, 0].astype(BF16)

        x = xs_ref[...].reshape(TM, D).astype(BF16)
        rows = lax.broadcasted_iota(jnp.int32, (TM, 1), 0)
        mine = (rows >= lo_ref[q]) & (rows < hi_ref[q])
        acts = []
        for j in range(D_EXP // MXU_N):
            cs = slice(j * MXU_N, (j + 1) * MXU_N)
            a = jnp.dot(x, wb1[:, cs], preferred_element_type=F32)
            b = jnp.dot(x, wb3[:, cs], preferred_element_type=F32)
            acts.append(jnp.where(mine, a * jax.nn.sigmoid(a) * b, 0.0).astype(BF16))
        act = jnp.concatenate(acts, axis=1)
        y = jnp.dot(act, wb2[...], preferred_element_type=F32).reshape(TM, *ROW_TILE)

        @pl.when(first_ref[q] == 1)
        def _():
            ys_ref[...] = y

        @pl.when(first_ref[q] == 0)
        def _():
            ys_ref[...] += y


def _grouped(plan, xs, w1, w3, w2, layer):
    n_pairs = plan[0].shape[0]
    tile_map = lambda q, tile, exp, *_: (tile[q], 0, 0)
    exp_map = lambda q, tile, exp, *_: (layer, exp[q], 0, 0)
    return pl.pallas_call(
        _grouped_kernel,
        grid_spec=pltpu.PrefetchScalarGridSpec(
            num_scalar_prefetch=7,
            grid=(n_pairs,),
            in_specs=[
                pl.BlockSpec((TM,) + ROW_TILE, tile_map),
                pl.BlockSpec((1, 1, D, D_EXP), exp_map),
                pl.BlockSpec((1, 1, D, D_EXP), exp_map),
                pl.BlockSpec((1, 1, D_EXP, D), exp_map),
            ],
            out_specs=pl.BlockSpec((TM,) + ROW_TILE, tile_map),
            scratch_shapes=[pltpu.VMEM((D, D_EXP), BF16), pltpu.VMEM((D, D_EXP), BF16),
                            pltpu.VMEM((D_EXP, D), BF16)],
        ),
        out_shape=jax.ShapeDtypeStruct(xs.shape, F32),
        compiler_params=_cparams(("arbitrary",)),
        name="moe_grouped",
    )(*plan, xs, w1, w3, w2)


def _gather_kernel(p1_ref, p2_ref, x_ref, mod_ref, tm_ref, fg_ref, ys_ref, o_ref, buf, sem,
                   *, final_norm):
    i = pl.program_id(0)
    n_steps = pl.num_programs(0)
    slot = lax.rem(i, 2)

    def issue(tile, s):
        base = tile * TM

        def body(r, carry):
            pltpu.make_async_copy(ys_ref.at[p1_ref[base + r]], buf.at[s, 0, r], sem.at[s]).start()
            pltpu.make_async_copy(ys_ref.at[p2_ref[base + r]], buf.at[s, 1, r],
                                  sem.at[s]).start(priority=1)
            return carry

        lax.fori_loop(0, TM, body, 0, unroll=ISSUE_UNROLL)

    @pl.when(i == 0)
    def _():
        issue(0, 0)

    @pl.when(i + 1 < n_steps)
    def _():
        issue(i + 1, 1 - slot)

    _row_copy_wait(buf.at[slot, 0], sem.at[slot])
    _row_copy_wait(buf.at[slot, 1], sem.at[slot])
    y = (tm_ref[:, 0:1] * buf[slot, 0].reshape(TM, D)
         + tm_ref[:, 1:2] * buf[slot, 1].reshape(TM, D))
    out = x_ref[...] + mod_ref[0, 5:6, :] * y
    if final_norm:
        out = out * lax.rsqrt(jnp.mean(out * out, axis=-1, keepdims=True) + EPS) * fg_ref[...]
    o_ref[...] = out


def _gather(p1, p2, xa, mod, tmeta, final_g, ys, n_tiles, final_norm):
    return pl.pallas_call(
        functools.partial(_gather_kernel, final_norm=final_norm),
        grid_spec=pltpu.PrefetchScalarGridSpec(
            num_scalar_prefetch=2,
            grid=(n_tiles,),
            in_specs=[
                pl.BlockSpec((TM, D), lambda i, p1, p2: (i, 0)),
                pl.BlockSpec((1, 6, D), lambda i, p1, p2: (_mod_row(i), 0, 0)),
                pl.BlockSpec((TM, LANES), lambda i, p1, p2: (i, 0)),
                pl.BlockSpec((1, D), lambda i, p1, p2: (0, 0)),
                pl.BlockSpec(memory_space=pl.ANY),
            ],
            out_specs=pl.BlockSpec((TM, D), lambda i, p1, p2: (i, 0)),
            scratch_shapes=[pltpu.VMEM((2, 2, TM) + ROW_TILE, F32), pltpu.SemaphoreType.DMA((2,))],
        ),
        out_shape=jax.ShapeDtypeStruct((n_tiles * TM, D), F32),
        compiler_params=_cparams(("arbitrary",)),
        name="moe_gather",
    )(p1, p2, xa, mod, tmeta, final_g, ys)


def _issue_row_gathers(p1_ref, p2_ref, ys_ref, buf, sem, tile, s, unrolled):
    base = tile * TM

    def one(r):
        pltpu.make_async_copy(ys_ref.at[p1_ref[base + r]], buf.at[s, 0, r], sem.at[s]).start()
        pltpu.make_async_copy(ys_ref.at[p2_ref[base + r]], buf.at[s, 1, r],
                              sem.at[s]).start(priority=1)

    if unrolled:
        for r in range(TM):
            one(r)
    else:
        def body(r, carry):
            one(r)
            return carry

        lax.fori_loop(0, TM, body, 0, unroll=ISSUE_UNROLL)


def _inproj_kernel(p1_ref, p2_ref, x_ref, mod0_ref, tm_ref, ys_ref, mod_ref, g_ref, win_ref,
                   xn_ref, xm_ref, op_ref, buf, sem):
    i = pl.program_id(0)
    n_steps = pl.num_programs(0)
    slot = lax.rem(i, 2)

    @pl.when(i == 0)
    def _():
        _issue_row_gathers(p1_ref, p2_ref, ys_ref, buf, sem, 0, 0, False)

    _row_copy_wait(buf.at[slot, 0], sem.at[slot])
    _row_copy_wait(buf.at[slot, 1], sem.at[slot])
    y = (tm_ref[:, 0:1] * buf[slot, 0].reshape(TM, D)
         + tm_ref[:, 1:2] * buf[slot, 1].reshape(TM, D))
    xn = x_ref[...] + mod0_ref[0, 5:6, :] * y
    xn_ref[...] = xn
    _issue_row_gathers(p1_ref, p2_ref, ys_ref, buf, sem, jnp.minimum(i + 1, n_steps - 1), 1 - slot,
                       True)
    h = _norm_mod(xn, g_ref[...], mod_ref[0, 0:1, :], mod_ref[0, 1:2, :])
    hb = h.astype(BF16)
    xm_ref[...] = jnp.dot(hb, win_ref[:, :B_INNER], preferred_element_type=F32).astype(BF16)
    op_ref[...] = jnp.dot(hb, win_ref[:, B_INNER:], preferred_element_type=F32).astype(BF16)

    @pl.when(i == n_steps - 1)
    def _():
        _row_copy_wait(buf.at[1 - slot, 0], sem.at[1 - slot])
        _row_copy_wait(buf.at[1 - slot, 1], sem.at[1 - slot])


def _inproj(p1, p2, xa, mod0, tmeta, ys, mod, g, w_in):
    const = lambda i, p1, p2: (0, 0)
    row = lambda i, p1, p2: (i, 0)
    mod_map = lambda i, p1, p2: (_mod_row(i), 0, 0)
    return pl.pallas_call(
        _inproj_kernel,
        grid_spec=pltpu.PrefetchScalarGridSpec(
            num_scalar_prefetch=2,
            grid=(ALL_TILES,),
            in_specs=[
                pl.BlockSpec((TM, D), row),
                pl.BlockSpec((1, 6, D), mod_map),
                pl.BlockSpec((TM, LANES), row),
                pl.BlockSpec(memory_space=pl.ANY),
                pl.BlockSpec((1, 6, D), mod_map),
                pl.BlockSpec((1, D), const),
                pl.BlockSpec((D, 2 * B_INNER), const),
            ],
            out_specs=[pl.BlockSpec((TM, D), row),
                       pl.BlockSpec((TM, B_INNER), row),
                       pl.BlockSpec((TM, B_INNER), row)],
            scratch_shapes=[pltpu.VMEM((2, 2, TM) + ROW_TILE, F32), pltpu.SemaphoreType.DMA((2,))],
        ),
        out_shape=[jax.ShapeDtypeStruct((N_ALL, D), F32),
                   jax.ShapeDtypeStruct((N_ALL, B_INNER), BF16),
                   jax.ShapeDtypeStruct((N_ALL, B_INNER), BF16)],
        compiler_params=_cparams(("arbitrary",)),
        name="mlstm_inproj",
    )(p1, p2, xa, mod0, tmeta, ys, mod, g, w_in)


def _log_sigmoid(x):
    return jnp.minimum(x, 0.0) - jnp.log(1.0 + jnp.exp(-jnp.abs(x)))


def _qkv_kernel(xm_ref, prev_ref, next_ref, cw_ref, cb_ref, wq_ref, wkt_ref, wv_ref, wg_ref, wgt_ref,
                bg_ref, bgt_ref, xc_ref, q_ref, kt_ref, v_ref, gt_ref, gtt_ref, tot_ref):
    i = pl.program_id(0)
    lat = i < LAT_TILES
    first = jnp.where(lat, i % TILES_PER_SEQ == 0, True)
    last = jnp.where(lat, i % TILES_PER_SEQ == TILES_PER_SEQ - 1, True)
    xmb = xm_ref[...]
    prev = jnp.where(first, 0.0, prev_ref[HALO - 8:, :].astype(F32))
    nxt = jnp.where(last, 0.0, next_ref[0:8, :].astype(F32))
    half = CONV_K // 2

    rr = lax.broadcasted_iota(jnp.int32, (TM, TM), 0)
    cc = lax.broadcasted_iota(jnp.int32, (TM, TM), 1)
    shifts = {t: (cc == rr + (t - half)).astype(BF16) for t in range(CONV_K) if t != half}

    def conv_head(h):
        hs = slice(h * DH, (h + 1) * DH)
        xh = xmb[:, hs]

        def conv_rows(ext):
            out = jnp.zeros((8, DH), F32) + cb_ref[:, hs]
            for t in range(CONV_K):
                out = out + ext[8 + t - half:16 + t - half] * cw_ref[t:t + 1, hs]
            return out

        acc = xh.astype(F32) * cw_ref[half:half + 1, hs] + cb_ref[:, hs]
        for t, shift in shifts.items():
            acc = acc + jnp.dot(shift, xh, preferred_element_type=F32) * cw_ref[t:t + 1, hs]
        top = conv_rows(jnp.concatenate([prev[:, hs], xh[0:16].astype(F32)], axis=0))
        bot = conv_rows(jnp.concatenate([xh[TM - 16:].astype(F32), nxt[:, hs]], axis=0))
        acc = jnp.concatenate([top, acc[8:TM - 8], bot], axis=0)
        xcb = (acc * jax.nn.sigmoid(acc)).astype(BF16)
        xc_ref[:, hs] = xcb
        return xcb

    def project(h, xcb):
        qh = jnp.dot(xcb, wq_ref[h], preferred_element_type=F32)
        q_ref[:, h * DQK:(h + 1) * DQK] = (qh * (DQK ** -0.5)).astype(BF16)
        kt_ref[h * DQK:(h + 1) * DQK, :] = _dot_nt(wkt_ref[h], xcb).astype(BF16)

    gts = jnp.dot(xmb, wg_ref[...], preferred_element_type=F32) + bg_ref[...]
    gtt = _dot_nt(wgt_ref[...], xmb) + bgt_ref[...]

    pending = None
    for h in range(HEADS):
        hs = slice(h * DH, (h + 1) * DH)
        xcb = conv_head(h)
        v_ref[:, hs] = jnp.dot(xmb[:, hs], wv_ref[h], preferred_element_type=F32).astype(BF16)
        if pending is not None:
            project(*pending)
        pending = (h, xcb)
    project(*pending)
    lane = lax.broadcasted_iota(jnp.int32, (CHUNK, LANES), 1)
    lane_kind = (lane >> 3) & 3
    row = lax.broadcasted_iota(jnp.int32, (4 * HEADS, CHUNK), 0)
    row_kind = (row >> 3) & 3
    r = lax.broadcasted_iota(jnp.int32, (CHUNK, CHUNK), 0)
    c = lax.broadcasted_iota(jnp.int32, (CHUNK, CHUNK), 1)
    lower = (c <= r).astype(BF16)
    upper = (c >= r).astype(BF16)
    row_sums = jnp.concatenate([upper, lower, jnp.ones((CHUNK, LANES), BF16)], axis=1)
    for ch in range(TM // CHUNK):
        ts = slice(ch * CHUNK, (ch + 1) * CHUNK)
        g = gts[ts]
        g = jnp.where((lane_kind & 1) == 1, _log_sigmoid(g), g)
        terms = _bf16_terms(g, 3)
        pre = sum(_dot_nn(lower, t) for t in terms)
        suf = sum(_dot_nn(upper, t) for t in terms)
        gt_ref[ts, :] = jnp.where(lane_kind == 1, pre, jnp.where(lane_kind == 3, suf, g))
        gt = gtt[:, ts]
        gt = jnp.where((row_kind & 1) == 1, _log_sigmoid(gt), gt)
        sums = _dot_exact_rhs(gt, row_sums)
        gtt_ref[:, ts] = jnp.where(row_kind == 1, sums[:, :CHUNK],
                                   jnp.where(row_kind == 3, sums[:, CHUNK:2 * CHUNK], gt))
        tot_ref[:, ts] = sums[:, 2 * CHUNK:]


def _qkv(xm, conv_w, conv_b, wq, wkt, wv, wg, wgt, bg, bgt):
    const = lambda i: (0, 0)
    const3 = lambda i: (0, 0, 0)
    hb = TM // HALO
    n_hb = N_ALL // HALO
    return pl.pallas_call(
        _qkv_kernel,
        grid=(ALL_TILES,),
        in_specs=[
            pl.BlockSpec((TM, B_INNER), lambda i: (i, 0)),
            pl.BlockSpec((HALO, B_INNER), lambda i: (jnp.maximum(i * hb - 1, 0), 0)),
            pl.BlockSpec((HALO, B_INNER), lambda i: (jnp.minimum((i + 1) * hb, n_hb - 1), 0)),
            pl.BlockSpec((CONV_K, B_INNER), const),
            pl.BlockSpec((1, B_INNER), const),
            pl.BlockSpec((HEADS, DH, DQK), const3),
            pl.BlockSpec((HEADS, DQK, DH), const3),
            pl.BlockSpec((HEADS, DH, DV), const3),
            pl.BlockSpec((B_INNER, LANES), const),
            pl.BlockSpec((4 * HEADS, B_INNER), const),
            pl.BlockSpec((1, LANES), const),
            pl.BlockSpec((4 * HEADS, 1), const),
        ],
        out_specs=[
            pl.BlockSpec((TM, B_INNER), lambda i: (i, 0)),
            pl.BlockSpec((TM, HEADS * DQK), lambda i: (i, 0)),
            pl.BlockSpec((HEADS * DQK, TM), lambda i: (0, i)),
            pl.BlockSpec((TM, B_INNER), lambda i: (i, 0)),
            pl.BlockSpec((TM, LANES), lambda i: (i, 0)),
            pl.BlockSpec((4 * HEADS, TM), lambda i: (0, i)),
            pl.BlockSpec((4 * HEADS, TM), lambda i: (0, i)),
        ],
        out_shape=[
            jax.ShapeDtypeStruct((N_ALL, B_INNER), BF16),
            jax.ShapeDtypeStruct((N_ALL, HEADS * DQK), BF16),
            jax.ShapeDtypeStruct((HEADS * DQK, N_ALL), BF16),
            jax.ShapeDtypeStruct((N_ALL, B_INNER), BF16),
            jax.ShapeDtypeStruct((N_ALL, LANES), F32),
            jax.ShapeDtypeStruct((4 * HEADS, N_ALL), F32),
            jax.ShapeDtypeStruct((4 * HEADS, N_ALL), F32),
        ],
        compiler_params=_cparams(("parallel",)),
        name="mlstm_qkv",
    )(xm, xm, xm, conv_w, conv_b, wq, wkt, wv, wg, wgt, bg, bgt)


CTX_CHUNKS = CTX // CHUNK
LAT_CHUNKS = SEQ // CHUNK
SCAN_STEPS = CTX_CHUNKS + LAT_CHUNKS


def _scan_dir(nb, d, q_ref, kt_ref, v_ref, g_ref, gt_ref, tot_ref, o_ref, c_scr, m_scr):
    r = lax.broadcasted_iota(jnp.int32, (CHUNK, CHUNK), 0)
    c = lax.broadcasted_iota(jnp.int32, (CHUNK, CHUNK), 1)
    if d == 0:
        keep = c <= r
    else:
        keep = c >= r
    ones_blk = jnp.ones((CHUNK, LANES), BF16)
    bcol_all = g_ref[...]
    gtt = gt_ref[...]
    tot_all = tot_ref[...]
    base = 2 * HEADS * d

    def head_first(h):
        li_r = gtt[base + h:base + h + 1, :]
        b_r = gtt[base + HEADS + h:base + HEADS + h + 1, :]
        b_last = tot_all[base + HEADS + h:base + HEADS + h + 1, :]
        m_old = m_scr[nb, d, h]
        c_old = c_scr[nb, d, h]
        qh = q_ref[:, h * DQK:(h + 1) * DQK]
        kth = kt_ref[h * DQK:(h + 1) * DQK, :]
        vaug = jnp.concatenate([v_ref[:, h * DV:(h + 1) * DV], ones_blk], axis=1)
        g_r = b_last - b_r + li_r
        m_new = jnp.maximum(b_last + m_old, jnp.max(g_r, axis=-1, keepdims=True))
        decay = jnp.exp(b_last + m_old - m_new)
        wk = jnp.exp(g_r - m_new)
        kwt = (kth.astype(F32) * wk).astype(BF16)
        qk = jnp.dot(qh, kth, preferred_element_type=F32)
        qc = jnp.dot(qh, c_old.astype(BF16), preferred_element_type=F32)
        c_scr[nb, d, h] = jnp.concatenate([decay] * 3, axis=1) * c_old + jnp.dot(
            kwt, vaug, preferred_element_type=F32)
        m_scr[nb, d, h] = m_new
        return h, li_r, b_r, m_old, vaug, qk, qc

    def head_second(h, li_r, b_r, m_old, vaug, qk, qc):
        b_full = jnp.broadcast_to(bcol_all[:, base + HEADS + h:base + HEADS + h + 1], (CHUNK, LANES))
        dmat = jnp.where(keep, b_full - (b_r - li_r), NEG_INF)
        inter = b_full + m_old
        m_t = jnp.maximum(inter, jnp.max(dmat, axis=-1, keepdims=True))
        p = jnp.exp(dmat - m_t)
        s = qk * p
        a = jnp.exp(inter - m_t)
        num = jnp.concatenate([a] * 3, axis=1) * qc + jnp.dot(
            s.astype(BF16), vaug, preferred_element_type=F32)
        inv = 1.0 / jnp.maximum(jnp.abs(num[:, DV:]), jnp.exp(-m_t))
        o_ref[nb, :, h * DV:(h + 1) * DV] = (
            num[:, :DV] * jnp.concatenate([inv] * 2, axis=1)).astype(BF16)

    return head_first, head_second


SCAN_NB = 2
SCAN_IN = 6
SCAN_SKEW = 1


def _scan_kernel(*refs):
    n_in = SCAN_NB * 2 * SCAN_IN
    ins, outs = refs[:n_in], refs[n_in:n_in + 2]
    c_scr, m_scr = refs[n_in + 2:]

    @pl.when(pl.program_id(1) == 0)
    def _():
        c_scr[...] = jnp.zeros_like(c_scr)
        m_scr[...] = jnp.zeros_like(m_scr)

    units = []
    for nb in range(SCAN_NB):
        for d in range(2):
            k = nb * 2 + d
            first, second = _scan_dir(nb, d, *ins[k * SCAN_IN:(k + 1) * SCAN_IN], outs[d], c_scr, m_scr)
            units += [(first, second, h) for h in range(HEADS)]
    pending = []
    for first, second, h in units:
        pending.append((second, first(h)))
        if len(pending) > SCAN_SKEW:
            fn, old_vals = pending.pop(0)
            fn(*old_vals)
    for fn, old_vals in pending:
        fn(*old_vals)


def _scan(q, kt, v, gts, gtt, tot):
    lat_blk = N_LAT // CHUNK

    def fwd_in(b, j):
        return jnp.where(j < CTX_CHUNKS, lat_blk + b * CTX_CHUNKS + j, b * LAT_CHUNKS + j - CTX_CHUNKS)

    def bwd_in(b, j):
        return jnp.where(j < CTX_CHUNKS, lat_blk + b * CTX_CHUNKS + (CTX_CHUNKS - 1 - j),
                         b * LAT_CHUNKS + (SCAN_STEPS - 1 - j))

    def fwd_out(b, j):
        return b * LAT_CHUNKS + jnp.maximum(j - CTX_CHUNKS, 0)

    def bwd_out(b, j):
        return b * LAT_CHUNKS + jnp.minimum(SCAN_STEPS - 1 - j, LAT_CHUNKS - 1)

    n_g = BATCH // SCAN_NB

    def specs(fn, nb):
        blk = lambda g, j: fn(nb * n_g + g, j)
        return [
            pl.BlockSpec((CHUNK, HEADS * DQK), lambda g, j: (blk(g, j), 0)),
            pl.BlockSpec((HEADS * DQK, CHUNK), lambda g, j: (0, blk(g, j))),
            pl.BlockSpec((CHUNK, B_INNER), lambda g, j: (blk(g, j), 0)),
            pl.BlockSpec((CHUNK, LANES), lambda g, j: (blk(g, j), 0)),
            pl.BlockSpec((4 * HEADS, CHUNK), lambda g, j: (0, blk(g, j))),
            pl.BlockSpec((4 * HEADS, CHUNK), lambda g, j: (0, blk(g, j))),
        ]

    def out_spec(fn):
        return pl.BlockSpec((SCAN_NB, CHUNK, B_INNER), lambda g, j: (0, fn(g, j), 0))

    in_specs, operands = [], []
    for nb in range(SCAN_NB):
        for fn_in in (fwd_in, bwd_in):
            in_specs += specs(fn_in, nb)
            operands += [q, kt, v, gts, gtt, tot]
    h_shape = jax.ShapeDtypeStruct((SCAN_NB, N_LAT // SCAN_NB, B_INNER), BF16)
    hf, hb = pl.pallas_call(
        _scan_kernel,
        grid=(n_g, SCAN_STEPS),
        in_specs=in_specs,
        out_specs=[out_spec(fwd_out), out_spec(bwd_out)],
        out_shape=[h_shape, h_shape],
        scratch_shapes=[pltpu.VMEM((SCAN_NB, 2, HEADS, DQK, DV + LANES), F32),
                        pltpu.VMEM((SCAN_NB, 2, HEADS, 1, LANES), F32)],
        compiler_params=_cparams(("parallel", "arbitrary")),
        name="mlstm_scan",
    )(*operands)
    return hf.reshape(N_LAT, B_INNER), hb.reshape(N_LAT, B_INNER)


def _combine_kernel(x_ref, mod_ref, hf_ref, hb_ref, xc_ref, op_ref, hg_ref, sk_ref, wout_ref,
                    g2_ref, wr_ref, br_ref, o_ref, meta_ref, tmeta_ref, cnt_ref, cntr_ref,
                    y_scr, carry, carry_r):
    for h in range(HEADS):
        hs = slice(h * DV, (h + 1) * DV)
        s = hf_ref[:, hs].astype(F32) + hb_ref[:, hs].astype(F32)
        s = s * lax.rsqrt(jnp.mean(s * s, axis=-1, keepdims=True) + EPS)
        y = jax.nn.sigmoid(op_ref[:, hs].astype(F32)) * (
            s * hg_ref[:, hs] + sk_ref[:, hs] * xc_ref[:, hs].astype(F32))
        y_scr[:, hs] = y.astype(BF16)
        if h % 2 == 1:
            ks = slice((h - 1) * DV, (h + 1) * DV)
            part = jnp.dot(y_scr[:, ks], wout_ref[ks, :], preferred_element_type=F32)
            out = part if h == 1 else out + part
    xn = x_ref[...] + mod_ref[0, 2:3, :] * out
    o_ref[...] = xn
    _route_tile(xn, mod_ref, g2_ref, wr_ref, br_ref, meta_ref, tmeta_ref, cnt_ref, cntr_ref,
                carry, carry_r)


def _combine(xa, mod, hf, hb, xc, op, head_g, skip, w_out, g2, wr, br):
    const = lambda i: (0, 0)
    row = lambda i: (i, 0)
    r_in, r_out, r_shape, r_scratch = _router_specs(LAT_TILES)
    return pl.pallas_call(
        _combine_kernel,
        grid=(LAT_TILES,),
        in_specs=[
            pl.BlockSpec((TM, D), row),
            pl.BlockSpec((1, 6, D), lambda i: (_mod_row(i), 0, 0)),
            pl.BlockSpec((TM, B_INNER), row),
            pl.BlockSpec((TM, B_INNER), row),
            pl.BlockSpec((TM, B_INNER), row),
            pl.BlockSpec((TM, B_INNER), row),
            pl.BlockSpec((1, B_INNER), const),
            pl.BlockSpec((1, B_INNER), const),
            pl.BlockSpec((B_INNER, D), const),
        ] + r_in,
        out_specs=[pl.BlockSpec((TM, D), row)] + r_out,
        out_shape=[jax.ShapeDtypeStruct((N_LAT, D), F32)] + r_shape,
        scratch_shapes=[pltpu.VMEM((TM, B_INNER), BF16)] + r_scratch,
        compiler_params=_cparams(("arbitrary",)),
        name="mlstm_combine",
    )(xa, mod, hf, hb, xc, op, head_g, skip, w_out, g2, wr, br)


def _router_weights(w_grp, b_grp, w_exp, b_exp):
    wr = jnp.zeros((LANES, D), F32).at[:N_EXP].set(w_exp.T).at[N_EXP:N_EXP + N_GRP].set(w_grp.T)
    br = jnp.zeros((LANES, 1), F32).at[:N_EXP, 0].set(b_exp).at[N_EXP:N_EXP + N_GRP, 0].set(b_grp)
    return wr.astype(BF16), br


PLAN_BLK = 2048
PLAN_PAIRS = 256


def _plan_kernel(meta_ref, cc_ref, cr_ref, pp_ref, plan_ref):
    row = lax.broadcasted_iota(jnp.int32, (LANES, LANES), 0)
    col = lax.broadcasted_iota(jnp.int32, (LANES, LANES), 1)
    cnt_c = cc_ref[...]
    cnt_r = cr_ref[0:1, :]
    starts_c = jnp.sum(jnp.where(col < row, cnt_r, 0.0), axis=1, keepdims=True)
    erow = lax.broadcasted_iota(jnp.int32, (LANES, PLAN_BLK), 0).astype(F32)
    for j in range(2):
        pos = jnp.sum(jnp.where(erow == meta_ref[j:j + 1, :], starts_c, 0.0), axis=0,
                      keepdims=True) + meta_ref[j + 2:j + 3, :]
        pp_ref[j:j + 1, :] = pos.astype(jnp.int32)
    pp_ref[2:8, :] = jnp.zeros((6, PLAN_BLK), jnp.int32)

    @pl.when(pl.program_id(0) == 0)
    def _():
        ends_c = starts_c + cnt_c
        starts_r = jnp.sum(jnp.where(row < col, cnt_c, 0.0), axis=0, keepdims=True)
        ends_r = starts_r + cnt_r

        def tiles(st, en, cn):
            first_tile = jnp.floor(st * (1.0 / TM))
            n = jnp.where(cn > 0.0, jnp.floor((en - 1.0) * (1.0 / TM)) - first_tile + 1.0, 0.0)
            return first_tile, n

        ft_c, pairs_c = tiles(starts_c, ends_c, cnt_c)
        _, pairs_r = tiles(starts_r, ends_r, cnt_r)
        pend_c = jnp.sum(jnp.where(col <= row, pairs_r, 0.0), axis=1, keepdims=True)
        pstart_c = pend_c - pairs_c
        total = jnp.sum(pairs_r, axis=1, keepdims=True)
        q = lax.broadcasted_iota(jnp.int32, (1, PLAN_PAIRS), 1).astype(F32)
        erow_p = lax.broadcasted_iota(jnp.int32, (LANES, PLAN_PAIRS), 0).astype(F32)

        def at(qv):
            qc = jnp.maximum(jnp.minimum(qv, total - 1.0), 0.0)
            e = jnp.sum(jnp.where(qc >= pend_c, 1.0, 0.0), axis=0, keepdims=True)
            oh = erow_p == e
            tile = jnp.sum(jnp.where(oh, ft_c - pstart_c, 0.0), axis=0, keepdims=True) + qc
            return e, oh, tile

        e_q, oh, tile_q = at(q)
        e_p, _, tile_p = at(q - 1.0)
        lo = jnp.sum(jnp.where(oh, starts_c, 0.0), axis=0, keepdims=True) - tile_q * TM
        hi = jnp.sum(jnp.where(oh, ends_c, 0.0), axis=0, keepdims=True) - tile_q * TM
        rows = (tile_q, e_q, jnp.clip(lo, 0.0, TM), jnp.clip(hi, 0.0, TM),
                jnp.where((q == 0.0) | (tile_q != tile_p), 1.0, 0.0),
                jnp.where((q == 0.0) | (e_q != e_p), 1.0, 0.0),
                jnp.where(q < total, 1.0, 0.0),
                jnp.zeros((1, PLAN_PAIRS), F32))
        for j, v in enumerate(rows):
            plan_ref[j:j + 1, :] = v.astype(jnp.int32)


def _route_plan(meta, cnt_c, cnt_r, n_rows):
    n_pairs = 2 * n_rows // TM + N_EXP - 1
    assert n_pairs <= PLAN_PAIRS and n_rows % PLAN_BLK == 0
    const = lambda i: (0, 0)
    pp, plan = pl.pallas_call(
        _plan_kernel,
        grid=(n_rows // PLAN_BLK,),
        in_specs=[pl.BlockSpec((8, PLAN_BLK), lambda i: (0, i)),
                  pl.BlockSpec((LANES, 1), const),
                  pl.BlockSpec((8, LANES), const)],
        out_specs=[pl.BlockSpec((8, PLAN_BLK), lambda i: (0, i)),
                   pl.BlockSpec((8, PLAN_PAIRS), const)],
        out_shape=[jax.ShapeDtypeStruct((8, n_rows), jnp.int32),
                   jax.ShapeDtypeStruct((8, PLAN_PAIRS), jnp.int32)],
        compiler_params=_cparams(("arbitrary",)),
        name="moe_plan",
    )(meta, cnt_c, cnt_r)
    return pp[0], pp[1], tuple(plan[j, :n_pairs] for j in range(7))


def _moe_experts(xa, routing, mod, g2, w1, w3, w2, layer, n_rows):
    meta, _, cnt_c, cnt_r = routing
    p1, p2, plan = _route_plan(meta, cnt_c, cnt_r, n_rows)
    xs = _scatter(p1, p2, xa, mod, g2, n_rows // TM)
    return p1, p2, _grouped(plan, xs, w1, w3, w2, layer)


def kernel(x, c, ctx, c_ctx, norm_g, w_ada, b_ada, a_w_in, a_b_in, a_g_v, a_w_s, a_b_s, a_w_out,
           b_w_in, b_conv_w, b_conv_b, b_w_q, b_w_k, b_w_v, b_w_gate, b_b_gate, b_head_g, b_skip,
           b_w_out, moe_w_grp, moe_b_grp, moe_w_exp, moe_b_exp, moe_w1, moe_w3, moe_w2, final_g):
    cc = jnp.zeros((MOD_ROWS, D), F32).at[:BATCH].set(c).at[BATCH].set(c_ctx)
    mods = _ada(cc, w_ada, b_ada).reshape(2, MOD_ROWS, 6, D)

    mod = mods[0]
    b_s_full = jnp.repeat(a_b_s[0].T, A_GC, axis=1)
    g2 = norm_g[0, 1].reshape(1, D)
    wr, br = _router_weights(moe_w_grp[0], moe_b_grp[0], moe_w_exp[0], moe_b_exp[0])
    xa, *routing = _gmlp(x.reshape(N_LAT, D), ctx.reshape(N_CTX, D), mod, norm_g[0, 0].reshape(1, D),
                         a_w_in[0].astype(BF16), a_b_in[0].reshape(1, -1), a_g_v[0].reshape(1, -1),
                         a_w_s[0].astype(BF16), b_s_full, a_w_out[0].astype(BF16), g2, wr, br)
    p1, p2, ys = _moe_experts(xa, routing, mod, g2, moe_w1, moe_w3, moe_w2, 0, N_ALL)

    mod = mods[1]
    xa, xm, op = _inproj(p1, p2, xa, mods[0], routing[1], ys, mod, norm_g[1, 0].reshape(1, D),
                         b_w_in[0].astype(BF16))
    wg = jnp.zeros((B_INNER, LANES), F32).at[:, :4 * HEADS].set(b_w_gate[0]).astype(BF16)
    bg = jnp.zeros((1, LANES), F32).at[0, :4 * HEADS].set(b_b_gate[0])
    xc, q, kt, v, gts, gtt, tot = _qkv(xm, b_conv_w[0], b_conv_b[0].reshape(1, -1),
                                  b_w_q[0].astype(BF16),
                                  jnp.transpose(b_w_k[0], (0, 2, 1)).astype(BF16),
                                  b_w_v[0].astype(BF16), wg, b_w_gate[0].T.astype(BF16), bg,
                                  b_b_gate[0].reshape(-1, 1))
    hf, hb = _scan(q, kt, v, gts, gtt, tot)
    g2 = norm_g[1, 1].reshape(1, D)
    wr, br = _router_weights(moe_w_grp[1], moe_b_grp[1], moe_w_exp[1], moe_b_exp[1])
    xl, *routing = _combine(xa, mod, hf, hb, xc, op, b_head_g[0].reshape(1, -1),
                            b_skip[0].reshape(1, -1), b_w_out[0].astype(BF16), g2, wr, br)
    p1, p2, ys = _moe_experts(xl, routing, mod, g2, moe_w1, moe_w3, moe_w2, 1, N_LAT)
    out = _gather(p1, p2, xl, mod, routing[1], final_g.reshape(1, D), ys, LAT_TILES, True)
    return out.reshape(BATCH, SEQ, D)
```

```python
import functools

import jax
import jax.numpy as jnp
from jax import lax
from jax.experimental import pallas as pl
from jax.experimental.pallas import tpu as pltpu

F32 = jnp.float32
BF16 = jnp.bfloat16

D = 1024
BATCH = 8
SEQ = 2048
CTX = 256
EPS = 1e-6
NEG_INF = -1e30
N_LAT = BATCH * SEQ
N_CTX = BATCH * CTX
N_ALL = N_LAT + N_CTX

TM = 256
LAT_TILES = N_LAT // TM
ALL_TILES = N_ALL // TM
TILES_PER_SEQ = SEQ // TM
MOD_ROWS = 16

CHUNK = 128
A_HALF = 2048
A_GROUPS = 8
A_GC = A_HALF // A_GROUPS
B_INNER = 2048
HEADS = 8
DH = B_INNER // HEADS
DQK = DH // 2
DV = DH
CONV_K = 5
HALO = 16
N_EXP = 32
N_GRP = 4
EXP_PER_GRP = 8
D_EXP = 512
LANES = 128

VMEM_LIMIT = 56 * 1024 * 1024


def _cparams(sem):
    return pltpu.CompilerParams(dimension_semantics=sem, vmem_limit_bytes=VMEM_LIMIT)


def _mod_row(i):
    return jnp.where(i < LAT_TILES, i // TILES_PER_SEQ, BATCH)


def _norm_mod(x, g, shift, scale):
    y = x * lax.rsqrt(jnp.mean(x * x, axis=-1, keepdims=True) + EPS) * g
    return y * (1.0 + scale) + shift


def _bf16_terms(x, n):
    terms = []
    r = x
    for _ in range(n):
        t = r.astype(BF16)
        terms.append(t)
        r = r - t.astype(F32)
    return terms


def _dot_nt(a, b):
    return lax.dot_general(a, b, (((1,), (1,)), ((), ())), preferred_element_type=F32)


def _dot_nn(a, b):
    return jnp.dot(a, b, preferred_element_type=F32)


def _dot3(a, b, dot=_dot_nn):
    a1, a2 = _bf16_terms(a, 2)
    b1, b2 = _bf16_terms(b, 2)
    return dot(a1, b1) + (dot(a1, b2) + dot(a2, b1))


def _dot_exact_lhs(a_bf16, b, dot=_dot_nn):
    return sum(dot(a_bf16, t) for t in _bf16_terms(b, 3))


def _dot_exact_rhs(a, b_bf16, dot=_dot_nn):
    return sum(dot(t, b_bf16) for t in _bf16_terms(a, 3))


def _gelu_tanh(x):
    half = 0.5 * x
    t = jnp.tanh(x * (0.7978845608028654 + 0.035677408136300125 * (x * x)))
    return half + half * t


ADA_BN = 1536


def _ada_kernel(c_ref, w_ref, b_ref, o_ref):
    c = c_ref[...]
    a = c * jax.nn.sigmoid(c)
    o_ref[0] = _dot3(a, w_ref[0]) + b_ref[0]


def _ada(cc, w_ada, b_ada):
    depth = w_ada.shape[0]
    return pl.pallas_call(
        _ada_kernel,
        grid=(depth, 6 * D // ADA_BN),
        in_specs=[
            pl.BlockSpec((MOD_ROWS, D), lambda l, j: (0, 0)),
            pl.BlockSpec((1, D, ADA_BN), lambda l, j: (l, 0, j)),
            pl.BlockSpec((1, 1, ADA_BN), lambda l, j: (l, 0, j)),
        ],
        out_specs=pl.BlockSpec((1, MOD_ROWS, ADA_BN), lambda l, j: (l, 0, j)),
        out_shape=jax.ShapeDtypeStruct((depth, MOD_ROWS, 6 * D), F32),
        compiler_params=_cparams(("parallel", "parallel")),
        name="ada",
    )(cc, w_ada, b_ada.reshape(depth, 1, 6 * D))


GM_CH = 512


def _gmlp_kernel(xl_ref, xc_ref, mod_ref, g_ref, win_ref, bin_ref, gv_ref, ws_ref, bs_ref, wout_ref,
                 g2_ref, wr_ref, br_ref, o_ref, meta_ref, tmeta_ref, cnt_ref, cntr_ref,
                 z_scr, y_scr, carry, carry_r):
    x = jnp.where(pl.program_id(0) < LAT_TILES, xl_ref[...], xc_ref[...])
    h = _norm_mod(x, g_ref[...], mod_ref[0, 0:1, :], mod_ref[0, 1:2, :])
    hb = h.astype(BF16)
    s1 = jnp.zeros((TM, 1), F32)
    s2 = jnp.zeros((TM, 1), F32)
    n_ch = 2 * A_HALF // GM_CH
    for j in list(range(n_ch // 2, n_ch)) + list(range(n_ch // 2)):
        cs = slice(j * GM_CH, (j + 1) * GM_CH)
        zc = jnp.dot(hb, win_ref[:, cs], preferred_element_type=F32) + bin_ref[:, cs]
        zc = _gelu_tanh(zc)
        z_scr[:, cs] = zc
        if j * GM_CH >= A_HALF:
            s1 = s1 + jnp.sum(zc, axis=-1, keepdims=True)
            s2 = s2 + jnp.sum(zc * zc, axis=-1, keepdims=True)
    mu = s1 * (1.0 / A_HALF)
    rstd = lax.rsqrt(s2 * (1.0 / A_HALF) - mu * mu + EPS)
    for c in range(TM // CHUNK):
        rs = slice(c * CHUNK, (c + 1) * CHUNK)
        for g in range(A_GROUPS):
            cs = slice(g * A_GC, (g + 1) * A_GC)
            vs = slice(A_HALF + g * A_GC, A_HALF + (g + 1) * A_GC)
            v = (z_scr[rs, vs] - mu[rs]) * rstd[rs] * gv_ref[:, cs]
            s = jnp.dot(ws_ref[g], v.astype(BF16), preferred_element_type=F32) + bs_ref[:, cs]
            y_scr[rs, cs] = (z_scr[rs, cs] * s).astype(BF16)
    out = jnp.dot(y_scr[...], wout_ref[...], preferred_element_type=F32)
    xn = x + mod_ref[0, 2:3, :] * out
    o_ref[...] = xn
    _route_tile(xn, mod_ref, g2_ref, wr_ref, br_ref, meta_ref, tmeta_ref, cnt_ref, cntr_ref,
                carry, carry_r)


def _gmlp(xl, xc, mod, g, w_in, b_in, g_v, w_s, b_s_full, w_out, g2, wr, br):
    const = lambda i: (0, 0)
    r_in, r_out, r_shape, r_scratch = _router_specs(ALL_TILES)
    return pl.pallas_call(
        _gmlp_kernel,
        grid=(ALL_TILES,),
        in_specs=[
            pl.BlockSpec((TM, D), lambda i: (jnp.minimum(i, LAT_TILES - 1), 0)),
            pl.BlockSpec((TM, D), lambda i: (jnp.maximum(i - LAT_TILES, 0), 0)),
            pl.BlockSpec((1, 6, D), lambda i: (_mod_row(i), 0, 0)),
            pl.BlockSpec((1, D), const),
            pl.BlockSpec((D, 2 * A_HALF), const),
            pl.BlockSpec((1, 2 * A_HALF), const),
            pl.BlockSpec((1, A_HALF), const),
            pl.BlockSpec((A_GROUPS, CHUNK, CHUNK), lambda i: (0, 0, 0)),
            pl.BlockSpec((CHUNK, A_HALF), const),
            pl.BlockSpec((A_HALF, D), const),
        ] + r_in,
        out_specs=[pl.BlockSpec((TM, D), lambda i: (i, 0))] + r_out,
        out_shape=[jax.ShapeDtypeStruct((N_ALL, D), F32)] + r_shape,
        scratch_shapes=[pltpu.VMEM((TM, 2 * A_HALF), F32), pltpu.VMEM((TM, A_HALF), BF16)] + r_scratch,
        compiler_params=_cparams(("arbitrary",)),
        name="gmlp",
    )(xl, xc, mod, g, w_in, b_in, g_v, w_s, b_s_full, w_out, g2, wr, br)


def _route_tile(x, mod_ref, g_ref, wr_ref, br_ref, meta_ref, tmeta_ref, cnt_ref, cntr_ref,
                carry, carry_r):
    @pl.when(pl.program_id(0) == 0)
    def _():
        carry[...] = jnp.zeros_like(carry)
        carry_r[...] = jnp.zeros_like(carry_r)

    h = _norm_mod(x, g_ref[...], mod_ref[0, 3:4, :], mod_ref[0, 4:5, :])
    lt = _dot_nt(wr_ref[...], h.astype(BF16)) + br_ref[...]
    e_t = lt[0:N_EXP]
    row8 = lax.broadcasted_iota(jnp.int32, (EXP_PER_GRP, TM), 0).astype(F32)
    g_t = jnp.where(row8 < N_GRP, lt[N_EXP:N_EXP + EXP_PER_GRP], -jnp.inf)
    gmax = jnp.max(g_t, axis=0, keepdims=True)
    p_g = 1.0 / jnp.sum(jnp.exp(g_t - gmax), axis=0, keepdims=True)
    g_idx = jnp.min(jnp.where(g_t == gmax, row8, float(EXP_PER_GRP)), axis=0, keepdims=True)
    sel = jnp.zeros((EXP_PER_GRP, TM), F32)
    for g in range(N_GRP):
        sel = sel + jnp.where(g_idx == g, e_t[g * EXP_PER_GRP:(g + 1) * EXP_PER_GRP], 0.0)
    m1 = jnp.max(sel, axis=0, keepdims=True)
    i1 = jnp.min(jnp.where(sel == m1, row8, float(EXP_PER_GRP)), axis=0, keepdims=True)
    sel2 = jnp.where(row8 == i1, -jnp.inf, sel)
    m2 = jnp.max(sel2, axis=0, keepdims=True)
    i2 = jnp.min(jnp.where(sel2 == m2, row8, float(EXP_PER_GRP)), axis=0, keepdims=True)
    e2 = jnp.exp(m2 - m1)
    w1 = p_g / (1.0 + e2)
    w2 = p_g * e2 / (1.0 + e2)
    row = lax.broadcasted_iota(jnp.int32, (LANES, TM), 0).astype(F32)
    id1 = g_idx * EXP_PER_GRP + i1
    id2 = g_idx * EXP_PER_GRP + i2
    oh1 = row == id1
    oh2 = row == id2
    oh = jnp.where(oh1, 1.0, 0.0) + jnp.where(oh2, 1.0, 0.0)
    before = (lax.broadcasted_iota(jnp.int32, (TM, TM), 0)
              < lax.broadcasted_iota(jnp.int32, (TM, TM), 1)).astype(BF16)
    ohb = oh.astype(BF16)
    tot = jnp.dot(ohb, before, preferred_element_type=F32) + carry[...]
    r1 = jnp.sum(jnp.where(oh1, tot, 0.0), axis=0, keepdims=True)
    r2 = jnp.sum(jnp.where(oh2, tot, 0.0), axis=0, keepdims=True)
    carry[...] += jnp.sum(oh, axis=-1, keepdims=True)
    carry_r[...] += _dot_nt(jnp.ones((8, TM), BF16), ohb)
    cnt_ref[...] = carry[...]
    cntr_ref[...] = carry_r[...]
    meta_ref[0:1, :] = id1
    meta_ref[1:2, :] = id2
    meta_ref[2:3, :] = r1
    meta_ref[3:4, :] = r2
    meta_ref[4:8, :] = jnp.zeros((4, TM), F32)
    tmeta_ref[...] = (jnp.where(row == 0.0, w1, 0.0) + jnp.where(row == 1.0, w2, 0.0)).T


def _router_specs(n_tiles):
    const = lambda i: (0, 0)
    n = n_tiles * TM
    in_specs = [pl.BlockSpec((1, D), const), pl.BlockSpec((LANES, D), const),
                pl.BlockSpec((LANES, 1), const)]
    out_specs = [pl.BlockSpec((8, TM), lambda i: (0, i)),
                 pl.BlockSpec((TM, LANES), lambda i: (i, 0)),
                 pl.BlockSpec((LANES, 1), const),
                 pl.BlockSpec((8, LANES), const)]
    out_shape = [jax.ShapeDtypeStruct((8, n), F32), jax.ShapeDtypeStruct((n, LANES), F32),
                 jax.ShapeDtypeStruct((LANES, 1), F32), jax.ShapeDtypeStruct((8, LANES), F32)]
    scratch = [pltpu.VMEM((LANES, 1), F32), pltpu.VMEM((8, LANES), F32)]
    return in_specs, out_specs, out_shape, scratch


ISSUE_UNROLL = 8
ROW_TILE = (8, LANES)
MXU_N = 256


def _row_copy_wait(buf_slot, sem_slot):
    pltpu.make_async_copy(buf_slot, buf_slot, sem_slot).wait()


def _scatter_kernel(p1_ref, p2_ref, x_ref, mod_ref, g_ref, xs_ref, buf, sem):
    i = pl.program_id(0)
    n_steps = pl.num_programs(0)
    slot = lax.rem(i, 2)

    def wait_slot(s):
        _row_copy_wait(buf.at[s], sem.at[s])
        _row_copy_wait(buf.at[s], sem.at[s])

    @pl.when(i >= 2)
    def _():
        wait_slot(slot)

    h = _norm_mod(x_ref[...], g_ref[...], mod_ref[0, 3:4, :], mod_ref[0, 4:5, :])
    buf[slot] = h.reshape(TM, *ROW_TILE)
    base = i * TM

    def body(r, carry):
        src = buf.at[slot, r]
        pltpu.make_async_copy(src, xs_ref.at[p1_ref[base + r]], sem.at[slot]).start()
        pltpu.make_async_copy(src, xs_ref.at[p2_ref[base + r]], sem.at[slot]).start(priority=1)
        return carry

    lax.fori_loop(0, TM, body, 0, unroll=ISSUE_UNROLL)

    @pl.when(i == n_steps - 1)
    def _():
        @pl.when(i >= 1)
        def _():
            wait_slot(1 - slot)

        wait_slot(slot)


def _scatter(p1, p2, xa, mod, g, n_tiles):
    n = n_tiles * TM
    return pl.pallas_call(
        _scatter_kernel,
        grid_spec=pltpu.PrefetchScalarGridSpec(
            num_scalar_prefetch=2,
            grid=(n_tiles,),
            in_specs=[
                pl.BlockSpec((TM, D), lambda i, p1, p2: (i, 0)),
                pl.BlockSpec((1, 6, D), lambda i, p1, p2: (_mod_row(i), 0, 0)),
                pl.BlockSpec((1, D), lambda i, p1, p2: (0, 0)),
            ],
            out_specs=pl.BlockSpec(memory_space=pl.ANY),
            scratch_shapes=[pltpu.VMEM((2, TM) + ROW_TILE, F32), pltpu.SemaphoreType.DMA((2,))],
        ),
        out_shape=jax.ShapeDtypeStruct((2 * n,) + ROW_TILE, F32),
        compiler_params=_cparams(("arbitrary",)),
        name="moe_scatter",
    )(p1, p2, xa, mod, g)


def _grouped_kernel(tile_ref, exp_ref, lo_ref, hi_ref, first_ref, newexp_ref, valid_ref,
                    xs_ref, w1_ref, w3_ref, w2_ref, ys_ref, wb1, wb3, wb2):
    q = pl.program_id(0)

    @pl.when(valid_ref[q] == 1)
    def _():
        @pl.when(newexp_ref[q] == 1)
        def _():
            wb1[...] = w1_ref[0, 0].astype(BF16)
            wb3[...] = w3_ref[0, 0].astype(BF16)
            wb2[...] = w2_ref[0, 0].astype(BF16)

        x = xs_ref[...].reshape(TM, D).astype(BF16)
        rows = lax.broadcasted_iota(jnp.int32, (TM, 1), 0)
        mine = (rows >= lo_ref[q]) & (rows < hi_ref[q])
        acts = []
        for j in range(D_EXP // MXU_N):
            cs = slice(j * MXU_N, (j + 1) * MXU_N)
            a = jnp.dot(x, wb1[:, cs], preferred_element_type=F32)
            b = jnp.dot(x, wb3[:, cs], preferred_element_type=F32)
            acts.append(jnp.where(mine, a * jax.nn.sigmoid(a) * b, 0.0).astype(BF16))
        act = jnp.concatenate(acts, axis=1)
        y = jnp.dot(act, wb2[...], preferred_element_type=F32).reshape(TM, *ROW_TILE)

        @pl.when(first_ref[q] == 1)
        def _():
            ys_ref[...] = y

        @pl.when(first_ref[q] == 0)
        def _():
            ys_ref[...] += y


def _grouped(plan, xs, w1, w3, w2, layer):
    n_pairs = plan[0].shape[0]
    tile_map = lambda q, tile, exp, *_: (tile[q], 0, 0)
    exp_map = lambda q, tile, exp, *_: (layer, exp[q], 0, 0)
    return pl.pallas_call(
        _grouped_kernel,
        grid_spec=pltpu.PrefetchScalarGridSpec(
            num_scalar_prefetch=7,
            grid=(n_pairs,),
            in_specs=[
                pl.BlockSpec((TM,) + ROW_TILE, tile_map),
                pl.BlockSpec((1, 1, D, D_EXP), exp_map),
                pl.BlockSpec((1, 1, D, D_EXP), exp_map),
                pl.BlockSpec((1, 1, D_EXP, D), exp_map),
            ],
            out_specs=pl.BlockSpec((TM,) + ROW_TILE, tile_map),
            scratch_shapes=[pltpu.VMEM((D, D_EXP), BF16), pltpu.VMEM((D, D_EXP), BF16),
                            pltpu.VMEM((D_EXP, D), BF16)],
        ),
        out_shape=jax.ShapeDtypeStruct(xs.shape, F32),
        compiler_params=_cparams(("arbitrary",)),
        name="moe_grouped",
    )(*plan, xs, w1, w3, w2)


def _gather_kernel(p1_ref, p2_ref, x_ref, mod_ref, tm_ref, fg_ref, ys_ref, o_ref, buf, sem,
                   *, final_norm):
    i = pl.program_id(0)
    n_steps = pl.num_programs(0)
    slot = lax.rem(i, 2)

    def issue(tile, s):
        base = tile * TM

        def body(r, carry):
            pltpu.make_async_copy(ys_ref.at[p1_ref[base + r]], buf.at[s, 0, r], sem.at[s]).start()
            pltpu.make_async_copy(ys_ref.at[p2_ref[base + r]], buf.at[s, 1, r],
                                  sem.at[s]).start(priority=1)
            return carry

        lax.fori_loop(0, TM, body, 0, unroll=ISSUE_UNROLL)

    @pl.when(i == 0)
    def _():
        issue(0, 0)

    @pl.when(i + 1 < n_steps)
    def _():
        issue(i + 1, 1 - slot)

    _row_copy_wait(buf.at[slot, 0], sem.at[slot])
    _row_copy_wait(buf.at[slot, 1], sem.at[slot])
    y = (tm_ref[:, 0:1] * buf[slot, 0].reshape(TM, D)
         + tm_ref[:, 1:2] * buf[slot, 1].reshape(TM, D))
    out = x_ref[...] + mod_ref[0, 5:6, :] * y
    if final_norm:
        out = out * lax.rsqrt(jnp.mean(out * out, axis=-1, keepdims=True) + EPS) * fg_ref[...]
    o_ref[...] = out


def _gather(p1, p2, xa, mod, tmeta, final_g, ys, n_tiles, final_norm):
    return pl.pallas_call(
        functools.partial(_gather_kernel, final_norm=final_norm),
        grid_spec=pltpu.PrefetchScalarGridSpec(
            num_scalar_prefetch=2,
            grid=(n_tiles,),
            in_specs=[
                pl.BlockSpec((TM, D), lambda i, p1, p2: (i, 0)),
                pl.BlockSpec((1, 6, D), lambda i, p1, p2: (_mod_row(i), 0, 0)),
                pl.BlockSpec((TM, LANES), lambda i, p1, p2: (i, 0)),
                pl.BlockSpec((1, D), lambda i, p1, p2: (0, 0)),
                pl.BlockSpec(memory_space=pl.ANY),
            ],
            out_specs=pl.BlockSpec((TM, D), lambda i, p1, p2: (i, 0)),
            scratch_shapes=[pltpu.VMEM((2, 2, TM) + ROW_TILE, F32), pltpu.SemaphoreType.DMA((2,))],
        ),
        out_shape=jax.ShapeDtypeStruct((n_tiles * TM, D), F32),
        compiler_params=_cparams(("arbitrary",)),
        name="moe_gather",
    )(p1, p2, xa, mod, tmeta, final_g, ys)


def _issue_row_gathers(p1_ref, p2_ref, ys_ref, buf, sem, tile, s, unrolled):
    base = tile * TM

    def one(r):
        pltpu.make_async_copy(ys_ref.at[p1_ref[base + r]], buf.at[s, 0, r], sem.at[s]).start()
        pltpu.make_async_copy(ys_ref.at[p2_ref[base + r]], buf.at[s, 1, r],
                              sem.at[s]).start(priority=1)

    if unrolled:
        for r in range(TM):
            one(r)
    else:
        def body(r, carry):
            one(r)
            return carry

        lax.fori_loop(0, TM, body, 0, unroll=ISSUE_UNROLL)


def _inproj_kernel(p1_ref, p2_ref, x_ref, mod0_ref, tm_ref, ys_ref, mod_ref, g_ref, win_ref,
                   xn_ref, xm_ref, op_ref, buf, sem):
    i = pl.program_id(0)
    n_steps = pl.num_programs(0)
    slot = lax.rem(i, 2)

    @pl.when(i == 0)
    def _():
        _issue_row_gathers(p1_ref, p2_ref, ys_ref, buf, sem, 0, 0, False)

    _row_copy_wait(buf.at[slot, 0], sem.at[slot])
    _row_copy_wait(buf.at[slot, 1], sem.at[slot])
    y = (tm_ref[:, 0:1] * buf[slot, 0].reshape(TM, D)
         + tm_ref[:, 1:2] * buf[slot, 1].reshape(TM, D))
    xn = x_ref[...] + mod0_ref[0, 5:6, :] * y
    xn_ref[...] = xn
    _issue_row_gathers(p1_ref, p2_ref, ys_ref, buf, sem, jnp.minimum(i + 1, n_steps - 1), 1 - slot,
                       True)
    h = _norm_mod(xn, g_ref[...], mod_ref[0, 0:1, :], mod_ref[0, 1:2, :])
    hb = h.astype(BF16)
    xm_ref[...] = jnp.dot(hb, win_ref[:, :B_INNER], preferred_element_type=F32).astype(BF16)
    op_ref[...] = jnp.dot(hb, win_ref[:, B_INNER:], preferred_element_type=F32).astype(BF16)

    @pl.when(i == n_steps - 1)
    def _():
        _row_copy_wait(buf.at[1 - slot, 0], sem.at[1 - slot])
        _row_copy_wait(buf.at[1 - slot, 1], sem.at[1 - slot])


def _inproj(p1, p2, xa, mod0, tmeta, ys, mod, g, w_in):
    const = lambda i, p1, p2: (0, 0)
    row = lambda i, p1, p2: (i, 0)
    mod_map = lambda i, p1, p2: (_mod_row(i), 0, 0)
    return pl.pallas_call(
        _inproj_kernel,
        grid_spec=pltpu.PrefetchScalarGridSpec(
            num_scalar_prefetch=2,
            grid=(ALL_TILES,),
            in_specs=[
                pl.BlockSpec((TM, D), row),
                pl.BlockSpec((1, 6, D), mod_map),
                pl.BlockSpec((TM, LANES), row),
                pl.BlockSpec(memory_space=pl.ANY),
                pl.BlockSpec((1, 6, D), mod_map),
                pl.BlockSpec((1, D), const),
                pl.BlockSpec((D, 2 * B_INNER), const),
            ],
            out_specs=[pl.BlockSpec((TM, D), row),
                       pl.BlockSpec((TM, B_INNER), row),
                       pl.BlockSpec((TM, B_INNER), row)],
            scratch_shapes=[pltpu.VMEM((2, 2, TM) + ROW_TILE, F32), pltpu.SemaphoreType.DMA((2,))],
        ),
        out_shape=[jax.ShapeDtypeStruct((N_ALL, D), F32),
                   jax.ShapeDtypeStruct((N_ALL, B_INNER), BF16),
                   jax.ShapeDtypeStruct((N_ALL, B_INNER), BF16)],
        compiler_params=_cparams(("arbitrary",)),
        name="mlstm_inproj",
    )(p1, p2, xa, mod0, tmeta, ys, mod, g, w_in)


def _log_sigmoid(x):
    return jnp.minimum(x, 0.0) - jnp.log(1.0 + jnp.exp(-jnp.abs(x)))


def _qkv_kernel(xm_ref, prev_ref, next_ref, cw_ref, cb_ref, wq_ref, wkt_ref, wv_ref, wg_ref, wgt_ref,
                bg_ref, bgt_ref, xc_ref, q_ref, kt_ref, v_ref, gt_ref, gtt_ref, tot_ref):
    i = pl.program_id(0)
    lat = i < LAT_TILES
    first = jnp.where(lat, i % TILES_PER_SEQ == 0, True)
    last = jnp.where(lat, i % TILES_PER_SEQ == TILES_PER_SEQ - 1, True)
    xmb = xm_ref[...]
    prev = jnp.where(first, 0.0, prev_ref[HALO - 8:, :].astype(F32))
    nxt = jnp.where(last, 0.0, next_ref[0:8, :].astype(F32))
    half = CONV_K // 2

    rr = lax.broadcasted_iota(jnp.int32, (TM, TM), 0)
    cc = lax.broadcasted_iota(jnp.int32, (TM, TM), 1)
    shifts = {t: (cc == rr + (t - half)).astype(BF16) for t in range(CONV_K) if t != half}

    def conv_head(h):
        hs = slice(h * DH, (h + 1) * DH)
        xh = xmb[:, hs]

        def conv_rows(ext):
            out = jnp.zeros((8, DH), F32) + cb_ref[:, hs]
            for t in range(CONV_K):
                out = out + ext[8 + t - half:16 + t - half] * cw_ref[t:t + 1, hs]
            return out

        acc = xh.astype(F32) * cw_ref[half:half + 1, hs] + cb_ref[:, hs]
        for t, shift in shifts.items():
            acc = acc + jnp.dot(shift, xh, preferred_element_type=F32) * cw_ref[t:t + 1, hs]
        top = conv_rows(jnp.concatenate([prev[:, hs], xh[0:16].astype(F32)], axis=0))
        bot = conv_rows(jnp.concatenate([xh[TM - 16:].astype(F32), nxt[:, hs]], axis=0))
        acc = jnp.concatenate([top, acc[8:TM - 8], bot], axis=0)
        xcb = (acc * jax.nn.sigmoid(acc)).astype(BF16)
        xc_ref[:, hs] = xcb
        return xcb

    def project(h, xcb):
        qh = jnp.dot(xcb, wq_ref[h], preferred_element_type=F32)
        q_ref[:, h * DQK:(h + 1) * DQK] = (qh * (DQK ** -0.5)).astype(BF16)
        kt_ref[h * DQK:(h + 1) * DQK, :] = _dot_nt(wkt_ref[h], xcb).astype(BF16)

    gts = jnp.dot(xmb, wg_ref[...], preferred_element_type=F32) + bg_ref[...]
    gtt = _dot_nt(wgt_ref[...], xmb) + bgt_ref[...]

    pending = None
    for h in range(HEADS):
        hs = slice(h * DH, (h + 1) * DH)
        xcb = conv_head(h)
        v_ref[:, hs] = jnp.dot(xmb[:, hs], wv_ref[h], preferred_element_type=F32).astype(BF16)
        if pending is not None:
            project(*pending)
        pending = (h, xcb)
    project(*pending)
    lane = lax.broadcasted_iota(jnp.int32, (CHUNK, LANES), 1)
    lane_kind = (lane >> 3) & 3
    row = lax.broadcasted_iota(jnp.int32, (4 * HEADS, CHUNK), 0)
    row_kind = (row >> 3) & 3
    r = lax.broadcasted_iota(jnp.int32, (CHUNK, CHUNK), 0)
    c = lax.broadcasted_iota(jnp.int32, (CHUNK, CHUNK), 1)
    lower = (c <= r).astype(BF16)
    upper = (c >= r).astype(BF16)
    row_sums = jnp.concatenate([upper, lower, jnp.ones((CHUNK, LANES), BF16)], axis=1)
    for ch in range(TM // CHUNK):
        ts = slice(ch * CHUNK, (ch + 1) * CHUNK)
        g = gts[ts]
        g = jnp.where((lane_kind & 1) == 1, _log_sigmoid(g), g)
        terms = _bf16_terms(g, 3)
        pre = sum(_dot_nn(lower, t) for t in terms)
        suf = sum(_dot_nn(upper, t) for t in terms)
        gt_ref[ts, :] = jnp.where(lane_kind == 1, pre, jnp.where(lane_kind == 3, suf, g))
        gt = gtt[:, ts]
        gt = jnp.where((row_kind & 1) == 1, _log_sigmoid(gt), gt)
        sums = _dot_exact_rhs(gt, row_sums)
        gtt_ref[:, ts] = jnp.where(row_kind == 1, sums[:, :CHUNK],
                                   jnp.where(row_kind == 3, sums[:, CHUNK:2 * CHUNK], gt))
        tot_ref[:, ts] = sums[:, 2 * CHUNK:]


def _qkv(xm, conv_w, conv_b, wq, wkt, wv, wg, wgt, bg, bgt):
    const = lambda i: (0, 0)
    const3 = lambda i: (0, 0, 0)
    hb = TM // HALO
    n_hb = N_ALL // HALO
    return pl.pallas_call(
        _qkv_kernel,
        grid=(ALL_TILES,),
        in_specs=[
            pl.BlockSpec((TM, B_INNER), lambda i: (i, 0)),
            pl.BlockSpec((HALO, B_INNER), lambda i: (jnp.maximum(i * hb - 1, 0), 0)),
            pl.BlockSpec((HALO, B_INNER), lambda i: (jnp.minimum((i + 1) * hb, n_hb - 1), 0)),
            pl.BlockSpec((CONV_K, B_INNER), const),
            pl.BlockSpec((1, B_INNER), const),
            pl.BlockSpec((HEADS, DH, DQK), const3),
            pl.BlockSpec((HEADS, DQK, DH), const3),
            pl.BlockSpec((HEADS, DH, DV), const3),
            pl.BlockSpec((B_INNER, LANES), const),
            pl.BlockSpec((4 * HEADS, B_INNER), const),
            pl.BlockSpec((1, LANES), const),
            pl.BlockSpec((4 * HEADS, 1), const),
        ],
        out_specs=[
            pl.BlockSpec((TM, B_INNER), lambda i: (i, 0)),
            pl.BlockSpec((TM, HEADS * DQK), lambda i: (i, 0)),
            pl.BlockSpec((HEADS * DQK, TM), lambda i: (0, i)),
            pl.BlockSpec((TM, B_INNER), lambda i: (i, 0)),
            pl.BlockSpec((TM, LANES), lambda i: (i, 0)),
            pl.BlockSpec((4 * HEADS, TM), lambda i: (0, i)),
            pl.BlockSpec((4 * HEADS, TM), lambda i: (0, i)),
        ],
        out_shape=[
            jax.ShapeDtypeStruct((N_ALL, B_INNER), BF16),
            jax.ShapeDtypeStruct((N_ALL, HEADS * DQK), BF16),
            jax.ShapeDtypeStruct((HEADS * DQK, N_ALL), BF16),
            jax.ShapeDtypeStruct((N_ALL, B_INNER), BF16),
            jax.ShapeDtypeStruct((N_ALL, LANES), F32),
            jax.ShapeDtypeStruct((4 * HEADS, N_ALL), F32),
            jax.ShapeDtypeStruct((4 * HEADS, N_ALL), F32),
        ],
        compiler_params=_cparams(("parallel",)),
        name="mlstm_qkv",
    )(xm, xm, xm, conv_w, conv_b, wq, wkt, wv, wg, wgt, bg, bgt)


CTX_CHUNKS = CTX // CHUNK
LAT_CHUNKS = SEQ // CHUNK
SCAN_STEPS = CTX_CHUNKS + LAT_CHUNKS


def _scan_dir(nb, d, q_ref, kt_ref, v_ref, g_ref, gt_ref, tot_ref, o_ref, c_scr, m_scr):
    r = lax.broadcasted_iota(jnp.int32, (CHUNK, CHUNK), 0)
    c = lax.broadcasted_iota(jnp.int32, (CHUNK, CHUNK), 1)
    if d == 0:
        keep = c <= r
    else:
        keep = c >= r
    ones_blk = jnp.ones((CHUNK, LANES), BF16)
    bcol_all = g_ref[...]
    gtt = gt_ref[...]
    tot_all = tot_ref[...]
    base = 2 * HEADS * d

    def head_first(h):
        li_r = gtt[base + h:base + h + 1, :]
        b_r = gtt[base + HEADS + h:base + HEADS + h + 1, :]
        b_last = tot_all[base + HEADS + h:base + HEADS + h + 1, :]
        m_old = m_scr[nb, d, h]
        c_old = c_scr[nb, d, h]
        qh = q_ref[:, h * DQK:(h + 1) * DQK]
        kth = kt_ref[h * DQK:(h + 1) * DQK, :]
        vaug = jnp.concatenate([v_ref[:, h * DV:(h + 1) * DV], ones_blk], axis=1)
        g_r = b_last - b_r + li_r
        m_new = jnp.maximum(b_last + m_old, jnp.max(g_r, axis=-1, keepdims=True))
        decay = jnp.exp(b_last + m_old - m_new)
        wk = jnp.exp(g_r - m_new)
        kwt = (kth.astype(F32) * wk).astype(BF16)
        qk = jnp.dot(qh, kth, preferred_element_type=F32)
        qc = jnp.dot(qh, c_old.astype(BF16), preferred_element_type=F32)
        c_scr[nb, d, h] = jnp.concatenate([decay] * 3, axis=1) * c_old + jnp.dot(
            kwt, vaug, preferred_element_type=F32)
        m_scr[nb, d, h] = m_new
        return h, li_r, b_r, m_old, vaug, qk, qc

    def head_second(h, li_r, b_r, m_old, vaug, qk, qc):
        b_full = jnp.broadcast_to(bcol_all[:, base + HEADS + h:base + HEADS + h + 1], (CHUNK, LANES))
        dmat = jnp.where(keep, b_full - (b_r - li_r), NEG_INF)
        inter = b_full + m_old
        m_t = jnp.maximum(inter, jnp.max(dmat, axis=-1, keepdims=True))
        p = jnp.exp(dmat - m_t)
        s = qk * p
        a = jnp.exp(inter - m_t)
        num = jnp.concatenate([a] * 3, axis=1) * qc + jnp.dot(
            s.astype(BF16), vaug, preferred_element_type=F32)
        inv = 1.0 / jnp.maximum(jnp.abs(num[:, DV:]), jnp.exp(-m_t))
        o_ref[nb, :, h * DV:(h + 1) * DV] = (
            num[:, :DV] * jnp.concatenate([inv] * 2, axis=1)).astype(BF16)

    return head_first, head_second


SCAN_NB = 4
SCAN_IN = 6
SCAN_SKEW = 1


def _scan_kernel(*refs):
    n_in = SCAN_NB * 2 * SCAN_IN
    ins, outs = refs[:n_in], refs[n_in:n_in + 2]
    c_scr, m_scr = refs[n_in + 2:]

    @pl.when(pl.program_id(1) == 0)
    def _():
        c_scr[...] = jnp.zeros_like(c_scr)
        m_scr[...] = jnp.zeros_like(m_scr)

    units = []
    for nb in range(SCAN_NB):
        for d in range(2):
            k = nb * 2 + d
            first, second = _scan_dir(nb, d, *ins[k * SCAN_IN:(k + 1) * SCAN_IN], outs[d], c_scr, m_scr)
            units += [(first, second, h) for h in range(HEADS)]
    pending = []
    for first, second, h in units:
        pending.append((second, first(h)))
        if len(pending) > SCAN_SKEW:
            fn, old_vals = pending.pop(0)
            fn(*old_vals)
    for fn, old_vals in pending:
        fn(*old_vals)


def _scan(q, kt, v, gts, gtt, tot):
    lat_blk = N_LAT // CHUNK

    def fwd_in(b, j):
        return jnp.where(j < CTX_CHUNKS, lat_blk + b * CTX_CHUNKS + j, b * LAT_CHUNKS + j - CTX_CHUNKS)

    def bwd_in(b, j):
        return jnp.where(j < CTX_CHUNKS, lat_blk + b * CTX_CHUNKS + (CTX_CHUNKS - 1 - j),
                         b * LAT_CHUNKS + (SCAN_STEPS - 1 - j))

    def fwd_out(b, j):
        return b * LAT_CHUNKS + jnp.maximum(j - CTX_CHUNKS, 0)

    def bwd_out(b, j):
        return b * LAT_CHUNKS + jnp.minimum(SCAN_STEPS - 1 - j, LAT_CHUNKS - 1)

    n_g = BATCH // SCAN_NB

    def specs(fn, nb):
        blk = lambda g, j: fn(nb * n_g + g, j)
        return [
            pl.BlockSpec((CHUNK, HEADS * DQK), lambda g, j: (blk(g, j), 0)),
            pl.BlockSpec((HEADS * DQK, CHUNK), lambda g, j: (0, blk(g, j))),
            pl.BlockSpec((CHUNK, B_INNER), lambda g, j: (blk(g, j), 0)),
            pl.BlockSpec((CHUNK, LANES), lambda g, j: (blk(g, j), 0)),
            pl.BlockSpec((4 * HEADS, CHUNK), lambda g, j: (0, blk(g, j))),
            pl.BlockSpec((4 * HEADS, CHUNK), lambda g, j: (0, blk(g, j))),
        ]

    def out_spec(fn):
        return pl.BlockSpec((SCAN_NB, CHUNK, B_INNER), lambda g, j: (0, fn(g, j), 0))

    in_specs, operands = [], []
    for nb in range(SCAN_NB):
        for fn_in in (fwd_in, bwd_in):
            in_specs += specs(fn_in, nb)
            operands += [q, kt, v, gts, gtt, tot]
    h_shape = jax.ShapeDtypeStruct((SCAN_NB, N_LAT // SCAN_NB, B_INNER), BF16)
    hf, hb = pl.pallas_call(
        _scan_kernel,
        grid=(n_g, SCAN_STEPS),
        in_specs=in_specs,
        out_specs=[out_spec(fwd_out), out_spec(bwd_out)],
        out_shape=[h_shape, h_shape],
        scratch_shapes=[pltpu.VMEM((SCAN_NB, 2, HEADS, DQK, DV + LANES), F32),
                        pltpu.VMEM((SCAN_NB, 2, HEADS, 1, LANES), F32)],
        compiler_params=_cparams(("parallel", "arbitrary")),
        name="mlstm_scan",
    )(*operands)
    return hf.reshape(N_LAT, B_INNER), hb.reshape(N_LAT, B_INNER)


def _combine_kernel(x_ref, mod_ref, hf_ref, hb_ref, xc_ref, op_ref, hg_ref, sk_ref, wout_ref,
                    g2_ref, wr_ref, br_ref, o_ref, meta_ref, tmeta_ref, cnt_ref, cntr_ref,
                    y_scr, carry, carry_r):
    for h in range(HEADS):
        hs = slice(h * DV, (h + 1) * DV)
        s = hf_ref[:, hs].astype(F32) + hb_ref[:, hs].astype(F32)
        s = s * lax.rsqrt(jnp.mean(s * s, axis=-1, keepdims=True) + EPS)
        y = jax.nn.sigmoid(op_ref[:, hs].astype(F32)) * (
            s * hg_ref[:, hs] + sk_ref[:, hs] * xc_ref[:, hs].astype(F32))
        y_scr[:, hs] = y.astype(BF16)
        if h % 2 == 1:
            ks = slice((h - 1) * DV, (h + 1) * DV)
            part = jnp.dot(y_scr[:, ks], wout_ref[ks, :], preferred_element_type=F32)
            out = part if h == 1 else out + part
    xn = x_ref[...] + mod_ref[0, 2:3, :] * out
    o_ref[...] = xn
    _route_tile(xn, mod_ref, g2_ref, wr_ref, br_ref, meta_ref, tmeta_ref, cnt_ref, cntr_ref,
                carry, carry_r)


def _combine(xa, mod, hf, hb, xc, op, head_g, skip, w_out, g2, wr, br):
    const = lambda i: (0, 0)
    row = lambda i: (i, 0)
    r_in, r_out, r_shape, r_scratch = _router_specs(LAT_TILES)
    return pl.pallas_call(
        _combine_kernel,
        grid=(LAT_TILES,),
        in_specs=[
            pl.BlockSpec((TM, D), row),
            pl.BlockSpec((1, 6, D), lambda i: (_mod_row(i), 0, 0)),
            pl.BlockSpec((TM, B_INNER), row),
            pl.BlockSpec((TM, B_INNER), row),
            pl.BlockSpec((TM, B_INNER), row),
            pl.BlockSpec((TM, B_INNER), row),
            pl.BlockSpec((1, B_INNER), const),
            pl.BlockSpec((1, B_INNER), const),
            pl.BlockSpec((B_INNER, D), const),
        ] + r_in,
        out_specs=[pl.BlockSpec((TM, D), row)] + r_out,
        out_shape=[jax.ShapeDtypeStruct((N_LAT, D), F32)] + r_shape,
        scratch_shapes=[pltpu.VMEM((TM, B_INNER), BF16)] + r_scratch,
        compiler_params=_cparams(("arbitrary",)),
        name="mlstm_combine",
    )(xa, mod, hf, hb, xc, op, head_g, skip, w_out, g2, wr, br)


def _router_weights(w_grp, b_grp, w_exp, b_exp):
    wr = jnp.zeros((LANES, D), F32).at[:N_EXP].set(w_exp.T).at[N_EXP:N_EXP + N_GRP].set(w_grp.T)
    br = jnp.zeros((LANES, 1), F32).at[:N_EXP, 0].set(b_exp).at[N_EXP:N_EXP + N_GRP, 0].set(b_grp)
    return wr.astype(BF16), br


PLAN_BLK = 2048
PLAN_PAIRS = 256


def _plan_kernel(meta_ref, cc_ref, cr_ref, pp_ref, plan_ref):
    row = lax.broadcasted_iota(jnp.int32, (LANES, LANES), 0)
    col = lax.broadcasted_iota(jnp.int32, (LANES, LANES), 1)
    cnt_c = cc_ref[...]
    cnt_r = cr_ref[0:1, :]
    starts_c = jnp.sum(jnp.where(col < row, cnt_r, 0.0), axis=1, keepdims=True)
    erow = lax.broadcasted_iota(jnp.int32, (LANES, PLAN_BLK), 0).astype(F32)
    for j in range(2):
        pos = jnp.sum(jnp.where(erow == meta_ref[j:j + 1, :], starts_c, 0.0), axis=0,
                      keepdims=True) + meta_ref[j + 2:j + 3, :]
        pp_ref[j:j + 1, :] = pos.astype(jnp.int32)
    pp_ref[2:8, :] = jnp.zeros((6, PLAN_BLK), jnp.int32)

    @pl.when(pl.program_id(0) == 0)
    def _():
        ends_c = starts_c + cnt_c
        starts_r = jnp.sum(jnp.where(row < col, cnt_c, 0.0), axis=0, keepdims=True)
        ends_r = starts_r + cnt_r

        def tiles(st, en, cn):
            first_tile = jnp.floor(st * (1.0 / TM))
            n = jnp.where(cn > 0.0, jnp.floor((en - 1.0) * (1.0 / TM)) - first_tile + 1.0, 0.0)
            return first_tile, n

        ft_c, pairs_c = tiles(starts_c, ends_c, cnt_c)
        _, pairs_r = tiles(starts_r, ends_r, cnt_r)
        pend_c = jnp.sum(jnp.where(col <= row, pairs_r, 0.0), axis=1, keepdims=True)
        pstart_c = pend_c - pairs_c
        total = jnp.sum(pairs_r, axis=1, keepdims=True)
        q = lax.broadcasted_iota(jnp.int32, (1, PLAN_PAIRS), 1).astype(F32)
        erow_p = lax.broadcasted_iota(jnp.int32, (LANES, PLAN_PAIRS), 0).astype(F32)

        def at(qv):
            qc = jnp.maximum(jnp.minimum(qv, total - 1.0), 0.0)
            e = jnp.sum(jnp.where(qc >= pend_c, 1.0, 0.0), axis=0, keepdims=True)
            oh = erow_p == e
            tile = jnp.sum(jnp.where(oh, ft_c - pstart_c, 0.0), axis=0, keepdims=True) + qc
            return e, oh, tile

        e_q, oh, tile_q = at(q)
        e_p, _, tile_p = at(q - 1.0)
        lo = jnp.sum(jnp.where(oh, starts_c, 0.0), axis=0, keepdims=True) - tile_q * TM
        hi = jnp.sum(jnp.where(oh, ends_c, 0.0), axis=0, keepdims=True) - tile_q * TM
        rows = (tile_q, e_q, jnp.clip(lo, 0.0, TM), jnp.clip(hi, 0.0, TM),
                jnp.where((q == 0.0) | (tile_q != tile_p), 1.0, 0.0),
                jnp.where((q == 0.0) | (e_q != e_p), 1.0, 0.0),
                jnp.where(q < total, 1.0, 0.0),
                jnp.zeros((1, PLAN_PAIRS), F32))
        for j, v in enumerate(rows):
            plan_ref[j:j + 1, :] = v.astype(jnp.int32)


def _route_plan(meta, cnt_c, cnt_r, n_rows):
    n_pairs = 2 * n_rows // TM + N_EXP - 1
    assert n_pairs <= PLAN_PAIRS and n_rows % PLAN_BLK == 0
    const = lambda i: (0, 0)
    pp, plan = pl.pallas_call(
        _plan_kernel,
        grid=(n_rows // PLAN_BLK,),
        in_specs=[pl.BlockSpec((8, PLAN_BLK), lambda i: (0, i)),
                  pl.BlockSpec((LANES, 1), const),
                  pl.BlockSpec((8, LANES), const)],
        out_specs=[pl.BlockSpec((8, PLAN_BLK), lambda i: (0, i)),
                   pl.BlockSpec((8, PLAN_PAIRS), const)],
        out_shape=[jax.ShapeDtypeStruct((8, n_rows), jnp.int32),
                   jax.ShapeDtypeStruct((8, PLAN_PAIRS), jnp.int32)],
        compiler_params=_cparams(("arbitrary",)),
        name="moe_plan",
    )(meta, cnt_c, cnt_r)
    return pp[0], pp[1], tuple(plan[j, :n_pairs] for j in range(7))


def _moe_experts(xa, routing, mod, g2, w1, w3, w2, layer, n_rows):
    meta, _, cnt_c, cnt_r = routing
    p1, p2, plan = _route_plan(meta, cnt_c, cnt_r, n_rows)
    xs = _scatter(p1, p2, xa, mod, g2, n_rows // TM)
    return p1, p2, _grouped(plan, xs, w1, w3, w2, layer)


def kernel(x, c, ctx, c_ctx, norm_g, w_ada, b_ada, a_w_in, a_b_in, a_g_v, a_w_s, a_b_s, a_w_out,
           b_w_in, b_conv_w, b_conv_b, b_w_q, b_w_k, b_w_v, b_w_gate, b_b_gate, b_head_g, b_skip,
           b_w_out, moe_w_grp, moe_b_grp, moe_w_exp, moe_b_exp, moe_w1, moe_w3, moe_w2, final_g):
    cc = jnp.zeros((MOD_ROWS, D), F32).at[:BATCH].set(c).at[BATCH].set(c_ctx)
    mods = _ada(cc, w_ada, b_ada).reshape(2, MOD_ROWS, 6, D)

    mod = mods[0]
    b_s_full = jnp.repeat(a_b_s[0].T, A_GC, axis=1)
    g2 = norm_g[0, 1].reshape(1, D)
    wr, br = _router_weights(moe_w_grp[0], moe_b_grp[0], moe_w_exp[0], moe_b_exp[0])
    xa, *routing = _gmlp(x.reshape(N_LAT, D), ctx.reshape(N_CTX, D), mod, norm_g[0, 0].reshape(1, D),
                         a_w_in[0].astype(BF16), a_b_in[0].reshape(1, -1), a_g_v[0].reshape(1, -1),
                         a_w_s[0].astype(BF16), b_s_full, a_w_out[0].astype(BF16), g2, wr, br)
    p1, p2, ys = _moe_experts(xa, routing, mod, g2, moe_w1, moe_w3, moe_w2, 0, N_ALL)

    mod = mods[1]
    xa, xm, op = _inproj(p1, p2, xa, mods[0], routing[1], ys, mod, norm_g[1, 0].reshape(1, D),
                         b_w_in[0].astype(BF16))
    wg = jnp.zeros((B_INNER, LANES), F32).at[:, :4 * HEADS].set(b_w_gate[0]).astype(BF16)
    bg = jnp.zeros((1, LANES), F32).at[0, :4 * HEADS].set(b_b_gate[0])
    xc, q, kt, v, gts, gtt, tot = _qkv(xm, b_conv_w[0], b_conv_b[0].reshape(1, -1),
                                  b_w_q[0].astype(BF16),
                                  jnp.transpose(b_w_k[0], (0, 2, 1)).astype(BF16),
                                  b_w_v[0].astype(BF16), wg, b_w_gate[0].T.astype(BF16), bg,
                                  b_b_gate[0].reshape(-1, 1))
    hf, hb = _scan(q, kt, v, gts, gtt, tot)
    g2 = norm_g[1, 1].reshape(1, D)
    wr, br = _router_weights(moe_w_grp[1], moe_b_grp[1], moe_w_exp[1], moe_b_exp[1])
    xl, *routing = _combine(xa, mod, hf, hb, xc, op, b_head_g[0].reshape(1, -1),
                            b_skip[0].reshape(1, -1), b_w_out[0].astype(BF16), g2, wr, br)
    p1, p2, ys = _moe_experts(xl, routing, mod, g2, moe_w1, moe_w3, moe_w2, 1, N_LAT)
    out = _gather(p1, p2, xl, mod, routing[1], final_g.reshape(1, D), ys, LAT_TILES, True)
    return out.reshape(BATCH, SEQ, D)
```

```python
import functools

import jax
import jax.numpy as jnp
from jax import lax
from jax.experimental import pallas as pl
from jax.experimental.pallas import tpu as pltpu

F32 = jnp.float32
BF16 = jnp.bfloat16

D = 1024
BATCH = 8
SEQ = 2048
CTX = 256
EPS = 1e-6
NEG_INF = -1e30
N_LAT = BATCH * SEQ
N_CTX = BATCH * CTX
N_ALL = N_LAT + N_CTX

TM = 256
LAT_TILES = N_LAT // TM
ALL_TILES = N_ALL // TM
TILES_PER_SEQ = SEQ // TM
MOD_ROWS = 16

CHUNK = 128
A_HALF = 2048
A_GROUPS = 8
A_GC = A_HALF // A_GROUPS
B_INNER = 2048
HEADS = 8
DH = B_INNER // HEADS
DQK = DH // 2
DV = DH
CONV_K = 5
HALO = 16
N_EXP = 32
N_GRP = 4
EXP_PER_GRP = 8
D_EXP = 512
LANES = 128

VMEM_LIMIT = 56 * 1024 * 1024


def _cparams(sem):
    return pltpu.CompilerParams(dimension_semantics=sem, vmem_limit_bytes=VMEM_LIMIT)


def _mod_row(i):
    return jnp.where(i < LAT_TILES, i // TILES_PER_SEQ, BATCH)


def _norm_mod(x, g, shift, scale):
    y = x * lax.rsqrt(jnp.mean(x * x, axis=-1, keepdims=True) + EPS) * g
    return y * (1.0 + scale) + shift


def _bf16_terms(x, n):
    terms = []
    r = x
    for _ in range(n):
        t = r.astype(BF16)
        terms.append(t)
        r = r - t.astype(F32)
    return terms


def _dot_nt(a, b):
    return lax.dot_general(a, b, (((1,), (1,)), ((), ())), preferred_element_type=F32)


def _dot_nn(a, b):
    return jnp.dot(a, b, preferred_element_type=F32)


def _dot3(a, b, dot=_dot_nn):
    a1, a2 = _bf16_terms(a, 2)
    b1, b2 = _bf16_terms(b, 2)
    return dot(a1, b1) + (dot(a1, b2) + dot(a2, b1))


def _dot_exact_lhs(a_bf16, b, dot=_dot_nn):
    return sum(dot(a_bf16, t) for t in _bf16_terms(b, 3))


def _dot_exact_rhs(a, b_bf16, dot=_dot_nn):
    return sum(dot(t, b_bf16) for t in _bf16_terms(a, 3))


def _gelu_tanh(x):
    half = 0.5 * x
    t = jnp.tanh(x * (0.7978845608028654 + 0.035677408136300125 * (x * x)))
    return half + half * t


ADA_BN = 1536


def _ada_kernel(c_ref, w_ref, b_ref, o_ref):
    c = c_ref[...]
    a = c * jax.nn.sigmoid(c)
    o_ref[0] = _dot3(a, w_ref[0]) + b_ref[0]


def _ada(cc, w_ada, b_ada):
    depth = w_ada.shape[0]
    return pl.pallas_call(
        _ada_kernel,
        grid=(depth, 6 * D // ADA_BN),
        in_specs=[
            pl.BlockSpec((MOD_ROWS, D), lambda l, j: (0, 0)),
            pl.BlockSpec((1, D, ADA_BN), lambda l, j: (l, 0, j)),
            pl.BlockSpec((1, 1, ADA_BN), lambda l, j: (l, 0, j)),
        ],
        out_specs=pl.BlockSpec((1, MOD_ROWS, ADA_BN), lambda l, j: (l, 0, j)),
        out_shape=jax.ShapeDtypeStruct((depth, MOD_ROWS, 6 * D), F32),
        compiler_params=_cparams(("parallel", "parallel")),
        name="ada",
    )(cc, w_ada, b_ada.reshape(depth, 1, 6 * D))


GM_CH = 512


def _gmlp_kernel(xl_ref, xc_ref, mod_ref, g_ref, win_ref, bin_ref, gv_ref, ws_ref, bs_ref, wout_ref,
                 g2_ref, wr_ref, br_ref, o_ref, meta_ref, tmeta_ref, cnt_ref, cntr_ref,
                 z_scr, y_scr, carry, carry_r):
    x = jnp.where(pl.program_id(0) < LAT_TILES, xl_ref[...], xc_ref[...])
    h = _norm_mod(x, g_ref[...], mod_ref[0, 0:1, :], mod_ref[0, 1:2, :])
    hb = h.astype(BF16)
    s1 = jnp.zeros((TM, 1), F32)
    s2 = jnp.zeros((TM, 1), F32)
    n_ch = 2 * A_HALF // GM_CH
    for j in list(range(n_ch // 2, n_ch)) + list(range(n_ch // 2)):
        cs = slice(j * GM_CH, (j + 1) * GM_CH)
        zc = jnp.dot(hb, win_ref[:, cs], preferred_element_type=F32) + bin_ref[:, cs]
        zc = _gelu_tanh(zc)
        z_scr[:, cs] = zc
        if j * GM_CH >= A_HALF:
            s1 = s1 + jnp.sum(zc, axis=-1, keepdims=True)
            s2 = s2 + jnp.sum(zc * zc, axis=-1, keepdims=True)
    mu = s1 * (1.0 / A_HALF)
    rstd = lax.rsqrt(s2 * (1.0 / A_HALF) - mu * mu + EPS)
    for c in range(TM // CHUNK):
        rs = slice(c * CHUNK, (c + 1) * CHUNK)
        for g in range(A_GROUPS):
            cs = slice(g * A_GC, (g + 1) * A_GC)
            vs = slice(A_HALF + g * A_GC, A_HALF + (g + 1) * A_GC)
            v = (z_scr[rs, vs] - mu[rs]) * rstd[rs] * gv_ref[:, cs]
            s = jnp.dot(ws_ref[g], v.astype(BF16), preferred_element_type=F32) + bs_ref[:, cs]
            y_scr[rs, cs] = (z_scr[rs, cs] * s).astype(BF16)
    out = jnp.dot(y_scr[...], wout_ref[...], preferred_element_type=F32)
    xn = x + mod_ref[0, 2:3, :] * out
    o_ref[...] = xn
    _route_tile(xn, mod_ref, g2_ref, wr_ref, br_ref, meta_ref, tmeta_ref, cnt_ref, cntr_ref,
                carry, carry_r)


def _gmlp(xl, xc, mod, g, w_in, b_in, g_v, w_s, b_s_full, w_out, g2, wr, br):
    const = lambda i: (0, 0)
    r_in, r_out, r_shape, r_scratch = _router_specs(ALL_TILES)
    return pl.pallas_call(
        _gmlp_kernel,
        grid=(ALL_TILES,),
        in_specs=[
            pl.BlockSpec((TM, D), lambda i: (jnp.minimum(i, LAT_TILES - 1), 0)),
            pl.BlockSpec((TM, D), lambda i: (jnp.maximum(i - LAT_TILES, 0), 0)),
            pl.BlockSpec((1, 6, D), lambda i: (_mod_row(i), 0, 0)),
            pl.BlockSpec((1, D), const),
            pl.BlockSpec((D, 2 * A_HALF), const),
            pl.BlockSpec((1, 2 * A_HALF), const),
            pl.BlockSpec((1, A_HALF), const),
            pl.BlockSpec((A_GROUPS, CHUNK, CHUNK), lambda i: (0, 0, 0)),
            pl.BlockSpec((CHUNK, A_HALF), const),
            pl.BlockSpec((A_HALF, D), const),
        ] + r_in,
        out_specs=[pl.BlockSpec((TM, D), lambda i: (i, 0))] + r_out,
        out_shape=[jax.ShapeDtypeStruct((N_ALL, D), F32)] + r_shape,
        scratch_shapes=[pltpu.VMEM((TM, 2 * A_HALF), F32), pltpu.VMEM((TM, A_HALF), BF16)] + r_scratch,
        compiler_params=_cparams(("arbitrary",)),
        name="gmlp",
    )(xl, xc, mod, g, w_in, b_in, g_v, w_s, b_s_full, w_out, g2, wr, br)


def _route_tile(x, mod_ref, g_ref, wr_ref, br_ref, meta_ref, tmeta_ref, cnt_ref, cntr_ref,
                carry, carry_r):
    @pl.when(pl.program_id(0) == 0)
    def _():
        carry[...] = jnp.zeros_like(carry)
        carry_r[...] = jnp.zeros_like(carry_r)

    h = _norm_mod(x, g_ref[...], mod_ref[0, 3:4, :], mod_ref[0, 4:5, :])
    lt = _dot_nt(wr_ref[...], h.astype(BF16)) + br_ref[...]
    e_t = lt[0:N_EXP]
    row8 = lax.broadcasted_iota(jnp.int32, (EXP_PER_GRP, TM), 0).astype(F32)
    g_t = jnp.where(row8 < N_GRP, lt[N_EXP:N_EXP + EXP_PER_GRP], -jnp.inf)
    gmax = jnp.max(g_t, axis=0, keepdims=True)
    p_g = 1.0 / jnp.sum(jnp.exp(g_t - gmax), axis=0, keepdims=True)
    g_idx = jnp.min(jnp.where(g_t == gmax, row8, float(EXP_PER_GRP)), axis=0, keepdims=True)
    sel = jnp.zeros((EXP_PER_GRP, TM), F32)
    for g in range(N_GRP):
        sel = sel + jnp.where(g_idx == g, e_t[g * EXP_PER_GRP:(g + 1) * EXP_PER_GRP], 0.0)
    m1 = jnp.max(sel, axis=0, keepdims=True)
    i1 = jnp.min(jnp.where(sel == m1, row8, float(EXP_PER_GRP)), axis=0, keepdims=True)
    sel2 = jnp.where(row8 == i1, -jnp.inf, sel)
    m2 = jnp.max(sel2, axis=0, keepdims=True)
    i2 = jnp.min(jnp.where(sel2 == m2, row8, float(EXP_PER_GRP)), axis=0, keepdims=True)
    e2 = jnp.exp(m2 - m1)
    w1 = p_g / (1.0 + e2)
    w2 = p_g * e2 / (1.0 + e2)
    row = lax.broadcasted_iota(jnp.int32, (LANES, TM), 0).astype(F32)
    id1 = g_idx * EXP_PER_GRP + i1
    id2 = g_idx * EXP_PER_GRP + i2
    oh1 = row == id1
    oh2 = row == id2
    oh = jnp.where(oh1, 1.0, 0.0) + jnp.where(oh2, 1.0, 0.0)
    before = (lax.broadcasted_iota(jnp.int32, (TM, TM), 0)
              < lax.broadcasted_iota(jnp.int32, (TM, TM), 1)).astype(BF16)
    ohb = oh.astype(BF16)
    tot = jnp.dot(ohb, before, preferred_element_type=F32) + carry[...]
    r1 = jnp.sum(jnp.where(oh1, tot, 0.0), axis=0, keepdims=True)
    r2 = jnp.sum(jnp.where(oh2, tot, 0.0), axis=0, keepdims=True)
    carry[...] += jnp.sum(oh, axis=-1, keepdims=True)
    carry_r[...] += _dot_nt(jnp.ones((8, TM), BF16), ohb)
    cnt_ref[...] = carry[...]
    cntr_ref[...] = carry_r[...]
    meta_ref[0:1, :] = id1
    meta_ref[1:2, :] = id2
    meta_ref[2:3, :] = r1
    meta_ref[3:4, :] = r2
    meta_ref[4:8, :] = jnp.zeros((4, TM), F32)
    tmeta_ref[...] = (jnp.where(row == 0.0, w1, 0.0) + jnp.where(row == 1.0, w2, 0.0)).T


def _router_specs(n_tiles):
    const = lambda i: (0, 0)
    n = n_tiles * TM
    in_specs = [pl.BlockSpec((1, D), const), pl.BlockSpec((LANES, D), const),
                pl.BlockSpec((LANES, 1), const)]
    out_specs = [pl.BlockSpec((8, TM), lambda i: (0, i)),
                 pl.BlockSpec((TM, LANES), lambda i: (i, 0)),
                 pl.BlockSpec((LANES, 1), const),
                 pl.BlockSpec((8, LANES), const)]
    out_shape = [jax.ShapeDtypeStruct((8, n), F32), jax.ShapeDtypeStruct((n, LANES), F32),
                 jax.ShapeDtypeStruct((LANES, 1), F32), jax.ShapeDtypeStruct((8, LANES), F32)]
    scratch = [pltpu.VMEM((LANES, 1), F32), pltpu.VMEM((8, LANES), F32)]
    return in_specs, out_specs, out_shape, scratch


ISSUE_UNROLL = 8
ROW_TILE = (8, LANES)
MXU_N = 256


def _row_copy_wait(buf_slot, sem_slot):
    pltpu.make_async_copy(buf_slot, buf_slot, sem_slot).wait()


def _scatter_kernel(p1_ref, p2_ref, x_ref, mod_ref, g_ref, xs_ref, buf, sem):
    i = pl.program_id(0)
    n_steps = pl.num_programs(0)
    slot = lax.rem(i, 2)

    def wait_slot(s):
        _row_copy_wait(buf.at[s], sem.at[s])
        _row_copy_wait(buf.at[s], sem.at[s])

    @pl.when(i >= 2)
    def _():
        wait_slot(slot)

    h = _norm_mod(x_ref[...], g_ref[...], mod_ref[0, 3:4, :], mod_ref[0, 4:5, :])
    buf[slot] = h.reshape(TM, *ROW_TILE)
    base = i * TM

    def body(r, carry):
        src = buf.at[slot, r]
        pltpu.make_async_copy(src, xs_ref.at[p1_ref[base + r]], sem.at[slot]).start()
        pltpu.make_async_copy(src, xs_ref.at[p2_ref[base + r]], sem.at[slot]).start(priority=1)
        return carry

    lax.fori_loop(0, TM, body, 0, unroll=ISSUE_UNROLL)

    @pl.when(i == n_steps - 1)
    def _():
        @pl.when(i >= 1)
        def _():
            wait_slot(1 - slot)

        wait_slot(slot)


def _scatter(p1, p2, xa, mod, g, n_tiles):
    n = n_tiles * TM
    return pl.pallas_call(
        _scatter_kernel,
        grid_spec=pltpu.PrefetchScalarGridSpec(
            num_scalar_prefetch=2,
            grid=(n_tiles,),
            in_specs=[
                pl.BlockSpec((TM, D), lambda i, p1, p2: (i, 0)),
                pl.BlockSpec((1, 6, D), lambda i, p1, p2: (_mod_row(i), 0, 0)),
                pl.BlockSpec((1, D), lambda i, p1, p2: (0, 0)),
            ],
            out_specs=pl.BlockSpec(memory_space=pl.ANY),
            scratch_shapes=[pltpu.VMEM((2, TM) + ROW_TILE, F32), pltpu.SemaphoreType.DMA((2,))],
        ),
        out_shape=jax.ShapeDtypeStruct((2 * n,) + ROW_TILE, F32),
        compiler_params=_cparams(("arbitrary",)),
        name="moe_scatter",
    )(p1, p2, xa, mod, g)


def _grouped_kernel(tile_ref, exp_ref, lo_ref, hi_ref, first_ref, newexp_ref, valid_ref,
                    xs_ref, w1_ref, w3_ref, w2_ref, ys_ref, wb1, wb3, wb2):
    q = pl.program_id(0)

    @pl.when(valid_ref[q] == 1)
    def _():
        @pl.when(newexp_ref[q] == 1)
        def _():
            wb1[...] = w1_ref[0, 0].astype(BF16)
            wb3[...] = w3_ref[0, 0].astype(BF16)
            wb2[...] = w2_ref[0, 0].astype(BF16)

        x = xs_ref[...].reshape(TM, D).astype(BF16)
        rows = lax.broadcasted_iota(jnp.int32, (TM, 1), 0)
        mine = (rows >= lo_ref[q]) & (rows < hi_ref[q])
        acts = []
        for j in range(D_EXP // MXU_N):
            cs = slice(j * MXU_N, (j + 1) * MXU_N)
            a = jnp.dot(x, wb1[:, cs], preferred_element_type=F32)
            b = jnp.dot(x, wb3[:, cs], preferred_element_type=F32)
            acts.append(jnp.where(mine, a * jax.nn.sigmoid(a) * b, 0.0).astype(BF16))
        act = jnp.concatenate(acts, axis=1)
        y = jnp.dot(act, wb2[...], preferred_element_type=F32).reshape(TM, *ROW_TILE)

        @pl.when(first_ref[q] == 1)
        def _():
            ys_ref[...] = y

        @pl.when(first_ref[q] == 0)
        def _():
            ys_ref[...] += y


def _grouped(plan, xs, w1, w3, w2, layer):
    n_pairs = plan[0].shape[0]
    tile_map = lambda q, tile, exp, *_: (tile[q], 0, 0)
    exp_map = lambda q, tile, exp, *_: (layer, exp[q], 0, 0)
    return pl.pallas_call(
        _grouped_kernel,
        grid_spec=pltpu.PrefetchScalarGridSpec(
            num_scalar_prefetch=7,
            grid=(n_pairs,),
            in_specs=[
                pl.BlockSpec((TM,) + ROW_TILE, tile_map),
                pl.BlockSpec((1, 1, D, D_EXP), exp_map),
                pl.BlockSpec((1, 1, D, D_EXP), exp_map),
                pl.BlockSpec((1, 1, D_EXP, D), exp_map),
            ],
            out_specs=pl.BlockSpec((TM,) + ROW_TILE, tile_map),
            scratch_shapes=[pltpu.VMEM((D, D_EXP), BF16), pltpu.VMEM((D, D_EXP), BF16),
                            pltpu.VMEM((D_EXP, D), BF16)],
        ),
        out_shape=jax.ShapeDtypeStruct(xs.shape, F32),
        compiler_params=_cparams(("arbitrary",)),
        name="moe_grouped",
    )(*plan, xs, w1, w3, w2)


def _gather_kernel(p1_ref, p2_ref, x_ref, mod_ref, tm_ref, fg_ref, ys_ref, o_ref, buf, sem,
                   *, final_norm):
    i = pl.program_id(0)
    n_steps = pl.num_programs(0)
    slot = lax.rem(i, 2)

    def issue(tile, s):
        base = tile * TM

        def body(r, carry):
            pltpu.make_async_copy(ys_ref.at[p1_ref[base + r]], buf.at[s, 0, r], sem.at[s]).start()
            pltpu.make_async_copy(ys_ref.at[p2_ref[base + r]], buf.at[s, 1, r],
                                  sem.at[s]).start(priority=1)
            return carry

        lax.fori_loop(0, TM, body, 0, unroll=ISSUE_UNROLL)

    @pl.when(i == 0)
    def _():
        issue(0, 0)

    @pl.when(i + 1 < n_steps)
    def _():
        issue(i + 1, 1 - slot)

    _row_copy_wait(buf.at[slot, 0], sem.at[slot])
    _row_copy_wait(buf.at[slot, 1], sem.at[slot])
    y = (tm_ref[:, 0:1] * buf[slot, 0].reshape(TM, D)
         + tm_ref[:, 1:2] * buf[slot, 1].reshape(TM, D))
    out = x_ref[...] + mod_ref[0, 5:6, :] * y
    if final_norm:
        out = out * lax.rsqrt(jnp.mean(out * out, axis=-1, keepdims=True) + EPS) * fg_ref[...]
    o_ref[...] = out


def _gather(p1, p2, xa, mod, tmeta, final_g, ys, n_tiles, final_norm):
    return pl.pallas_call(
        functools.partial(_gather_kernel, final_norm=final_norm),
        grid_spec=pltpu.PrefetchScalarGridSpec(
            num_scalar_prefetch=2,
            grid=(n_tiles,),
            in_specs=[
                pl.BlockSpec((TM, D), lambda i, p1, p2: (i, 0)),
                pl.BlockSpec((1, 6, D), lambda i, p1, p2: (_mod_row(i), 0, 0)),
                pl.BlockSpec((TM, LANES), lambda i, p1, p2: (i, 0)),
                pl.BlockSpec((1, D), lambda i, p1, p2: (0, 0)),
                pl.BlockSpec(memory_space=pl.ANY),
            ],
            out_specs=pl.BlockSpec((TM, D), lambda i, p1, p2: (i, 0)),
            scratch_shapes=[pltpu.VMEM((2, 2, TM) + ROW_TILE, F32), pltpu.SemaphoreType.DMA((2,))],
        ),
        out_shape=jax.ShapeDtypeStruct((n_tiles * TM, D), F32),
        compiler_params=_cparams(("arbitrary",)),
        name="moe_gather",
    )(p1, p2, xa, mod, tmeta, final_g, ys)


def _issue_row_gathers(p1_ref, p2_ref, ys_ref, buf, sem, tile, s, unrolled):
    base = tile * TM

    def one(r):
        pltpu.make_async_copy(ys_ref.at[p1_ref[base + r]], buf.at[s, 0, r], sem.at[s]).start()
        pltpu.make_async_copy(ys_ref.at[p2_ref[base + r]], buf.at[s, 1, r],
                              sem.at[s]).start(priority=1)

    if unrolled:
        for r in range(TM):
            one(r)
    else:
        def body(r, carry):
            one(r)
            return carry

        lax.fori_loop(0, TM, body, 0, unroll=ISSUE_UNROLL)


def _inproj_kernel(p1_ref, p2_ref, x_ref, mod0_ref, tm_ref, ys_ref, mod_ref, g_ref, win_ref,
                   xn_ref, xm_ref, op_ref, buf, sem):
    i = pl.program_id(0)
    n_steps = pl.num_programs(0)
    slot = lax.rem(i, 2)

    @pl.when(i == 0)
    def _():
        _issue_row_gathers(p1_ref, p2_ref, ys_ref, buf, sem, 0, 0, False)

    _row_copy_wait(buf.at[slot, 0], sem.at[slot])
    _row_copy_wait(buf.at[slot, 1], sem.at[slot])
    y = (tm_ref[:, 0:1] * buf[slot, 0].reshape(TM, D)
         + tm_ref[:, 1:2] * buf[slot, 1].reshape(TM, D))
    xn = x_ref[...] + mod0_ref[0, 5:6, :] * y
    xn_ref[...] = xn
    _issue_row_gathers(p1_ref, p2_ref, ys_ref, buf, sem, jnp.minimum(i + 1, n_steps - 1), 1 - slot,
                       True)
    h = _norm_mod(xn, g_ref[...], mod_ref[0, 0:1, :], mod_ref[0, 1:2, :])
    hb = h.astype(BF16)
    xm_ref[...] = jnp.dot(hb, win_ref[:, :B_INNER], preferred_element_type=F32).astype(BF16)
    op_ref[...] = jnp.dot(hb, win_ref[:, B_INNER:], preferred_element_type=F32).astype(BF16)

    @pl.when(i == n_steps - 1)
    def _():
        _row_copy_wait(buf.at[1 - slot, 0], sem.at[1 - slot])
        _row_copy_wait(buf.at[1 - slot, 1], sem.at[1 - slot])


def _inproj(p1, p2, xa, mod0, tmeta, ys, mod, g, w_in):
    const = lambda i, p1, p2: (0, 0)
    row = lambda i, p1, p2: (i, 0)
    mod_map = lambda i, p1, p2: (_mod_row(i), 0, 0)
    return pl.pallas_call(
        _inproj_kernel,
        grid_spec=pltpu.PrefetchScalarGridSpec(
            num_scalar_prefetch=2,
            grid=(ALL_TILES,),
            in_specs=[
                pl.BlockSpec((TM, D), row),
                pl.BlockSpec((1, 6, D), mod_map),
                pl.BlockSpec((TM, LANES), row),
                pl.BlockSpec(memory_space=pl.ANY),
                pl.BlockSpec((1, 6, D), mod_map),
                pl.BlockSpec((1, D), const),
                pl.BlockSpec((D, 2 * B_INNER), const),
            ],
            out_specs=[pl.BlockSpec((TM, D), row),
                       pl.BlockSpec((TM, B_INNER), row),
                       pl.BlockSpec((TM, B_INNER), row)],
            scratch_shapes=[pltpu.VMEM((2, 2, TM) + ROW_TILE, F32), pltpu.SemaphoreType.DMA((2,))],
        ),
        out_shape=[jax.ShapeDtypeStruct((N_ALL, D), F32),
                   jax.ShapeDtypeStruct((N_ALL, B_INNER), BF16),
                   jax.ShapeDtypeStruct((N_ALL, B_INNER), BF16)],
        compiler_params=_cparams(("arbitrary",)),
        name="mlstm_inproj",
    )(p1, p2, xa, mod0, tmeta, ys, mod, g, w_in)


QV_V0 = HEADS * DQK
QV_W = QV_V0 + B_INNER


def _log_sigmoid(x):
    return jnp.minimum(x, 0.0) - jnp.log(1.0 + jnp.exp(-jnp.abs(x)))


def _qkv_kernel(xm_ref, prev_ref, next_ref, cw_ref, cb_ref, wq_ref, wkt_ref, wv_ref, wg_ref, wgt_ref,
                bg_ref, bgt_ref, xc_ref, qv_ref, kt_ref, gt_ref, gtt_ref):
    i = pl.program_id(0)
    lat = i < LAT_TILES
    first = jnp.where(lat, i % TILES_PER_SEQ == 0, True)
    last = jnp.where(lat, i % TILES_PER_SEQ == TILES_PER_SEQ - 1, True)
    xmb = xm_ref[...]
    prev = jnp.where(first, 0.0, prev_ref[HALO - 8:, :].astype(F32))
    nxt = jnp.where(last, 0.0, next_ref[0:8, :].astype(F32))
    half = CONV_K // 2

    rr = lax.broadcasted_iota(jnp.int32, (TM, TM), 0)
    cc = lax.broadcasted_iota(jnp.int32, (TM, TM), 1)
    shifts = {t: (cc == rr + (t - half)).astype(BF16) for t in range(CONV_K) if t != half}

    def conv_head(h):
        hs = slice(h * DH, (h + 1) * DH)
        xh = xmb[:, hs]

        def conv_rows(ext):
            out = jnp.zeros((8, DH), F32) + cb_ref[:, hs]
            for t in range(CONV_K):
                out = out + ext[8 + t - half:16 + t - half] * cw_ref[t:t + 1, hs]
            return out

        acc = xh.astype(F32) * cw_ref[half:half + 1, hs] + cb_ref[:, hs]
        for t, shift in shifts.items():
            acc = acc + jnp.dot(shift, xh, preferred_element_type=F32) * cw_ref[t:t + 1, hs]
        top = conv_rows(jnp.concatenate([prev[:, hs], xh[0:16].astype(F32)], axis=0))
        bot = conv_rows(jnp.concatenate([xh[TM - 16:].astype(F32), nxt[:, hs]], axis=0))
        acc = jnp.concatenate([top, acc[8:TM - 8], bot], axis=0)
        xcb = (acc * jax.nn.sigmoid(acc)).astype(BF16)
        xc_ref[:, hs] = xcb
        return xcb

    def project(h, xcb):
        qh = jnp.dot(xcb, wq_ref[h], preferred_element_type=F32)
        qv_ref[:, h * DQK:(h + 1) * DQK] = (qh * (DQK ** -0.5)).astype(BF16)
        kt_ref[h * DQK:(h + 1) * DQK, :] = _dot_nt(wkt_ref[h], xcb).astype(BF16)

    gts = jnp.dot(xmb, wg_ref[...], preferred_element_type=F32) + bg_ref[...]
    gtt = _dot_nt(wgt_ref[...], xmb) + bgt_ref[...]

    pending = None
    for h in range(HEADS):
        hs = slice(h * DH, (h + 1) * DH)
        xcb = conv_head(h)
        qv_ref[:, QV_V0 + h * DV:QV_V0 + (h + 1) * DV] = jnp.dot(
            xmb[:, hs], wv_ref[h], preferred_element_type=F32).astype(BF16)
        if pending is not None:
            project(*pending)
        pending = (h, xcb)
    project(*pending)
    lane = lax.broadcasted_iota(jnp.int32, (CHUNK, LANES), 1)
    lane_kind = (lane >> 3) & 3
    row = lax.broadcasted_iota(jnp.int32, (4 * HEADS, CHUNK), 0)
    row_kind = (row >> 3) & 3
    r = lax.broadcasted_iota(jnp.int32, (CHUNK, CHUNK), 0)
    c = lax.broadcasted_iota(jnp.int32, (CHUNK, CHUNK), 1)
    lower = (c <= r).astype(BF16)
    upper = (c >= r).astype(BF16)
    row_sums = jnp.concatenate([upper, lower, jnp.ones((CHUNK, LANES), BF16)], axis=1)
    for ch in range(TM // CHUNK):
        ts = slice(ch * CHUNK, (ch + 1) * CHUNK)
        g = gts[ts]
        g = jnp.where((lane_kind & 1) == 1, _log_sigmoid(g), g)
        terms = _bf16_terms(g, 3)
        pre = sum(_dot_nn(lower, t) for t in terms)
        suf = sum(_dot_nn(upper, t) for t in terms)
        gt_ref[ts, :] = jnp.where(lane_kind == 1, pre, jnp.where(lane_kind == 3, suf, g))
        gt = gtt[:, ts]
        gt = jnp.where((row_kind & 1) == 1, _log_sigmoid(gt), gt)
        sums = _dot_exact_rhs(gt, row_sums)
        gtt_ref[0:4 * HEADS, ts] = jnp.where(row_kind == 1, sums[:, :CHUNK],
                                             jnp.where(row_kind == 3, sums[:, CHUNK:2 * CHUNK], gt))
        gtt_ref[4 * HEADS:, ts] = sums[:, 2 * CHUNK:]


def _qkv(xm, conv_w, conv_b, wq, wkt, wv, wg, wgt, bg, bgt):
    const = lambda i: (0, 0)
    const3 = lambda i: (0, 0, 0)
    hb = TM // HALO
    n_hb = N_ALL // HALO
    return pl.pallas_call(
        _qkv_kernel,
        grid=(ALL_TILES,),
        in_specs=[
            pl.BlockSpec((TM, B_INNER), lambda i: (i, 0)),
            pl.BlockSpec((HALO, B_INNER), lambda i: (jnp.maximum(i * hb - 1, 0), 0)),
            pl.BlockSpec((HALO, B_INNER), lambda i: (jnp.minimum((i + 1) * hb, n_hb - 1), 0)),
            pl.BlockSpec((CONV_K, B_INNER), const),
            pl.BlockSpec((1, B_INNER), const),
            pl.BlockSpec((HEADS, DH, DQK), const3),
            pl.BlockSpec((HEADS, DQK, DH), const3),
            pl.BlockSpec((HEADS, DH, DV), const3),
            pl.BlockSpec((B_INNER, LANES), const),
            pl.BlockSpec((4 * HEADS, B_INNER), const),
            pl.BlockSpec((1, LANES), const),
            pl.BlockSpec((4 * HEADS, 1), const),
        ],
        out_specs=[
            pl.BlockSpec((TM, B_INNER), lambda i: (i, 0)),
            pl.BlockSpec((TM, QV_W), lambda i: (i, 0)),
            pl.BlockSpec((HEADS * DQK, TM), lambda i: (0, i)),
            pl.BlockSpec((TM, LANES), lambda i: (i, 0)),
            pl.BlockSpec((8 * HEADS, TM), lambda i: (0, i)),
        ],
        out_shape=[
            jax.ShapeDtypeStruct((N_ALL, B_INNER), BF16),
            jax.ShapeDtypeStruct((N_ALL, QV_W), BF16),
            jax.ShapeDtypeStruct((HEADS * DQK, N_ALL), BF16),
            jax.ShapeDtypeStruct((N_ALL, LANES), F32),
            jax.ShapeDtypeStruct((8 * HEADS, N_ALL), F32),
        ],
        compiler_params=_cparams(("parallel",)),
        name="mlstm_qkv",
    )(xm, xm, xm, conv_w, conv_b, wq, wkt, wv, wg, wgt, bg, bgt)


CTX_CHUNKS = CTX // CHUNK
LAT_CHUNKS = SEQ // CHUNK
SCAN_STEPS = CTX_CHUNKS + LAT_CHUNKS


def _scan_dir(nb, d, qv_ref, kt_ref, g_ref, gt_ref, o_ref, c_scr, m_scr):
    r = lax.broadcasted_iota(jnp.int32, (CHUNK, CHUNK), 0)
    c = lax.broadcasted_iota(jnp.int32, (CHUNK, CHUNK), 1)
    if d == 0:
        keep = c <= r
    else:
        keep = c >= r
    ones_blk = jnp.ones((CHUNK, LANES), BF16)
    bcol_all = g_ref[...]
    gtt = gt_ref[0:4 * HEADS, :]
    tot_all = gt_ref[4 * HEADS:, :]
    base = 2 * HEADS * d

    def head_first(h):
        li_r = gtt[base + h:base + h + 1, :]
        b_r = gtt[base + HEADS + h:base + HEADS + h + 1, :]
        b_last = tot_all[base + HEADS + h:base + HEADS + h + 1, :]
        m_old = m_scr[nb, d, h]
        c_old = c_scr[nb, d, h]
        qh = qv_ref[:, h * DQK:(h + 1) * DQK]
        kth = kt_ref[h * DQK:(h + 1) * DQK, :]
        vaug = jnp.concatenate([qv_ref[:, QV_V0 + h * DV:QV_V0 + (h + 1) * DV], ones_blk], axis=1)
        g_r = b_last - b_r + li_r
        m_new = jnp.maximum(b_last + m_old, jnp.max(g_r, axis=-1, keepdims=True))
        decay = jnp.exp(b_last + m_old - m_new)
        wk = jnp.exp(g_r - m_new)
        kwt = (kth.astype(F32) * wk).astype(BF16)
        qk = jnp.dot(qh, kth, preferred_element_type=F32)
        qc = jnp.dot(qh, c_old.astype(BF16), preferred_element_type=F32)
        c_scr[nb, d, h] = jnp.concatenate([decay] * 3, axis=1) * c_old + jnp.dot(
            kwt, vaug, preferred_element_type=F32)
        m_scr[nb, d, h] = m_new
        return h, li_r, b_r, m_old, vaug, qk, qc

    def head_second(h, li_r, b_r, m_old, vaug, qk, qc):
        b_full = jnp.broadcast_to(bcol_all[:, base + HEADS + h:base + HEADS + h + 1], (CHUNK, LANES))
        dmat = jnp.where(keep, b_full - (b_r - li_r), NEG_INF)
        inter = b_full + m_old
        m_t = jnp.maximum(inter, jnp.max(dmat, axis=-1, keepdims=True))
        p = jnp.exp(dmat - m_t)
        s = qk * p
        a = jnp.exp(inter - m_t)
        num = jnp.concatenate([a] * 3, axis=1) * qc + jnp.dot(
            s.astype(BF16), vaug, preferred_element_type=F32)
        inv = 1.0 / jnp.maximum(jnp.abs(num[:, DV:]), jnp.exp(-m_t))
        o_ref[nb, :, h * DV:(h + 1) * DV] = (
            num[:, :DV] * jnp.concatenate([inv] * 2, axis=1)).astype(BF16)

    return head_first, head_second


SCAN_NB = 4
SCAN_IN = 4
SCAN_SKEW = 1


def _scan_kernel(*refs):
    n_in = SCAN_NB * 2 * SCAN_IN
    ins, outs = refs[:n_in], refs[n_in:n_in + 2]
    c_scr, m_scr = refs[n_in + 2:]

    @pl.when(pl.program_id(1) == 0)
    def _():
        c_scr[...] = jnp.zeros_like(c_scr)
        m_scr[...] = jnp.zeros_like(m_scr)

    units = []
    for nb in range(SCAN_NB):
        for d in range(2):
            k = nb * 2 + d
            first, second = _scan_dir(nb, d, *ins[k * SCAN_IN:(k + 1) * SCAN_IN], outs[d], c_scr, m_scr)
            units += [(first, second, h) for h in range(HEADS)]
    pending = []
    for first, second, h in units:
        pending.append((second, first(h)))
        if len(pending) > SCAN_SKEW:
            fn, old_vals = pending.pop(0)
            fn(*old_vals)
    for fn, old_vals in pending:
        fn(*old_vals)


def _scan(qv, kt, gts, gtt):
    lat_blk = N_LAT // CHUNK

    def fwd_in(b, j):
        return jnp.where(j < CTX_CHUNKS, lat_blk + b * CTX_CHUNKS + j, b * LAT_CHUNKS + j - CTX_CHUNKS)

    def bwd_in(b, j):
        return jnp.where(j < CTX_CHUNKS, lat_blk + b * CTX_CHUNKS + (CTX_CHUNKS - 1 - j),
                         b * LAT_CHUNKS + (SCAN_STEPS - 1 - j))

    def fwd_out(b, j):
        return b * LAT_CHUNKS + jnp.maximum(j - CTX_CHUNKS, 0)

    def bwd_out(b, j):
        return b * LAT_CHUNKS + jnp.minimum(SCAN_STEPS - 1 - j, LAT_CHUNKS - 1)

    n_g = BATCH // SCAN_NB

    def specs(fn, nb):
        blk = lambda g, j: fn(nb * n_g + g, j)
        return [
            pl.BlockSpec((CHUNK, QV_W), lambda g, j: (blk(g, j), 0)),
            pl.BlockSpec((HEADS * DQK, CHUNK), lambda g, j: (0, blk(g, j))),
            pl.BlockSpec((CHUNK, LANES), lambda g, j: (blk(g, j), 0)),
            pl.BlockSpec((8 * HEADS, CHUNK), lambda g, j: (0, blk(g, j))),
        ]

    def out_spec(fn):
        return pl.BlockSpec((SCAN_NB, CHUNK, B_INNER), lambda g, j: (0, fn(g, j), 0))

    in_specs, operands = [], []
    for nb in range(SCAN_NB):
        for fn_in in (fwd_in, bwd_in):
            in_specs += specs(fn_in, nb)
            operands += [qv, kt, gts, gtt]
    h_shape = jax.ShapeDtypeStruct((SCAN_NB, N_LAT // SCAN_NB, B_INNER), BF16)
    hf, hb = pl.pallas_call(
        _scan_kernel,
        grid=(n_g, SCAN_STEPS),
        in_specs=in_specs,
        out_specs=[out_spec(fwd_out), out_spec(bwd_out)],
        out_shape=[h_shape, h_shape],
        scratch_shapes=[pltpu.VMEM((SCAN_NB, 2, HEADS, DQK, DV + LANES), F32),
                        pltpu.VMEM((SCAN_NB, 2, HEADS, 1, LANES), F32)],
        compiler_params=_cparams(("parallel", "arbitrary")),
        name="mlstm_scan",
    )(*operands)
    return hf.reshape(N_LAT, B_INNER), hb.reshape(N_LAT, B_INNER)


def _combine_kernel(x_ref, mod_ref, hf_ref, hb_ref, xc_ref, op_ref, hg_ref, sk_ref, wout_ref,
                    g2_ref, wr_ref, br_ref, o_ref, meta_ref, tmeta_ref, cnt_ref, cntr_ref,
                    y_scr, carry, carry_r):
    for h in range(HEADS):
        hs = slice(h * DV, (h + 1) * DV)
        s = hf_ref[:, hs].astype(F32) + hb_ref[:, hs].astype(F32)
        s = s * lax.rsqrt(jnp.mean(s * s, axis=-1, keepdims=True) + EPS)
        y = jax.nn.sigmoid(op_ref[:, hs].astype(F32)) * (
            s * hg_ref[:, hs] + sk_ref[:, hs] * xc_ref[:, hs].astype(F32))
        y_scr[:, hs] = y.astype(BF16)
        if h % 2 == 1:
            ks = slice((h - 1) * DV, (h + 1) * DV)
            part = jnp.dot(y_scr[:, ks], wout_ref[ks, :], preferred_element_type=F32)
            out = part if h == 1 else out + part
    xn = x_ref[...] + mod_ref[0, 2:3, :] * out
    o_ref[...] = xn
    _route_tile(xn, mod_ref, g2_ref, wr_ref, br_ref, meta_ref, tmeta_ref, cnt_ref, cntr_ref,
                carry, carry_r)


def _combine(xa, mod, hf, hb, xc, op, head_g, skip, w_out, g2, wr, br):
    const = lambda i: (0, 0)
    row = lambda i: (i, 0)
    r_in, r_out, r_shape, r_scratch = _router_specs(LAT_TILES)
    return pl.pallas_call(
        _combine_kernel,
        grid=(LAT_TILES,),
        in_specs=[
            pl.BlockSpec((TM, D), row),
            pl.BlockSpec((1, 6, D), lambda i: (_mod_row(i), 0, 0)),
            pl.BlockSpec((TM, B_INNER), row),
            pl.BlockSpec((TM, B_INNER), row),
            pl.BlockSpec((TM, B_INNER), row),
            pl.BlockSpec((TM, B_INNER), row),
            pl.BlockSpec((1, B_INNER), const),
            pl.BlockSpec((1, B_INNER), const),
            pl.BlockSpec((B_INNER, D), const),
        ] + r_in,
        out_specs=[pl.BlockSpec((TM, D), row)] + r_out,
        out_shape=[jax.ShapeDtypeStruct((N_LAT, D), F32)] + r_shape,
        scratch_shapes=[pltpu.VMEM((TM, B_INNER), BF16)] + r_scratch,
        compiler_params=_cparams(("arbitrary",)),
        name="mlstm_combine",
    )(xa, mod, hf, hb, xc, op, head_g, skip, w_out, g2, wr, br)


def _router_weights(w_grp, b_grp, w_exp, b_exp):
    wr = jnp.zeros((LANES, D), F32).at[:N_EXP].set(w_exp.T).at[N_EXP:N_EXP + N_GRP].set(w_grp.T)
    br = jnp.zeros((LANES, 1), F32).at[:N_EXP, 0].set(b_exp).at[N_EXP:N_EXP + N_GRP, 0].set(b_grp)
    return wr.astype(BF16), br


PLAN_BLK = 2048
PLAN_PAIRS = 256


def _plan_kernel(meta_ref, cc_ref, cr_ref, pp_ref, plan_ref):
    row = lax.broadcasted_iota(jnp.int32, (LANES, LANES), 0)
    col = lax.broadcasted_iota(jnp.int32, (LANES, LANES), 1)
    cnt_c = cc_ref[...]
    cnt_r = cr_ref[0:1, :]
    starts_c = jnp.sum(jnp.where(col < row, cnt_r, 0.0), axis=1, keepdims=True)
    erow = lax.broadcasted_iota(jnp.int32, (LANES, PLAN_BLK), 0).astype(F32)
    for j in range(2):
        pos = jnp.sum(jnp.where(erow == meta_ref[j:j + 1, :], starts_c, 0.0), axis=0,
                      keepdims=True) + meta_ref[j + 2:j + 3, :]
        pp_ref[j:j + 1, :] = pos.astype(jnp.int32)
    pp_ref[2:8, :] = jnp.zeros((6, PLAN_BLK), jnp.int32)

    @pl.when(pl.program_id(0) == 0)
    def _():
        ends_c = starts_c + cnt_c
        starts_r = jnp.sum(jnp.where(row < col, cnt_c, 0.0), axis=0, keepdims=True)
        ends_r = starts_r + cnt_r

        def tiles(st, en, cn):
            first_tile = jnp.floor(st * (1.0 / TM))
            n = jnp.where(cn > 0.0, jnp.floor((en - 1.0) * (1.0 / TM)) - first_tile + 1.0, 0.0)
            return first_tile, n

        ft_c, pairs_c = tiles(starts_c, ends_c, cnt_c)
        _, pairs_r = tiles(starts_r, ends_r, cnt_r)
        pend_c = jnp.sum(jnp.where(col <= row, pairs_r, 0.0), axis=1, keepdims=True)
        pstart_c = pend_c - pairs_c
        total = jnp.sum(pairs_r, axis=1, keepdims=True)
        q = lax.broadcasted_iota(jnp.int32, (1, PLAN_PAIRS), 1).astype(F32)
        erow_p = lax.broadcasted_iota(jnp.int32, (LANES, PLAN_PAIRS), 0).astype(F32)

        def at(qv):
            qc = jnp.maximum(jnp.minimum(qv, total - 1.0), 0.0)
            e = jnp.sum(jnp.where(qc >= pend_c, 1.0, 0.0), axis=0, keepdims=True)
            oh = erow_p == e
            tile = jnp.sum(jnp.where(oh, ft_c - pstart_c, 0.0), axis=0, keepdims=True) + qc
            return e, oh, tile

        e_q, oh, tile_q = at(q)
        e_p, _, tile_p = at(q - 1.0)
        lo = jnp.sum(jnp.where(oh, starts_c, 0.0), axis=0, keepdims=True) - tile_q * TM
        hi = jnp.sum(jnp.where(oh, ends_c, 0.0), axis=0, keepdims=True) - tile_q * TM
        rows = (tile_q, e_q, jnp.clip(lo, 0.0, TM), jnp.clip(hi, 0.0, TM),
                jnp.where((q == 0.0) | (tile_q != tile_p), 1.0, 0.0),
                jnp.where((q == 0.0) | (e_q != e_p), 1.0, 0.0),
                jnp.where(q < total, 1.0, 0.0),
                jnp.zeros((1, PLAN_PAIRS), F32))
        for j, v in enumerate(rows):
            plan_ref[j:j + 1, :] = v.astype(jnp.int32)


def _route_plan(meta, cnt_c, cnt_r, n_rows):
    n_pairs = 2 * n_rows // TM + N_EXP - 1
    assert n_pairs <= PLAN_PAIRS and n_rows % PLAN_BLK == 0
    const = lambda i: (0, 0)
    pp, plan = pl.pallas_call(
        _plan_kernel,
        grid=(n_rows // PLAN_BLK,),
        in_specs=[pl.BlockSpec((8, PLAN_BLK), lambda i: (0, i)),
                  pl.BlockSpec((LANES, 1), const),
                  pl.BlockSpec((8, LANES), const)],
        out_specs=[pl.BlockSpec((8, PLAN_BLK), lambda i: (0, i)),
                   pl.BlockSpec((8, PLAN_PAIRS), const)],
        out_shape=[jax.ShapeDtypeStruct((8, n_rows), jnp.int32),
                   jax.ShapeDtypeStruct((8, PLAN_PAIRS), jnp.int32)],
        compiler_params=_cparams(("arbitrary",)),
        name="moe_plan",
    )(meta, cnt_c, cnt_r)
    return pp[0], pp[1], tuple(plan[j, :n_pairs] for j in range(7))


def _moe_experts(xa, routing, mod, g2, w1, w3, w2, layer, n_rows):
    meta, _, cnt_c, cnt_r = routing
    p1, p2, plan = _route_plan(meta, cnt_c, cnt_r, n_rows)
    xs = _scatter(p1, p2, xa, mod, g2, n_rows // TM)
    return p1, p2, _grouped(plan, xs, w1, w3, w2, layer)


def kernel(x, c, ctx, c_ctx, norm_g, w_ada, b_ada, a_w_in, a_b_in, a_g_v, a_w_s, a_b_s, a_w_out,
           b_w_in, b_conv_w, b_conv_b, b_w_q, b_w_k, b_w_v, b_w_gate, b_b_gate, b_head_g, b_skip,
           b_w_out, moe_w_grp, moe_b_grp, moe_w_exp, moe_b_exp, moe_w1, moe_w3, moe_w2, final_g):
    cc = jnp.zeros((MOD_ROWS, D), F32).at[:BATCH].set(c).at[BATCH].set(c_ctx)
    mods = _ada(cc, w_ada, b_ada).reshape(2, MOD_ROWS, 6, D)

    mod = mods[0]
    b_s_full = jnp.repeat(a_b_s[0].T, A_GC, axis=1)
    g2 = norm_g[0, 1].reshape(1, D)
    wr, br = _router_weights(moe_w_grp[0], moe_b_grp[0], moe_w_exp[0], moe_b_exp[0])
    xa, *routing = _gmlp(x.reshape(N_LAT, D), ctx.reshape(N_CTX, D), mod, norm_g[0, 0].reshape(1, D),
                         a_w_in[0].astype(BF16), a_b_in[0].reshape(1, -1), a_g_v[0].reshape(1, -1),
                         a_w_s[0].astype(BF16), b_s_full, a_w_out[0].astype(BF16), g2, wr, br)
    p1, p2, ys = _moe_experts(xa, routing, mod, g2, moe_w1, moe_w3, moe_w2, 0, N_ALL)

    mod = mods[1]
    xa, xm, op = _inproj(p1, p2, xa, mods[0], routing[1], ys, mod, norm_g[1, 0].reshape(1, D),
                         b_w_in[0].astype(BF16))
    wg = jnp.zeros((B_INNER, LANES), F32).at[:, :4 * HEADS].set(b_w_gate[0]).astype(BF16)
    bg = jnp.zeros((1, LANES), F32).at[0, :4 * HEADS].set(b_b_gate[0])
    xc, qv, kt, gts, gtt = _qkv(xm, b_conv_w[0], b_conv_b[0].reshape(1, -1),
                                  b_w_q[0].astype(BF16),
                                  jnp.transpose(b_w_k[0], (0, 2, 1)).astype(BF16),
                                  b_w_v[0].astype(BF16), wg, b_w_gate[0].T.astype(BF16), bg,
                                  b_b_gate[0].reshape(-1, 1))
    hf, hb = _scan(qv, kt, gts, gtt)
    g2 = norm_g[1, 1].reshape(1, D)
    wr, br = _router_weights(moe_w_grp[1], moe_b_grp[1], moe_w_exp[1], moe_b_exp[1])
    xl, *routing = _combine(xa, mod, hf, hb, xc, op, b_head_g[0].reshape(1, -1),
                            b_skip[0].reshape(1, -1), b_w_out[0].astype(BF16), g2, wr, br)
    p1, p2, ys = _moe_experts(xl, routing, mod, g2, moe_w1, moe_w3, moe_w2, 1, N_LAT)
    out = _gather(p1, p2, xl, mod, routing[1], final_g.reshape(1, D), ys, LAT_TILES, True)
    return out.reshape(BATCH, SEQ, D)
```

```python
import functools

import jax
import jax.numpy as jnp
from jax import lax
from jax.experimental import pallas as pl
from jax.experimental.pallas import tpu as pltpu

F32 = jnp.float32
BF16 = jnp.bfloat16

D = 1024
BATCH = 8
SEQ = 2048
CTX = 256
EPS = 1e-6
NEG_INF = -1e30
N_LAT = BATCH * SEQ
N_CTX = BATCH * CTX
N_ALL = N_LAT + N_CTX

TM = 256
LAT_TILES = N_LAT // TM
ALL_TILES = N_ALL // TM
TILES_PER_SEQ = SEQ // TM
MOD_ROWS = 16

CHUNK = 128
A_HALF = 2048
A_GROUPS = 8
A_GC = A_HALF // A_GROUPS
B_INNER = 2048
HEADS = 8
DH = B_INNER // HEADS
DQK = DH // 2
DV = DH
CONV_K = 5
HALO = 16
N_EXP = 32
N_GRP = 4
EXP_PER_GRP = 8
D_EXP = 512
LANES = 128

VMEM_LIMIT = 56 * 1024 * 1024


def _cparams(sem):
    return pltpu.CompilerParams(dimension_semantics=sem, vmem_limit_bytes=VMEM_LIMIT)


def _mod_row(i):
    return jnp.where(i < LAT_TILES, i // TILES_PER_SEQ, BATCH)


def _norm_mod(x, g, shift, scale):
    y = x * lax.rsqrt(jnp.mean(x * x, axis=-1, keepdims=True) + EPS) * g
    return y * (1.0 + scale) + shift


def _bf16_terms(x, n):
    terms = []
    r = x
    for _ in range(n):
        t = r.astype(BF16)
        terms.append(t)
        r = r - t.astype(F32)
    return terms


def _dot_nt(a, b):
    return lax.dot_general(a, b, (((1,), (1,)), ((), ())), preferred_element_type=F32)


def _dot_nn(a, b):
    return jnp.dot(a, b, preferred_element_type=F32)


def _dot3(a, b, dot=_dot_nn):
    a1, a2 = _bf16_terms(a, 2)
    b1, b2 = _bf16_terms(b, 2)
    return dot(a1, b1) + (dot(a1, b2) + dot(a2, b1))


def _dot_exact_lhs(a_bf16, b, dot=_dot_nn):
    return sum(dot(a_bf16, t) for t in _bf16_terms(b, 3))


def _dot_exact_rhs(a, b_bf16, dot=_dot_nn):
    return sum(dot(t, b_bf16) for t in _bf16_terms(a, 3))


def _gelu_tanh(x):
    half = 0.5 * x
    t = jnp.tanh(x * (0.7978845608028654 + 0.035677408136300125 * (x * x)))
    return half + half * t


ADA_BN = 1536


def _ada_kernel(c_ref, w_ref, b_ref, o_ref):
    c = c_ref[...]
    a = c * jax.nn.sigmoid(c)
    o_ref[0] = _dot3(a, w_ref[0]) + b_ref[0]


def _ada(cc, w_ada, b_ada):
    depth = w_ada.shape[0]
    return pl.pallas_call(
        _ada_kernel,
        grid=(depth, 6 * D // ADA_BN),
        in_specs=[
            pl.BlockSpec((MOD_ROWS, D), lambda l, j: (0, 0)),
            pl.BlockSpec((1, D, ADA_BN), lambda l, j: (l, 0, j)),
            pl.BlockSpec((1, 1, ADA_BN), lambda l, j: (l, 0, j)),
        ],
        out_specs=pl.BlockSpec((1, MOD_ROWS, ADA_BN), lambda l, j: (l, 0, j)),
        out_shape=jax.ShapeDtypeStruct((depth, MOD_ROWS, 6 * D), F32),
        compiler_params=_cparams(("parallel", "parallel")),
        name="ada",
    )(cc, w_ada, b_ada.reshape(depth, 1, 6 * D))


GM_CH = 512


def _gmlp_kernel(xl_ref, xc_ref, mod_ref, g_ref, win_ref, bin_ref, gv_ref, ws_ref, bs_ref, wout_ref,
                 g2_ref, wr_ref, br_ref, o_ref, meta_ref, tmeta_ref, cnt_ref, cntr_ref,
                 z_scr, y_scr, carry, carry_r):
    x = jnp.where(pl.program_id(0) < LAT_TILES, xl_ref[...], xc_ref[...])
    h = _norm_mod(x, g_ref[...], mod_ref[0, 0:1, :], mod_ref[0, 1:2, :])
    hb = h.astype(BF16)
    s1 = jnp.zeros((TM, 1), F32)
    s2 = jnp.zeros((TM, 1), F32)
    n_ch = 2 * A_HALF // GM_CH
    for j in list(range(n_ch // 2, n_ch)) + list(range(n_ch // 2)):
        cs = slice(j * GM_CH, (j + 1) * GM_CH)
        zc = jnp.dot(hb, win_ref[:, cs], preferred_element_type=F32) + bin_ref[:, cs]
        zc = _gelu_tanh(zc)
        z_scr[:, cs] = zc
        if j * GM_CH >= A_HALF:
            s1 = s1 + jnp.sum(zc, axis=-1, keepdims=True)
            s2 = s2 + jnp.sum(zc * zc, axis=-1, keepdims=True)
    mu = s1 * (1.0 / A_HALF)
    rstd = lax.rsqrt(s2 * (1.0 / A_HALF) - mu * mu + EPS)
    for c in range(TM // CHUNK):
        rs = slice(c * CHUNK, (c + 1) * CHUNK)
        for g in range(A_GROUPS):
            cs = slice(g * A_GC, (g + 1) * A_GC)
            vs = slice(A_HALF + g * A_GC, A_HALF + (g + 1) * A_GC)
            v = (z_scr[rs, vs] - mu[rs]) * rstd[rs] * gv_ref[:, cs]
            s = jnp.dot(ws_ref[g], v.astype(BF16), preferred_element_type=F32) + bs_ref[:, cs]
            y_scr[rs, cs] = (z_scr[rs, cs] * s).astype(BF16)
    out = jnp.dot(y_scr[...], wout_ref[...], preferred_element_type=F32)
    xn = x + mod_ref[0, 2:3, :] * out
    o_ref[...] = xn
    _route_tile(xn, mod_ref, g2_ref, wr_ref, br_ref, meta_ref, tmeta_ref, cnt_ref, cntr_ref,
                carry, carry_r)


def _gmlp(xl, xc, mod, g, w_in, b_in, g_v, w_s, b_s_full, w_out, g2, wr, br):
    const = lambda i: (0, 0)
    r_in, r_out, r_shape, r_scratch = _router_specs(ALL_TILES)
    return pl.pallas_call(
        _gmlp_kernel,
        grid=(ALL_TILES,),
        in_specs=[
            pl.BlockSpec((TM, D), lambda i: (jnp.minimum(i, LAT_TILES - 1), 0)),
            pl.BlockSpec((TM, D), lambda i: (jnp.maximum(i - LAT_TILES, 0), 0)),
            pl.BlockSpec((1, 6, D), lambda i: (_mod_row(i), 0, 0)),
            pl.BlockSpec((1, D), const),
            pl.BlockSpec((D, 2 * A_HALF), const),
            pl.BlockSpec((1, 2 * A_HALF), const),
            pl.BlockSpec((1, A_HALF), const),
            pl.BlockSpec((A_GROUPS, CHUNK, CHUNK), lambda i: (0, 0, 0)),
            pl.BlockSpec((CHUNK, A_HALF), const),
            pl.BlockSpec((A_HALF, D), const),
        ] + r_in,
        out_specs=[pl.BlockSpec((TM, D), lambda i: (i, 0))] + r_out,
        out_shape=[jax.ShapeDtypeStruct((N_ALL, D), F32)] + r_shape,
        scratch_shapes=[pltpu.VMEM((TM, 2 * A_HALF), F32), pltpu.VMEM((TM, A_HALF), BF16)] + r_scratch,
        compiler_params=_cparams(("arbitrary",)),
        name="gmlp",
    )(xl, xc, mod, g, w_in, b_in, g_v, w_s, b_s_full, w_out, g2, wr, br)


def _route_tile(x, mod_ref, g_ref, wr_ref, br_ref, meta_ref, tmeta_ref, cnt_ref, cntr_ref,
                carry, carry_r):
    @pl.when(pl.program_id(0) == 0)
    def _():
        carry[...] = jnp.zeros_like(carry)
        carry_r[...] = jnp.zeros_like(carry_r)

    h = _norm_mod(x, g_ref[...], mod_ref[0, 3:4, :], mod_ref[0, 4:5, :])
    lt = _dot_nt(wr_ref[...], h.astype(BF16)) + br_ref[...]
    e_t = lt[0:N_EXP]
    row8 = lax.broadcasted_iota(jnp.int32, (EXP_PER_GRP, TM), 0).astype(F32)
    g_t = jnp.where(row8 < N_GRP, lt[N_EXP:N_EXP + EXP_PER_GRP], -jnp.inf)
    gmax = jnp.max(g_t, axis=0, keepdims=True)
    p_g = 1.0 / jnp.sum(jnp.exp(g_t - gmax), axis=0, keepdims=True)
    g_idx = jnp.min(jnp.where(g_t == gmax, row8, float(EXP_PER_GRP)), axis=0, keepdims=True)
    sel = jnp.zeros((EXP_PER_GRP, TM), F32)
    for g in range(N_GRP):
        sel = sel + jnp.where(g_idx == g, e_t[g * EXP_PER_GRP:(g + 1) * EXP_PER_GRP], 0.0)
    m1 = jnp.max(sel, axis=0, keepdims=True)
    i1 = jnp.min(jnp.where(sel == m1, row8, float(EXP_PER_GRP)), axis=0, keepdims=True)
    sel2 = jnp.where(row8 == i1, -jnp.inf, sel)
    m2 = jnp.max(sel2, axis=0, keepdims=True)
    i2 = jnp.min(jnp.where(sel2 == m2, row8, float(EXP_PER_GRP)), axis=0, keepdims=True)
    e2 = jnp.exp(m2 - m1)
    w1 = p_g / (1.0 + e2)
    w2 = p_g * e2 / (1.0 + e2)
    row = lax.broadcasted_iota(jnp.int32, (LANES, TM), 0).astype(F32)
    id1 = g_idx * EXP_PER_GRP + i1
    id2 = g_idx * EXP_PER_GRP + i2
    oh1 = row == id1
    oh2 = row == id2
    oh = jnp.where(oh1, 1.0, 0.0) + jnp.where(oh2, 1.0, 0.0)
    before = (lax.broadcasted_iota(jnp.int32, (TM, TM), 0)
              < lax.broadcasted_iota(jnp.int32, (TM, TM), 1)).astype(BF16)
    ohb = oh.astype(BF16)
    tot = jnp.dot(ohb, before, preferred_element_type=F32) + carry[...]
    r1 = jnp.sum(jnp.where(oh1, tot, 0.0), axis=0, keepdims=True)
    r2 = jnp.sum(jnp.where(oh2, tot, 0.0), axis=0, keepdims=True)
    carry[...] += jnp.sum(oh, axis=-1, keepdims=True)
    carry_r[...] += _dot_nt(jnp.ones((8, TM), BF16), ohb)
    cnt_ref[...] = carry[...]
    cntr_ref[...] = carry_r[...]
    meta_ref[0:1, :] = id1
    meta_ref[1:2, :] = id2
    meta_ref[2:3, :] = r1
    meta_ref[3:4, :] = r2
    meta_ref[4:8, :] = jnp.zeros((4, TM), F32)
    tmeta_ref[...] = (jnp.where(row == 0.0, w1, 0.0) + jnp.where(row == 1.0, w2, 0.0)).T


def _router_specs(n_tiles):
    const = lambda i: (0, 0)
    n = n_tiles * TM
    in_specs = [pl.BlockSpec((1, D), const), pl.BlockSpec((LANES, D), const),
                pl.BlockSpec((LANES, 1), const)]
    out_specs = [pl.BlockSpec((8, TM), lambda i: (0, i)),
                 pl.BlockSpec((TM, LANES), lambda i: (i, 0)),
                 pl.BlockSpec((LANES, 1), const),
                 pl.BlockSpec((8, LANES), const)]
    out_shape = [jax.ShapeDtypeStruct((8, n), F32), jax.ShapeDtypeStruct((n, LANES), F32),
                 jax.ShapeDtypeStruct((LANES, 1), F32), jax.ShapeDtypeStruct((8, LANES), F32)]
    scratch = [pltpu.VMEM((LANES, 1), F32), pltpu.VMEM((8, LANES), F32)]
    return in_specs, out_specs, out_shape, scratch


ISSUE_UNROLL = 8
ROW_TILE = (8, LANES)
XS_RING = 3
MXU_N = 256


def _row_copy_wait(buf_slot, sem_slot):
    pltpu.make_async_copy(buf_slot, buf_slot, sem_slot).wait()


def _scatter_kernel(p1_ref, p2_ref, x_ref, mod_ref, g_ref, xs_ref, buf, sem):
    i = pl.program_id(0)
    n_steps = pl.num_programs(0)
    slot = lax.rem(i, 2)

    def wait_slot(s):
        _row_copy_wait(buf.at[s], sem.at[s])
        _row_copy_wait(buf.at[s], sem.at[s])

    @pl.when(i >= 2)
    def _():
        wait_slot(slot)

    h = _norm_mod(x_ref[...], g_ref[...], mod_ref[0, 3:4, :], mod_ref[0, 4:5, :])
    buf[slot] = h.reshape(TM, *ROW_TILE)
    base = i * TM

    def body(r, carry):
        src = buf.at[slot, r]
        pltpu.make_async_copy(src, xs_ref.at[p1_ref[base + r]], sem.at[slot]).start()
        pltpu.make_async_copy(src, xs_ref.at[p2_ref[base + r]], sem.at[slot]).start(priority=1)
        return carry

    lax.fori_loop(0, TM, body, 0, unroll=ISSUE_UNROLL)

    @pl.when(i == n_steps - 1)
    def _():
        @pl.when(i >= 1)
        def _():
            wait_slot(1 - slot)

        wait_slot(slot)


def _scatter(p1, p2, xa, mod, g, n_tiles):
    n = n_tiles * TM
    return pl.pallas_call(
        _scatter_kernel,
        grid_spec=pltpu.PrefetchScalarGridSpec(
            num_scalar_prefetch=2,
            grid=(n_tiles,),
            in_specs=[
                pl.BlockSpec((TM, D), lambda i, p1, p2: (i, 0)),
                pl.BlockSpec((1, 6, D), lambda i, p1, p2: (_mod_row(i), 0, 0)),
                pl.BlockSpec((1, D), lambda i, p1, p2: (0, 0)),
            ],
            out_specs=pl.BlockSpec(memory_space=pl.ANY),
            scratch_shapes=[pltpu.VMEM((2, TM) + ROW_TILE, F32), pltpu.SemaphoreType.DMA((2,))],
        ),
        out_shape=jax.ShapeDtypeStruct((2 * n,) + ROW_TILE, F32),
        compiler_params=_cparams(("arbitrary",)),
        name="moe_scatter",
    )(p1, p2, xa, mod, g)


def _grouped_kernel(tile_ref, exp_ref, lo_ref, hi_ref, first_ref, newexp_ref, valid_ref,
                    xs_ref, w1_ref, w3_ref, w2_ref, ys_ref, wb1, wb3, wb2, xbuf, xsem,
                    *, n_sorted_tiles):
    q = pl.program_id(0)

    def tile_copy(t):
        s = lax.rem(t, XS_RING)
        return pltpu.make_async_copy(xs_ref.at[pl.ds(t * TM, TM)], xbuf.at[s], xsem.at[s])

    @pl.when(valid_ref[q] == 1)
    def _():
        tile = tile_ref[q]

        @pl.when(q == 0)
        def _():
            for t in range(min(XS_RING - 1, n_sorted_tiles)):
                tile_copy(t).start()

        @pl.when(first_ref[q] == 1)
        def _():
            tile_copy(tile).wait()

            @pl.when(tile + XS_RING - 1 < n_sorted_tiles)
            def _():
                tile_copy(tile + XS_RING - 1).start()

        @pl.when(newexp_ref[q] == 1)
        def _():
            wb1[...] = w1_ref[0, 0].astype(BF16)
            wb3[...] = w3_ref[0, 0].astype(BF16)
            wb2[...] = w2_ref[0, 0].astype(BF16)

        x = xbuf[lax.rem(tile, XS_RING)].reshape(TM, D).astype(BF16)
        rows = lax.broadcasted_iota(jnp.int32, (TM, 1), 0)
        mine = (rows >= lo_ref[q]) & (rows < hi_ref[q])
        acts = []
        for j in range(D_EXP // MXU_N):
            cs = slice(j * MXU_N, (j + 1) * MXU_N)
            a = jnp.dot(x, wb1[:, cs], preferred_element_type=F32)
            b = jnp.dot(x, wb3[:, cs], preferred_element_type=F32)
            acts.append(jnp.where(mine, a * jax.nn.sigmoid(a) * b, 0.0).astype(BF16))
        act = jnp.concatenate(acts, axis=1)
        y = jnp.dot(act, wb2[...], preferred_element_type=F32).reshape(TM, *ROW_TILE)

        @pl.when(first_ref[q] == 1)
        def _():
            ys_ref[...] = y

        @pl.when(first_ref[q] == 0)
        def _():
            ys_ref[...] += y


def _grouped(plan, xs, w1, w3, w2, layer):
    n_pairs = plan[0].shape[0]
    tile_map = lambda q, tile, exp, *_: (tile[q], 0, 0)
    exp_map = lambda q, tile, exp, *_: (layer, exp[q], 0, 0)
    return pl.pallas_call(
        functools.partial(_grouped_kernel, n_sorted_tiles=xs.shape[0] // TM),
        grid_spec=pltpu.PrefetchScalarGridSpec(
            num_scalar_prefetch=7,
            grid=(n_pairs,),
            in_specs=[
                pl.BlockSpec(memory_space=pl.ANY),
                pl.BlockSpec((1, 1, D, D_EXP), exp_map),
                pl.BlockSpec((1, 1, D, D_EXP), exp_map),
                pl.BlockSpec((1, 1, D_EXP, D), exp_map),
            ],
            out_specs=pl.BlockSpec((TM,) + ROW_TILE, tile_map),
            scratch_shapes=[pltpu.VMEM((D, D_EXP), BF16), pltpu.VMEM((D, D_EXP), BF16),
                            pltpu.VMEM((D_EXP, D), BF16),
                            pltpu.VMEM((XS_RING, TM) + ROW_TILE, F32),
                            pltpu.SemaphoreType.DMA((XS_RING,))],
        ),
        out_shape=jax.ShapeDtypeStruct(xs.shape, F32),
        compiler_params=_cparams(("arbitrary",)),
        name="moe_grouped",
    )(*plan, xs, w1, w3, w2)


def _gather_kernel(p1_ref, p2_ref, x_ref, mod_ref, tm_ref, fg_ref, ys_ref, o_ref, buf, sem,
                   *, final_norm):
    i = pl.program_id(0)
    n_steps = pl.num_programs(0)
    slot = lax.rem(i, 2)

    def issue(tile, s):
        base = tile * TM

        def body(r, carry):
            pltpu.make_async_copy(ys_ref.at[p1_ref[base + r]], buf.at[s, 0, r], sem.at[s]).start()
            pltpu.make_async_copy(ys_ref.at[p2_ref[base + r]], buf.at[s, 1, r],
                                  sem.at[s]).start(priority=1)
            return carry

        lax.fori_loop(0, TM, body, 0, unroll=ISSUE_UNROLL)

    @pl.when(i == 0)
    def _():
        issue(0, 0)

    @pl.when(i + 1 < n_steps)
    def _():
        issue(i + 1, 1 - slot)

    _row_copy_wait(buf.at[slot, 0], sem.at[slot])
    _row_copy_wait(buf.at[slot, 1], sem.at[slot])
    y = (tm_ref[:, 0:1] * buf[slot, 0].reshape(TM, D)
         + tm_ref[:, 1:2] * buf[slot, 1].reshape(TM, D))
    out = x_ref[...] + mod_ref[0, 5:6, :] * y
    if final_norm:
        out = out * lax.rsqrt(jnp.mean(out * out, axis=-1, keepdims=True) + EPS) * fg_ref[...]
    o_ref[...] = out


def _gather(p1, p2, xa, mod, tmeta, final_g, ys, n_tiles, final_norm):
    return pl.pallas_call(
        functools.partial(_gather_kernel, final_norm=final_norm),
        grid_spec=pltpu.PrefetchScalarGridSpec(
            num_scalar_prefetch=2,
            grid=(n_tiles,),
            in_specs=[
                pl.BlockSpec((TM, D), lambda i, p1, p2: (i, 0)),
                pl.BlockSpec((1, 6, D), lambda i, p1, p2: (_mod_row(i), 0, 0)),
                pl.BlockSpec((TM, LANES), lambda i, p1, p2: (i, 0)),
                pl.BlockSpec((1, D), lambda i, p1, p2: (0, 0)),
                pl.BlockSpec(memory_space=pl.ANY),
            ],
            out_specs=pl.BlockSpec((TM, D), lambda i, p1, p2: (i, 0)),
            scratch_shapes=[pltpu.VMEM((2, 2, TM) + ROW_TILE, F32), pltpu.SemaphoreType.DMA((2,))],
        ),
        out_shape=jax.ShapeDtypeStruct((n_tiles * TM, D), F32),
        compiler_params=_cparams(("arbitrary",)),
        name="moe_gather",
    )(p1, p2, xa, mod, tmeta, final_g, ys)


def _issue_row_gathers(p1_ref, p2_ref, ys_ref, buf, sem, tile, s, unrolled):
    base = tile * TM

    def one(r):
        pltpu.make_async_copy(ys_ref.at[p1_ref[base + r]], buf.at[s, 0, r], sem.at[s]).start()
        pltpu.make_async_copy(ys_ref.at[p2_ref[base + r]], buf.at[s, 1, r],
                              sem.at[s]).start(priority=1)

    if unrolled:
        for r in range(TM):
            one(r)
    else:
        def body(r, carry):
            one(r)
            return carry

        lax.fori_loop(0, TM, body, 0, unroll=ISSUE_UNROLL)


def _inproj_kernel(p1_ref, p2_ref, x_ref, mod0_ref, tm_ref, ys_ref, mod_ref, g_ref, win_ref,
                   xn_ref, xm_ref, op_ref, buf, sem):
    i = pl.program_id(0)
    n_steps = pl.num_programs(0)
    slot = lax.rem(i, 2)

    @pl.when(i == 0)
    def _():
        _issue_row_gathers(p1_ref, p2_ref, ys_ref, buf, sem, 0, 0, False)

    _row_copy_wait(buf.at[slot, 0], sem.at[slot])
    _row_copy_wait(buf.at[slot, 1], sem.at[slot])
    y = (tm_ref[:, 0:1] * buf[slot, 0].reshape(TM, D)
         + tm_ref[:, 1:2] * buf[slot, 1].reshape(TM, D))
    xn = x_ref[...] + mod0_ref[0, 5:6, :] * y
    xn_ref[...] = xn
    _issue_row_gathers(p1_ref, p2_ref, ys_ref, buf, sem, jnp.minimum(i + 1, n_steps - 1), 1 - slot,
                       True)
    h = _norm_mod(xn, g_ref[...], mod_ref[0, 0:1, :], mod_ref[0, 1:2, :])
    hb = h.astype(BF16)
    xm_ref[...] = jnp.dot(hb, win_ref[:, :B_INNER], preferred_element_type=F32).astype(BF16)
    op_ref[...] = jnp.dot(hb, win_ref[:, B_INNER:], preferred_element_type=F32).astype(BF16)

    @pl.when(i == n_steps - 1)
    def _():
        _row_copy_wait(buf.at[1 - slot, 0], sem.at[1 - slot])
        _row_copy_wait(buf.at[1 - slot, 1], sem.at[1 - slot])


def _inproj(p1, p2, xa, mod0, tmeta, ys, mod, g, w_in):
    const = lambda i, p1, p2: (0, 0)
    row = lambda i, p1, p2: (i, 0)
    mod_map = lambda i, p1, p2: (_mod_row(i), 0, 0)
    return pl.pallas_call(
        _inproj_kernel,
        grid_spec=pltpu.PrefetchScalarGridSpec(
            num_scalar_prefetch=2,
            grid=(ALL_TILES,),
            in_specs=[
                pl.BlockSpec((TM, D), row),
                pl.BlockSpec((1, 6, D), mod_map),
                pl.BlockSpec((TM, LANES), row),
                pl.BlockSpec(memory_space=pl.ANY),
                pl.BlockSpec((1, 6, D), mod_map),
                pl.BlockSpec((1, D), const),
                pl.BlockSpec((D, 2 * B_INNER), const),
            ],
            out_specs=[pl.BlockSpec((TM, D), row),
                       pl.BlockSpec((TM, B_INNER), row),
                       pl.BlockSpec((TM, B_INNER), row)],
            scratch_shapes=[pltpu.VMEM((2, 2, TM) + ROW_TILE, F32), pltpu.SemaphoreType.DMA((2,))],
        ),
        out_shape=[jax.ShapeDtypeStruct((N_ALL, D), F32),
                   jax.ShapeDtypeStruct((N_ALL, B_INNER), BF16),
                   jax.ShapeDtypeStruct((N_ALL, B_INNER), BF16)],
        compiler_params=_cparams(("arbitrary",)),
        name="mlstm_inproj",
    )(p1, p2, xa, mod0, tmeta, ys, mod, g, w_in)


QV_V0 = HEADS * DQK
QV_W = QV_V0 + B_INNER


def _log_sigmoid(x):
    return jnp.minimum(x, 0.0) - jnp.log(1.0 + jnp.exp(-jnp.abs(x)))


def _qkv_kernel(xm_ref, prev_ref, next_ref, cw_ref, cb_ref, wq_ref, wkt_ref, wv_ref, wg_ref, wgt_ref,
                bg_ref, bgt_ref, xc_ref, qv_ref, kt_ref, gt_ref, gtt_ref):
    i = pl.program_id(0)
    lat = i < LAT_TILES
    first = jnp.where(lat, i % TILES_PER_SEQ == 0, True)
    last = jnp.where(lat, i % TILES_PER_SEQ == TILES_PER_SEQ - 1, True)
    xmb = xm_ref[...]
    prev = jnp.where(first, 0.0, prev_ref[HALO - 8:, :].astype(F32))
    nxt = jnp.where(last, 0.0, next_ref[0:8, :].astype(F32))
    half = CONV_K // 2

    rr = lax.broadcasted_iota(jnp.int32, (TM, TM), 0)
    cc = lax.broadcasted_iota(jnp.int32, (TM, TM), 1)
    shifts = {t: (cc == rr + (t - half)).astype(BF16) for t in range(CONV_K) if t != half}

    def conv_head(h):
        hs = slice(h * DH, (h + 1) * DH)
        xh = xmb[:, hs]

        def conv_rows(ext):
            out = jnp.zeros((8, DH), F32) + cb_ref[:, hs]
            for t in range(CONV_K):
                out = out + ext[8 + t - half:16 + t - half] * cw_ref[t:t + 1, hs]
            return out

        acc = xh.astype(F32) * cw_ref[half:half + 1, hs] + cb_ref[:, hs]
        for t, shift in shifts.items():
            acc = acc + jnp.dot(shift, xh, preferred_element_type=F32) * cw_ref[t:t + 1, hs]
        top = conv_rows(jnp.concatenate([prev[:, hs], xh[0:16].astype(F32)], axis=0))
        bot = conv_rows(jnp.concatenate([xh[TM - 16:].astype(F32), nxt[:, hs]], axis=0))
        acc = jnp.concatenate([top, acc[8:TM - 8], bot], axis=0)
        xcb = (acc * jax.nn.sigmoid(acc)).astype(BF16)
        xc_ref[:, hs] = xcb
        return xcb

    def project(h, xcb):
        qh = jnp.dot(xcb, wq_ref[h], preferred_element_type=F32)
        qv_ref[:, h * DQK:(h + 1) * DQK] = (qh * (DQK ** -0.5)).astype(BF16)
        kt_ref[h * DQK:(h + 1) * DQK, :] = _dot_nt(wkt_ref[h], xcb).astype(BF16)

    gts = jnp.dot(xmb, wg_ref[...], preferred_element_type=F32) + bg_ref[...]
    gtt = _dot_nt(wgt_ref[...], xmb) + bgt_ref[...]

    pending = None
    for h in range(HEADS):
        hs = slice(h * DH, (h + 1) * DH)
        xcb = conv_head(h)
        qv_ref[:, QV_V0 + h * DV:QV_V0 + (h + 1) * DV] = jnp.dot(
            xmb[:, hs], wv_ref[h], preferred_element_type=F32).astype(BF16)
        if pending is not None:
            project(*pending)
        pending = (h, xcb)
    project(*pending)
    lane = lax.broadcasted_iota(jnp.int32, (CHUNK, LANES), 1)
    lane_kind = (lane >> 3) & 3
    row = lax.broadcasted_iota(jnp.int32, (4 * HEADS, CHUNK), 0)
    row_kind = (row >> 3) & 3
    r = lax.broadcasted_iota(jnp.int32, (CHUNK, CHUNK), 0)
    c = lax.broadcasted_iota(jnp.int32, (CHUNK, CHUNK), 1)
    lower = (c <= r).astype(BF16)
    upper = (c >= r).astype(BF16)
    row_sums = jnp.concatenate([upper, lower, jnp.ones((CHUNK, LANES), BF16)], axis=1)
    for ch in range(TM // CHUNK):
        ts = slice(ch * CHUNK, (ch + 1) * CHUNK)
        g = gts[ts]
        g = jnp.where((lane_kind & 1) == 1, _log_sigmoid(g), g)
        terms = _bf16_terms(g, 3)
        pre = sum(_dot_nn(lower, t) for t in terms)
        suf = sum(_dot_nn(upper, t) for t in terms)
        gt_ref[ts, :] = jnp.where(lane_kind == 1, pre, jnp.where(lane_kind == 3, suf, g))
        gt = gtt[:, ts]
        gt = jnp.where((row_kind & 1) == 1, _log_sigmoid(gt), gt)
        sums = _dot_exact_rhs(gt, row_sums)
        gtt_ref[0:4 * HEADS, ts] = jnp.where(row_kind == 1, sums[:, :CHUNK],
                                             jnp.where(row_kind == 3, sums[:, CHUNK:2 * CHUNK], gt))
        gtt_ref[4 * HEADS:, ts] = sums[:, 2 * CHUNK:]


def _qkv(xm, conv_w, conv_b, wq, wkt, wv, wg, wgt, bg, bgt):
    const = lambda i: (0, 0)
    const3 = lambda i: (0, 0, 0)
    hb = TM // HALO
    n_hb = N_ALL // HALO
    return pl.pallas_call(
        _qkv_kernel,
        grid=(ALL_TILES,),
        in_specs=[
            pl.BlockSpec((TM, B_INNER), lambda i: (i, 0)),
            pl.BlockSpec((HALO, B_INNER), lambda i: (jnp.maximum(i * hb - 1, 0), 0)),
            pl.BlockSpec((HALO, B_INNER), lambda i: (jnp.minimum((i + 1) * hb, n_hb - 1), 0)),
            pl.BlockSpec((CONV_K, B_INNER), const),
            pl.BlockSpec((1, B_INNER), const),
            pl.BlockSpec((HEADS, DH, DQK), const3),
            pl.BlockSpec((HEADS, DQK, DH), const3),
            pl.BlockSpec((HEADS, DH, DV), const3),
            pl.BlockSpec((B_INNER, LANES), const),
            pl.BlockSpec((4 * HEADS, B_INNER), const),
            pl.BlockSpec((1, LANES), const),
            pl.BlockSpec((4 * HEADS, 1), const),
        ],
        out_specs=[
            pl.BlockSpec((TM, B_INNER), lambda i: (i, 0)),
            pl.BlockSpec((TM, QV_W), lambda i: (i, 0)),
            pl.BlockSpec((HEADS * DQK, TM), lambda i: (0, i)),
            pl.BlockSpec((TM, LANES), lambda i: (i, 0)),
            pl.BlockSpec((8 * HEADS, TM), lambda i: (0, i)),
        ],
        out_shape=[
            jax.ShapeDtypeStruct((N_ALL, B_INNER), BF16),
            jax.ShapeDtypeStruct((N_ALL, QV_W), BF16),
            jax.ShapeDtypeStruct((HEADS * DQK, N_ALL), BF16),
            jax.ShapeDtypeStruct((N_ALL, LANES), F32),
            jax.ShapeDtypeStruct((8 * HEADS, N_ALL), F32),
        ],
        compiler_params=_cparams(("parallel",)),
        name="mlstm_qkv",
    )(xm, xm, xm, conv_w, conv_b, wq, wkt, wv, wg, wgt, bg, bgt)


CTX_CHUNKS = CTX // CHUNK
LAT_CHUNKS = SEQ // CHUNK
SCAN_STEPS = CTX_CHUNKS + LAT_CHUNKS


def _scan_dir(nb, d, qv_ref, kt_ref, g_ref, gt_ref, o_ref, c_scr, m_scr):
    r = lax.broadcasted_iota(jnp.int32, (CHUNK, CHUNK), 0)
    c = lax.broadcasted_iota(jnp.int32, (CHUNK, CHUNK), 1)
    if d == 0:
        keep = c <= r
    else:
        keep = c >= r
    ones_blk = jnp.ones((CHUNK, LANES), BF16)
    bcol_all = g_ref[...]
    gtt = gt_ref[0:4 * HEADS, :]
    tot_all = gt_ref[4 * HEADS:, :]
    base = 2 * HEADS * d

    def head_first(h):
        li_r = gtt[base + h:base + h + 1, :]
        b_r = gtt[base + HEADS + h:base + HEADS + h + 1, :]
        b_last = tot_all[base + HEADS + h:base + HEADS + h + 1, :]
        m_old = m_scr[nb, d, h]
        c_old = c_scr[nb, d, h]
        qh = qv_ref[:, h * DQK:(h + 1) * DQK]
        kth = kt_ref[h * DQK:(h + 1) * DQK, :]
        vaug = jnp.concatenate([qv_ref[:, QV_V0 + h * DV:QV_V0 + (h + 1) * DV], ones_blk], axis=1)
        g_r = b_last - b_r + li_r
        m_new = jnp.maximum(b_last + m_old, jnp.max(g_r, axis=-1, keepdims=True))
        decay = jnp.exp(b_last + m_old - m_new)
        wk = jnp.exp(g_r - m_new)
        kwt = (kth.astype(F32) * wk).astype(BF16)
        qk = jnp.dot(qh, kth, preferred_element_type=F32)
        qc = jnp.dot(qh, c_old.astype(BF16), preferred_element_type=F32)
        c_scr[nb, d, h] = jnp.concatenate([decay] * 3, axis=1) * c_old + jnp.dot(
            kwt, vaug, preferred_element_type=F32)
        m_scr[nb, d, h] = m_new
        return h, li_r, b_r, m_old, vaug, qk, qc

    def head_second(h, li_r, b_r, m_old, vaug, qk, qc):
        b_full = jnp.broadcast_to(bcol_all[:, base + HEADS + h:base + HEADS + h + 1], (CHUNK, LANES))
        dmat = jnp.where(keep, b_full - (b_r - li_r), NEG_INF)
        inter = b_full + m_old
        m_t = jnp.maximum(inter, jnp.max(dmat, axis=-1, keepdims=True))
        p = jnp.exp(dmat - m_t)
        s = qk * p
        a = jnp.exp(inter - m_t)
        num = jnp.concatenate([a] * 3, axis=1) * qc + jnp.dot(
            s.astype(BF16), vaug, preferred_element_type=F32)
        inv = 1.0 / jnp.maximum(jnp.abs(num[:, DV:]), jnp.exp(-m_t))
        o_ref[nb, :, h * DV:(h + 1) * DV] = (
            num[:, :DV] * jnp.concatenate([inv] * 2, axis=1)).astype(BF16)

    return head_first, head_second


SCAN_NB = 4
SCAN_IN = 4
SCAN_SKEW = 1


def _scan_kernel(*refs):
    n_in = SCAN_NB * 2 * SCAN_IN
    ins, outs = refs[:n_in], refs[n_in:n_in + 2]
    c_scr, m_scr = refs[n_in + 2:]

    @pl.when(pl.program_id(1) == 0)
    def _():
        c_scr[...] = jnp.zeros_like(c_scr)
        m_scr[...] = jnp.zeros_like(m_scr)

    units = []
    for nb in range(SCAN_NB):
        for d in range(2):
            k = nb * 2 + d
            first, second = _scan_dir(nb, d, *ins[k * SCAN_IN:(k + 1) * SCAN_IN], outs[d], c_scr, m_scr)
            units += [(first, second, h) for h in range(HEADS)]
    pending = []
    for first, second, h in units:
        pending.append((second, first(h)))
        if len(pending) > SCAN_SKEW:
            fn, old_vals = pending.pop(0)
            fn(*old_vals)
    for fn, old_vals in pending:
        fn(*old_vals)


def _scan(qv, kt, gts, gtt):
    lat_blk = N_LAT // CHUNK

    def fwd_in(b, j):
        return jnp.where(j < CTX_CHUNKS, lat_blk + b * CTX_CHUNKS + j, b * LAT_CHUNKS + j - CTX_CHUNKS)

    def bwd_in(b, j):
        return jnp.where(j < CTX_CHUNKS, lat_blk + b * CTX_CHUNKS + (CTX_CHUNKS - 1 - j),
                         b * LAT_CHUNKS + (SCAN_STEPS - 1 - j))

    def fwd_out(b, j):
        return b * LAT_CHUNKS + jnp.maximum(j - CTX_CHUNKS, 0)

    def bwd_out(b, j):
        return b * LAT_CHUNKS + jnp.minimum(SCAN_STEPS - 1 - j, LAT_CHUNKS - 1)

    n_g = BATCH // SCAN_NB

    def specs(fn, nb):
        blk = lambda g, j: fn(nb * n_g + g, j)
        return [
            pl.BlockSpec((CHUNK, QV_W), lambda g, j: (blk(g, j), 0)),
            pl.BlockSpec((HEADS * DQK, CHUNK), lambda g, j: (0, blk(g, j))),
            pl.BlockSpec((CHUNK, LANES), lambda g, j: (blk(g, j), 0)),
            pl.BlockSpec((8 * HEADS, CHUNK), lambda g, j: (0, blk(g, j))),
        ]

    def out_spec(fn):
        return pl.BlockSpec((SCAN_NB, CHUNK, B_INNER), lambda g, j: (0, fn(g, j), 0))

    in_specs, operands = [], []
    for nb in range(SCAN_NB):
        for fn_in in (fwd_in, bwd_in):
            in_specs += specs(fn_in, nb)
            operands += [qv, kt, gts, gtt]
    h_shape = jax.ShapeDtypeStruct((SCAN_NB, N_LAT // SCAN_NB, B_INNER), BF16)
    hf, hb = pl.pallas_call(
        _scan_kernel,
        grid=(n_g, SCAN_STEPS),
        in_specs=in_specs,
        out_specs=[out_spec(fwd_out), out_spec(bwd_out)],
        out_shape=[h_shape, h_shape],
        scratch_shapes=[pltpu.VMEM((SCAN_NB, 2, HEADS, DQK, DV + LANES), F32),
                        pltpu.VMEM((SCAN_NB, 2, HEADS, 1, LANES), F32)],
        compiler_params=_cparams(("parallel", "arbitrary")),
        name="mlstm_scan",
    )(*operands)
    return hf.reshape(N_LAT, B_INNER), hb.reshape(N_LAT, B_INNER)


def _combine_kernel(x_ref, mod_ref, hf_ref, hb_ref, xc_ref, op_ref, hg_ref, sk_ref, wout_ref,
                    g2_ref, wr_ref, br_ref, o_ref, meta_ref, tmeta_ref, cnt_ref, cntr_ref,
                    y_scr, carry, carry_r):
    for h in range(HEADS):
        hs = slice(h * DV, (h + 1) * DV)
        s = hf_ref[:, hs].astype(F32) + hb_ref[:, hs].astype(F32)
        s = s * lax.rsqrt(jnp.mean(s * s, axis=-1, keepdims=True) + EPS)
        y = jax.nn.sigmoid(op_ref[:, hs].astype(F32)) * (
            s * hg_ref[:, hs] + sk_ref[:, hs] * xc_ref[:, hs].astype(F32))
        y_scr[:, hs] = y.astype(BF16)
        if h % 2 == 1:
            ks = slice((h - 1) * DV, (h + 1) * DV)
            part = jnp.dot(y_scr[:, ks], wout_ref[ks, :], preferred_element_type=F32)
            out = part if h == 1 else out + part
    xn = x_ref[...] + mod_ref[0, 2:3, :] * out
    o_ref[...] = xn
    _route_tile(xn, mod_ref, g2_ref, wr_ref, br_ref, meta_ref, tmeta_ref, cnt_ref, cntr_ref,
                carry, carry_r)


def _combine(xa, mod, hf, hb, xc, op, head_g, skip, w_out, g2, wr, br):
    const = lambda i: (0, 0)
    row = lambda i: (i, 0)
    r_in, r_out, r_shape, r_scratch = _router_specs(LAT_TILES)
    return pl.pallas_call(
        _combine_kernel,
        grid=(LAT_TILES,),
        in_specs=[
            pl.BlockSpec((TM, D), row),
            pl.BlockSpec((1, 6, D), lambda i: (_mod_row(i), 0, 0)),
            pl.BlockSpec((TM, B_INNER), row),
            pl.BlockSpec((TM, B_INNER), row),
            pl.BlockSpec((TM, B_INNER), row),
            pl.BlockSpec((TM, B_INNER), row),
            pl.BlockSpec((1, B_INNER), const),
            pl.BlockSpec((1, B_INNER), const),
            pl.BlockSpec((B_INNER, D), const),
        ] + r_in,
        out_specs=[pl.BlockSpec((TM, D), row)] + r_out,
        out_shape=[jax.ShapeDtypeStruct((N_LAT, D), F32)] + r_shape,
        scratch_shapes=[pltpu.VMEM((TM, B_INNER), BF16)] + r_scratch,
        compiler_params=_cparams(("arbitrary",)),
        name="mlstm_combine",
    )(xa, mod, hf, hb, xc, op, head_g, skip, w_out, g2, wr, br)


def _router_weights(w_grp, b_grp, w_exp, b_exp):
    wr = jnp.zeros((LANES, D), F32).at[:N_EXP].set(w_exp.T).at[N_EXP:N_EXP + N_GRP].set(w_grp.T)
    br = jnp.zeros((LANES, 1), F32).at[:N_EXP, 0].set(b_exp).at[N_EXP:N_EXP + N_GRP, 0].set(b_grp)
    return wr.astype(BF16), br


PLAN_BLK = 2048
PLAN_PAIRS = 256


def _plan_kernel(meta_ref, cc_ref, cr_ref, pp_ref, plan_ref):
    row = lax.broadcasted_iota(jnp.int32, (LANES, LANES), 0)
    col = lax.broadcasted_iota(jnp.int32, (LANES, LANES), 1)
    cnt_c = cc_ref[...]
    cnt_r = cr_ref[0:1, :]
    starts_c = jnp.sum(jnp.where(col < row, cnt_r, 0.0), axis=1, keepdims=True)
    erow = lax.broadcasted_iota(jnp.int32, (LANES, PLAN_BLK), 0).astype(F32)
    for j in range(2):
        pos = jnp.sum(jnp.where(erow == meta_ref[j:j + 1, :], starts_c, 0.0), axis=0,
                      keepdims=True) + meta_ref[j + 2:j + 3, :]
        pp_ref[j:j + 1, :] = pos.astype(jnp.int32)
    pp_ref[2:8, :] = jnp.zeros((6, PLAN_BLK), jnp.int32)

    @pl.when(pl.program_id(0) == 0)
    def _():
        ends_c = starts_c + cnt_c
        starts_r = jnp.sum(jnp.where(row < col, cnt_c, 0.0), axis=0, keepdims=True)
        ends_r = starts_r + cnt_r

        def tiles(st, en, cn):
            first_tile = jnp.floor(st * (1.0 / TM))
            n = jnp.where(cn > 0.0, jnp.floor((en - 1.0) * (1.0 / TM)) - first_tile + 1.0, 0.0)
            return first_tile, n

        ft_c, pairs_c = tiles(starts_c, ends_c, cnt_c)
        _, pairs_r = tiles(starts_r, ends_r, cnt_r)
        pend_c = jnp.sum(jnp.where(col <= row, pairs_r, 0.0), axis=1, keepdims=True)
        pstart_c = pend_c - pairs_c
        total = jnp.sum(pairs_r, axis=1, keepdims=True)
        q = lax.broadcasted_iota(jnp.int32, (1, PLAN_PAIRS), 1).astype(F32)
        erow_p = lax.broadcasted_iota(jnp.int32, (LANES, PLAN_PAIRS), 0).astype(F32)

        def at(qv):
            qc = jnp.maximum(jnp.minimum(qv, total - 1.0), 0.0)
            e = jnp.sum(jnp.where(qc >= pend_c, 1.0, 0.0), axis=0, keepdims=True)
            oh = erow_p == e
            tile = jnp.sum(jnp.where(oh, ft_c - pstart_c, 0.0), axis=0, keepdims=True) + qc
            return e, oh, tile

        e_q, oh, tile_q = at(q)
        e_p, _, tile_p = at(q - 1.0)
        lo = jnp.sum(jnp.where(oh, starts_c, 0.0), axis=0, keepdims=True) - tile_q * TM
        hi = jnp.sum(jnp.where(oh, ends_c, 0.0), axis=0, keepdims=True) - tile_q * TM
        rows = (tile_q, e_q, jnp.clip(lo, 0.0, TM), jnp.clip(hi, 0.0, TM),
                jnp.where((q == 0.0) | (tile_q != tile_p), 1.0, 0.0),
                jnp.where((q == 0.0) | (e_q != e_p), 1.0, 0.0),
                jnp.where(q < total, 1.0, 0.0),
                jnp.zeros((1, PLAN_PAIRS), F32))
        for j, v in enumerate(rows):
            plan_ref[j:j + 1, :] = v.astype(jnp.int32)


def _route_plan(meta, cnt_c, cnt_r, n_rows):
    n_pairs = 2 * n_rows // TM + N_EXP - 1
    assert n_pairs <= PLAN_PAIRS and n_rows % PLAN_BLK == 0
    const = lambda i: (0, 0)
    pp, plan = pl.pallas_call(
        _plan_kernel,
        grid=(n_rows // PLAN_BLK,),
        in_specs=[pl.BlockSpec((8, PLAN_BLK), lambda i: (0, i)),
                  pl.BlockSpec((LANES, 1), const),
                  pl.BlockSpec((8, LANES), const)],
        out_specs=[pl.BlockSpec((8, PLAN_BLK), lambda i: (0, i)),
                   pl.BlockSpec((8, PLAN_PAIRS), const)],
        out_shape=[jax.ShapeDtypeStruct((8, n_rows), jnp.int32),
                   jax.ShapeDtypeStruct((8, PLAN_PAIRS), jnp.int32)],
        compiler_params=_cparams(("arbitrary",)),
        name="moe_plan",
    )(meta, cnt_c, cnt_r)
    return pp[0], pp[1], tuple(plan[j, :n_pairs] for j in range(7))


def _moe_experts(xa, routing, mod, g2, w1, w3, w2, layer, n_rows):
    meta, _, cnt_c, cnt_r = routing
    p1, p2, plan = _route_plan(meta, cnt_c, cnt_r, n_rows)
    xs = _scatter(p1, p2, xa, mod, g2, n_rows // TM)
    return p1, p2, _grouped(plan, xs, w1, w3, w2, layer)


def kernel(x, c, ctx, c_ctx, norm_g, w_ada, b_ada, a_w_in, a_b_in, a_g_v, a_w_s, a_b_s, a_w_out,
           b_w_in, b_conv_w, b_conv_b, b_w_q, b_w_k, b_w_v, b_w_gate, b_b_gate, b_head_g, b_skip,
           b_w_out, moe_w_grp, moe_b_grp, moe_w_exp, moe_b_exp, moe_w1, moe_w3, moe_w2, final_g):
    cc = jnp.zeros((MOD_ROWS, D), F32).at[:BATCH].set(c).at[BATCH].set(c_ctx)
    mods = _ada(cc, w_ada, b_ada).reshape(2, MOD_ROWS, 6, D)

    mod = mods[0]
    b_s_full = jnp.repeat(a_b_s[0].T, A_GC, axis=1)
    g2 = norm_g[0, 1].reshape(1, D)
    wr, br = _router_weights(moe_w_grp[0], moe_b_grp[0], moe_w_exp[0], moe_b_exp[0])
    xa, *routing = _gmlp(x.reshape(N_LAT, D), ctx.reshape(N_CTX, D), mod, norm_g[0, 0].reshape(1, D),
                         a_w_in[0].astype(BF16), a_b_in[0].reshape(1, -1), a_g_v[0].reshape(1, -1),
                         a_w_s[0].astype(BF16), b_s_full, a_w_out[0].astype(BF16), g2, wr, br)
    p1, p2, ys = _moe_experts(xa, routing, mod, g2, moe_w1, moe_w3, moe_w2, 0, N_ALL)

    mod = mods[1]
    xa, xm, op = _inproj(p1, p2, xa, mods[0], routing[1], ys, mod, norm_g[1, 0].reshape(1, D),
                         b_w_in[0].astype(BF16))
    wg = jnp.zeros((B_INNER, LANES), F32).at[:, :4 * HEADS].set(b_w_gate[0]).astype(BF16)
    bg = jnp.zeros((1, LANES), F32).at[0, :4 * HEADS].set(b_b_gate[0])
    xc, qv, kt, gts, gtt = _qkv(xm, b_conv_w[0], b_conv_b[0].reshape(1, -1),
                                  b_w_q[0].astype(BF16),
                                  jnp.transpose(b_w_k[0], (0, 2, 1)).astype(BF16),
                                  b_w_v[0].astype(BF16), wg, b_w_gate[0].T.astype(BF16), bg,
                                  b_b_gate[0].reshape(-1, 1))
    hf, hb = _scan(qv, kt, gts, gtt)
    g2 = norm_g[1, 1].reshape(1, D)
    wr, br = _router_weights(moe_w_grp[1], moe_b_grp[1], moe_w_exp[1], moe_b_exp[1])
    xl, *routing = _combine(xa, mod, hf, hb, xc, op, b_head_g[0].reshape(1, -1),
                            b_skip[0].reshape(1, -1), b_w_out[0].astype(BF16), g2, wr, br)
    p1, p2, ys = _moe_experts(xl, routing, mod, g2, moe_w1, moe_w3, moe_w2, 1, N_LAT)
    out = _gather(p1, p2, xl, mod, routing[1], final_g.reshape(1, D), ys, LAT_TILES, True)
    return out.reshape(BATCH, SEQ, D)
```
